```python
import math
import jax, jax.numpy as jnp
from jax import lax
import numpy as np

D_MODEL = 2048
BATCH = 8
SEQ = 2048
DEPTH = 2

GRID_W = 64
CTX_LEN = 256
EPS = 1e-6
NA_HEADS = 8
NA_HEAD_DIM = 128
NA_WIDTH = NA_HEADS * NA_HEAD_DIM
NA_WIN_R = 8
NA_WIN_C = 16
NA_QBLK = 16
NA_KBLK = 32
FOUR_GROUPS = 4
FOUR_GROUP_DIM = 256
FOUR_WIDTH = FOUR_GROUPS * FOUR_GROUP_DIM
SSD_HEADS = 16
SSD_HEAD_DIM = 64
SSD_WIDTH = SSD_HEADS * SSD_HEAD_DIM
SSD_GROUPS = 4
SSD_HPG = SSD_HEADS // SSD_GROUPS
SSD_STATE = 128
SSD_CONV = 7
SSD_CHUNK = 128
SSD_CONV_CH = SSD_WIDTH + 2 * SSD_GROUPS * SSD_STATE
ROPE_BASE = 10000.0
IN_SPLITS = (NA_WIDTH, NA_WIDTH, NA_WIDTH, NA_WIDTH, FOUR_WIDTH, FOUR_WIDTH,
             SSD_CONV_CH, SSD_WIDTH, 2 * SSD_HEADS, D_MODEL, D_MODEL, D_MODEL)
IN_WIDTH = sum(IN_SPLITS)

kernel_name = "hybrid_natten_fnet_ssd_prefix_block"


def rmsnorm(x, w):
    xf = x.astype(jnp.float32)
    y = xf * lax.rsqrt(jnp.mean(xf * xf, axis=-1, keepdims=True) + EPS)
    return (y * w.astype(jnp.float32)).astype(x.dtype)


def split_cols(p):
    offs = np.cumsum(IN_SPLITS)[:-1].tolist()
    return jnp.split(p, offs, axis=-1)


def axial_rope(t):
    L, n = t.shape[1], t.shape[-1]
    quarter = n // 4
    pos = jnp.arange(L)
    inv = ROPE_BASE ** (-jnp.arange(quarter, dtype=jnp.float32) / quarter)

    def rot(u, p):
        ang = p.astype(jnp.float32)[:, None] * inv
        cos = jnp.cos(ang)[None, :, None, :].astype(u.dtype)
        sin = jnp.sin(ang)[None, :, None, :].astype(u.dtype)
        u1, u2 = u[..., :quarter], u[..., quarter:]
        return jnp.concatenate([u1 * cos - u2 * sin, u1 * sin + u2 * cos], axis=-1)

    return jnp.concatenate([rot(t[..., :2 * quarter], pos // GRID_W),
                            rot(t[..., 2 * quarter:], pos % GRID_W)], axis=-1)


def dwconv(u, w, b):
    k = w.shape[0]
    y = lax.conv_general_dilated(u, w[:, None, :], window_strides=(1,), padding=[(k // 2, k // 2)],
                                 dimension_numbers=("NWC", "WIO", "NWC"), feature_group_count=u.shape[-1])
    return y + b


def segsum(a):
    t = a.shape[-1]
    cs = jnp.cumsum(a, axis=-1)
    d = cs[..., :, None] - cs[..., None, :]
    return jnp.where(jnp.tril(jnp.ones((t, t), dtype=bool)), d, -jnp.inf)


def ssd_chunked(xdt, a, bm, cm, init):
    b, L, g, k, p = xdt.shape
    nc = L // SSD_CHUNK
    x = xdt.reshape(b, nc, SSD_CHUNK, g, k, p)
    A = a.astype(jnp.float32).reshape(b, nc, SSD_CHUNK, g, k).transpose(0, 3, 4, 1, 2)
    Bc = bm.reshape(b, nc, SSD_CHUNK, g, -1)
    Cc = cm.reshape(b, nc, SSD_CHUNK, g, -1)
    a_cs = jnp.cumsum(A, axis=-1)
    lmat = jnp.exp(segsum(A))
    y_diag = jnp.einsum("bclgn,bcsgn,bgkcls,bcsgkp->bclgkp", Cc, Bc, lmat, x)
    decay_states = jnp.exp(a_cs[..., -1:] - a_cs)
    states = jnp.einsum("bclgn,bgkcl,bclgkp->bcgkpn", Bc, decay_states, x)
    states = jnp.concatenate([init[:, None].astype(states.dtype), states], axis=1)
    chunk_decay = jnp.exp(segsum(jnp.pad(a_cs[..., -1], ((0, 0), (0, 0), (0, 0), (1, 0)))))
    states = jnp.einsum("bgkzc,bcgkpn->bzgkpn", chunk_decay, states)
    y_off = jnp.einsum("bclgn,bcgkpn,bgkcl->bclgkp", Cc, states[:, :-1], jnp.exp(a_cs))
    y = (y_diag + y_off).reshape(b, L, g, k, p)
    return y, states[:, -1]


def ssd_mixer(xbc_c, dt_c, xbc_l, dt_l, conv_w, conv_b, dt_bias, a_log, d_skip):
    out_dtype = xbc_l.dtype

    def prep(xbc, dt_raw, rope):
        u = jax.nn.silu(dwconv(xbc, conv_w, conv_b))
        xs, bm, cm = jnp.split(u, [SSD_WIDTH, SSD_WIDTH + SSD_GROUPS * SSD_STATE], axis=-1)
        b, L = xs.shape[:2]
        xs = xs.reshape(b, L, SSD_GROUPS, SSD_HPG, SSD_HEAD_DIM)
        bm = bm.reshape(b, L, SSD_GROUPS, SSD_STATE)
        cm = cm.reshape(b, L, SSD_GROUPS, SSD_STATE)
        if rope:
            bm, cm = axial_rope(bm), axial_rope(cm)
        dt = jax.nn.softplus(dt_raw.astype(jnp.float32).reshape(b, L, 2, SSD_GROUPS, SSD_HPG)
                             + dt_bias.astype(jnp.float32).reshape(2, SSD_GROUPS, SSD_HPG))
        return xs, bm, cm, dt

    xc, bc, cc, dtc = prep(xbc_c, dt_c, False)
    xl, bl, cl, dtl = prep(xbc_l, dt_l, True)
    a = -jnp.exp(a_log.astype(jnp.float32)).reshape(2, SSD_GROUPS, SSD_HPG)
    dsk = d_skip.astype(jnp.float32).reshape(SSD_GROUPS, SSD_HPG, 1)
    b = xl.shape[0]
    init = jnp.zeros((b, SSD_GROUPS, SSD_HPG, SSD_HEAD_DIM, SSD_STATE), jnp.float32)
    yc = dsk * xc.astype(jnp.float32)
    yl = dsk * xl.astype(jnp.float32)
    for direction in range(2):
        flip = (lambda t: t[:, ::-1]) if direction == 1 else (lambda t: t)

        def run(xs, bm, cm, dt, s0):
            dtd = dt[:, :, direction]
            y, s = ssd_chunked(flip(xs * dtd[..., None]), flip(dtd * a[direction]), flip(bm), flip(cm), s0)
            return flip(y), s

        y_c, s_c = run(xc, bc, cc, dtc, init)
        y_l, _ = run(xl, bl, cl, dtl, s_c)
        yc = yc + y_c
        yl = yl + y_l
    return (yc.reshape(yc.shape[0], yc.shape[1], SSD_WIDTH).astype(out_dtype),
            yl.reshape(b, yl.shape[1], SSD_WIDTH).astype(out_dtype))


def na_latent(q, k, v, k_ctx, v_ctx, rpb):
    b, L, h, d = q.shape
    rows = L // GRID_W
    win_r = min(NA_WIN_R, rows)
    nblk = GRID_W // NA_QBLK
    scale = d ** -0.5
    qcol = np.arange(GRID_W).reshape(nblk, NA_QBLK)
    qstart = np.clip(qcol - NA_WIN_C // 2, 0, GRID_W - NA_WIN_C)
    kstart = np.clip(qcol[:, 0] - NA_WIN_C // 2, 0, GRID_W - NA_KBLK)
    kcol = (kstart[:, None] + np.arange(NA_KBLK)[None]).astype(np.int32)
    col_mask = ((kcol[:, None, :] >= qstart[..., None]) &
                (kcol[:, None, :] < qstart[..., None] + NA_WIN_C))
    dc = np.clip(kcol[:, None, :] - qcol[..., None] + NA_WIN_C - 1, 0, 2 * NA_WIN_C - 2).astype(np.int32)
    kg = k.reshape(b, rows, GRID_W, h, d)
    vg = v.reshape(b, rows, GRID_W, h, d)
    qg = q.reshape(b, rows, nblk, NA_QBLK, h, d).transpose(1, 0, 2, 3, 4, 5)
    rpb32 = rpb.astype(jnp.float32)
    nwin = win_r * NA_KBLK

    def row_block(args):
        r, q_r = args
        start = jnp.clip(r - win_r // 2, 0, rows - win_r)
        k_r = lax.dynamic_slice_in_dim(kg, start, win_r, axis=1)[:, :, kcol]
        v_r = lax.dynamic_slice_in_dim(vg, start, win_r, axis=1)[:, :, kcol]
        dr = start + jnp.arange(win_r) - r + NA_WIN_R - 1
        bias = rpb32[:, dr][:, :, dc].transpose(0, 2, 3, 1, 4)
        s_w = jnp.einsum("bjqhd,bwjkhd->bhjqwk", q_r, k_r).astype(jnp.float32) * scale + bias[None]
        s_w = jnp.where(col_mask[:, :, None, :], s_w, -1e30)
        s_c = jnp.einsum("bjqhd,bchd->bhjqc", q_r, k_ctx).astype(jnp.float32) * scale
        s = jnp.concatenate([s_w.reshape(b, h, nblk, NA_QBLK, nwin), s_c], axis=-1)
        p = jax.nn.softmax(s, axis=-1).astype(v.dtype)
        p_w = p[..., :nwin].reshape(b, h, nblk, NA_QBLK, win_r, NA_KBLK)
        return (jnp.einsum("bhjqwk,bwjkhd->bjqhd", p_w, v_r)
                + jnp.einsum("bhjqc,bchd->bjqhd", p[..., nwin:], v_ctx))

    o = lax.map(row_block, (jnp.arange(rows), qg))
    return o.transpose(1, 0, 2, 3, 4, 5).reshape(b, L, h * d)


def ctx_attention(q, k, v):
    b, n, h, d = q.shape
    s = jnp.einsum("bqhd,bkhd->bhqk", q, k).astype(jnp.float32) * (d ** -0.5)
    p = jax.nn.softmax(s, axis=-1).astype(v.dtype)
    return jnp.einsum("bhqk,bkhd->bqhd", p, v).reshape(b, n, h * d)


def fourier_mix(u, w):
    b, L, _ = u.shape
    ug = u.astype(jnp.float32).reshape(b, L, FOUR_GROUPS, FOUR_GROUP_DIM)
    f = jnp.fft.fft2(ug, axes=(1, 3), norm="ortho").real
    return f.reshape(b, L, FOUR_WIDTH).astype(u.dtype) @ w


def heads(t):
    return t.reshape(t.shape[0], t.shape[1], NA_HEADS, NA_HEAD_DIM)


def layer(xc, xl, c, c_ctx, w_ada, b_ada, norm_w, w_in, rpb, four_w, conv_w, conv_b, dt_bias, a_log, d_skip,
          ssd_norm_w, wb_na, wb_four, wb_ssd, w_out, update_ctx):
    mod_l = jax.nn.silu(c) @ w_ada + b_ada
    mod_c = jax.nn.silu(c_ctx) @ w_ada + b_ada
    sh_l, sc_l, g_l = jnp.split(mod_l[:, None, :], 3, axis=-1)
    sh_c, sc_c, g_c = jnp.split(mod_c, 3, axis=-1)
    hl = rmsnorm(xl, norm_w) * (1 + sc_l) + sh_l
    hc = rmsnorm(xc, norm_w) * (1 + sc_c) + sh_c
    q_l, k_l, v_l, zna_l, uf_l, zf_l, xbc_l, zs_l, dt_l, gna_l, gf_l, gs_l = split_cols(hl @ w_in)
    q_c, k_c, v_c, zna_c, uf_c, zf_c, xbc_c, zs_c, dt_c, gna_c, gf_c, gs_c = split_cols(hc @ w_in)

    def merge(a, za, f, zf, s, zs, ga, gf, gs):
        o_a = (a * jax.nn.silu(za)) @ wb_na
        o_f = (f * jax.nn.silu(zf)) @ wb_four
        o_s = rmsnorm(s * jax.nn.silu(zs), ssd_norm_w) @ wb_ssd
        return (jax.nn.sigmoid(ga) * o_a + jax.nn.sigmoid(gf) * o_f + jax.nn.sigmoid(gs) * o_s) @ w_out

    kc_h, vc_h = heads(k_c), heads(v_c)
    a_l = na_latent(heads(q_l), heads(k_l), heads(v_l), kc_h, vc_h, rpb)
    f_l = fourier_mix(uf_l, four_w)
    s_c, s_l = ssd_mixer(xbc_c, dt_c, xbc_l, dt_l, conv_w, conv_b, dt_bias, a_log, d_skip)
    xl_new = xl + g_l * merge(a_l, zna_l, f_l, zf_l, s_l, zs_l, gna_l, gf_l, gs_l)
    if update_ctx:
        a_c = ctx_attention(heads(q_c), kc_h, vc_h)
        f_c = fourier_mix(uf_c, four_w)
        xc = xc + g_c * merge(a_c, zna_c, f_c, zf_c, s_c, zs_c, gna_c, gf_c, gs_c)
    return xc, xl_new


def setup_inputs(seed: int = 0) -> dict:
    key = jax.random.key(seed)
    ks = jax.random.split(key, 24)

    def nrm(k, shape, s):
        return jax.random.normal(k, shape, jnp.float32) * s

    dt0 = jnp.exp(jax.random.uniform(ks[12], (DEPTH, 2, SSD_HEADS), jnp.float32,
                                     minval=math.log(1e-3), maxval=math.log(1e-1)))
    return {
        "x": nrm(ks[0], (BATCH, SEQ, D_MODEL), 1.0),
        "c": nrm(ks[1], (BATCH, D_MODEL), 1.0),
        "ctx": nrm(ks[2], (BATCH, CTX_LEN, D_MODEL), 1.0),
        "c_ctx": nrm(ks[3], (D_MODEL,), 1.0),
        "w_ada": nrm(ks[4], (DEPTH, D_MODEL, 3 * D_MODEL), 0.5 * D_MODEL ** -0.5),
        "b_ada": nrm(ks[5], (DEPTH, 3 * D_MODEL), 0.02),
        "norm_w": 1.0 + nrm(ks[6], (DEPTH, D_MODEL), 0.05),
        "w_in": nrm(ks[7], (DEPTH, D_MODEL, IN_WIDTH), D_MODEL ** -0.5),
        "na_rpb": nrm(ks[8], (DEPTH, NA_HEADS, 2 * NA_WIN_R - 1, 2 * NA_WIN_C - 1), 0.05),
        "four_w": nrm(ks[9], (DEPTH, FOUR_WIDTH, FOUR_WIDTH), FOUR_WIDTH ** -0.5),
        "ssd_conv_w": nrm(ks[10], (DEPTH, SSD_CONV, SSD_CONV_CH), SSD_CONV ** -0.5),
        "ssd_conv_b": nrm(ks[11], (DEPTH, SSD_CONV_CH), 0.01),
        "ssd_dt_bias": dt0 + jnp.log(-jnp.expm1(-dt0)),
        "ssd_a_log": jnp.log(jax.random.uniform(ks[13], (DEPTH, 2, SSD_HEADS), jnp.float32, minval=1.0, maxval=16.0)),
        "ssd_d": 1.0 + nrm(ks[14], (DEPTH, SSD_HEADS), 0.1),
        "ssd_norm_w": 1.0 + nrm(ks[15], (DEPTH, SSD_WIDTH), 0.05),
        "wb_na": nrm(ks[16], (DEPTH, NA_WIDTH, D_MODEL), NA_WIDTH ** -0.5),
        "wb_four": nrm(ks[17], (DEPTH, FOUR_WIDTH, D_MODEL), FOUR_WIDTH ** -0.5),
        "wb_ssd": nrm(ks[18], (DEPTH, SSD_WIDTH, D_MODEL), SSD_WIDTH ** -0.5),
        "w_out": nrm(ks[19], (DEPTH, D_MODEL, D_MODEL), D_MODEL ** -0.5),
        "final_norm_w": 1.0 + nrm(ks[20], (D_MODEL,), 0.05),
    }


def reference(x, c, ctx, c_ctx, w_ada, b_ada, norm_w, w_in, na_rpb, four_w, ssd_conv_w, ssd_conv_b,
              ssd_dt_bias, ssd_a_log, ssd_d, ssd_norm_w, wb_na, wb_four, wb_ssd, w_out, final_norm_w):
    xc, xl = ctx, x
    for l in range(DEPTH):
        xc, xl = layer(xc, xl, c, c_ctx, w_ada[l], b_ada[l], norm_w[l], w_in[l], na_rpb[l], four_w[l],
                       ssd_conv_w[l], ssd_conv_b[l], ssd_dt_bias[l], ssd_a_log[l], ssd_d[l], ssd_norm_w[l],
                       wb_na[l], wb_four[l], wb_ssd[l], w_out[l], update_ctx=(l < DEPTH - 1))
    return rmsnorm(xl, final_norm_w)
```

```python
import math
from functools import partial

import jax
import jax.numpy as jnp
import numpy as np
from jax import lax
from jax.experimental import pallas as pl
from jax.experimental.pallas import tpu as pltpu

D_MODEL = 2048
DEPTH = 2
GRID_W = 64
EPS = 1e-6
NA_HEADS = 8
NA_HEAD_DIM = 128
NA_WIDTH = NA_HEADS * NA_HEAD_DIM
NA_WIN_R = 8
NA_WIN_C = 16
NA_QBLK = 16
NA_KBLK = 32
FOUR_GROUPS = 4
FOUR_GROUP_DIM = 256
FOUR_WIDTH = FOUR_GROUPS * FOUR_GROUP_DIM
SSD_HEADS = 16
SSD_HEAD_DIM = 64
SSD_WIDTH = SSD_HEADS * SSD_HEAD_DIM
SSD_GROUPS = 4
SSD_HPG = SSD_HEADS // SSD_GROUPS
SSD_STATE = 128
SSD_CONV = 7
SSD_CHUNK = 128
SSD_CONV_CH = SSD_WIDTH + 2 * SSD_GROUPS * SSD_STATE
ROPE_BASE = 10000.0
DT_WIDTH = 2 * SSD_HEADS
MAIN_SPLITS = (NA_WIDTH, NA_WIDTH, NA_WIDTH, NA_WIDTH, FOUR_WIDTH, FOUR_WIDTH, SSD_CONV_CH, SSD_WIDTH,
               D_MODEL, D_MODEL, D_MODEL)
MAIN_WIDTH = sum(MAIN_SPLITS)
DT_OFFSET = sum(MAIN_SPLITS[:8])
DT_PAD = 128

V7X_VMEM_LIMIT_BYTES = 56 * 1024 * 1024


def _rmsnorm(x, w):
    xf = x.astype(jnp.float32)
    y = xf * lax.rsqrt(jnp.mean(xf * xf, axis=-1, keepdims=True) + EPS)
    return (y * w.astype(jnp.float32)).astype(x.dtype)


def _mm_kernel(a_ref, b_ref, o_ref):
    a = a_ref[...].astype(jnp.bfloat16)
    o_ref[...] = jnp.dot(a, b_ref[...], preferred_element_type=jnp.float32).astype(o_ref.dtype)


def matmul(a, b, *, tm, tn, out_dtype=jnp.float32):
    m, k = a.shape
    _, n = b.shape
    assert m % tm == 0 and n % tn == 0, (a.shape, b.shape, tm, tn)
    return pl.pallas_call(
        _mm_kernel,
        grid=(m // tm, n // tn),
        in_specs=[pl.BlockSpec((tm, k), lambda i, j: (i, 0)),
                  pl.BlockSpec((k, tn), lambda i, j: (0, j))],
        out_specs=pl.BlockSpec((tm, tn), lambda i, j: (i, j)),
        out_shape=jax.ShapeDtypeStruct((m, n), out_dtype),
        compiler_params=pltpu.CompilerParams(dimension_semantics=("arbitrary", "arbitrary"),
                                             vmem_limit_bytes=V7X_VMEM_LIMIT_BYTES),
        name="matmul",
    )(a, b)


def _inproj_kernel(x_ref, nw_ref, sc_ref, sh_ref, w_ref, wdt_ref, o_ref, dt_ref, h_ref):
    @pl.when(pl.program_id(1) == 0)
    def _():
        x = x_ref[...]
        y = x * lax.rsqrt(jnp.mean(x * x, axis=-1, keepdims=True) + EPS) * nw_ref[...]
        h = (y * (1.0 + sc_ref[0]) + sh_ref[0]).astype(jnp.bfloat16)
        h_ref[...] = h
        dt_ref[...] = jnp.dot(h, wdt_ref[...], preferred_element_type=jnp.float32)

    o_ref[...] = jnp.dot(h_ref[...], w_ref[...], preferred_element_type=jnp.float32).astype(o_ref.dtype)


def in_projection(x2d, norm_w, scale, shift, w_main, w_dt, *, rows_per_mod, tm, tn, out_dtype):
    m, d = x2d.shape
    n = w_main.shape[1]
    assert m % tm == 0 and n % tn == 0 and rows_per_mod % tm == 0
    per = rows_per_mod // tm
    return pl.pallas_call(
        _inproj_kernel,
        grid=(m // tm, n // tn),
        in_specs=[pl.BlockSpec((tm, d), lambda i, j: (i, 0)),
                  pl.BlockSpec((1, d), lambda i, j: (0, 0)),
                  pl.BlockSpec((1, 1, d), lambda i, j: (i // per, 0, 0)),
                  pl.BlockSpec((1, 1, d), lambda i, j: (i // per, 0, 0)),
                  pl.BlockSpec((d, tn), lambda i, j: (0, j)),
                  pl.BlockSpec((d, DT_PAD), lambda i, j: (0, 0))],
        out_specs=[pl.BlockSpec((tm, tn), lambda i, j: (i, j)),
                   pl.BlockSpec((tm, DT_PAD), lambda i, j: (i, 0))],
        out_shape=[jax.ShapeDtypeStruct((m, n), out_dtype),
                   jax.ShapeDtypeStruct((m, DT_PAD), jnp.float32)],
        scratch_shapes=[pltpu.VMEM((tm, d), jnp.bfloat16)],
        compiler_params=pltpu.CompilerParams(dimension_semantics=("arbitrary", "arbitrary"),
                                             vmem_limit_bytes=V7X_VMEM_LIMIT_BYTES),
        name="in_projection",
    )(x2d, norm_w.reshape(1, d), scale, shift, w_main, w_dt)


def _axial_rope(t):
    L, n = t.shape[1], t.shape[-1]
    quarter = n // 4
    pos = jnp.arange(L)
    inv = ROPE_BASE ** (-jnp.arange(quarter, dtype=jnp.float32) / quarter)

    def rot(u, p):
        ang = p.astype(jnp.float32)[:, None] * inv
        cos = jnp.cos(ang)[None, :, None, :].astype(u.dtype)
        sin = jnp.sin(ang)[None, :, None, :].astype(u.dtype)
        u1, u2 = u[..., :quarter], u[..., quarter:]
        return jnp.concatenate([u1 * cos - u2 * sin, u1 * sin + u2 * cos], axis=-1)

    return jnp.concatenate([rot(t[..., :2 * quarter], pos // GRID_W),
                            rot(t[..., 2 * quarter:], pos % GRID_W)], axis=-1)


def _dwconv(u, w, b):
    k = w.shape[0]
    y = lax.conv_general_dilated(u, w[:, None, :], window_strides=(1,), padding=[(k // 2, k // 2)],
                                 dimension_numbers=("NWC", "WIO", "NWC"), feature_group_count=u.shape[-1])
    return y + b


def _segsum(a):
    t = a.shape[-1]
    cs = jnp.cumsum(a, axis=-1)
    d = cs[..., :, None] - cs[..., None, :]
    return jnp.where(jnp.tril(jnp.ones((t, t), dtype=bool)), d, -jnp.inf)


def _ssd_chunked(xdt, a, bm, cm, init):
    b, L, g, k, p = xdt.shape
    nc = L // SSD_CHUNK
    x = xdt.reshape(b, nc, SSD_CHUNK, g, k, p)
    A = a.astype(jnp.float32).reshape(b, nc, SSD_CHUNK, g, k).transpose(0, 3, 4, 1, 2)
    Bc = bm.reshape(b, nc, SSD_CHUNK, g, -1)
    Cc = cm.reshape(b, nc, SSD_CHUNK, g, -1)
    a_cs = jnp.cumsum(A, axis=-1)
    lmat = jnp.exp(_segsum(A))
    y_diag = jnp.einsum("bclgn,bcsgn,bgkcls,bcsgkp->bclgkp", Cc, Bc, lmat, x)
    decay_states = jnp.exp(a_cs[..., -1:] - a_cs)
    states = jnp.einsum("bclgn,bgkcl,bclgkp->bcgkpn", Bc, decay_states, x)
    states = jnp.concatenate([init[:, None].astype(states.dtype), states], axis=1)
    chunk_decay = jnp.exp(_segsum(jnp.pad(a_cs[..., -1], ((0, 0), (0, 0), (0, 0), (1, 0)))))
    states = jnp.einsum("bgkzc,bcgkpn->bzgkpn", chunk_decay, states)
    y_off = jnp.einsum("bclgn,bcgkpn,bgkcl->bclgkp", Cc, states[:, :-1], jnp.exp(a_cs))
    y = (y_diag + y_off).reshape(b, L, g, k, p)
    return y, states[:, -1]


def _ssd_mixer(xbc_c, dt_c, xbc_l, dt_l, conv_w, conv_b, dt_bias, a_log, d_skip):
    out_dtype = xbc_l.dtype

    def prep(xbc, dt_raw, rope):
        u = jax.nn.silu(_dwconv(xbc, conv_w, conv_b))
        xs, bm, cm = jnp.split(u, [SSD_WIDTH, SSD_WIDTH + SSD_GROUPS * SSD_STATE], axis=-1)
        b, L = xs.shape[:2]
        xs = xs.reshape(b, L, SSD_GROUPS, SSD_HPG, SSD_HEAD_DIM)
        bm = bm.reshape(b, L, SSD_GROUPS, SSD_STATE)
        cm = cm.reshape(b, L, SSD_GROUPS, SSD_STATE)
        if rope:
            bm, cm = _axial_rope(bm), _axial_rope(cm)
        dt = jax.nn.softplus(dt_raw.astype(jnp.float32).reshape(b, L, 2, SSD_GROUPS, SSD_HPG)
                             + dt_bias.astype(jnp.float32).reshape(2, SSD_GROUPS, SSD_HPG))
        return xs, bm, cm, dt

    xc, bc, cc, dtc = prep(xbc_c, dt_c, False)
    xl, bl, cl, dtl = prep(xbc_l, dt_l, True)
    a = -jnp.exp(a_log.astype(jnp.float32)).reshape(2, SSD_GROUPS, SSD_HPG)
    dsk = d_skip.astype(jnp.float32).reshape(SSD_GROUPS, SSD_HPG, 1)
    b = xl.shape[0]
    init = jnp.zeros((b, SSD_GROUPS, SSD_HPG, SSD_HEAD_DIM, SSD_STATE), jnp.float32)
    yc = dsk * xc.astype(jnp.float32)
    yl = dsk * xl.astype(jnp.float32)
    for direction in range(2):
        flip = (lambda t: t[:, ::-1]) if direction == 1 else (lambda t: t)

        def run(xs, bm, cm, dt, s0):
            dtd = dt[:, :, direction]
            y, s = _ssd_chunked(flip(xs * dtd[..., None]), flip(dtd * a[direction]), flip(bm), flip(cm), s0)
            return flip(y), s

        y_c, s_c = run(xc, bc, cc, dtc, init)
        y_l, _ = run(xl, bl, cl, dtl, s_c)
        yc = yc + y_c
        yl = yl + y_l
    return (yc.reshape(yc.shape[0], yc.shape[1], SSD_WIDTH).astype(out_dtype),
            yl.reshape(b, yl.shape[1], SSD_WIDTH).astype(out_dtype))


def _na_latent(q, k, v, k_ctx, v_ctx, rpb):
    b, L, h, d = q.shape
    rows = L // GRID_W
    win_r = min(NA_WIN_R, rows)
    nblk = GRID_W // NA_QBLK
    scale = d ** -0.5
    qcol = np.arange(GRID_W).reshape(nblk, NA_QBLK)
    qstart = np.clip(qcol - NA_WIN_C // 2, 0, GRID_W - NA_WIN_C)
    kstart = np.clip(qcol[:, 0] - NA_WIN_C // 2, 0, GRID_W - NA_KBLK)
    kcol = (kstart[:, None] + np.arange(NA_KBLK)[None]).astype(np.int32)
    col_mask = ((kcol[:, None, :] >= qstart[..., None]) &
                (kcol[:, None, :] < qstart[..., None] + NA_WIN_C))
    dc = np.clip(kcol[:, None, :] - qcol[..., None] + NA_WIN_C - 1, 0, 2 * NA_WIN_C - 2).astype(np.int32)
    kg = k.reshape(b, rows, GRID_W, h, d)
    vg = v.reshape(b, rows, GRID_W, h, d)
    qg = q.reshape(b, rows, nblk, NA_QBLK, h, d).transpose(1, 0, 2, 3, 4, 5)
    rpb32 = rpb.astype(jnp.float32)
    nwin = win_r * NA_KBLK

    def row_block(args):
        r, q_r = args
        start = jnp.clip(r - win_r // 2, 0, rows - win_r)
        k_r = lax.dynamic_slice_in_dim(kg, start, win_r, axis=1)[:, :, kcol]
        v_r = lax.dynamic_slice_in_dim(vg, start, win_r, axis=1)[:, :, kcol]
        dr = start + jnp.arange(win_r) - r + NA_WIN_R - 1
        bias = rpb32[:, dr][:, :, dc].transpose(0, 2, 3, 1, 4)
        s_w = jnp.einsum("bjqhd,bwjkhd->bhjqwk", q_r, k_r).astype(jnp.float32) * scale + bias[None]
        s_w = jnp.where(col_mask[:, :, None, :], s_w, -1e30)
        s_c = jnp.einsum("bjqhd,bchd->bhjqc", q_r, k_ctx).astype(jnp.float32) * scale
        s = jnp.concatenate([s_w.reshape(b, h, nblk, NA_QBLK, nwin), s_c], axis=-1)
        p = jax.nn.softmax(s, axis=-1).astype(v.dtype)
        p_w = p[..., :nwin].reshape(b, h, nblk, NA_QBLK, win_r, NA_KBLK)
        return (jnp.einsum("bhjqwk,bwjkhd->bjqhd", p_w, v_r)
                + jnp.einsum("bhjqc,bchd->bjqhd", p[..., nwin:], v_ctx))

    o = lax.map(row_block, (jnp.arange(rows), qg))
    return o.transpose(1, 0, 2, 3, 4, 5).reshape(b, L, h * d)


def _ctx_attention(q, k, v):
    b, n, h, d = q.shape
    s = jnp.einsum("bqhd,bkhd->bhqk", q, k).astype(jnp.float32) * (d ** -0.5)
    p = jax.nn.softmax(s, axis=-1).astype(v.dtype)
    return jnp.einsum("bhqk,bkhd->bqhd", p, v).reshape(b, n, h * d)


def _fourier_real(u):
    b, L, _ = u.shape
    ug = u.astype(jnp.float32).reshape(b, L, FOUR_GROUPS, FOUR_GROUP_DIM)
    f = jnp.fft.fft2(ug, axes=(1, 3), norm="ortho").real
    return f.reshape(b, L, FOUR_WIDTH).astype(u.dtype)


def _heads(t):
    return t.reshape(t.shape[0], t.shape[1], NA_HEADS, NA_HEAD_DIM)


def _split_main(p):
    offs = np.cumsum(MAIN_SPLITS)[:-1].tolist()
    return jnp.split(p, offs, axis=-1)


def _mm3(a, w, tm=1024, tn=1024):
    b, L, k = a.shape
    y = matmul(a.reshape(b * L, k), w, tm=min(tm, b * L), tn=tn)
    return y.reshape(b, L, -1)


def _layer(xc, xl, c, c_ctx, w_ada, b_ada, norm_w, w_in, rpb, four_w, conv_w, conv_b, dt_bias, a_log, d_skip,
           ssd_norm_w, wb_na, wb_four, wb_ssd, w_out, update_ctx):
    bsz, seq, d = xl.shape
    ctx_len = xc.shape[1]
    bf = jnp.bfloat16

    cc = jnp.concatenate([c, c_ctx[None], jnp.zeros((16 - bsz - 1, d), c.dtype)], axis=0)
    mod = matmul(jax.nn.silu(cc), w_ada.astype(bf), tm=16, tn=1024) + b_ada
    sh, sc, gt = jnp.split(mod, 3, axis=-1)
    sh_l, sc_l, g_l = sh[:bsz], sc[:bsz], gt[:bsz]
    sh_c, sc_c, g_c = sh[bsz:bsz + 1], sc[bsz:bsz + 1], gt[bsz]

    w_main = jnp.concatenate([w_in[:, :DT_OFFSET], w_in[:, DT_OFFSET + DT_WIDTH:]], axis=1).astype(bf)
    w_dt = jnp.pad(w_in[:, DT_OFFSET:DT_OFFSET + DT_WIDTH], ((0, 0), (0, DT_PAD - DT_WIDTH))).astype(bf)

    pl_, dt_l = in_projection(xl.reshape(bsz * seq, d), norm_w, sc_l[:, None], sh_l[:, None], w_main, w_dt,
                              rows_per_mod=seq, tm=1024, tn=1536, out_dtype=jnp.float32)
    pc_, dt_c = in_projection(xc.reshape(bsz * ctx_len, d), norm_w, sc_c[:, None], sh_c[:, None], w_main, w_dt,
                              rows_per_mod=bsz * ctx_len, tm=1024, tn=1536, out_dtype=jnp.float32)
    pl_ = pl_.reshape(bsz, seq, MAIN_WIDTH)
    pc_ = pc_.reshape(bsz, ctx_len, MAIN_WIDTH)
    dt_l = dt_l.reshape(bsz, seq, DT_PAD)[..., :DT_WIDTH]
    dt_c = dt_c.reshape(bsz, ctx_len, DT_PAD)[..., :DT_WIDTH]
    q_l, k_l, v_l, zna_l, uf_l, zf_l, xbc_l, zs_l, gna_l, gf_l, gs_l = _split_main(pl_)
    q_c, k_c, v_c, zna_c, uf_c, zf_c, xbc_c, zs_c, gna_c, gf_c, gs_c = _split_main(pc_)

    four_wb, wb_nab, wb_fourb, wb_ssdb, w_outb = (t.astype(bf) for t in (four_w, wb_na, wb_four, wb_ssd, w_out))

    def merge(a, za, f, zf, s, zs, ga, gf, gs):
        o_a = _mm3(a * jax.nn.silu(za), wb_nab)
        o_f = _mm3(f * jax.nn.silu(zf), wb_fourb)
        o_s = _mm3(_rmsnorm(s * jax.nn.silu(zs), ssd_norm_w), wb_ssdb)
        return _mm3(jax.nn.sigmoid(ga) * o_a + jax.nn.sigmoid(gf) * o_f + jax.nn.sigmoid(gs) * o_s, w_outb)

    kc_h, vc_h = _heads(k_c), _heads(v_c)
    a_l = _na_latent(_heads(q_l), _heads(k_l), _heads(v_l), kc_h, vc_h, rpb)
    f_l = _mm3(_fourier_real(uf_l), four_wb)
    s_c, s_l = _ssd_mixer(xbc_c, dt_c, xbc_l, dt_l, conv_w, conv_b, dt_bias, a_log, d_skip)
    xl_new = xl + g_l[:, None, :] * merge(a_l, zna_l, f_l, zf_l, s_l, zs_l, gna_l, gf_l, gs_l)
    if update_ctx:
        a_c = _ctx_attention(_heads(q_c), kc_h, vc_h)
        f_c = _mm3(_fourier_real(uf_c), four_wb)
        xc = xc + g_c * merge(a_c, zna_c, f_c, zf_c, s_c, zs_c, gna_c, gf_c, gs_c)
    return xc, xl_new


def kernel(x, c, ctx, c_ctx, w_ada, b_ada, norm_w, w_in, na_rpb, four_w, ssd_conv_w, ssd_conv_b, ssd_dt_bias,
           ssd_a_log, ssd_d, ssd_norm_w, wb_na, wb_four, wb_ssd, w_out, final_norm_w):
    xc, xl = ctx, x
    for l in range(DEPTH):
        xc, xl = _layer(xc, xl, c, c_ctx, w_ada[l], b_ada[l], norm_w[l], w_in[l], na_rpb[l], four_w[l],
                        ssd_conv_w[l], ssd_conv_b[l], ssd_dt_bias[l], ssd_a_log[l], ssd_d[l], ssd_norm_w[l],
                        wb_na[l], wb_four[l], wb_ssd[l], w_out[l], update_ctx=(l < DEPTH - 1))
    return _rmsnorm(xl, final_norm_w)
```

```python
import math
from functools import partial

import jax
import jax.numpy as jnp
import numpy as np
from jax import lax
from jax.experimental import pallas as pl
from jax.experimental.pallas import tpu as pltpu

D_MODEL = 2048
DEPTH = 2
GRID_W = 64
EPS = 1e-6
NA_HEADS = 8
NA_HEAD_DIM = 128
NA_WIDTH = NA_HEADS * NA_HEAD_DIM
NA_WIN_R = 8
NA_WIN_C = 16
NA_QBLK = 16
NA_KBLK = 32
FOUR_GROUPS = 4
FOUR_GROUP_DIM = 256
FOUR_WIDTH = FOUR_GROUPS * FOUR_GROUP_DIM
SSD_HEADS = 16
SSD_HEAD_DIM = 64
SSD_WIDTH = SSD_HEADS * SSD_HEAD_DIM
SSD_GROUPS = 4
SSD_HPG = SSD_HEADS // SSD_GROUPS
SSD_STATE = 128
SSD_CONV = 7
SSD_CHUNK = 128
SSD_CONV_CH = SSD_WIDTH + 2 * SSD_GROUPS * SSD_STATE
ROPE_BASE = 10000.0
DT_WIDTH = 2 * SSD_HEADS
MAIN_SPLITS = (NA_WIDTH, NA_WIDTH, NA_WIDTH, NA_WIDTH, FOUR_WIDTH, FOUR_WIDTH, SSD_CONV_CH, SSD_WIDTH,
               D_MODEL, D_MODEL, D_MODEL)
MAIN_WIDTH = sum(MAIN_SPLITS)
DT_OFFSET = sum(MAIN_SPLITS[:8])
DT_PAD = 128

V7X_VMEM_LIMIT_BYTES = 56 * 1024 * 1024


def _rmsnorm(x, w):
    xf = x.astype(jnp.float32)
    y = xf * lax.rsqrt(jnp.mean(xf * xf, axis=-1, keepdims=True) + EPS)
    return (y * w.astype(jnp.float32)).astype(x.dtype)


def _mm_kernel(a_ref, b_ref, o_ref):
    a = a_ref[...].astype(jnp.bfloat16)
    o_ref[...] = jnp.dot(a, b_ref[...], preferred_element_type=jnp.float32).astype(o_ref.dtype)


def matmul(a, b, *, tm, tn, out_dtype=jnp.float32):
    m, k = a.shape
    _, n = b.shape
    assert m % tm == 0 and n % tn == 0, (a.shape, b.shape, tm, tn)
    return pl.pallas_call(
        _mm_kernel,
        grid=(m // tm, n // tn),
        in_specs=[pl.BlockSpec((tm, k), lambda i, j: (i, 0)),
                  pl.BlockSpec((k, tn), lambda i, j: (0, j))],
        out_specs=pl.BlockSpec((tm, tn), lambda i, j: (i, j)),
        out_shape=jax.ShapeDtypeStruct((m, n), out_dtype),
        compiler_params=pltpu.CompilerParams(dimension_semantics=("arbitrary", "arbitrary"),
                                             vmem_limit_bytes=V7X_VMEM_LIMIT_BYTES),
        name="matmul",
    )(a, b)


def _inproj_kernel(x_ref, nw_ref, sc_ref, sh_ref, w_ref, wdt_ref, o_ref, dt_ref, h_ref):
    @pl.when(pl.program_id(1) == 0)
    def _():
        x = x_ref[...]
        y = x * lax.rsqrt(jnp.mean(x * x, axis=-1, keepdims=True) + EPS) * nw_ref[...]
        h = (y * (1.0 + sc_ref[0]) + sh_ref[0]).astype(jnp.bfloat16)
        h_ref[...] = h
        dt_ref[...] = jnp.dot(h, wdt_ref[...], preferred_element_type=jnp.float32)

    o_ref[...] = jnp.dot(h_ref[...], w_ref[...], preferred_element_type=jnp.float32).astype(o_ref.dtype)


def in_projection(x2d, norm_w, scale, shift, w_main, w_dt, *, rows_per_mod, tm, tn, out_dtype):
    m, d = x2d.shape
    n = w_main.shape[1]
    assert m % tm == 0 and n % tn == 0 and rows_per_mod % tm == 0
    per = rows_per_mod // tm
    return pl.pallas_call(
        _inproj_kernel,
        grid=(m // tm, n // tn),
        in_specs=[pl.BlockSpec((tm, d), lambda i, j: (i, 0)),
                  pl.BlockSpec((1, d), lambda i, j: (0, 0)),
                  pl.BlockSpec((1, 1, d), lambda i, j: (i // per, 0, 0)),
                  pl.BlockSpec((1, 1, d), lambda i, j: (i // per, 0, 0)),
                  pl.BlockSpec((d, tn), lambda i, j: (0, j)),
                  pl.BlockSpec((d, DT_PAD), lambda i, j: (0, 0))],
        out_specs=[pl.BlockSpec((tm, tn), lambda i, j: (i, j)),
                   pl.BlockSpec((tm, DT_PAD), lambda i, j: (i, 0))],
        out_shape=[jax.ShapeDtypeStruct((m, n), out_dtype),
                   jax.ShapeDtypeStruct((m, DT_PAD), jnp.float32)],
        scratch_shapes=[pltpu.VMEM((tm, d), jnp.bfloat16)],
        compiler_params=pltpu.CompilerParams(dimension_semantics=("arbitrary", "arbitrary"),
                                             vmem_limit_bytes=V7X_VMEM_LIMIT_BYTES),
        name="in_projection",
    )(x2d, norm_w.reshape(1, d), scale, shift, w_main, w_dt)


GRID_ROWS = 32
NA_WIN_TOKENS = NA_WIN_R * GRID_W
_NT = (((1,), (1,)), ((), ()))


def _na_kernel(*refs, with_ctx):
    if with_ctx:
        q_ref, k_ref, v_ref, kc_ref, vc_ref, bias_ref, qc_ref, o_ref, oc_ref = refs
    else:
        q_ref, k_ref, v_ref, kc_ref, vc_ref, bias_ref, o_ref = refs
    bf = jnp.bfloat16
    scale = NA_HEAD_DIM ** -0.5
    kc = kc_ref[0].astype(bf)
    vc = vc_ref[0].astype(bf)

    def row(r, carry):
        start = jnp.clip(r - NA_WIN_R // 2, 0, GRID_ROWS - NA_WIN_R)
        q0 = pl.multiple_of(r * GRID_W, GRID_W)
        k0 = pl.multiple_of(start * GRID_W, GRID_W)
        q = q_ref[0, pl.ds(q0, GRID_W), :].astype(bf)
        kw = k_ref[0, pl.ds(k0, NA_WIN_TOKENS), :].astype(bf)
        vw = v_ref[0, pl.ds(k0, NA_WIN_TOKENS), :].astype(bf)
        s_w = lax.dot_general(q, kw, _NT, preferred_element_type=jnp.float32) * scale + bias_ref[0, r - start]
        s_c = lax.dot_general(q, kc, _NT, preferred_element_type=jnp.float32) * scale
        m = jnp.maximum(jnp.max(s_w, axis=-1, keepdims=True), jnp.max(s_c, axis=-1, keepdims=True))
        e_w = jnp.exp(s_w - m)
        e_c = jnp.exp(s_c - m)
        denom = jnp.sum(e_w, axis=-1, keepdims=True) + jnp.sum(e_c, axis=-1, keepdims=True)
        o = (jnp.dot(e_w.astype(bf), vw, preferred_element_type=jnp.float32)
             + jnp.dot(e_c.astype(bf), vc, preferred_element_type=jnp.float32))
        o_ref[0, pl.ds(q0, GRID_W), :] = (o / denom).astype(o_ref.dtype)
        return carry

    lax.fori_loop(0, GRID_ROWS, row, 0)

    if with_ctx:
        s = lax.dot_general(qc_ref[0].astype(bf), kc, _NT, preferred_element_type=jnp.float32) * scale
        e = jnp.exp(s - jnp.max(s, axis=-1, keepdims=True))
        o = jnp.dot(e.astype(bf), vc, preferred_element_type=jnp.float32)
        oc_ref[0] = (o / jnp.sum(e, axis=-1, keepdims=True)).astype(oc_ref.dtype)


def na_bias_table(rpb):
    delta = np.arange(NA_WIN_R)[:, None]
    w = np.arange(NA_WIN_R)[None, :]
    dr = w - delta + NA_WIN_R - 1
    qcol = np.arange(GRID_W)[:, None]
    kcol = np.arange(GRID_W)[None, :]
    qstart = np.clip(qcol - NA_WIN_C // 2, 0, GRID_W - NA_WIN_C)
    mask = (kcol >= qstart) & (kcol < qstart + NA_WIN_C)
    dc = np.clip(kcol - qcol + NA_WIN_C - 1, 0, 2 * NA_WIN_C - 2)
    t = rpb.astype(jnp.float32)[:, dr][:, :, :, dc]
    t = jnp.where(mask[None, None, None], t, -1e30)
    return t.transpose(0, 1, 3, 2, 4).reshape(rpb.shape[0], NA_WIN_R, GRID_W, NA_WIN_TOKENS)


def na_attention(p_lat, p_ctx, bias, *, with_ctx, out_dtype):
    bsz, seq, _ = p_lat.shape
    ctx_len = p_ctx.shape[1]
    assert seq == GRID_ROWS * GRID_W
    d, h = NA_HEAD_DIM, NA_HEADS
    in_specs = [pl.BlockSpec((1, seq, d), lambda b, i: (b, 0, i)),
                pl.BlockSpec((1, seq, d), lambda b, i: (b, 0, h + i)),
                pl.BlockSpec((1, seq, d), lambda b, i: (b, 0, 2 * h + i)),
                pl.BlockSpec((1, ctx_len, d), lambda b, i: (b, 0, h + i)),
                pl.BlockSpec((1, ctx_len, d), lambda b, i: (b, 0, 2 * h + i)),
                pl.BlockSpec((1, NA_WIN_R, GRID_W, NA_WIN_TOKENS), lambda b, i: (i, 0, 0, 0))]
    args = [p_lat, p_lat, p_lat, p_ctx, p_ctx, bias]
    out_specs = [pl.BlockSpec((1, seq, d), lambda b, i: (b, 0, i))]
    out_shape = [jax.ShapeDtypeStruct((bsz, seq, h * d), out_dtype)]
    if with_ctx:
        in_specs.append(pl.BlockSpec((1, ctx_len, d), lambda b, i: (b, 0, i)))
        args.append(p_ctx)
        out_specs.append(pl.BlockSpec((1, ctx_len, d), lambda b, i: (b, 0, i)))
        out_shape.append(jax.ShapeDtypeStruct((bsz, ctx_len, h * d), out_dtype))
    outs = pl.pallas_call(
        partial(_na_kernel, with_ctx=with_ctx),
        grid=(bsz, h),
        in_specs=in_specs, out_specs=out_specs, out_shape=out_shape,
        compiler_params=pltpu.CompilerParams(dimension_semantics=("arbitrary", "arbitrary"),
                                             vmem_limit_bytes=V7X_VMEM_LIMIT_BYTES),
        name="na_attention",
    )(*args)
    return (outs[0], outs[1]) if with_ctx else (outs[0], None)


def _axial_rope(t):
    L, n = t.shape[1], t.shape[-1]
    quarter = n // 4
    pos = jnp.arange(L)
    inv = ROPE_BASE ** (-jnp.arange(quarter, dtype=jnp.float32) / quarter)

    def rot(u, p):
        ang = p.astype(jnp.float32)[:, None] * inv
        cos = jnp.cos(ang)[None, :, None, :].astype(u.dtype)
        sin = jnp.sin(ang)[None, :, None, :].astype(u.dtype)
        u1, u2 = u[..., :quarter], u[..., quarter:]
        return jnp.concatenate([u1 * cos - u2 * sin, u1 * sin + u2 * cos], axis=-1)

    return jnp.concatenate([rot(t[..., :2 * quarter], pos // GRID_W),
                            rot(t[..., 2 * quarter:], pos % GRID_W)], axis=-1)


def _dwconv(u, w, b):
    k = w.shape[0]
    y = lax.conv_general_dilated(u, w[:, None, :], window_strides=(1,), padding=[(k // 2, k // 2)],
                                 dimension_numbers=("NWC", "WIO", "NWC"), feature_group_count=u.shape[-1])
    return y + b


def _segsum(a):
    t = a.shape[-1]
    cs = jnp.cumsum(a, axis=-1)
    d = cs[..., :, None] - cs[..., None, :]
    return jnp.where(jnp.tril(jnp.ones((t, t), dtype=bool)), d, -jnp.inf)


def _ssd_chunked(xdt, a, bm, cm, init):
    b, L, g, k, p = xdt.shape
    nc = L // SSD_CHUNK
    x = xdt.reshape(b, nc, SSD_CHUNK, g, k, p)
    A = a.astype(jnp.float32).reshape(b, nc, SSD_CHUNK, g, k).transpose(0, 3, 4, 1, 2)
    Bc = bm.reshape(b, nc, SSD_CHUNK, g, -1)
    Cc = cm.reshape(b, nc, SSD_CHUNK, g, -1)
    a_cs = jnp.cumsum(A, axis=-1)
    lmat = jnp.exp(_segsum(A))
    y_diag = jnp.einsum("bclgn,bcsgn,bgkcls,bcsgkp->bclgkp", Cc, Bc, lmat, x)
    decay_states = jnp.exp(a_cs[..., -1:] - a_cs)
    states = jnp.einsum("bclgn,bgkcl,bclgkp->bcgkpn", Bc, decay_states, x)
    states = jnp.concatenate([init[:, None].astype(states.dtype), states], axis=1)
    chunk_decay = jnp.exp(_segsum(jnp.pad(a_cs[..., -1], ((0, 0), (0, 0), (0, 0), (1, 0)))))
    states = jnp.einsum("bgkzc,bcgkpn->bzgkpn", chunk_decay, states)
    y_off = jnp.einsum("bclgn,bcgkpn,bgkcl->bclgkp", Cc, states[:, :-1], jnp.exp(a_cs))
    y = (y_diag + y_off).reshape(b, L, g, k, p)
    return y, states[:, -1]


def _ssd_mixer(xbc_c, dt_c, xbc_l, dt_l, conv_w, conv_b, dt_bias, a_log, d_skip):
    out_dtype = xbc_l.dtype

    def prep(xbc, dt_raw, rope):
        u = jax.nn.silu(_dwconv(xbc, conv_w, conv_b))
        xs, bm, cm = jnp.split(u, [SSD_WIDTH, SSD_WIDTH + SSD_GROUPS * SSD_STATE], axis=-1)
        b, L = xs.shape[:2]
        xs = xs.reshape(b, L, SSD_GROUPS, SSD_HPG, SSD_HEAD_DIM)
        bm = bm.reshape(b, L, SSD_GROUPS, SSD_STATE)
        cm = cm.reshape(b, L, SSD_GROUPS, SSD_STATE)
        if rope:
            bm, cm = _axial_rope(bm), _axial_rope(cm)
        dt = jax.nn.softplus(dt_raw.astype(jnp.float32).reshape(b, L, 2, SSD_GROUPS, SSD_HPG)
                             + dt_bias.astype(jnp.float32).reshape(2, SSD_GROUPS, SSD_HPG))
        return xs, bm, cm, dt

    xc, bc, cc, dtc = prep(xbc_c, dt_c, False)
    xl, bl, cl, dtl = prep(xbc_l, dt_l, True)
    a = -jnp.exp(a_log.astype(jnp.float32)).reshape(2, SSD_GROUPS, SSD_HPG)
    dsk = d_skip.astype(jnp.float32).reshape(SSD_GROUPS, SSD_HPG, 1)
    b = xl.shape[0]
    init = jnp.zeros((b, SSD_GROUPS, SSD_HPG, SSD_HEAD_DIM, SSD_STATE), jnp.float32)
    yc = dsk * xc.astype(jnp.float32)
    yl = dsk * xl.astype(jnp.float32)
    for direction in range(2):
        flip = (lambda t: t[:, ::-1]) if direction == 1 else (lambda t: t)

        def run(xs, bm, cm, dt, s0):
            dtd = dt[:, :, direction]
            y, s = _ssd_chunked(flip(xs * dtd[..., None]), flip(dtd * a[direction]), flip(bm), flip(cm), s0)
            return flip(y), s

        y_c, s_c = run(xc, bc, cc, dtc, init)
        y_l, _ = run(xl, bl, cl, dtl, s_c)
        yc = yc + y_c
        yl = yl + y_l
    return (yc.reshape(yc.shape[0], yc.shape[1], SSD_WIDTH).astype(out_dtype),
            yl.reshape(b, yl.shape[1], SSD_WIDTH).astype(out_dtype))


def _na_latent(q, k, v, k_ctx, v_ctx, rpb):
    b, L, h, d = q.shape
    rows = L // GRID_W
    win_r = min(NA_WIN_R, rows)
    nblk = GRID_W // NA_QBLK
    scale = d ** -0.5
    qcol = np.arange(GRID_W).reshape(nblk, NA_QBLK)
    qstart = np.clip(qcol - NA_WIN_C // 2, 0, GRID_W - NA_WIN_C)
    kstart = np.clip(qcol[:, 0] - NA_WIN_C // 2, 0, GRID_W - NA_KBLK)
    kcol = (kstart[:, None] + np.arange(NA_KBLK)[None]).astype(np.int32)
    col_mask = ((kcol[:, None, :] >= qstart[..., None]) &
                (kcol[:, None, :] < qstart[..., None] + NA_WIN_C))
    dc = np.clip(kcol[:, None, :] - qcol[..., None] + NA_WIN_C - 1, 0, 2 * NA_WIN_C - 2).astype(np.int32)
    kg = k.reshape(b, rows, GRID_W, h, d)
    vg = v.reshape(b, rows, GRID_W, h, d)
    qg = q.reshape(b, rows, nblk, NA_QBLK, h, d).transpose(1, 0, 2, 3, 4, 5)
    rpb32 = rpb.astype(jnp.float32)
    nwin = win_r * NA_KBLK

    def row_block(args):
        r, q_r = args
        start = jnp.clip(r - win_r // 2, 0, rows - win_r)
        k_r = lax.dynamic_slice_in_dim(kg, start, win_r, axis=1)[:, :, kcol]
        v_r = lax.dynamic_slice_in_dim(vg, start, win_r, axis=1)[:, :, kcol]
        dr = start + jnp.arange(win_r) - r + NA_WIN_R - 1
        bias = rpb32[:, dr][:, :, dc].transpose(0, 2, 3, 1, 4)
        s_w = jnp.einsum("bjqhd,bwjkhd->bhjqwk", q_r, k_r).astype(jnp.float32) * scale + bias[None]
        s_w = jnp.where(col_mask[:, :, None, :], s_w, -1e30)
        s_c = jnp.einsum("bjqhd,bchd->bhjqc", q_r, k_ctx).astype(jnp.float32) * scale
        s = jnp.concatenate([s_w.reshape(b, h, nblk, NA_QBLK, nwin), s_c], axis=-1)
        p = jax.nn.softmax(s, axis=-1).astype(v.dtype)
        p_w = p[..., :nwin].reshape(b, h, nblk, NA_QBLK, win_r, NA_KBLK)
        return (jnp.einsum("bhjqwk,bwjkhd->bjqhd", p_w, v_r)
                + jnp.einsum("bhjqc,bchd->bjqhd", p[..., nwin:], v_ctx))

    o = lax.map(row_block, (jnp.arange(rows), qg))
    return o.transpose(1, 0, 2, 3, 4, 5).reshape(b, L, h * d)


def _ctx_attention(q, k, v):
    b, n, h, d = q.shape
    s = jnp.einsum("bqhd,bkhd->bhqk", q, k).astype(jnp.float32) * (d ** -0.5)
    p = jax.nn.softmax(s, axis=-1).astype(v.dtype)
    return jnp.einsum("bhqk,bkhd->bqhd", p, v).reshape(b, n, h * d)


def _fourier_real(u):
    b, L, _ = u.shape
    ug = u.astype(jnp.float32).reshape(b, L, FOUR_GROUPS, FOUR_GROUP_DIM)
    f = jnp.fft.fft2(ug, axes=(1, 3), norm="ortho").real
    return f.reshape(b, L, FOUR_WIDTH).astype(u.dtype)


def _heads(t):
    return t.reshape(t.shape[0], t.shape[1], NA_HEADS, NA_HEAD_DIM)


def _split_main(p):
    offs = np.cumsum(MAIN_SPLITS)[:-1].tolist()
    return jnp.split(p, offs, axis=-1)


def _mm3(a, w, tm=1024, tn=1024):
    b, L, k = a.shape
    y = matmul(a.reshape(b * L, k), w, tm=min(tm, b * L), tn=tn)
    return y.reshape(b, L, -1)


def _layer(xc, xl, c, c_ctx, w_ada, b_ada, norm_w, w_in, rpb, four_w, conv_w, conv_b, dt_bias, a_log, d_skip,
           ssd_norm_w, wb_na, wb_four, wb_ssd, w_out, update_ctx):
    bsz, seq, d = xl.shape
    ctx_len = xc.shape[1]
    bf = jnp.bfloat16

    cc = jnp.concatenate([c, c_ctx[None], jnp.zeros((16 - bsz - 1, d), c.dtype)], axis=0)
    mod = matmul(jax.nn.silu(cc), w_ada.astype(bf), tm=16, tn=1024) + b_ada
    sh, sc, gt = jnp.split(mod, 3, axis=-1)
    sh_l, sc_l, g_l = sh[:bsz], sc[:bsz], gt[:bsz]
    sh_c, sc_c, g_c = sh[bsz:bsz + 1], sc[bsz:bsz + 1], gt[bsz]

    w_main = jnp.concatenate([w_in[:, :DT_OFFSET], w_in[:, DT_OFFSET + DT_WIDTH:]], axis=1).astype(bf)
    w_dt = jnp.pad(w_in[:, DT_OFFSET:DT_OFFSET + DT_WIDTH], ((0, 0), (0, DT_PAD - DT_WIDTH))).astype(bf)

    pl_, dt_l = in_projection(xl.reshape(bsz * seq, d), norm_w, sc_l[:, None], sh_l[:, None], w_main, w_dt,
                              rows_per_mod=seq, tm=1024, tn=1536, out_dtype=jnp.float32)
    pc_, dt_c = in_projection(xc.reshape(bsz * ctx_len, d), norm_w, sc_c[:, None], sh_c[:, None], w_main, w_dt,
                              rows_per_mod=bsz * ctx_len, tm=1024, tn=1536, out_dtype=jnp.float32)
    pl_ = pl_.reshape(bsz, seq, MAIN_WIDTH)
    pc_ = pc_.reshape(bsz, ctx_len, MAIN_WIDTH)
    dt_l = dt_l.reshape(bsz, seq, DT_PAD)[..., :DT_WIDTH]
    dt_c = dt_c.reshape(bsz, ctx_len, DT_PAD)[..., :DT_WIDTH]
    q_l, k_l, v_l, zna_l, uf_l, zf_l, xbc_l, zs_l, gna_l, gf_l, gs_l = _split_main(pl_)
    q_c, k_c, v_c, zna_c, uf_c, zf_c, xbc_c, zs_c, gna_c, gf_c, gs_c = _split_main(pc_)

    four_wb, wb_nab, wb_fourb, wb_ssdb, w_outb = (t.astype(bf) for t in (four_w, wb_na, wb_four, wb_ssd, w_out))

    def merge(a, za, f, zf, s, zs, ga, gf, gs):
        o_a = _mm3(a * jax.nn.silu(za), wb_nab)
        o_f = _mm3(f * jax.nn.silu(zf), wb_fourb)
        o_s = _mm3(_rmsnorm(s * jax.nn.silu(zs), ssd_norm_w), wb_ssdb)
        return _mm3(jax.nn.sigmoid(ga) * o_a + jax.nn.sigmoid(gf) * o_f + jax.nn.sigmoid(gs) * o_s, w_outb)

    a_l, a_c = na_attention(pl_, pc_, na_bias_table(rpb), with_ctx=update_ctx, out_dtype=jnp.float32)
    f_l = _mm3(_fourier_real(uf_l), four_wb)
    s_c, s_l = _ssd_mixer(xbc_c, dt_c, xbc_l, dt_l, conv_w, conv_b, dt_bias, a_log, d_skip)
    xl_new = xl + g_l[:, None, :] * merge(a_l, zna_l, f_l, zf_l, s_l, zs_l, gna_l, gf_l, gs_l)
    if update_ctx:
        f_c =_mm3(_fourier_real(uf_c), four_wb)
        xc = xc + g_c * merge(a_c, zna_c, f_c, zf_c, s_c, zs_c, gna_c, gf_c, gs_c)
    return xc, xl_new


def kernel(x, c, ctx, c_ctx, w_ada, b_ada, norm_w, w_in, na_rpb, four_w, ssd_conv_w, ssd_conv_b, ssd_dt_bias,
           ssd_a_log, ssd_d, ssd_norm_w, wb_na, wb_four, wb_ssd, w_out, final_norm_w):
    xc, xl = ctx, x
    for l in range(DEPTH):
        xc, xl = _layer(xc, xl, c, c_ctx, w_ada[l], b_ada[l], norm_w[l], w_in[l], na_rpb[l], four_w[l],
                        ssd_conv_w[l], ssd_conv_b[l], ssd_dt_bias[l], ssd_a_log[l], ssd_d[l], ssd_norm_w[l],
                        wb_na[l], wb_four[l], wb_ssd[l], w_out[l], update_ctx=(l < DEPTH - 1))
    return _rmsnorm(xl, final_norm_w)
```

```python
import math
from functools import partial

import jax
import jax.numpy as jnp
import numpy as np
from jax import lax
from jax.experimental import pallas as pl
from jax.experimental.pallas import tpu as pltpu

D_MODEL = 2048
DEPTH = 2
GRID_W = 64
EPS = 1e-6
NA_HEADS = 8
NA_HEAD_DIM = 128
NA_WIDTH = NA_HEADS * NA_HEAD_DIM
NA_WIN_R = 8
NA_WIN_C = 16
NA_QBLK = 16
NA_KBLK = 32
FOUR_GROUPS = 4
FOUR_GROUP_DIM = 256
FOUR_WIDTH = FOUR_GROUPS * FOUR_GROUP_DIM
SSD_HEADS = 16
SSD_HEAD_DIM = 64
SSD_WIDTH = SSD_HEADS * SSD_HEAD_DIM
SSD_GROUPS = 4
SSD_HPG = SSD_HEADS // SSD_GROUPS
SSD_STATE = 128
SSD_CONV = 7
SSD_CHUNK = 128
SSD_CONV_CH = SSD_WIDTH + 2 * SSD_GROUPS * SSD_STATE
ROPE_BASE = 10000.0
DT_WIDTH = 2 * SSD_HEADS
MAIN_SPLITS = (NA_WIDTH, NA_WIDTH, NA_WIDTH, NA_WIDTH, FOUR_WIDTH, FOUR_WIDTH, SSD_CONV_CH, SSD_WIDTH,
               D_MODEL, D_MODEL, D_MODEL)
MAIN_WIDTH = sum(MAIN_SPLITS)
DT_OFFSET = sum(MAIN_SPLITS[:8])
DT_PAD = 128

V7X_VMEM_LIMIT_BYTES = 56 * 1024 * 1024


def _rmsnorm(x, w):
    xf = x.astype(jnp.float32)
    y = xf * lax.rsqrt(jnp.mean(xf * xf, axis=-1, keepdims=True) + EPS)
    return (y * w.astype(jnp.float32)).astype(x.dtype)


def _mm_kernel(a_ref, b_ref, o_ref):
    a = a_ref[...].astype(jnp.bfloat16)
    o_ref[...] = jnp.dot(a, b_ref[...], preferred_element_type=jnp.float32).astype(o_ref.dtype)


def matmul(a, b, *, tm, tn, out_dtype=jnp.float32):
    m, k = a.shape
    _, n = b.shape
    assert m % tm == 0 and n % tn == 0, (a.shape, b.shape, tm, tn)
    return pl.pallas_call(
        _mm_kernel,
        grid=(m // tm, n // tn),
        in_specs=[pl.BlockSpec((tm, k), lambda i, j: (i, 0)),
                  pl.BlockSpec((k, tn), lambda i, j: (0, j))],
        out_specs=pl.BlockSpec((tm, tn), lambda i, j: (i, j)),
        out_shape=jax.ShapeDtypeStruct((m, n), out_dtype),
        compiler_params=pltpu.CompilerParams(dimension_semantics=("arbitrary", "arbitrary"),
                                             vmem_limit_bytes=V7X_VMEM_LIMIT_BYTES),
        name="matmul",
    )(a, b)


def _inproj_kernel(x_ref, nw_ref, sc_ref, sh_ref, w_ref, wdt_ref, o_ref, dt_ref, h_ref):
    @pl.when(pl.program_id(1) == 0)
    def _():
        x = x_ref[...]
        y = x * lax.rsqrt(jnp.mean(x * x, axis=-1, keepdims=True) + EPS) * nw_ref[...]
        h = (y * (1.0 + sc_ref[0]) + sh_ref[0]).astype(jnp.bfloat16)
        h_ref[...] = h
        dt_ref[...] = jnp.dot(h, wdt_ref[...], preferred_element_type=jnp.float32)

    o_ref[...] = jnp.dot(h_ref[...], w_ref[...], preferred_element_type=jnp.float32).astype(o_ref.dtype)


def in_projection(x2d, norm_w, scale, shift, w_main, w_dt, *, rows_per_mod, tm, tn, out_dtype):
    m, d = x2d.shape
    n = w_main.shape[1]
    assert m % tm == 0 and n % tn == 0 and rows_per_mod % tm == 0
    per = rows_per_mod // tm
    return pl.pallas_call(
        _inproj_kernel,
        grid=(m // tm, n // tn),
        in_specs=[pl.BlockSpec((tm, d), lambda i, j: (i, 0)),
                  pl.BlockSpec((1, d), lambda i, j: (0, 0)),
                  pl.BlockSpec((1, 1, d), lambda i, j: (i // per, 0, 0)),
                  pl.BlockSpec((1, 1, d), lambda i, j: (i // per, 0, 0)),
                  pl.BlockSpec((d, tn), lambda i, j: (0, j)),
                  pl.BlockSpec((d, DT_PAD), lambda i, j: (0, 0))],
        out_specs=[pl.BlockSpec((tm, tn), lambda i, j: (i, j)),
                   pl.BlockSpec((tm, DT_PAD), lambda i, j: (i, 0))],
        out_shape=[jax.ShapeDtypeStruct((m, n), out_dtype),
                   jax.ShapeDtypeStruct((m, DT_PAD), jnp.float32)],
        scratch_shapes=[pltpu.VMEM((tm, d), jnp.bfloat16)],
        compiler_params=pltpu.CompilerParams(dimension_semantics=("arbitrary", "arbitrary"),
                                             vmem_limit_bytes=V7X_VMEM_LIMIT_BYTES),
        name="in_projection",
    )(x2d, norm_w.reshape(1, d), scale, shift, w_main, w_dt)


GRID_ROWS = 32
NA_WIN_TOKENS = NA_WIN_R * GRID_W
_NT = (((1,), (1,)), ((), ()))


def _na_kernel(*refs, with_ctx):
    if with_ctx:
        q_ref, k_ref, v_ref, kc_ref, vc_ref, bias_ref, qc_ref, o_ref, oc_ref = refs
    else:
        q_ref, k_ref, v_ref, kc_ref, vc_ref, bias_ref, o_ref = refs
    bf = jnp.bfloat16
    scale = NA_HEAD_DIM ** -0.5
    kc = kc_ref[0].astype(bf)
    vc = vc_ref[0].astype(bf)

    def row(r, carry):
        start = jnp.clip(r - NA_WIN_R // 2, 0, GRID_ROWS - NA_WIN_R)
        q0 = pl.multiple_of(r * GRID_W, GRID_W)
        k0 = pl.multiple_of(start * GRID_W, GRID_W)
        q = q_ref[0, pl.ds(q0, GRID_W), :].astype(bf)
        kw = k_ref[0, pl.ds(k0, NA_WIN_TOKENS), :].astype(bf)
        vw = v_ref[0, pl.ds(k0, NA_WIN_TOKENS), :].astype(bf)
        s_w = lax.dot_general(q, kw, _NT, preferred_element_type=jnp.float32) * scale + bias_ref[0, r - start]
        s_c = lax.dot_general(q, kc, _NT, preferred_element_type=jnp.float32) * scale
        m = jnp.maximum(jnp.max(s_w, axis=-1, keepdims=True), jnp.max(s_c, axis=-1, keepdims=True))
        e_w = jnp.exp(s_w - m)
        e_c = jnp.exp(s_c - m)
        denom = jnp.sum(e_w, axis=-1, keepdims=True) + jnp.sum(e_c, axis=-1, keepdims=True)
        o = (jnp.dot(e_w.astype(bf), vw, preferred_element_type=jnp.float32)
             + jnp.dot(e_c.astype(bf), vc, preferred_element_type=jnp.float32))
        o_ref[0, pl.ds(q0, GRID_W), :] = (o / denom).astype(o_ref.dtype)
        return carry

    lax.fori_loop(0, GRID_ROWS, row, 0)

    if with_ctx:
        s = lax.dot_general(qc_ref[0].astype(bf), kc, _NT, preferred_element_type=jnp.float32) * scale
        e = jnp.exp(s - jnp.max(s, axis=-1, keepdims=True))
        o = jnp.dot(e.astype(bf), vc, preferred_element_type=jnp.float32)
        oc_ref[0] = (o / jnp.sum(e, axis=-1, keepdims=True)).astype(oc_ref.dtype)


def na_bias_table(rpb):
    delta = np.arange(NA_WIN_R)[:, None]
    w = np.arange(NA_WIN_R)[None, :]
    dr = w - delta + NA_WIN_R - 1
    qcol = np.arange(GRID_W)[:, None]
    kcol = np.arange(GRID_W)[None, :]
    qstart = np.clip(qcol - NA_WIN_C // 2, 0, GRID_W - NA_WIN_C)
    mask = (kcol >= qstart) & (kcol < qstart + NA_WIN_C)
    dc = np.clip(kcol - qcol + NA_WIN_C - 1, 0, 2 * NA_WIN_C - 2)
    t = rpb.astype(jnp.float32)[:, dr][:, :, :, dc]
    t = jnp.where(mask[None, None, None], t, -1e30)
    return t.transpose(0, 1, 3, 2, 4).reshape(rpb.shape[0], NA_WIN_R, GRID_W, NA_WIN_TOKENS)


def na_attention(p_lat, p_ctx, bias, *, with_ctx, out_dtype):
    bsz, seq, _ = p_lat.shape
    ctx_len = p_ctx.shape[1]
    assert seq == GRID_ROWS * GRID_W
    d, h = NA_HEAD_DIM, NA_HEADS
    in_specs = [pl.BlockSpec((1, seq, d), lambda b, i: (b, 0, i)),
                pl.BlockSpec((1, seq, d), lambda b, i: (b, 0, h + i)),
                pl.BlockSpec((1, seq, d), lambda b, i: (b, 0, 2 * h + i)),
                pl.BlockSpec((1, ctx_len, d), lambda b, i: (b, 0, h + i)),
                pl.BlockSpec((1, ctx_len, d), lambda b, i: (b, 0, 2 * h + i)),
                pl.BlockSpec((1, NA_WIN_R, GRID_W, NA_WIN_TOKENS), lambda b, i: (i, 0, 0, 0))]
    args = [p_lat, p_lat, p_lat, p_ctx, p_ctx, bias]
    out_specs = [pl.BlockSpec((1, seq, d), lambda b, i: (b, 0, i))]
    out_shape = [jax.ShapeDtypeStruct((bsz, seq, h * d), out_dtype)]
    if with_ctx:
        in_specs.append(pl.BlockSpec((1, ctx_len, d), lambda b, i: (b, 0, i)))
        args.append(p_ctx)
        out_specs.append(pl.BlockSpec((1, ctx_len, d), lambda b, i: (b, 0, i)))
        out_shape.append(jax.ShapeDtypeStruct((bsz, ctx_len, h * d), out_dtype))
    outs = pl.pallas_call(
        partial(_na_kernel, with_ctx=with_ctx),
        grid=(bsz, h),
        in_specs=in_specs, out_specs=out_specs, out_shape=out_shape,
        compiler_params=pltpu.CompilerParams(dimension_semantics=("arbitrary", "arbitrary"),
                                             vmem_limit_bytes=V7X_VMEM_LIMIT_BYTES),
        name="na_attention",
    )(*args)
    return (outs[0], outs[1]) if with_ctx else (outs[0], None)


def _dft_cos_sin(n):
    jk = np.outer(np.arange(n), np.arange(n)) % n
    ang = 2.0 * np.pi * jk / n
    return np.cos(ang) / np.sqrt(n), np.sin(ang) / np.sqrt(n)


def _fourier_kernel(u_ref, w1_ref, a_ref, fw_ref, zf_ref, o_ref, z_ref, *, seq, row_blk):
    bf = jnp.bfloat16
    gd = FOUR_GROUP_DIM

    @pl.when(pl.program_id(1) == 0)
    def _():
        for r0 in range(0, seq, row_blk):
            for g in range(FOUR_GROUPS):
                ug = u_ref[0, r0:r0 + row_blk, g * gd:(g + 1) * gd].astype(bf)
                z = jnp.dot(ug, w1_ref[...], preferred_element_type=jnp.float32)
                z_ref[r0:r0 + row_blk, g * gd:(g + 1) * gd] = z[:, :gd].astype(bf)
                z_ref[seq + r0:seq + r0 + row_blk, g * gd:(g + 1) * gd] = z[:, gd:].astype(bf)

    y = jnp.dot(a_ref[...], z_ref[...], preferred_element_type=jnp.float32)
    f = jnp.dot(y.astype(bf), fw_ref[...], preferred_element_type=jnp.float32)
    zf = zf_ref[0].astype(jnp.float32)
    o_ref[0] = (f * (zf * jax.nn.sigmoid(zf))).astype(o_ref.dtype)


def fourier_branch(p, four_w_bf, *, tm):
    bsz, seq, _ = p.shape
    assert seq % tm == 0
    cc, sc = _dft_cos_sin(FOUR_GROUP_DIM)
    cl, sl = _dft_cos_sin(seq)
    w1 = jnp.asarray(np.concatenate([cc, sc], axis=1), jnp.bfloat16)
    a = jnp.asarray(np.concatenate([cl, -sl], axis=1), jnp.bfloat16)
    fw = FOUR_WIDTH
    return pl.pallas_call(
        partial(_fourier_kernel, seq=seq, row_blk=min(seq, 512)),
        grid=(bsz, seq // tm),
        in_specs=[pl.BlockSpec((1, seq, fw), lambda b, i: (b, 0, 4)),
                  pl.BlockSpec((FOUR_GROUP_DIM, 2 * FOUR_GROUP_DIM), lambda b, i: (0, 0)),
                  pl.BlockSpec((tm, 2 * seq), lambda b, i: (i, 0)),
                  pl.BlockSpec((fw, fw), lambda b, i: (0, 0)),
                  pl.BlockSpec((1, tm, fw), lambda b, i: (b, i, 5))],
        out_specs=pl.BlockSpec((1, tm, fw), lambda b, i: (b, i, 0)),
        out_shape=jax.ShapeDtypeStruct((bsz, seq, fw), jnp.bfloat16),
        scratch_shapes=[pltpu.VMEM((2 * seq, fw), jnp.bfloat16)],
        compiler_params=pltpu.CompilerParams(dimension_semantics=("arbitrary", "arbitrary"),
                                             vmem_limit_bytes=V7X_VMEM_LIMIT_BYTES),
        name="fourier_branch",
    )(p, w1, a, four_w_bf, p)


def _merge_kernel(a_ref, zna_ref, gfo_ref, s_ref, zs_ref, nw_ref, wa_ref, wf_ref, ws_ref,
                  ga_ref, gf_ref, gs_ref, o_ref, ap_ref, sp_ref):
    bf = jnp.bfloat16

    @pl.when(pl.program_id(1) == 0)
    def _():
        zna = zna_ref[...].astype(jnp.float32)
        ap_ref[...] = (a_ref[...].astype(jnp.float32) * (zna * jax.nn.sigmoid(zna))).astype(bf)
        zs = zs_ref[...].astype(jnp.float32)
        t = s_ref[...].astype(jnp.float32) * (zs * jax.nn.sigmoid(zs))
        t = t * lax.rsqrt(jnp.mean(t * t, axis=-1, keepdims=True) + EPS) * nw_ref[...]
        sp_ref[...] = t.astype(bf)

    o_a = jnp.dot(ap_ref[...], wa_ref[...], preferred_element_type=jnp.float32)
    o_f = jnp.dot(gfo_ref[...], wf_ref[...], preferred_element_type=jnp.float32)
    o_s = jnp.dot(sp_ref[...], ws_ref[...], preferred_element_type=jnp.float32)
    m = (jax.nn.sigmoid(ga_ref[...].astype(jnp.float32)) * o_a
         + jax.nn.sigmoid(gf_ref[...].astype(jnp.float32)) * o_f
         + jax.nn.sigmoid(gs_ref[...].astype(jnp.float32)) * o_s)
    o_ref[...] = m.astype(o_ref.dtype)


def merge_branches(p2d, a2d, gfo2d, s2d, ssd_norm_w, wa, wf, ws, *, tm):
    m = p2d.shape[0]
    w, tn = NA_WIDTH, 1024
    assert m % tm == 0 and D_MODEL % tn == 0
    row = lambda i, j: (i, 0)
    wcol = lambda i, j: (0, j)
    return pl.pallas_call(
        _merge_kernel,
        grid=(m // tm, D_MODEL // tn),
        in_specs=[pl.BlockSpec((tm, w), row),
                  pl.BlockSpec((tm, w), lambda i, j: (i, 3)),
                  pl.BlockSpec((tm, w), row),
                  pl.BlockSpec((tm, w), row),
                  pl.BlockSpec((tm, w), lambda i, j: (i, 8)),
                  pl.BlockSpec((1, w), lambda i, j: (0, 0)),
                  pl.BlockSpec((w, tn), wcol), pl.BlockSpec((w, tn), wcol), pl.BlockSpec((w, tn), wcol),
                  pl.BlockSpec((tm, tn), lambda i, j: (i, 9 + j)),
                  pl.BlockSpec((tm, tn), lambda i, j: (i, 11 + j)),
                  pl.BlockSpec((tm, tn), lambda i, j: (i, 13 + j))],
        out_specs=pl.BlockSpec((tm, tn), lambda i, j: (i, j)),
        out_shape=jax.ShapeDtypeStruct((m, D_MODEL), jnp.bfloat16),
        scratch_shapes=[pltpu.VMEM((tm, w), jnp.bfloat16), pltpu.VMEM((tm, w), jnp.bfloat16)],
        compiler_params=pltpu.CompilerParams(dimension_semantics=("arbitrary", "arbitrary"),
                                             vmem_limit_bytes=V7X_VMEM_LIMIT_BYTES),
        name="merge_branches",
    )(a2d, p2d, gfo2d, s2d, p2d, ssd_norm_w.reshape(1, w), wa, wf, ws, p2d, p2d, p2d)


def _outproj_kernel(m_ref, w_ref, x_ref, g_ref, o_ref):
    y = jnp.dot(m_ref[...], w_ref[...], preferred_element_type=jnp.float32)
    o_ref[...] = x_ref[...] + g_ref[0] * y


def out_projection(m2d, w_out_bf, x2d, gate, *, rows_per_mod, tm, tn):
    m, d = x2d.shape
    assert m % tm == 0 and d % tn == 0 and rows_per_mod % tm == 0
    per = rows_per_mod // tm
    return pl.pallas_call(
        _outproj_kernel,
        grid=(m // tm, d // tn),
        in_specs=[pl.BlockSpec((tm, d), lambda i, j: (i, 0)),
                  pl.BlockSpec((d, tn), lambda i, j: (0, j)),
                  pl.BlockSpec((tm, tn), lambda i, j: (i, j)),
                  pl.BlockSpec((1, 1, tn), lambda i, j: (i // per, 0, j))],
        out_specs=pl.BlockSpec((tm, tn), lambda i, j: (i, j)),
        out_shape=jax.ShapeDtypeStruct((m, d), jnp.float32),
        compiler_params=pltpu.CompilerParams(dimension_semantics=("arbitrary", "arbitrary"),
                                             vmem_limit_bytes=V7X_VMEM_LIMIT_BYTES),
        name="out_projection",
    )(m2d, w_out_bf, x2d, gate)


def _axial_rope(t):
    L, n = t.shape[1], t.shape[-1]
    quarter = n // 4
    pos = jnp.arange(L)
    inv = ROPE_BASE ** (-jnp.arange(quarter, dtype=jnp.float32) / quarter)

    def rot(u, p):
        ang = p.astype(jnp.float32)[:, None] * inv
        cos = jnp.cos(ang)[None, :, None, :].astype(u.dtype)
        sin = jnp.sin(ang)[None, :, None, :].astype(u.dtype)
        u1, u2 = u[..., :quarter], u[..., quarter:]
        return jnp.concatenate([u1 * cos - u2 * sin, u1 * sin + u2 * cos], axis=-1)

    return jnp.concatenate([rot(t[..., :2 * quarter], pos // GRID_W),
                            rot(t[..., 2 * quarter:], pos % GRID_W)], axis=-1)


def _dwconv(u, w, b):
    k = w.shape[0]
    y = lax.conv_general_dilated(u, w[:, None, :], window_strides=(1,), padding=[(k // 2, k // 2)],
                                 dimension_numbers=("NWC", "WIO", "NWC"), feature_group_count=u.shape[-1])
    return y + b


def _segsum(a):
    t = a.shape[-1]
    cs = jnp.cumsum(a, axis=-1)
    d = cs[..., :, None] - cs[..., None, :]
    return jnp.where(jnp.tril(jnp.ones((t, t), dtype=bool)), d, -jnp.inf)


def _ssd_chunked(xdt, a, bm, cm, init):
    b, L, g, k, p = xdt.shape
    nc = L // SSD_CHUNK
    x = xdt.reshape(b, nc, SSD_CHUNK, g, k, p)
    A = a.astype(jnp.float32).reshape(b, nc, SSD_CHUNK, g, k).transpose(0, 3, 4, 1, 2)
    Bc = bm.reshape(b, nc, SSD_CHUNK, g, -1)
    Cc = cm.reshape(b, nc, SSD_CHUNK, g, -1)
    a_cs = jnp.cumsum(A, axis=-1)
    lmat = jnp.exp(_segsum(A))
    y_diag = jnp.einsum("bclgn,bcsgn,bgkcls,bcsgkp->bclgkp", Cc, Bc, lmat, x)
    decay_states = jnp.exp(a_cs[..., -1:] - a_cs)
    states = jnp.einsum("bclgn,bgkcl,bclgkp->bcgkpn", Bc, decay_states, x)
    states = jnp.concatenate([init[:, None].astype(states.dtype), states], axis=1)
    chunk_decay = jnp.exp(_segsum(jnp.pad(a_cs[..., -1], ((0, 0), (0, 0), (0, 0), (1, 0)))))
    states = jnp.einsum("bgkzc,bcgkpn->bzgkpn", chunk_decay, states)
    y_off = jnp.einsum("bclgn,bcgkpn,bgkcl->bclgkp", Cc, states[:, :-1], jnp.exp(a_cs))
    y = (y_diag + y_off).reshape(b, L, g, k, p)
    return y, states[:, -1]


def _ssd_mixer(xbc_c, dt_c, xbc_l, dt_l, conv_w, conv_b, dt_bias, a_log, d_skip):
    out_dtype = xbc_l.dtype

    def prep(xbc, dt_raw, rope):
        u = jax.nn.silu(_dwconv(xbc, conv_w, conv_b))
        xs, bm, cm = jnp.split(u, [SSD_WIDTH, SSD_WIDTH + SSD_GROUPS * SSD_STATE], axis=-1)
        b, L = xs.shape[:2]
        xs = xs.reshape(b, L, SSD_GROUPS, SSD_HPG, SSD_HEAD_DIM)
        bm = bm.reshape(b, L, SSD_GROUPS, SSD_STATE)
        cm = cm.reshape(b, L, SSD_GROUPS, SSD_STATE)
        if rope:
            bm, cm = _axial_rope(bm), _axial_rope(cm)
        dt = jax.nn.softplus(dt_raw.astype(jnp.float32).reshape(b, L, 2, SSD_GROUPS, SSD_HPG)
                             + dt_bias.astype(jnp.float32).reshape(2, SSD_GROUPS, SSD_HPG))
        return xs, bm, cm, dt

    xc, bc, cc, dtc = prep(xbc_c, dt_c, False)
    xl, bl, cl, dtl = prep(xbc_l, dt_l, True)
    a = -jnp.exp(a_log.astype(jnp.float32)).reshape(2, SSD_GROUPS, SSD_HPG)
    dsk = d_skip.astype(jnp.float32).reshape(SSD_GROUPS, SSD_HPG, 1)
    b = xl.shape[0]
    init = jnp.zeros((b, SSD_GROUPS, SSD_HPG, SSD_HEAD_DIM, SSD_STATE), jnp.float32)
    yc = dsk * xc.astype(jnp.float32)
    yl = dsk * xl.astype(jnp.float32)
    for direction in range(2):
        flip = (lambda t: t[:, ::-1]) if direction == 1 else (lambda t: t)

        def run(xs, bm, cm, dt, s0):
            dtd = dt[:, :, direction]
            y, s = _ssd_chunked(flip(xs * dtd[..., None]), flip(dtd * a[direction]), flip(bm), flip(cm), s0)
            return flip(y), s

        y_c, s_c = run(xc, bc, cc, dtc, init)
        y_l, _ = run(xl, bl, cl, dtl, s_c)
        yc = yc + y_c
        yl = yl + y_l
    return (yc.reshape(yc.shape[0], yc.shape[1], SSD_WIDTH).astype(out_dtype),
            yl.reshape(b, yl.shape[1], SSD_WIDTH).astype(out_dtype))


def _na_latent(q, k, v, k_ctx, v_ctx, rpb):
    b, L, h, d = q.shape
    rows = L // GRID_W
    win_r = min(NA_WIN_R, rows)
    nblk = GRID_W // NA_QBLK
    scale = d ** -0.5
    qcol = np.arange(GRID_W).reshape(nblk, NA_QBLK)
    qstart = np.clip(qcol - NA_WIN_C // 2, 0, GRID_W - NA_WIN_C)
    kstart = np.clip(qcol[:, 0] - NA_WIN_C // 2, 0, GRID_W - NA_KBLK)
    kcol = (kstart[:, None] + np.arange(NA_KBLK)[None]).astype(np.int32)
    col_mask = ((kcol[:, None, :] >= qstart[..., None]) &
                (kcol[:, None, :] < qstart[..., None] + NA_WIN_C))
    dc = np.clip(kcol[:, None, :] - qcol[..., None] + NA_WIN_C - 1, 0, 2 * NA_WIN_C - 2).astype(np.int32)
    kg = k.reshape(b, rows, GRID_W, h, d)
    vg = v.reshape(b, rows, GRID_W, h, d)
    qg = q.reshape(b, rows, nblk, NA_QBLK, h, d).transpose(1, 0, 2, 3, 4, 5)
    rpb32 = rpb.astype(jnp.float32)
    nwin = win_r * NA_KBLK

    def row_block(args):
        r, q_r = args
        start = jnp.clip(r - win_r // 2, 0, rows - win_r)
        k_r = lax.dynamic_slice_in_dim(kg, start, win_r, axis=1)[:, :, kcol]
        v_r = lax.dynamic_slice_in_dim(vg, start, win_r, axis=1)[:, :, kcol]
        dr = start + jnp.arange(win_r) - r + NA_WIN_R - 1
        bias = rpb32[:, dr][:, :, dc].transpose(0, 2, 3, 1, 4)
        s_w = jnp.einsum("bjqhd,bwjkhd->bhjqwk", q_r, k_r).astype(jnp.float32) * scale + bias[None]
        s_w = jnp.where(col_mask[:, :, None, :], s_w, -1e30)
        s_c = jnp.einsum("bjqhd,bchd->bhjqc", q_r, k_ctx).astype(jnp.float32) * scale
        s = jnp.concatenate([s_w.reshape(b, h, nblk, NA_QBLK, nwin), s_c], axis=-1)
        p = jax.nn.softmax(s, axis=-1).astype(v.dtype)
        p_w = p[..., :nwin].reshape(b, h, nblk, NA_QBLK, win_r, NA_KBLK)
        return (jnp.einsum("bhjqwk,bwjkhd->bjqhd", p_w, v_r)
                + jnp.einsum("bhjqc,bchd->bjqhd", p[..., nwin:], v_ctx))

    o = lax.map(row_block, (jnp.arange(rows), qg))
    return o.transpose(1, 0, 2, 3, 4, 5).reshape(b, L, h * d)


def _ctx_attention(q, k, v):
    b, n, h, d = q.shape
    s = jnp.einsum("bqhd,bkhd->bhqk", q, k).astype(jnp.float32) * (d ** -0.5)
    p = jax.nn.softmax(s, axis=-1).astype(v.dtype)
    return jnp.einsum("bhqk,bkhd->bqhd", p, v).reshape(b, n, h * d)


def _fourier_real(u):
    b, L, _ = u.shape
    ug = u.astype(jnp.float32).reshape(b, L, FOUR_GROUPS, FOUR_GROUP_DIM)
    f = jnp.fft.fft2(ug, axes=(1, 3), norm="ortho").real
    return f.reshape(b, L, FOUR_WIDTH).astype(u.dtype)


def _heads(t):
    return t.reshape(t.shape[0], t.shape[1], NA_HEADS, NA_HEAD_DIM)


def _split_main(p):
    offs = np.cumsum(MAIN_SPLITS)[:-1].tolist()
    return jnp.split(p, offs, axis=-1)


def _mm3(a, w, tm=1024, tn=1024):
    b, L, k = a.shape
    y = matmul(a.reshape(b * L, k), w, tm=min(tm, b * L), tn=tn)
    return y.reshape(b, L, -1)


def _layer(xc, xl, c, c_ctx, w_ada, b_ada, norm_w, w_in, rpb, four_w, conv_w, conv_b, dt_bias, a_log, d_skip,
           ssd_norm_w, wb_na, wb_four, wb_ssd, w_out, update_ctx):
    bsz, seq, d = xl.shape
    ctx_len = xc.shape[1]
    bf = jnp.bfloat16

    cc = jnp.concatenate([c, c_ctx[None], jnp.zeros((16 - bsz - 1, d), c.dtype)], axis=0)
    mod = matmul(jax.nn.silu(cc), w_ada.astype(bf), tm=16, tn=1024) + b_ada
    sh, sc, gt = jnp.split(mod, 3, axis=-1)
    sh_l, sc_l, g_l = sh[:bsz], sc[:bsz], gt[:bsz]
    sh_c, sc_c, g_c = sh[bsz:bsz + 1], sc[bsz:bsz + 1], gt[bsz]

    w_main = jnp.concatenate([w_in[:, :DT_OFFSET], w_in[:, DT_OFFSET + DT_WIDTH:]], axis=1).astype(bf)
    w_dt = jnp.pad(w_in[:, DT_OFFSET:DT_OFFSET + DT_WIDTH], ((0, 0), (0, DT_PAD - DT_WIDTH))).astype(bf)

    xl2d = xl.reshape(bsz * seq, d)
    xc2d = xc.reshape(bsz * ctx_len, d)
    pl2d, dt_l = in_projection(xl2d, norm_w, sc_l[:, None], sh_l[:, None], w_main, w_dt,
                               rows_per_mod=seq, tm=1024, tn=1536, out_dtype=jnp.float32)
    pc2d, dt_c = in_projection(xc2d, norm_w, sc_c[:, None], sh_c[:, None], w_main, w_dt,
                               rows_per_mod=bsz * ctx_len, tm=1024, tn=1536, out_dtype=jnp.float32)
    p_l = pl2d.reshape(bsz, seq, MAIN_WIDTH)
    p_c = pc2d.reshape(bsz, ctx_len, MAIN_WIDTH)
    dt_l = dt_l.reshape(bsz, seq, DT_PAD)[..., :DT_WIDTH]
    dt_c = dt_c.reshape(bsz, ctx_len, DT_PAD)[..., :DT_WIDTH]
    xbc0 = sum(MAIN_SPLITS[:6])
    xbc_l = p_l[..., xbc0:xbc0 + SSD_CONV_CH]
    xbc_c = p_c[..., xbc0:xbc0 + SSD_CONV_CH]

    four_wb, wb_nab, wb_fourb, wb_ssdb, w_outb = (t.astype(bf) for t in (four_w, wb_na, wb_four, wb_ssd, w_out))

    a_l, a_c = na_attention(p_l, p_c, na_bias_table(rpb), with_ctx=update_ctx, out_dtype=bf)
    gfo_l = fourier_branch(p_l, four_wb, tm=512)
    s_c, s_l = _ssd_mixer(xbc_c, dt_c, xbc_l, dt_l, conv_w, conv_b, dt_bias, a_log, d_skip)
    m_l = merge_branches(pl2d, a_l.reshape(bsz * seq, NA_WIDTH), gfo_l.reshape(bsz * seq, FOUR_WIDTH),
                         s_l.reshape(bsz * seq, SSD_WIDTH), ssd_norm_w, wb_nab, wb_fourb, wb_ssdb, tm=512)
    xl_new = out_projection(m_l, w_outb, xl2d, g_l[:, None], rows_per_mod=seq, tm=1024, tn=1024)
    xl_new = xl_new.reshape(bsz, seq, d)
    if update_ctx:
        gfo_c = fourier_branch(p_c, four_wb, tm=ctx_len)
        m_c = merge_branches(pc2d, a_c.reshape(bsz * ctx_len, NA_WIDTH), gfo_c.reshape(bsz * ctx_len, FOUR_WIDTH),
                             s_c.reshape(bsz * ctx_len, SSD_WIDTH), ssd_norm_w, wb_nab, wb_fourb, wb_ssdb, tm=512)
        xc = out_projection(m_c, w_outb, xc2d, g_c[None, None], rows_per_mod=bsz * ctx_len, tm=1024, tn=1024)
        xc = xc.reshape(bsz, ctx_len, d)
    return xc, xl_new


def kernel(x, c, ctx, c_ctx, w_ada, b_ada, norm_w, w_in, na_rpb, four_w, ssd_conv_w, ssd_conv_b, ssd_dt_bias,
           ssd_a_log, ssd_d, ssd_norm_w, wb_na, wb_four, wb_ssd, w_out, final_norm_w):
    xc, xl = ctx, x
    for l in range(DEPTH):
        xc, xl = _layer(xc, xl, c, c_ctx, w_ada[l], b_ada[l], norm_w[l], w_in[l], na_rpb[l], four_w[l],
                        ssd_conv_w[l], ssd_conv_b[l], ssd_dt_bias[l], ssd_a_log[l], ssd_d[l], ssd_norm_w[l],
                        wb_na[l], wb_four[l], wb_ssd[l], w_out[l], update_ctx=(l < DEPTH - 1))
    return _rmsnorm(xl, final_norm_w)
```

```python
import math
from functools import partial

import jax
import jax.numpy as jnp
import numpy as np
from jax import lax
from jax.experimental import pallas as pl
from jax.experimental.pallas import tpu as pltpu

D_MODEL = 2048
DEPTH = 2
GRID_W = 64
EPS = 1e-6
NA_HEADS = 8
NA_HEAD_DIM = 128
NA_WIDTH = NA_HEADS * NA_HEAD_DIM
NA_WIN_R = 8
NA_WIN_C = 16
NA_QBLK = 16
NA_KBLK = 32
FOUR_GROUPS = 4
FOUR_GROUP_DIM = 256
FOUR_WIDTH = FOUR_GROUPS * FOUR_GROUP_DIM
SSD_HEADS = 16
SSD_HEAD_DIM = 64
SSD_WIDTH = SSD_HEADS * SSD_HEAD_DIM
SSD_GROUPS = 4
SSD_HPG = SSD_HEADS // SSD_GROUPS
SSD_STATE = 128
SSD_CONV = 7
SSD_CHUNK = 128
SSD_CONV_CH = SSD_WIDTH + 2 * SSD_GROUPS * SSD_STATE
ROPE_BASE = 10000.0
DT_WIDTH = 2 * SSD_HEADS
MAIN_SPLITS = (NA_WIDTH, NA_WIDTH, NA_WIDTH, NA_WIDTH, FOUR_WIDTH, FOUR_WIDTH, SSD_CONV_CH, SSD_WIDTH,
               D_MODEL, D_MODEL, D_MODEL)
MAIN_WIDTH = sum(MAIN_SPLITS)
DT_OFFSET = sum(MAIN_SPLITS[:8])
DT_PAD = 128

V7X_VMEM_LIMIT_BYTES = 56 * 1024 * 1024


def _rmsnorm(x, w):
    xf = x.astype(jnp.float32)
    y = xf * lax.rsqrt(jnp.mean(xf * xf, axis=-1, keepdims=True) + EPS)
    return (y * w.astype(jnp.float32)).astype(x.dtype)


def _mm_kernel(a_ref, b_ref, o_ref):
    a = a_ref[...].astype(jnp.bfloat16)
    o_ref[...] = jnp.dot(a, b_ref[...], preferred_element_type=jnp.float32).astype(o_ref.dtype)


def matmul(a, b, *, tm, tn, out_dtype=jnp.float32):
    m, k = a.shape
    _, n = b.shape
    assert m % tm == 0 and n % tn == 0, (a.shape, b.shape, tm, tn)
    return pl.pallas_call(
        _mm_kernel,
        grid=(m // tm, n // tn),
        in_specs=[pl.BlockSpec((tm, k), lambda i, j: (i, 0)),
                  pl.BlockSpec((k, tn), lambda i, j: (0, j))],
        out_specs=pl.BlockSpec((tm, tn), lambda i, j: (i, j)),
        out_shape=jax.ShapeDtypeStruct((m, n), out_dtype),
        compiler_params=pltpu.CompilerParams(dimension_semantics=("arbitrary", "arbitrary"),
                                             vmem_limit_bytes=V7X_VMEM_LIMIT_BYTES),
        name="matmul",
    )(a, b)


def _inproj_kernel(x_ref, nw_ref, sc_ref, sh_ref, w_ref, wdt_ref, o_ref, dt_ref, h_ref):
    @pl.when(pl.program_id(1) == 0)
    def _():
        x = x_ref[...]
        y = x * lax.rsqrt(jnp.mean(x * x, axis=-1, keepdims=True) + EPS) * nw_ref[...]
        h = (y * (1.0 + sc_ref[0]) + sh_ref[0]).astype(jnp.bfloat16)
        h_ref[...] = h
        dt_ref[...] = jnp.dot(h, wdt_ref[...], preferred_element_type=jnp.float32)

    o_ref[...] = jnp.dot(h_ref[...], w_ref[...], preferred_element_type=jnp.float32).astype(o_ref.dtype)


def in_projection(x2d, norm_w, scale, shift, w_main, w_dt, *, rows_per_mod, tm, tn, out_dtype):
    m, d = x2d.shape
    n = w_main.shape[1]
    assert m % tm == 0 and n % tn == 0 and rows_per_mod % tm == 0
    per = rows_per_mod // tm
    return pl.pallas_call(
        _inproj_kernel,
        grid=(m // tm, n // tn),
        in_specs=[pl.BlockSpec((tm, d), lambda i, j: (i, 0)),
                  pl.BlockSpec((1, d), lambda i, j: (0, 0)),
                  pl.BlockSpec((1, 1, d), lambda i, j: (i // per, 0, 0)),
                  pl.BlockSpec((1, 1, d), lambda i, j: (i // per, 0, 0)),
                  pl.BlockSpec((d, tn), lambda i, j: (0, j)),
                  pl.BlockSpec((d, DT_PAD), lambda i, j: (0, 0))],
        out_specs=[pl.BlockSpec((tm, tn), lambda i, j: (i, j)),
                   pl.BlockSpec((tm, DT_PAD), lambda i, j: (i, 0))],
        out_shape=[jax.ShapeDtypeStruct((m, n), out_dtype),
                   jax.ShapeDtypeStruct((m, DT_PAD), jnp.float32)],
        scratch_shapes=[pltpu.VMEM((tm, d), jnp.bfloat16)],
        compiler_params=pltpu.CompilerParams(dimension_semantics=("arbitrary", "arbitrary"),
                                             vmem_limit_bytes=V7X_VMEM_LIMIT_BYTES),
        name="in_projection",
    )(x2d, norm_w.reshape(1, d), scale, shift, w_main, w_dt)


GRID_ROWS = 32
NA_WIN_TOKENS = NA_WIN_R * GRID_W
_NT = (((1,), (1,)), ((), ()))


def _na_kernel(*refs, with_ctx):
    if with_ctx:
        q_ref, k_ref, v_ref, kc_ref, vc_ref, bias_ref, qc_ref, o_ref, oc_ref = refs
    else:
        q_ref, k_ref, v_ref, kc_ref, vc_ref, bias_ref, o_ref = refs
    bf = jnp.bfloat16
    scale = NA_HEAD_DIM ** -0.5
    kc = kc_ref[0].astype(bf)
    vc = vc_ref[0].astype(bf)

    def row(r, carry):
        start = jnp.clip(r - NA_WIN_R // 2, 0, GRID_ROWS - NA_WIN_R)
        q0 = pl.multiple_of(r * GRID_W, GRID_W)
        k0 = pl.multiple_of(start * GRID_W, GRID_W)
        q = q_ref[0, pl.ds(q0, GRID_W), :].astype(bf)
        kw = k_ref[0, pl.ds(k0, NA_WIN_TOKENS), :].astype(bf)
        vw = v_ref[0, pl.ds(k0, NA_WIN_TOKENS), :].astype(bf)
        s_w = lax.dot_general(q, kw, _NT, preferred_element_type=jnp.float32) * scale + bias_ref[0, r - start]
        s_c = lax.dot_general(q, kc, _NT, preferred_element_type=jnp.float32) * scale
        m = jnp.maximum(jnp.max(s_w, axis=-1, keepdims=True), jnp.max(s_c, axis=-1, keepdims=True))
        e_w = jnp.exp(s_w - m)
        e_c = jnp.exp(s_c - m)
        denom = jnp.sum(e_w, axis=-1, keepdims=True) + jnp.sum(e_c, axis=-1, keepdims=True)
        o = (jnp.dot(e_w.astype(bf), vw, preferred_element_type=jnp.float32)
             + jnp.dot(e_c.astype(bf), vc, preferred_element_type=jnp.float32))
        o_ref[0, pl.ds(q0, GRID_W), :] = (o / denom).astype(o_ref.dtype)
        return carry

    lax.fori_loop(0, GRID_ROWS, row, 0)

    if with_ctx:
        s = lax.dot_general(qc_ref[0].astype(bf), kc, _NT, preferred_element_type=jnp.float32) * scale
        e = jnp.exp(s - jnp.max(s, axis=-1, keepdims=True))
        o = jnp.dot(e.astype(bf), vc, preferred_element_type=jnp.float32)
        oc_ref[0] = (o / jnp.sum(e, axis=-1, keepdims=True)).astype(oc_ref.dtype)


def na_bias_table(rpb):
    delta = np.arange(NA_WIN_R)[:, None]
    w = np.arange(NA_WIN_R)[None, :]
    dr = w - delta + NA_WIN_R - 1
    qcol = np.arange(GRID_W)[:, None]
    kcol = np.arange(GRID_W)[None, :]
    qstart = np.clip(qcol - NA_WIN_C // 2, 0, GRID_W - NA_WIN_C)
    mask = (kcol >= qstart) & (kcol < qstart + NA_WIN_C)
    dc = np.clip(kcol - qcol + NA_WIN_C - 1, 0, 2 * NA_WIN_C - 2)
    t = rpb.astype(jnp.float32)[:, dr][:, :, :, dc]
    t = jnp.where(mask[None, None, None], t, -1e30)
    return t.transpose(0, 1, 3, 2, 4).reshape(rpb.shape[0], NA_WIN_R, GRID_W, NA_WIN_TOKENS)


def na_attention(p_lat, p_ctx, bias, *, with_ctx, out_dtype):
    bsz, seq, _ = p_lat.shape
    ctx_len = p_ctx.shape[1]
    assert seq == GRID_ROWS * GRID_W
    d, h = NA_HEAD_DIM, NA_HEADS
    in_specs = [pl.BlockSpec((1, seq, d), lambda b, i: (b, 0, i)),
                pl.BlockSpec((1, seq, d), lambda b, i: (b, 0, h + i)),
                pl.BlockSpec((1, seq, d), lambda b, i: (b, 0, 2 * h + i)),
                pl.BlockSpec((1, ctx_len, d), lambda b, i: (b, 0, h + i)),
                pl.BlockSpec((1, ctx_len, d), lambda b, i: (b, 0, 2 * h + i)),
                pl.BlockSpec((1, NA_WIN_R, GRID_W, NA_WIN_TOKENS), lambda b, i: (i, 0, 0, 0))]
    args = [p_lat, p_lat, p_lat, p_ctx, p_ctx, bias]
    out_specs = [pl.BlockSpec((1, seq, d), lambda b, i: (b, 0, i))]
    out_shape = [jax.ShapeDtypeStruct((bsz, seq, h * d), out_dtype)]
    if with_ctx:
        in_specs.append(pl.BlockSpec((1, ctx_len, d), lambda b, i: (b, 0, i)))
        args.append(p_ctx)
        out_specs.append(pl.BlockSpec((1, ctx_len, d), lambda b, i: (b, 0, i)))
        out_shape.append(jax.ShapeDtypeStruct((bsz, ctx_len, h * d), out_dtype))
    outs = pl.pallas_call(
        partial(_na_kernel, with_ctx=with_ctx),
        grid=(bsz, h),
        in_specs=in_specs, out_specs=out_specs, out_shape=out_shape,
        compiler_params=pltpu.CompilerParams(dimension_semantics=("arbitrary", "arbitrary"),
                                             vmem_limit_bytes=V7X_VMEM_LIMIT_BYTES),
        name="na_attention",
    )(*args)
    return (outs[0], outs[1]) if with_ctx else (outs[0], None)


def _dft_cos_sin(n):
    jk = np.outer(np.arange(n), np.arange(n)) % n
    ang = 2.0 * np.pi * jk / n
    return np.cos(ang) / np.sqrt(n), np.sin(ang) / np.sqrt(n)


def _fourier_kernel(u_ref, w1_ref, a_ref, fw_ref, zf_ref, o_ref, z_ref, *, seq, row_blk):
    bf = jnp.bfloat16
    gd = FOUR_GROUP_DIM

    @pl.when(pl.program_id(1) == 0)
    def _():
        for r0 in range(0, seq, row_blk):
            for g in range(FOUR_GROUPS):
                ug = u_ref[0, r0:r0 + row_blk, g * gd:(g + 1) * gd].astype(bf)
                z = jnp.dot(ug, w1_ref[...], preferred_element_type=jnp.float32)
                z_ref[r0:r0 + row_blk, g * gd:(g + 1) * gd] = z[:, :gd].astype(bf)
                z_ref[seq + r0:seq + r0 + row_blk, g * gd:(g + 1) * gd] = z[:, gd:].astype(bf)

    y = jnp.dot(a_ref[...], z_ref[...], preferred_element_type=jnp.float32)
    f = jnp.dot(y.astype(bf), fw_ref[...], preferred_element_type=jnp.float32)
    zf = zf_ref[0].astype(jnp.float32)
    o_ref[0] = (f * (zf * jax.nn.sigmoid(zf))).astype(o_ref.dtype)


def fourier_branch(p, four_w_bf, *, tm):
    bsz, seq, _ = p.shape
    assert seq % tm == 0
    cc, sc = _dft_cos_sin(FOUR_GROUP_DIM)
    cl, sl = _dft_cos_sin(seq)
    w1 = jnp.asarray(np.concatenate([cc, sc], axis=1), jnp.bfloat16)
    a = jnp.asarray(np.concatenate([cl, -sl], axis=1), jnp.bfloat16)
    fw = FOUR_WIDTH
    return pl.pallas_call(
        partial(_fourier_kernel, seq=seq, row_blk=min(seq, 512)),
        grid=(bsz, seq // tm),
        in_specs=[pl.BlockSpec((1, seq, fw), lambda b, i: (b, 0, 4)),
                  pl.BlockSpec((FOUR_GROUP_DIM, 2 * FOUR_GROUP_DIM), lambda b, i: (0, 0)),
                  pl.BlockSpec((tm, 2 * seq), lambda b, i: (i, 0)),
                  pl.BlockSpec((fw, fw), lambda b, i: (0, 0)),
                  pl.BlockSpec((1, tm, fw), lambda b, i: (b, i, 5))],
        out_specs=pl.BlockSpec((1, tm, fw), lambda b, i: (b, i, 0)),
        out_shape=jax.ShapeDtypeStruct((bsz, seq, fw), jnp.bfloat16),
        scratch_shapes=[pltpu.VMEM((2 * seq, fw), jnp.bfloat16)],
        compiler_params=pltpu.CompilerParams(dimension_semantics=("arbitrary", "arbitrary"),
                                             vmem_limit_bytes=V7X_VMEM_LIMIT_BYTES),
        name="fourier_branch",
    )(p, w1, a, four_w_bf, p)


def _merge_kernel(a_ref, zna_ref, gfo_ref, s_ref, zs_ref, nw_ref, wa_ref, wf_ref, ws_ref,
                  ga_ref, gf_ref, gs_ref, o_ref, ap_ref, sp_ref):
    bf = jnp.bfloat16

    @pl.when(pl.program_id(1) == 0)
    def _():
        zna = zna_ref[...].astype(jnp.float32)
        ap_ref[...] = (a_ref[...].astype(jnp.float32) * (zna * jax.nn.sigmoid(zna))).astype(bf)
        zs = zs_ref[...].astype(jnp.float32)
        t = s_ref[...].astype(jnp.float32) * (zs * jax.nn.sigmoid(zs))
        t = t * lax.rsqrt(jnp.mean(t * t, axis=-1, keepdims=True) + EPS) * nw_ref[...]
        sp_ref[...] = t.astype(bf)

    o_a = jnp.dot(ap_ref[...], wa_ref[...], preferred_element_type=jnp.float32)
    o_f = jnp.dot(gfo_ref[...], wf_ref[...], preferred_element_type=jnp.float32)
    o_s = jnp.dot(sp_ref[...], ws_ref[...], preferred_element_type=jnp.float32)
    m = (jax.nn.sigmoid(ga_ref[...].astype(jnp.float32)) * o_a
         + jax.nn.sigmoid(gf_ref[...].astype(jnp.float32)) * o_f
         + jax.nn.sigmoid(gs_ref[...].astype(jnp.float32)) * o_s)
    o_ref[...] = m.astype(o_ref.dtype)


def merge_branches(p2d, a2d, gfo2d, s2d, ssd_norm_w, wa, wf, ws, *, tm):
    m = p2d.shape[0]
    w, tn = NA_WIDTH, 1024
    assert m % tm == 0 and D_MODEL % tn == 0
    row = lambda i, j: (i, 0)
    wcol = lambda i, j: (0, j)
    return pl.pallas_call(
        _merge_kernel,
        grid=(m // tm, D_MODEL // tn),
        in_specs=[pl.BlockSpec((tm, w), row),
                  pl.BlockSpec((tm, w), lambda i, j: (i, 3)),
                  pl.BlockSpec((tm, w), row),
                  pl.BlockSpec((tm, w), row),
                  pl.BlockSpec((tm, w), lambda i, j: (i, 8)),
                  pl.BlockSpec((1, w), lambda i, j: (0, 0)),
                  pl.BlockSpec((w, tn), wcol), pl.BlockSpec((w, tn), wcol), pl.BlockSpec((w, tn), wcol),
                  pl.BlockSpec((tm, tn), lambda i, j: (i, 9 + j)),
                  pl.BlockSpec((tm, tn), lambda i, j: (i, 11 + j)),
                  pl.BlockSpec((tm, tn), lambda i, j: (i, 13 + j))],
        out_specs=pl.BlockSpec((tm, tn), lambda i, j: (i, j)),
        out_shape=jax.ShapeDtypeStruct((m, D_MODEL), jnp.bfloat16),
        scratch_shapes=[pltpu.VMEM((tm, w), jnp.bfloat16), pltpu.VMEM((tm, w), jnp.bfloat16)],
        compiler_params=pltpu.CompilerParams(dimension_semantics=("arbitrary", "arbitrary"),
                                             vmem_limit_bytes=V7X_VMEM_LIMIT_BYTES),
        name="merge_branches",
    )(a2d, p2d, gfo2d, s2d, p2d, ssd_norm_w.reshape(1, w), wa, wf, ws, p2d, p2d, p2d)


def _outproj_kernel(m_ref, w_ref, x_ref, g_ref, o_ref):
    y = jnp.dot(m_ref[...], w_ref[...], preferred_element_type=jnp.float32)
    o_ref[...] = x_ref[...] + g_ref[0] * y


def out_projection(m2d, w_out_bf, x2d, gate, *, rows_per_mod, tm, tn):
    m, d = x2d.shape
    assert m % tm == 0 and d % tn == 0 and rows_per_mod % tm == 0
    per = rows_per_mod // tm
    return pl.pallas_call(
        _outproj_kernel,
        grid=(m // tm, d // tn),
        in_specs=[pl.BlockSpec((tm, d), lambda i, j: (i, 0)),
                  pl.BlockSpec((d, tn), lambda i, j: (0, j)),
                  pl.BlockSpec((tm, tn), lambda i, j: (i, j)),
                  pl.BlockSpec((1, 1, tn), lambda i, j: (i // per, 0, j))],
        out_specs=pl.BlockSpec((tm, tn), lambda i, j: (i, j)),
        out_shape=jax.ShapeDtypeStruct((m, d), jnp.float32),
        compiler_params=pltpu.CompilerParams(dimension_semantics=("arbitrary", "arbitrary"),
                                             vmem_limit_bytes=V7X_VMEM_LIMIT_BYTES),
        name="out_projection",
    )(m2d, w_out_bf, x2d, gate)


SSD_GW = SSD_HPG * SSD_HEAD_DIM
CONV_PAD = 8


def _split3_bf16(a):
    rnd = lambda v: v.astype(jnp.bfloat16).astype(jnp.float32)
    a1 = rnd(a)
    a2 = rnd(a - a1)
    a3 = rnd((a - a1) - a2)
    return a1, a2, a3


def _head_lanes(cols, i0):
    t = cols.shape[0]
    lo = lax.broadcasted_iota(jnp.int32, (t, 128), 1) < SSD_HEAD_DIM
    pick = lambda i: jnp.broadcast_to(cols[:, i:i + 1], (t, 128))
    return jnp.concatenate([jnp.where(lo, pick(i0), pick(i0 + 1)),
                            jnp.where(lo, pick(i0 + 2), pick(i0 + 3))], axis=1)


def _ssd_kernel(xl_ref, bl_ref, cl_ref, xc_ref, bc_ref, cc_ref, dt_ref, bias_ref, alog_ref, dsk_ref,
                wx_ref, wb_ref, wc_ref, bx_ref, bb_ref, bcb_ref, cos_ref, sin_ref,
                yl_ref, yc_ref,
                padx, padb, padc, xs_s, b_s, c_s, bt_s, row_s, col_s, dec_s, y_s, st_s, *, seq, ctx_len):
    f32, bf = jnp.float32, jnp.bfloat16
    t = SSD_CHUNK
    nc_c, nc_l = ctx_len // t, seq // t
    nc = nc_c + nc_l

    lane = lax.broadcasted_iota(jnp.int32, (t, 128), 1)
    first_half = (lane % 64) < 32

    def conv_tile(pad, w_ref, b_ref, t0, l0):
        acc = jnp.broadcast_to(b_ref[:, l0:l0 + 128], (t, 128))
        win = pad[pl.ds(t0, t + 2 * CONV_PAD), l0:l0 + 128]
        for j in range(SSD_CONV):
            off = CONV_PAD - SSD_CONV // 2 + j
            acc = acc + win[off:off + t] * w_ref[j:j + 1, l0:l0 + 128]
        return acc * jax.nn.sigmoid(acc)

    def rope(u, t0):
        swapped = jnp.where(first_half, pltpu.roll(u, 96, axis=1), pltpu.roll(u, 32, axis=1))
        return u * cos_ref[pl.ds(t0, t), :] + swapped * sin_ref[pl.ds(t0, t), :]

    def conv_seq(x_ref, bm_ref, cm_ref, n, base, use_rope):
        for pad, src in ((padx, x_ref), (padb, bm_ref), (padc, cm_ref)):
            w = pad.shape[1]
            pad[0:CONV_PAD, :] = jnp.zeros((CONV_PAD, w), f32)
            pad[CONV_PAD + n:2 * CONV_PAD + n, :] = jnp.zeros((CONV_PAD, w), f32)
            pad[CONV_PAD:CONV_PAD + n, :] = src[0].astype(f32)

        def chunk(ci, carry):
            t0 = pl.multiple_of(ci * t, t)
            r0 = pl.multiple_of(base + ci * t, t)
            for l0 in range(0, SSD_GW, 128):
                xs_s[pl.ds(r0, t), l0:l0 + 128] = conv_tile(padx, wx_ref, bx_ref, t0, l0)
            ub = conv_tile(padb, wb_ref, bb_ref, t0, 0)
            uc = conv_tile(padc, wc_ref, bcb_ref, t0, 0)
            if use_rope:
                ub, uc = rope(ub, t0), rope(uc, t0)
            b_s[pl.ds(r0, t), :] = ub.astype(bf)
            bt_s[base // t + ci] = ub.T.astype(bf)
            c_s[pl.ds(r0, t), :] = uc.astype(bf)
            return carry

        lax.fori_loop(0, n // t, chunk, 0)

    conv_seq(xc_ref, bc_ref, cc_ref, ctx_len, 0, False)
    conv_seq(xl_ref, bl_ref, cl_ref, seq, ctx_len, True)

    i0 = lax.broadcasted_iota(jnp.int32, (t, t), 0)
    i1 = lax.broadcasted_iota(jnp.int32, (t, t), 1)
    tri_f = (i0 <= i1).astype(bf)
    tri_r = (i0 >= i1).astype(bf)
    rowid = lax.broadcasted_iota(jnp.int32, (8, t), 0)
    lane8 = lax.broadcasted_iota(jnp.int32, (8, 128), 1)
    a_coef = -jnp.exp(alog_ref[0])

    def decay_chunk(c, carry):
        x = dt_ref[0, 0, c] + bias_ref[0]
        dt = jnp.maximum(x, 0.0) + jnp.log1p(jnp.exp(-jnp.abs(x)))
        a1, a2, a3 = _split3_bf16(dt * a_coef)
        stack = jnp.concatenate([a1, a2, a3, jnp.zeros_like(a1)], axis=0).astype(bf)
        pf = jnp.dot(stack, tri_f, preferred_element_type=f32)
        pr = jnp.dot(stack, tri_r, preferred_element_type=f32)
        csf = (pf[0:8] + pf[8:16]) + pf[16:24]
        csr = (pr[0:8] + pr[8:16]) + pr[16:24]
        cs = jnp.where(rowid < SSD_HPG, csf, csr)
        tot = jnp.where(rowid[:, 0:1] < SSD_HPG, cs[:, t - 1:t], cs[:, 0:1])
        e_in = jnp.exp(cs)
        e_out = dt * jnp.exp(tot - cs)
        tile = jnp.concatenate([cs, dt, e_in, e_out, jnp.zeros((t - 32, t), f32)], axis=0)
        row_s[c] = tile[0:16]
        col_s[pl.ds(pl.multiple_of(c * t, t), t), :] = tile.T
        decb = jnp.broadcast_to(jnp.exp(tot), (8, 128))
        for d in range(2):
            r = SSD_HPG * d
            halves = [jnp.where(lane8 < SSD_HEAD_DIM,
                                jnp.broadcast_to(decb[r + 2 * h:r + 2 * h + 1], (8, 128)),
                                jnp.broadcast_to(decb[r + 2 * h + 1:r + 2 * h + 2], (8, 128))) for h in range(2)]
            dec_s[2 * c + d] = jnp.concatenate(halves, axis=1)
        return carry

    lax.fori_loop(0, nc, decay_chunk, 0)

    lower = i1 <= i0
    upper = i1 >= i0
    lane_blk = lax.broadcasted_iota(jnp.int32, (t, SSD_GW), 1) // SSD_HEAD_DIM
    neg_inf = jnp.float32(-jnp.inf)

    def diag_chunk(c, carry):
        r0 = pl.multiple_of(c * t, t)
        g = lax.dot_general(c_s[pl.ds(r0, t), :], b_s[pl.ds(r0, t), :], _NT, preferred_element_type=f32)
        rows = row_s[c]
        cols = col_s[pl.ds(r0, t), :]
        xs = xs_s[pl.ds(r0, t), :]
        xs_bf = xs.astype(bf)
        y = dsk_ref[0] * xs
        for k in range(SSD_HPG):
            kr = SSD_HPG + k
            lf = jnp.exp(jnp.where(lower, cols[:, k:k + 1] - rows[k:k + 1, :], neg_inf)) * rows[8 + k:9 + k, :]
            lr = jnp.exp(jnp.where(upper, cols[:, kr:kr + 1] - rows[kr:kr + 1, :], neg_inf)) * rows[8 + kr:9 + kr, :]
            mk = (g * (lf + lr)).astype(bf)
            xk = jnp.where(lane_blk == k, xs_bf, jnp.zeros_like(xs_bf))
            y = y + jnp.dot(mk, xk, preferred_element_type=f32)
        y_s[pl.ds(r0, t), :] = y
        return carry

    lax.fori_loop(0, nc, diag_chunk, 0)

    st_s[...] = jnp.zeros(st_s.shape, f32)

    def state_step(c, d):
        r0 = pl.multiple_of(c * t, t)
        cols = col_s[pl.ds(r0, t), :]
        st = st_s[d]
        y_off = jnp.dot(c_s[pl.ds(r0, t), :], st.astype(bf), preferred_element_type=f32)
        y_s[pl.ds(r0, t), :] = y_s[pl.ds(r0, t), :] + y_off * _head_lanes(cols, 16 + SSD_HPG * d)
        w = (xs_s[pl.ds(r0, t), :] * _head_lanes(cols, 24 + SSD_HPG * d)).astype(bf)
        st_s[d] = st * dec_s[2 * c + d][0:1, :] + jnp.dot(bt_s[c], w, preferred_element_type=f32)

    def ctx_step(i, carry):
        state_step(i, 0)
        state_step(nc_c - 1 - i, 1)
        return carry

    def lat_step(i, carry):
        state_step(nc_c + i, 0)
        state_step(nc - 1 - i, 1)
        return carry

    lax.fori_loop(0, nc_c, ctx_step, 0)
    lax.fori_loop(0, nc_l, lat_step, 0)

    yc_ref[0] = y_s[0:ctx_len, :]
    yl_ref[0] = y_s[ctx_len:ctx_len + seq, :]


def _rope_tables(seq):
    quarter = SSD_STATE // 4
    inv = ROPE_BASE ** (-np.arange(quarter, dtype=np.float64) / quarter)
    pos = np.arange(seq)
    ang_r = (pos // GRID_W)[:, None] * inv
    ang_c = (pos % GRID_W)[:, None] * inv
    cos = np.concatenate([np.cos(ang_r)] * 2 + [np.cos(ang_c)] * 2, axis=1)
    sin = np.concatenate([-np.sin(ang_r), np.sin(ang_r), -np.sin(ang_c), np.sin(ang_c)], axis=1)
    return jnp.asarray(cos, jnp.float32), jnp.asarray(sin, jnp.float32)


def ssd_mixer(p_lat, p_ctx, dt_lat, dt_ctx, conv_w, conv_b, dt_bias, a_log, d_skip):
    bsz, seq, _ = p_lat.shape
    ctx_len = p_ctx.shape[1]
    t, g, hpg = SSD_CHUNK, SSD_GROUPS, SSD_HPG
    assert seq % t == 0 and ctx_len % t == 0
    nc = (seq + ctx_len) // t
    dt = jnp.concatenate([dt_ctx[..., :DT_WIDTH], dt_lat[..., :DT_WIDTH]], axis=1)
    dt = dt.reshape(bsz, nc, t, 2, g, hpg).transpose(0, 4, 1, 3, 5, 2).reshape(bsz, g, nc, 2 * hpg, t)
    per_row = lambda v: jnp.broadcast_to(
        v.astype(jnp.float32).reshape(2, g, hpg).transpose(1, 0, 2).reshape(g, 2 * hpg, 1), (g, 2 * hpg, 128))
    dsk = jnp.repeat(d_skip.astype(jnp.float32).reshape(g, 1, hpg), SSD_HEAD_DIM, axis=2)
    cos, sin = _rope_tables(seq)
    x0 = sum(MAIN_SPLITS[:6]) // SSD_GW
    b0 = (sum(MAIN_SPLITS[:6]) + SSD_WIDTH) // 128
    c0 = b0 + g
    cw = conv_w.astype(jnp.float32)
    cb = conv_b.astype(jnp.float32).reshape(1, SSD_CONV_CH)
    wb0 = SSD_WIDTH // 128
    ltot = seq + ctx_len
    f32, bf = jnp.float32, jnp.bfloat16
    y_lat, y_ctx = pl.pallas_call(
        partial(_ssd_kernel, seq=seq, ctx_len=ctx_len),
        grid=(bsz, g),
        in_specs=[pl.BlockSpec((1, seq, SSD_GW), lambda b, i: (b, 0, x0 + i)),
                  pl.BlockSpec((1, seq, 128), lambda b, i: (b, 0, b0 + i)),
                  pl.BlockSpec((1, seq, 128), lambda b, i: (b, 0, c0 + i)),
                  pl.BlockSpec((1, ctx_len, SSD_GW), lambda b, i: (b, 0, x0 + i)),
                  pl.BlockSpec((1, ctx_len, 128), lambda b, i: (b, 0, b0 + i)),
                  pl.BlockSpec((1, ctx_len, 128), lambda b, i: (b, 0, c0 + i)),
                  pl.BlockSpec((1, 1, nc, 2 * hpg, t), lambda b, i: (b, i, 0, 0, 0)),
                  pl.BlockSpec((1, 2 * hpg, 128), lambda b, i: (i, 0, 0)),
                  pl.BlockSpec((1, 2 * hpg, 128), lambda b, i: (i, 0, 0)),
                  pl.BlockSpec((1, 1, SSD_GW), lambda b, i: (i, 0, 0)),
                  pl.BlockSpec((SSD_CONV, SSD_GW), lambda b, i: (0, i)),
                  pl.BlockSpec((SSD_CONV, 128), lambda b, i: (0, wb0 + i)),
                  pl.BlockSpec((SSD_CONV, 128), lambda b, i: (0, wb0 + g + i)),
                  pl.BlockSpec((1, SSD_GW), lambda b, i: (0, i)),
                  pl.BlockSpec((1, 128), lambda b, i: (0, wb0 + i)),
                  pl.BlockSpec((1, 128), lambda b, i: (0, wb0 + g + i)),
                  pl.BlockSpec((seq, 128), lambda b, i: (0, 0)),
                  pl.BlockSpec((seq, 128), lambda b, i: (0, 0))],
        out_specs=[pl.BlockSpec((1, seq, SSD_GW), lambda b, i: (b, 0, i)),
                   pl.BlockSpec((1, ctx_len, SSD_GW), lambda b, i: (b, 0, i))],
        out_shape=[jax.ShapeDtypeStruct((bsz, seq, SSD_WIDTH), f32),
                   jax.ShapeDtypeStruct((bsz, ctx_len, SSD_WIDTH), f32)],
        scratch_shapes=[pltpu.VMEM((seq + 2 * CONV_PAD, SSD_GW), f32),
                        pltpu.VMEM((seq + 2 * CONV_PAD, 128), f32),
                        pltpu.VMEM((seq + 2 * CONV_PAD, 128), f32),
                        pltpu.VMEM((ltot, SSD_GW), f32),
                        pltpu.VMEM((ltot, 128), bf),
                        pltpu.VMEM((ltot, 128), bf),
                        pltpu.VMEM((nc, 128, t), bf),
                        pltpu.VMEM((nc, 16, t), f32),
                        pltpu.VMEM((ltot, 128), f32),
                        pltpu.VMEM((2 * nc, 8, SSD_GW), f32),
                        pltpu.VMEM((ltot, SSD_GW), f32),
                        pltpu.VMEM((2, SSD_STATE, SSD_GW), f32)],
        compiler_params=pltpu.CompilerParams(dimension_semantics=("arbitrary", "arbitrary"),
                                             vmem_limit_bytes=V7X_VMEM_LIMIT_BYTES),
        name="ssd_mixer",
    )(p_lat, p_lat, p_lat, p_ctx, p_ctx, p_ctx, dt, per_row(dt_bias), per_row(a_log), dsk,
      cw, cw, cw, cb, cb, cb, cos, sin)
    return y_ctx, y_lat


def _axial_rope(t):
    L, n = t.shape[1], t.shape[-1]
    quarter = n // 4
    pos = jnp.arange(L)
    inv = ROPE_BASE ** (-jnp.arange(quarter, dtype=jnp.float32) / quarter)

    def rot(u, p):
        ang = p.astype(jnp.float32)[:, None] * inv
        cos = jnp.cos(ang)[None, :, None, :].astype(u.dtype)
        sin = jnp.sin(ang)[None, :, None, :].astype(u.dtype)
        u1, u2 = u[..., :quarter], u[..., quarter:]
        return jnp.concatenate([u1 * cos - u2 * sin, u1 * sin + u2 * cos], axis=-1)

    return jnp.concatenate([rot(t[..., :2 * quarter], pos // GRID_W),
                            rot(t[..., 2 * quarter:], pos % GRID_W)], axis=-1)


def _dwconv(u, w, b):
    k = w.shape[0]
    y = lax.conv_general_dilated(u, w[:, None, :], window_strides=(1,), padding=[(k // 2, k // 2)],
                                 dimension_numbers=("NWC", "WIO", "NWC"), feature_group_count=u.shape[-1])
    return y + b


def _segsum(a):
    t = a.shape[-1]
    cs = jnp.cumsum(a, axis=-1)
    d = cs[..., :, None] - cs[..., None, :]
    return jnp.where(jnp.tril(jnp.ones((t, t), dtype=bool)), d, -jnp.inf)


def _ssd_chunked(xdt, a, bm, cm, init):
    b, L, g, k, p = xdt.shape
    nc = L // SSD_CHUNK
    x = xdt.reshape(b, nc, SSD_CHUNK, g, k, p)
    A = a.astype(jnp.float32).reshape(b, nc, SSD_CHUNK, g, k).transpose(0, 3, 4, 1, 2)
    Bc = bm.reshape(b, nc, SSD_CHUNK, g, -1)
    Cc = cm.reshape(b, nc, SSD_CHUNK, g, -1)
    a_cs = jnp.cumsum(A, axis=-1)
    lmat = jnp.exp(_segsum(A))
    y_diag = jnp.einsum("bclgn,bcsgn,bgkcls,bcsgkp->bclgkp", Cc, Bc, lmat, x)
    decay_states = jnp.exp(a_cs[..., -1:] - a_cs)
    states = jnp.einsum("bclgn,bgkcl,bclgkp->bcgkpn", Bc, decay_states, x)
    states = jnp.concatenate([init[:, None].astype(states.dtype), states], axis=1)
    chunk_decay = jnp.exp(_segsum(jnp.pad(a_cs[..., -1], ((0, 0), (0, 0), (0, 0), (1, 0)))))
    states = jnp.einsum("bgkzc,bcgkpn->bzgkpn", chunk_decay, states)
    y_off = jnp.einsum("bclgn,bcgkpn,bgkcl->bclgkp", Cc, states[:, :-1], jnp.exp(a_cs))
    y = (y_diag + y_off).reshape(b, L, g, k, p)
    return y, states[:, -1]


def _ssd_mixer(xbc_c, dt_c, xbc_l, dt_l, conv_w, conv_b, dt_bias, a_log, d_skip):
    out_dtype = xbc_l.dtype

    def prep(xbc, dt_raw, rope):
        u = jax.nn.silu(_dwconv(xbc, conv_w, conv_b))
        xs, bm, cm = jnp.split(u, [SSD_WIDTH, SSD_WIDTH + SSD_GROUPS * SSD_STATE], axis=-1)
        b, L = xs.shape[:2]
        xs = xs.reshape(b, L, SSD_GROUPS, SSD_HPG, SSD_HEAD_DIM)
        bm = bm.reshape(b, L, SSD_GROUPS, SSD_STATE)
        cm = cm.reshape(b, L, SSD_GROUPS, SSD_STATE)
        if rope:
            bm, cm = _axial_rope(bm), _axial_rope(cm)
        dt = jax.nn.softplus(dt_raw.astype(jnp.float32).reshape(b, L, 2, SSD_GROUPS, SSD_HPG)
                             + dt_bias.astype(jnp.float32).reshape(2, SSD_GROUPS, SSD_HPG))
        return xs, bm, cm, dt

    xc, bc, cc, dtc = prep(xbc_c, dt_c, False)
    xl, bl, cl, dtl = prep(xbc_l, dt_l, True)
    a = -jnp.exp(a_log.astype(jnp.float32)).reshape(2, SSD_GROUPS, SSD_HPG)
    dsk = d_skip.astype(jnp.float32).reshape(SSD_GROUPS, SSD_HPG, 1)
    b = xl.shape[0]
    init = jnp.zeros((b, SSD_GROUPS, SSD_HPG, SSD_HEAD_DIM, SSD_STATE), jnp.float32)
    yc = dsk * xc.astype(jnp.float32)
    yl = dsk * xl.astype(jnp.float32)
    for direction in range(2):
        flip = (lambda t: t[:, ::-1]) if direction == 1 else (lambda t: t)

        def run(xs, bm, cm, dt, s0):
            dtd = dt[:, :, direction]
            y, s = _ssd_chunked(flip(xs * dtd[..., None]), flip(dtd * a[direction]), flip(bm), flip(cm), s0)
            return flip(y), s

        y_c, s_c = run(xc, bc, cc, dtc, init)
        y_l, _ = run(xl, bl, cl, dtl, s_c)
        yc = yc + y_c
        yl = yl + y_l
    return (yc.reshape(yc.shape[0], yc.shape[1], SSD_WIDTH).astype(out_dtype),
            yl.reshape(b, yl.shape[1], SSD_WIDTH).astype(out_dtype))


def _na_latent(q, k, v, k_ctx, v_ctx, rpb):
    b, L, h, d = q.shape
    rows = L // GRID_W
    win_r = min(NA_WIN_R, rows)
    nblk = GRID_W // NA_QBLK
    scale = d ** -0.5
    qcol = np.arange(GRID_W).reshape(nblk, NA_QBLK)
    qstart = np.clip(qcol - NA_WIN_C // 2, 0, GRID_W - NA_WIN_C)
    kstart = np.clip(qcol[:, 0] - NA_WIN_C // 2, 0, GRID_W - NA_KBLK)
    kcol = (kstart[:, None] + np.arange(NA_KBLK)[None]).astype(np.int32)
    col_mask = ((kcol[:, None, :] >= qstart[..., None]) &
                (kcol[:, None, :] < qstart[..., None] + NA_WIN_C))
    dc = np.clip(kcol[:, None, :] - qcol[..., None] + NA_WIN_C - 1, 0, 2 * NA_WIN_C - 2).astype(np.int32)
    kg = k.reshape(b, rows, GRID_W, h, d)
    vg = v.reshape(b, rows, GRID_W, h, d)
    qg = q.reshape(b, rows, nblk, NA_QBLK, h, d).transpose(1, 0, 2, 3, 4, 5)
    rpb32 = rpb.astype(jnp.float32)
    nwin = win_r * NA_KBLK

    def row_block(args):
        r, q_r = args
        start = jnp.clip(r - win_r // 2, 0, rows - win_r)
        k_r = lax.dynamic_slice_in_dim(kg, start, win_r, axis=1)[:, :, kcol]
        v_r = lax.dynamic_slice_in_dim(vg, start, win_r, axis=1)[:, :, kcol]
        dr = start + jnp.arange(win_r) - r + NA_WIN_R - 1
        bias = rpb32[:, dr][:, :, dc].transpose(0, 2, 3, 1, 4)
        s_w = jnp.einsum("bjqhd,bwjkhd->bhjqwk", q_r, k_r).astype(jnp.float32) * scale + bias[None]
        s_w = jnp.where(col_mask[:, :, None, :], s_w, -1e30)
        s_c = jnp.einsum("bjqhd,bchd->bhjqc", q_r, k_ctx).astype(jnp.float32) * scale
        s = jnp.concatenate([s_w.reshape(b, h, nblk, NA_QBLK, nwin), s_c], axis=-1)
        p = jax.nn.softmax(s, axis=-1).astype(v.dtype)
        p_w = p[..., :nwin].reshape(b, h, nblk, NA_QBLK, win_r, NA_KBLK)
        return (jnp.einsum("bhjqwk,bwjkhd->bjqhd", p_w, v_r)
                + jnp.einsum("bhjqc,bchd->bjqhd", p[..., nwin:], v_ctx))

    o = lax.map(row_block, (jnp.arange(rows), qg))
    return o.transpose(1, 0, 2, 3, 4, 5).reshape(b, L, h * d)


def _ctx_attention(q, k, v):
    b, n, h, d = q.shape
    s = jnp.einsum("bqhd,bkhd->bhqk", q, k).astype(jnp.float32) * (d ** -0.5)
    p = jax.nn.softmax(s, axis=-1).astype(v.dtype)
    return jnp.einsum("bhqk,bkhd->bqhd", p, v).reshape(b, n, h * d)


def _fourier_real(u):
    b, L, _ = u.shape
    ug = u.astype(jnp.float32).reshape(b, L, FOUR_GROUPS, FOUR_GROUP_DIM)
    f = jnp.fft.fft2(ug, axes=(1, 3), norm="ortho").real
    return f.reshape(b, L, FOUR_WIDTH).astype(u.dtype)


def _heads(t):
    return t.reshape(t.shape[0], t.shape[1], NA_HEADS, NA_HEAD_DIM)


def _split_main(p):
    offs = np.cumsum(MAIN_SPLITS)[:-1].tolist()
    return jnp.split(p, offs, axis=-1)


def _mm3(a, w, tm=1024, tn=1024):
    b, L, k = a.shape
    y = matmul(a.reshape(b * L, k), w, tm=min(tm, b * L), tn=tn)
    return y.reshape(b, L, -1)


def _layer(xc, xl, c, c_ctx, w_ada, b_ada, norm_w, w_in, rpb, four_w, conv_w, conv_b, dt_bias, a_log, d_skip,
           ssd_norm_w, wb_na, wb_four, wb_ssd, w_out, update_ctx):
    bsz, seq, d = xl.shape
    ctx_len = xc.shape[1]
    bf = jnp.bfloat16

    cc = jnp.concatenate([c, c_ctx[None], jnp.zeros((16 - bsz - 1, d), c.dtype)], axis=0)
    mod = matmul(jax.nn.silu(cc), w_ada.astype(bf), tm=16, tn=1024) + b_ada
    sh, sc, gt = jnp.split(mod, 3, axis=-1)
    sh_l, sc_l, g_l = sh[:bsz], sc[:bsz], gt[:bsz]
    sh_c, sc_c, g_c = sh[bsz:bsz + 1], sc[bsz:bsz + 1], gt[bsz]

    w_main = jnp.concatenate([w_in[:, :DT_OFFSET], w_in[:, DT_OFFSET + DT_WIDTH:]], axis=1).astype(bf)
    w_dt = jnp.pad(w_in[:, DT_OFFSET:DT_OFFSET + DT_WIDTH], ((0, 0), (0, DT_PAD - DT_WIDTH))).astype(bf)

    xl2d = xl.reshape(bsz * seq, d)
    xc2d = xc.reshape(bsz * ctx_len, d)
    pl2d, dt_l = in_projection(xl2d, norm_w, sc_l[:, None], sh_l[:, None], w_main, w_dt,
                               rows_per_mod=seq, tm=1024, tn=1536, out_dtype=jnp.float32)
    pc2d, dt_c = in_projection(xc2d, norm_w, sc_c[:, None], sh_c[:, None], w_main, w_dt,
                               rows_per_mod=bsz * ctx_len, tm=1024, tn=1536, out_dtype=jnp.float32)
    p_l = pl2d.reshape(bsz, seq, MAIN_WIDTH)
    p_c = pc2d.reshape(bsz, ctx_len, MAIN_WIDTH)
    dt_l = dt_l.reshape(bsz, seq, DT_PAD)
    dt_c = dt_c.reshape(bsz, ctx_len, DT_PAD)

    four_wb, wb_nab, wb_fourb, wb_ssdb, w_outb = (t.astype(bf) for t in (four_w, wb_na, wb_four, wb_ssd, w_out))

    a_l, a_c = na_attention(p_l, p_c, na_bias_table(rpb), with_ctx=update_ctx, out_dtype=bf)
    gfo_l = fourier_branch(p_l, four_wb, tm=512)
    s_c, s_l = ssd_mixer(p_l, p_c, dt_l, dt_c, conv_w, conv_b, dt_bias, a_log, d_skip)
    m_l = merge_branches(pl2d, a_l.reshape(bsz * seq, NA_WIDTH), gfo_l.reshape(bsz * seq, FOUR_WIDTH),
                         s_l.reshape(bsz * seq, SSD_WIDTH), ssd_norm_w, wb_nab, wb_fourb, wb_ssdb, tm=512)
    xl_new = out_projection(m_l, w_outb, xl2d, g_l[:, None], rows_per_mod=seq, tm=1024, tn=1024)
    xl_new = xl_new.reshape(bsz, seq, d)
    if update_ctx:
        gfo_c = fourier_branch(p_c, four_wb, tm=ctx_len)
        m_c = merge_branches(pc2d, a_c.reshape(bsz * ctx_len, NA_WIDTH), gfo_c.reshape(bsz * ctx_len, FOUR_WIDTH),
                             s_c.reshape(bsz * ctx_len, SSD_WIDTH), ssd_norm_w, wb_nab, wb_fourb, wb_ssdb, tm=512)
        xc = out_projection(m_c, w_outb, xc2d, g_c[None, None], rows_per_mod=bsz * ctx_len, tm=1024, tn=1024)
        xc = xc.reshape(bsz, ctx_len, d)
    return xc, xl_new


def kernel(x, c, ctx, c_ctx, w_ada, b_ada, norm_w, w_in, na_rpb, four_w, ssd_conv_w, ssd_conv_b, ssd_dt_bias,
           ssd_a_log, ssd_d, ssd_norm_w, wb_na, wb_four, wb_ssd, w_out, final_norm_w):
    xc, xl = ctx, x
    for l in range(DEPTH):
        xc, xl = _layer(xc, xl, c, c_ctx, w_ada[l], b_ada[l], norm_w[l], w_in[l], na_rpb[l], four_w[l],
                        ssd_conv_w[l], ssd_conv_b[l], ssd_dt_bias[l], ssd_a_log[l], ssd_d[l], ssd_norm_w[l],
                        wb_na[l], wb_four[l], wb_ssd[l], w_out[l], update_ctx=(l < DEPTH - 1))
    return _rmsnorm(xl, final_norm_w)
```

```python
import math
from functools import partial

import jax
import jax.numpy as jnp
import numpy as np
from jax import lax
from jax.experimental import pallas as pl
from jax.experimental.pallas import tpu as pltpu

D_MODEL = 2048
DEPTH = 2
GRID_W = 64
EPS = 1e-6
NA_HEADS = 8
NA_HEAD_DIM = 128
NA_WIDTH = NA_HEADS * NA_HEAD_DIM
NA_WIN_R = 8
NA_WIN_C = 16
NA_QBLK = 16
NA_KBLK = 32
FOUR_GROUPS = 4
FOUR_GROUP_DIM = 256
FOUR_WIDTH = FOUR_GROUPS * FOUR_GROUP_DIM
SSD_HEADS = 16
SSD_HEAD_DIM = 64
SSD_WIDTH = SSD_HEADS * SSD_HEAD_DIM
SSD_GROUPS = 4
SSD_HPG = SSD_HEADS // SSD_GROUPS
SSD_STATE = 128
SSD_CONV = 7
SSD_CHUNK = 128
SSD_CONV_CH = SSD_WIDTH + 2 * SSD_GROUPS * SSD_STATE
ROPE_BASE = 10000.0
DT_WIDTH = 2 * SSD_HEADS
MAIN_SPLITS = (NA_WIDTH, NA_WIDTH, NA_WIDTH, NA_WIDTH, FOUR_WIDTH, FOUR_WIDTH, SSD_CONV_CH, SSD_WIDTH,
               D_MODEL, D_MODEL, D_MODEL)
MAIN_WIDTH = sum(MAIN_SPLITS)
DT_OFFSET = sum(MAIN_SPLITS[:8])
DT_PAD = 128

V7X_VMEM_LIMIT_BYTES = 56 * 1024 * 1024


def _rmsnorm(x, w):
    xf = x.astype(jnp.float32)
    y = xf * lax.rsqrt(jnp.mean(xf * xf, axis=-1, keepdims=True) + EPS)
    return (y * w.astype(jnp.float32)).astype(x.dtype)


def _mm_kernel(a_ref, b_ref, o_ref):
    a = a_ref[...].astype(jnp.bfloat16)
    o_ref[...] = jnp.dot(a, b_ref[...], preferred_element_type=jnp.float32).astype(o_ref.dtype)


def matmul(a, b, *, tm, tn, out_dtype=jnp.float32):
    m, k = a.shape
    _, n = b.shape
    assert m % tm == 0 and n % tn == 0, (a.shape, b.shape, tm, tn)
    return pl.pallas_call(
        _mm_kernel,
        grid=(m // tm, n // tn),
        in_specs=[pl.BlockSpec((tm, k), lambda i, j: (i, 0)),
                  pl.BlockSpec((k, tn), lambda i, j: (0, j))],
        out_specs=pl.BlockSpec((tm, tn), lambda i, j: (i, j)),
        out_shape=jax.ShapeDtypeStruct((m, n), out_dtype),
        compiler_params=pltpu.CompilerParams(dimension_semantics=("arbitrary", "arbitrary"),
                                             vmem_limit_bytes=V7X_VMEM_LIMIT_BYTES),
        name="matmul",
    )(a, b)


def _inproj_kernel(x_ref, nw_ref, sc_ref, sh_ref, w_ref, wdt_ref, o_ref, dt_ref, h_ref):
    @pl.when(pl.program_id(1) == 0)
    def _():
        x = x_ref[...]
        y = x * lax.rsqrt(jnp.mean(x * x, axis=-1, keepdims=True) + EPS) * nw_ref[...]
        h = (y * (1.0 + sc_ref[0]) + sh_ref[0]).astype(jnp.bfloat16)
        h_ref[...] = h
        dt_ref[...] = jnp.dot(h, wdt_ref[...], preferred_element_type=jnp.float32)

    o_ref[...] = jnp.dot(h_ref[...], w_ref[...], preferred_element_type=jnp.float32).astype(o_ref.dtype)


def in_projection(x2d, norm_w, scale, shift, w_main, w_dt, *, rows_per_mod, tm, tn, out_dtype):
    m, d = x2d.shape
    n = w_main.shape[1]
    assert m % tm == 0 and n % tn == 0 and rows_per_mod % tm == 0
    per = rows_per_mod // tm
    return pl.pallas_call(
        _inproj_kernel,
        grid=(m // tm, n // tn),
        in_specs=[pl.BlockSpec((tm, d), lambda i, j: (i, 0)),
                  pl.BlockSpec((1, d), lambda i, j: (0, 0)),
                  pl.BlockSpec((1, 1, d), lambda i, j: (i // per, 0, 0)),
                  pl.BlockSpec((1, 1, d), lambda i, j: (i // per, 0, 0)),
                  pl.BlockSpec((d, tn), lambda i, j: (0, j)),
                  pl.BlockSpec((d, DT_PAD), lambda i, j: (0, 0))],
        out_specs=[pl.BlockSpec((tm, tn), lambda i, j: (i, j)),
                   pl.BlockSpec((tm, DT_PAD), lambda i, j: (i, 0))],
        out_shape=[jax.ShapeDtypeStruct((m, n), out_dtype),
                   jax.ShapeDtypeStruct((m, DT_PAD), jnp.float32)],
        scratch_shapes=[pltpu.VMEM((tm, d), jnp.bfloat16)],
        compiler_params=pltpu.CompilerParams(dimension_semantics=("arbitrary", "arbitrary"),
                                             vmem_limit_bytes=V7X_VMEM_LIMIT_BYTES),
        name="in_projection",
    )(x2d, norm_w.reshape(1, d), scale, shift, w_main, w_dt)


GRID_ROWS = 32
NA_WIN_TOKENS = NA_WIN_R * GRID_W
_NT = (((1,), (1,)), ((), ()))
NA_ROW_UNROLL = 8


def _na_kernel(*refs, with_ctx):
    if with_ctx:
        q_ref, k_ref, v_ref, kc_ref, vc_ref, bias_ref, qc_ref, o_ref, oc_ref = refs
    else:
        q_ref, k_ref, v_ref, kc_ref, vc_ref, bias_ref, o_ref = refs
    bf = jnp.bfloat16
    scale = NA_HEAD_DIM ** -0.5
    kc = kc_ref[0].astype(bf)
    vc = vc_ref[0].astype(bf)

    def rows(rb, carry):
        idx, scores, probs = [], [], []
        for i in range(NA_ROW_UNROLL):
            r = rb * NA_ROW_UNROLL + i
            start = jnp.clip(r - NA_WIN_R // 2, 0, GRID_ROWS - NA_WIN_R)
            q0 = pl.multiple_of(r * GRID_W, GRID_W)
            k0 = pl.multiple_of(start * GRID_W, GRID_W)
            q = q_ref[0, pl.ds(q0, GRID_W), :].astype(bf)
            kw = k_ref[0, pl.ds(k0, NA_WIN_TOKENS), :].astype(bf)
            s_w = lax.dot_general(q, kw, _NT, preferred_element_type=jnp.float32) * scale + bias_ref[0, r - start]
            s_c = lax.dot_general(q, kc, _NT, preferred_element_type=jnp.float32) * scale
            idx.append((q0, k0))
            scores.append((s_w, s_c))
        for s_w, s_c in scores:
            m = jnp.maximum(jnp.max(s_w, axis=-1, keepdims=True), jnp.max(s_c, axis=-1, keepdims=True))
            e_w = jnp.exp(s_w - m)
            e_c = jnp.exp(s_c - m)
            denom = jnp.sum(e_w, axis=-1, keepdims=True) + jnp.sum(e_c, axis=-1, keepdims=True)
            probs.append((e_w.astype(bf), e_c.astype(bf), denom))
        for (q0, k0), (e_w, e_c, denom) in zip(idx, probs):
            vw = v_ref[0, pl.ds(k0, NA_WIN_TOKENS), :].astype(bf)
            o = (jnp.dot(e_w, vw, preferred_element_type=jnp.float32)
                 + jnp.dot(e_c, vc, preferred_element_type=jnp.float32))
            o_ref[0, pl.ds(q0, GRID_W), :] = (o / denom).astype(o_ref.dtype)
        return carry

    lax.fori_loop(0, GRID_ROWS // NA_ROW_UNROLL, rows, 0)

    if with_ctx:
        s = lax.dot_general(qc_ref[0].astype(bf), kc, _NT, preferred_element_type=jnp.float32) * scale
        e = jnp.exp(s - jnp.max(s, axis=-1, keepdims=True))
        o = jnp.dot(e.astype(bf), vc, preferred_element_type=jnp.float32)
        oc_ref[0] = (o / jnp.sum(e, axis=-1, keepdims=True)).astype(oc_ref.dtype)


def na_bias_table(rpb):
    delta = np.arange(NA_WIN_R)[:, None]
    w = np.arange(NA_WIN_R)[None, :]
    dr = w - delta + NA_WIN_R - 1
    qcol = np.arange(GRID_W)[:, None]
    kcol = np.arange(GRID_W)[None, :]
    qstart = np.clip(qcol - NA_WIN_C // 2, 0, GRID_W - NA_WIN_C)
    mask = (kcol >= qstart) & (kcol < qstart + NA_WIN_C)
    dc = np.clip(kcol - qcol + NA_WIN_C - 1, 0, 2 * NA_WIN_C - 2)
    t = rpb.astype(jnp.float32)[:, dr][:, :, :, dc]
    t = jnp.where(mask[None, None, None], t, -1e30)
    return t.transpose(0, 1, 3, 2, 4).reshape(rpb.shape[0], NA_WIN_R, GRID_W, NA_WIN_TOKENS)


def na_attention(p_lat, p_ctx, bias, *, with_ctx, out_dtype):
    bsz, seq, _ = p_lat.shape
    ctx_len = p_ctx.shape[1]
    assert seq == GRID_ROWS * GRID_W
    d, h = NA_HEAD_DIM, NA_HEADS
    in_specs = [pl.BlockSpec((1, seq, d), lambda b, i: (b, 0, i)),
                pl.BlockSpec((1, seq, d), lambda b, i: (b, 0, h + i)),
                pl.BlockSpec((1, seq, d), lambda b, i: (b, 0, 2 * h + i)),
                pl.BlockSpec((1, ctx_len, d), lambda b, i: (b, 0, h + i)),
                pl.BlockSpec((1, ctx_len, d), lambda b, i: (b, 0, 2 * h + i)),
                pl.BlockSpec((1, NA_WIN_R, GRID_W, NA_WIN_TOKENS), lambda b, i: (i, 0, 0, 0))]
    args = [p_lat, p_lat, p_lat, p_ctx, p_ctx, bias]
    out_specs = [pl.BlockSpec((1, seq, d), lambda b, i: (b, 0, i))]
    out_shape = [jax.ShapeDtypeStruct((bsz, seq, h * d), out_dtype)]
    if with_ctx:
        in_specs.append(pl.BlockSpec((1, ctx_len, d), lambda b, i: (b, 0, i)))
        args.append(p_ctx)
        out_specs.append(pl.BlockSpec((1, ctx_len, d), lambda b, i: (b, 0, i)))
        out_shape.append(jax.ShapeDtypeStruct((bsz, ctx_len, h * d), out_dtype))
    outs = pl.pallas_call(
        partial(_na_kernel, with_ctx=with_ctx),
        grid=(bsz, h),
        in_specs=in_specs, out_specs=out_specs, out_shape=out_shape,
        compiler_params=pltpu.CompilerParams(dimension_semantics=("arbitrary", "arbitrary"),
                                             vmem_limit_bytes=V7X_VMEM_LIMIT_BYTES),
        name="na_attention",
    )(*args)
    return (outs[0], outs[1]) if with_ctx else (outs[0], None)


def _dft_cos_sin(n):
    jk = np.outer(np.arange(n), np.arange(n)) % n
    ang = 2.0 * np.pi * jk / n
    return np.cos(ang) / np.sqrt(n), np.sin(ang) / np.sqrt(n)


def _fourier_kernel(u_ref, w1_ref, a_ref, fw_ref, zf_ref, o_ref, z_ref, *, seq, row_blk):
    bf = jnp.bfloat16
    gd = FOUR_GROUP_DIM

    @pl.when(pl.program_id(1) == 0)
    def _():
        for r0 in range(0, seq, row_blk):
            for g in range(FOUR_GROUPS):
                ug = u_ref[0, r0:r0 + row_blk, g * gd:(g + 1) * gd].astype(bf)
                z = jnp.dot(ug, w1_ref[...], preferred_element_type=jnp.float32)
                z_ref[r0:r0 + row_blk, g * gd:(g + 1) * gd] = z[:, :gd].astype(bf)
                z_ref[seq + r0:seq + r0 + row_blk, g * gd:(g + 1) * gd] = z[:, gd:].astype(bf)

    y = jnp.dot(a_ref[...], z_ref[...], preferred_element_type=jnp.float32)
    f = jnp.dot(y.astype(bf), fw_ref[...], preferred_element_type=jnp.float32)
    zf = zf_ref[0].astype(jnp.float32)
    o_ref[0] = (f * (zf * jax.nn.sigmoid(zf))).astype(o_ref.dtype)


def fourier_branch(p, four_w_bf, *, tm):
    bsz, seq, _ = p.shape
    assert seq % tm == 0
    cc, sc = _dft_cos_sin(FOUR_GROUP_DIM)
    cl, sl = _dft_cos_sin(seq)
    w1 = jnp.asarray(np.concatenate([cc, sc], axis=1), jnp.bfloat16)
    a = jnp.asarray(np.concatenate([cl, -sl], axis=1), jnp.bfloat16)
    fw = FOUR_WIDTH
    return pl.pallas_call(
        partial(_fourier_kernel, seq=seq, row_blk=min(seq, 512)),
        grid=(bsz, seq // tm),
        in_specs=[pl.BlockSpec((1, seq, fw), lambda b, i: (b, 0, 4)),
                  pl.BlockSpec((FOUR_GROUP_DIM, 2 * FOUR_GROUP_DIM), lambda b, i: (0, 0)),
                  pl.BlockSpec((tm, 2 * seq), lambda b, i: (i, 0)),
                  pl.BlockSpec((fw, fw), lambda b, i: (0, 0)),
                  pl.BlockSpec((1, tm, fw), lambda b, i: (b, i, 5))],
        out_specs=pl.BlockSpec((1, tm, fw), lambda b, i: (b, i, 0)),
        out_shape=jax.ShapeDtypeStruct((bsz, seq, fw), jnp.bfloat16),
        scratch_shapes=[pltpu.VMEM((2 * seq, fw), jnp.bfloat16)],
        compiler_params=pltpu.CompilerParams(dimension_semantics=("arbitrary", "arbitrary"),
                                             vmem_limit_bytes=V7X_VMEM_LIMIT_BYTES),
        name="fourier_branch",
    )(p, w1, a, four_w_bf, p)


def _merge_kernel(a_ref, zna_ref, gfo_ref, s_ref, zs_ref, nw_ref, wa_ref, wf_ref, ws_ref,
                  ga_ref, gf_ref, gs_ref, o_ref, ap_ref, sp_ref):
    bf = jnp.bfloat16

    @pl.when(pl.program_id(1) == 0)
    def _():
        zna = zna_ref[...].astype(jnp.float32)
        ap_ref[...] = (a_ref[...].astype(jnp.float32) * (zna * jax.nn.sigmoid(zna))).astype(bf)
        zs = zs_ref[...].astype(jnp.float32)
        t = s_ref[...].astype(jnp.float32) * (zs * jax.nn.sigmoid(zs))
        t = t * lax.rsqrt(jnp.mean(t * t, axis=-1, keepdims=True) + EPS) * nw_ref[...]
        sp_ref[...] = t.astype(bf)

    o_a = jnp.dot(ap_ref[...], wa_ref[...], preferred_element_type=jnp.float32)
    o_f = jnp.dot(gfo_ref[...], wf_ref[...], preferred_element_type=jnp.float32)
    o_s = jnp.dot(sp_ref[...], ws_ref[...], preferred_element_type=jnp.float32)
    m = (jax.nn.sigmoid(ga_ref[...].astype(jnp.float32)) * o_a
         + jax.nn.sigmoid(gf_ref[...].astype(jnp.float32)) * o_f
         + jax.nn.sigmoid(gs_ref[...].astype(jnp.float32)) * o_s)
    o_ref[...] = m.astype(o_ref.dtype)


def merge_branches(p2d, a2d, gfo2d, s2d, ssd_norm_w, wa, wf, ws, *, tm):
    m = p2d.shape[0]
    w, tn = NA_WIDTH, 1024
    assert m % tm == 0 and D_MODEL % tn == 0
    row = lambda i, j: (i, 0)
    wcol = lambda i, j: (0, j)
    return pl.pallas_call(
        _merge_kernel,
        grid=(m // tm, D_MODEL // tn),
        in_specs=[pl.BlockSpec((tm, w), row),
                  pl.BlockSpec((tm, w), lambda i, j: (i, 3)),
                  pl.BlockSpec((tm, w), row),
                  pl.BlockSpec((tm, w), row),
                  pl.BlockSpec((tm, w), lambda i, j: (i, 8)),
                  pl.BlockSpec((1, w), lambda i, j: (0, 0)),
                  pl.BlockSpec((w, tn), wcol), pl.BlockSpec((w, tn), wcol), pl.BlockSpec((w, tn), wcol),
                  pl.BlockSpec((tm, tn), lambda i, j: (i, 9 + j)),
                  pl.BlockSpec((tm, tn), lambda i, j: (i, 11 + j)),
                  pl.BlockSpec((tm, tn), lambda i, j: (i, 13 + j))],
        out_specs=pl.BlockSpec((tm, tn), lambda i, j: (i, j)),
        out_shape=jax.ShapeDtypeStruct((m, D_MODEL), jnp.bfloat16),
        scratch_shapes=[pltpu.VMEM((tm, w), jnp.bfloat16), pltpu.VMEM((tm, w), jnp.bfloat16)],
        compiler_params=pltpu.CompilerParams(dimension_semantics=("arbitrary", "arbitrary"),
                                             vmem_limit_bytes=V7X_VMEM_LIMIT_BYTES),
        name="merge_branches",
    )(a2d, p2d, gfo2d, s2d, p2d, ssd_norm_w.reshape(1, w), wa, wf, ws, p2d, p2d, p2d)


def _outproj_kernel(m_ref, w_ref, x_ref, g_ref, o_ref):
    y = jnp.dot(m_ref[...], w_ref[...], preferred_element_type=jnp.float32)
    o_ref[...] = x_ref[...] + g_ref[0] * y


def out_projection(m2d, w_out_bf, x2d, gate, *, rows_per_mod, tm, tn):
    m, d = x2d.shape
    assert m % tm == 0 and d % tn == 0 and rows_per_mod % tm == 0
    per = rows_per_mod // tm
    return pl.pallas_call(
        _outproj_kernel,
        grid=(m // tm, d // tn),
        in_specs=[pl.BlockSpec((tm, d), lambda i, j: (i, 0)),
                  pl.BlockSpec((d, tn), lambda i, j: (0, j)),
                  pl.BlockSpec((tm, tn), lambda i, j: (i, j)),
                  pl.BlockSpec((1, 1, tn), lambda i, j: (i // per, 0, j))],
        out_specs=pl.BlockSpec((tm, tn), lambda i, j: (i, j)),
        out_shape=jax.ShapeDtypeStruct((m, d), jnp.float32),
        compiler_params=pltpu.CompilerParams(dimension_semantics=("arbitrary", "arbitrary"),
                                             vmem_limit_bytes=V7X_VMEM_LIMIT_BYTES),
        name="out_projection",
    )(m2d, w_out_bf, x2d, gate)


SSD_GW = SSD_HPG * SSD_HEAD_DIM
CONV_PAD = 8


def _split3_bf16(a):
    rnd = lambda v: v.astype(jnp.bfloat16).astype(jnp.float32)
    a1 = rnd(a)
    a2 = rnd(a - a1)
    a3 = rnd((a - a1) - a2)
    return a1, a2, a3


def _head_lanes(cols, i0):
    t = cols.shape[0]
    lo = lax.broadcasted_iota(jnp.int32, (t, 128), 1) < SSD_HEAD_DIM
    pick = lambda i: jnp.broadcast_to(cols[:, i:i + 1], (t, 128))
    return jnp.concatenate([jnp.where(lo, pick(i0), pick(i0 + 1)),
                            jnp.where(lo, pick(i0 + 2), pick(i0 + 3))], axis=1)


def _ssd_kernel(xl_ref, bl_ref, cl_ref, xc_ref, bc_ref, cc_ref, dt_ref, bias_ref, alog_ref, dsk_ref,
                wx_ref, wb_ref, wc_ref, bx_ref, bb_ref, bcb_ref, cos_ref, sin_ref,
                yl_ref, yc_ref,
                padx, padb, padc, xs_s, b_s, c_s, bt_s, row_s, col_s, dec_s, y_s, st_s, *, seq, ctx_len):
    f32, bf = jnp.float32, jnp.bfloat16
    t = SSD_CHUNK
    nc_c, nc_l = ctx_len // t, seq // t
    nc = nc_c + nc_l

    lane = lax.broadcasted_iota(jnp.int32, (t, 128), 1)
    first_half = (lane % 64) < 32

    def conv_tile(pad, w_ref, b_ref, t0, l0):
        acc = jnp.broadcast_to(b_ref[:, l0:l0 + 128], (t, 128))
        win = pad[pl.ds(t0, t + 2 * CONV_PAD), l0:l0 + 128]
        for j in range(SSD_CONV):
            off = CONV_PAD - SSD_CONV // 2 + j
            acc = acc + win[off:off + t] * w_ref[j:j + 1, l0:l0 + 128]
        return acc * jax.nn.sigmoid(acc)

    def rope(u, t0):
        swapped = jnp.where(first_half, pltpu.roll(u, 96, axis=1), pltpu.roll(u, 32, axis=1))
        return u * cos_ref[pl.ds(t0, t), :] + swapped * sin_ref[pl.ds(t0, t), :]

    def conv_seq(x_ref, bm_ref, cm_ref, n, base, use_rope):
        for pad, src in ((padx, x_ref), (padb, bm_ref), (padc, cm_ref)):
            w = pad.shape[1]
            pad[0:CONV_PAD, :] = jnp.zeros((CONV_PAD, w), f32)
            pad[CONV_PAD + n:2 * CONV_PAD + n, :] = jnp.zeros((CONV_PAD, w), f32)
            pad[CONV_PAD:CONV_PAD + n, :] = src[0].astype(f32)

        def chunk(ci, carry):
            t0 = pl.multiple_of(ci * t, t)
            r0 = pl.multiple_of(base + ci * t, t)
            for l0 in range(0, SSD_GW, 128):
                xs_s[pl.ds(r0, t), l0:l0 + 128] = conv_tile(padx, wx_ref, bx_ref, t0, l0)
            ub = conv_tile(padb, wb_ref, bb_ref, t0, 0)
            uc = conv_tile(padc, wc_ref, bcb_ref, t0, 0)
            if use_rope:
                ub, uc = rope(ub, t0), rope(uc, t0)
            b_s[pl.ds(r0, t), :] = ub.astype(bf)
            bt_s[base // t + ci] = ub.T.astype(bf)
            c_s[pl.ds(r0, t), :] = uc.astype(bf)
            return carry

        lax.fori_loop(0, n // t, chunk, 0)

    conv_seq(xc_ref, bc_ref, cc_ref, ctx_len, 0, False)
    conv_seq(xl_ref, bl_ref, cl_ref, seq, ctx_len, True)

    i0 = lax.broadcasted_iota(jnp.int32, (t, t), 0)
    i1 = lax.broadcasted_iota(jnp.int32, (t, t), 1)
    tri_f = (i0 <= i1).astype(bf)
    tri_r = (i0 >= i1).astype(bf)
    rowid = lax.broadcasted_iota(jnp.int32, (8, t), 0)
    lane8 = lax.broadcasted_iota(jnp.int32, (8, 128), 1)
    a_coef = -jnp.exp(alog_ref[0])

    def decay_chunk(c, carry):
        x = dt_ref[0, 0, c] + bias_ref[0]
        dt = jnp.maximum(x, 0.0) + jnp.log1p(jnp.exp(-jnp.abs(x)))
        a1, a2, a3 = _split3_bf16(dt * a_coef)
        stack = jnp.concatenate([a1, a2, a3, jnp.zeros_like(a1)], axis=0).astype(bf)
        pf = jnp.dot(stack, tri_f, preferred_element_type=f32)
        pr = jnp.dot(stack, tri_r, preferred_element_type=f32)
        csf = (pf[0:8] + pf[8:16]) + pf[16:24]
        csr = (pr[0:8] + pr[8:16]) + pr[16:24]
        cs = jnp.where(rowid < SSD_HPG, csf, csr)
        tot = jnp.where(rowid[:, 0:1] < SSD_HPG, cs[:, t - 1:t], cs[:, 0:1])
        e_in = jnp.exp(cs)
        e_out = dt * jnp.exp(tot - cs)
        tile = jnp.concatenate([cs, dt, e_in, e_out, jnp.zeros((t - 32, t), f32)], axis=0)
        row_s[c] = tile[0:16]
        col_s[pl.ds(pl.multiple_of(c * t, t), t), :] = tile.T
        decb = jnp.broadcast_to(jnp.exp(tot), (8, 128))
        for d in range(2):
            r = SSD_HPG * d
            halves = [jnp.where(lane8 < SSD_HEAD_DIM,
                                jnp.broadcast_to(decb[r + 2 * h:r + 2 * h + 1], (8, 128)),
                                jnp.broadcast_to(decb[r + 2 * h + 1:r + 2 * h + 2], (8, 128))) for h in range(2)]
            dec_s[2 * c + d] = jnp.concatenate(halves, axis=1)
        return carry

    lax.fori_loop(0, nc, decay_chunk, 0)

    lower = i1 <= i0
    upper = i1 >= i0
    lane_blk = lax.broadcasted_iota(jnp.int32, (t, SSD_GW), 1) // SSD_HEAD_DIM
    neg_inf = jnp.float32(-jnp.inf)

    def diag_chunk(c, carry):
        r0 = pl.multiple_of(c * t, t)
        g = lax.dot_general(c_s[pl.ds(r0, t), :], b_s[pl.ds(r0, t), :], _NT, preferred_element_type=f32)
        rows = row_s[c]
        cols = col_s[pl.ds(r0, t), :]
        xs = xs_s[pl.ds(r0, t), :]
        xs_bf = xs.astype(bf)
        y = dsk_ref[0] * xs
        for k in range(SSD_HPG):
            kr = SSD_HPG + k
            lf = jnp.exp(jnp.where(lower, cols[:, k:k + 1] - rows[k:k + 1, :], neg_inf)) * rows[8 + k:9 + k, :]
            lr = jnp.exp(jnp.where(upper, cols[:, kr:kr + 1] - rows[kr:kr + 1, :], neg_inf)) * rows[8 + kr:9 + kr, :]
            mk = (g * (lf + lr)).astype(bf)
            xk = jnp.where(lane_blk == k, xs_bf, jnp.zeros_like(xs_bf))
            y = y + jnp.dot(mk, xk, preferred_element_type=f32)
        y_s[pl.ds(r0, t), :] = y
        return carry

    lax.fori_loop(0, nc, diag_chunk, 0)

    st_s[...] = jnp.zeros(st_s.shape, f32)

    def state_step(c, d):
        r0 = pl.multiple_of(c * t, t)
        cols = col_s[pl.ds(r0, t), :]
        st = st_s[d]
        y_off = jnp.dot(c_s[pl.ds(r0, t), :], st.astype(bf), preferred_element_type=f32)
        y_s[pl.ds(r0, t), :] = y_s[pl.ds(r0, t), :] + y_off * _head_lanes(cols, 16 + SSD_HPG * d)
        w = (xs_s[pl.ds(r0, t), :] * _head_lanes(cols, 24 + SSD_HPG * d)).astype(bf)
        st_s[d] = st * dec_s[2 * c + d][0:1, :] + jnp.dot(bt_s[c], w, preferred_element_type=f32)

    def ctx_step(i, carry):
        state_step(i, 0)
        state_step(nc_c - 1 - i, 1)
        return carry

    def lat_step(i, carry):
        state_step(nc_c + i, 0)
        state_step(nc - 1 - i, 1)
        return carry

    lax.fori_loop(0, nc_c, ctx_step, 0)
    lax.fori_loop(0, nc_l, lat_step, 0)

    yc_ref[0] = y_s[0:ctx_len, :]
    yl_ref[0] = y_s[ctx_len:ctx_len + seq, :]


def _rope_tables(seq):
    quarter = SSD_STATE // 4
    inv = ROPE_BASE ** (-np.arange(quarter, dtype=np.float64) / quarter)
    pos = np.arange(seq)
    ang_r = (pos // GRID_W)[:, None] * inv
    ang_c = (pos % GRID_W)[:, None] * inv
    cos = np.concatenate([np.cos(ang_r)] * 2 + [np.cos(ang_c)] * 2, axis=1)
    sin = np.concatenate([-np.sin(ang_r), np.sin(ang_r), -np.sin(ang_c), np.sin(ang_c)], axis=1)
    return jnp.asarray(cos, jnp.float32), jnp.asarray(sin, jnp.float32)


def ssd_mixer(p_lat, p_ctx, dt_lat, dt_ctx, conv_w, conv_b, dt_bias, a_log, d_skip):
    bsz, seq, _ = p_lat.shape
    ctx_len = p_ctx.shape[1]
    t, g, hpg = SSD_CHUNK, SSD_GROUPS, SSD_HPG
    assert seq % t == 0 and ctx_len % t == 0
    nc = (seq + ctx_len) // t
    dt = jnp.concatenate([dt_ctx[..., :DT_WIDTH], dt_lat[..., :DT_WIDTH]], axis=1)
    dt = dt.reshape(bsz, nc, t, 2, g, hpg).transpose(0, 4, 1, 3, 5, 2).reshape(bsz, g, nc, 2 * hpg, t)
    per_row = lambda v: jnp.broadcast_to(
        v.astype(jnp.float32).reshape(2, g, hpg).transpose(1, 0, 2).reshape(g, 2 * hpg, 1), (g, 2 * hpg, 128))
    dsk = jnp.repeat(d_skip.astype(jnp.float32).reshape(g, 1, hpg), SSD_HEAD_DIM, axis=2)
    cos, sin = _rope_tables(seq)
    x0 = sum(MAIN_SPLITS[:6]) // SSD_GW
    b0 = (sum(MAIN_SPLITS[:6]) + SSD_WIDTH) // 128
    c0 = b0 + g
    cw = conv_w.astype(jnp.float32)
    cb = conv_b.astype(jnp.float32).reshape(1, SSD_CONV_CH)
    wb0 = SSD_WIDTH // 128
    ltot = seq + ctx_len
    f32, bf = jnp.float32, jnp.bfloat16
    y_lat, y_ctx = pl.pallas_call(
        partial(_ssd_kernel, seq=seq, ctx_len=ctx_len),
        grid=(bsz, g),
        in_specs=[pl.BlockSpec((1, seq, SSD_GW), lambda b, i: (b, 0, x0 + i)),
                  pl.BlockSpec((1, seq, 128), lambda b, i: (b, 0, b0 + i)),
                  pl.BlockSpec((1, seq, 128), lambda b, i: (b, 0, c0 + i)),
                  pl.BlockSpec((1, ctx_len, SSD_GW), lambda b, i: (b, 0, x0 + i)),
                  pl.BlockSpec((1, ctx_len, 128), lambda b, i: (b, 0, b0 + i)),
                  pl.BlockSpec((1, ctx_len, 128), lambda b, i: (b, 0, c0 + i)),
                  pl.BlockSpec((1, 1, nc, 2 * hpg, t), lambda b, i: (b, i, 0, 0, 0)),
                  pl.BlockSpec((1, 2 * hpg, 128), lambda b, i: (i, 0, 0)),
                  pl.BlockSpec((1, 2 * hpg, 128), lambda b, i: (i, 0, 0)),
                  pl.BlockSpec((1, 1, SSD_GW), lambda b, i: (i, 0, 0)),
                  pl.BlockSpec((SSD_CONV, SSD_GW), lambda b, i: (0, i)),
                  pl.BlockSpec((SSD_CONV, 128), lambda b, i: (0, wb0 + i)),
                  pl.BlockSpec((SSD_CONV, 128), lambda b, i: (0, wb0 + g + i)),
                  pl.BlockSpec((1, SSD_GW), lambda b, i: (0, i)),
                  pl.BlockSpec((1, 128), lambda b, i: (0, wb0 + i)),
                  pl.BlockSpec((1, 128), lambda b, i: (0, wb0 + g + i)),
                  pl.BlockSpec((seq, 128), lambda b, i: (0, 0)),
                  pl.BlockSpec((seq, 128), lambda b, i: (0, 0))],
        out_specs=[pl.BlockSpec((1, seq, SSD_GW), lambda b, i: (b, 0, i)),
                   pl.BlockSpec((1, ctx_len, SSD_GW), lambda b, i: (b, 0, i))],
        out_shape=[jax.ShapeDtypeStruct((bsz, seq, SSD_WIDTH), f32),
                   jax.ShapeDtypeStruct((bsz, ctx_len, SSD_WIDTH), f32)],
        scratch_shapes=[pltpu.VMEM((seq + 2 * CONV_PAD, SSD_GW), f32),
                        pltpu.VMEM((seq + 2 * CONV_PAD, 128), f32),
                        pltpu.VMEM((seq + 2 * CONV_PAD, 128), f32),
                        pltpu.VMEM((ltot, SSD_GW), f32),
                        pltpu.VMEM((ltot, 128), bf),
                        pltpu.VMEM((ltot, 128), bf),
                        pltpu.VMEM((nc, 128, t), bf),
                        pltpu.VMEM((nc, 16, t), f32),
                        pltpu.VMEM((ltot, 128), f32),
                        pltpu.VMEM((2 * nc, 8, SSD_GW), f32),
                        pltpu.VMEM((ltot, SSD_GW), f32),
                        pltpu.VMEM((2, SSD_STATE, SSD_GW), f32)],
        compiler_params=pltpu.CompilerParams(dimension_semantics=("arbitrary", "arbitrary"),
                                             vmem_limit_bytes=V7X_VMEM_LIMIT_BYTES),
        name="ssd_mixer",
    )(p_lat, p_lat, p_lat, p_ctx, p_ctx, p_ctx, dt, per_row(dt_bias), per_row(a_log), dsk,
      cw, cw, cw, cb, cb, cb, cos, sin)
    return y_ctx, y_lat


def _axial_rope(t):
    L, n = t.shape[1], t.shape[-1]
    quarter = n // 4
    pos = jnp.arange(L)
    inv = ROPE_BASE ** (-jnp.arange(quarter, dtype=jnp.float32) / quarter)

    def rot(u, p):
        ang = p.astype(jnp.float32)[:, None] * inv
        cos = jnp.cos(ang)[None, :, None, :].astype(u.dtype)
        sin = jnp.sin(ang)[None, :, None, :].astype(u.dtype)
        u1, u2 = u[..., :quarter], u[..., quarter:]
        return jnp.concatenate([u1 * cos - u2 * sin, u1 * sin + u2 * cos], axis=-1)

    return jnp.concatenate([rot(t[..., :2 * quarter], pos // GRID_W),
                            rot(t[..., 2 * quarter:], pos % GRID_W)], axis=-1)


def _dwconv(u, w, b):
    k = w.shape[0]
    y = lax.conv_general_dilated(u, w[:, None, :], window_strides=(1,), padding=[(k // 2, k // 2)],
                                 dimension_numbers=("NWC", "WIO", "NWC"), feature_group_count=u.shape[-1])
    return y + b


def _segsum(a):
    t = a.shape[-1]
    cs = jnp.cumsum(a, axis=-1)
    d = cs[..., :, None] - cs[..., None, :]
    return jnp.where(jnp.tril(jnp.ones((t, t), dtype=bool)), d, -jnp.inf)


def _ssd_chunked(xdt, a, bm, cm, init):
    b, L, g, k, p = xdt.shape
    nc = L // SSD_CHUNK
    x = xdt.reshape(b, nc, SSD_CHUNK, g, k, p)
    A = a.astype(jnp.float32).reshape(b, nc, SSD_CHUNK, g, k).transpose(0, 3, 4, 1, 2)
    Bc = bm.reshape(b, nc, SSD_CHUNK, g, -1)
    Cc = cm.reshape(b, nc, SSD_CHUNK, g, -1)
    a_cs = jnp.cumsum(A, axis=-1)
    lmat = jnp.exp(_segsum(A))
    y_diag = jnp.einsum("bclgn,bcsgn,bgkcls,bcsgkp->bclgkp", Cc, Bc, lmat, x)
    decay_states = jnp.exp(a_cs[..., -1:] - a_cs)
    states = jnp.einsum("bclgn,bgkcl,bclgkp->bcgkpn", Bc, decay_states, x)
    states = jnp.concatenate([init[:, None].astype(states.dtype), states], axis=1)
    chunk_decay = jnp.exp(_segsum(jnp.pad(a_cs[..., -1], ((0, 0), (0, 0), (0, 0), (1, 0)))))
    states = jnp.einsum("bgkzc,bcgkpn->bzgkpn", chunk_decay, states)
    y_off = jnp.einsum("bclgn,bcgkpn,bgkcl->bclgkp", Cc, states[:, :-1], jnp.exp(a_cs))
    y = (y_diag + y_off).reshape(b, L, g, k, p)
    return y, states[:, -1]


def _ssd_mixer(xbc_c, dt_c, xbc_l, dt_l, conv_w, conv_b, dt_bias, a_log, d_skip):
    out_dtype = xbc_l.dtype

    def prep(xbc, dt_raw, rope):
        u = jax.nn.silu(_dwconv(xbc, conv_w, conv_b))
        xs, bm, cm = jnp.split(u, [SSD_WIDTH, SSD_WIDTH + SSD_GROUPS * SSD_STATE], axis=-1)
        b, L = xs.shape[:2]
        xs = xs.reshape(b, L, SSD_GROUPS, SSD_HPG, SSD_HEAD_DIM)
        bm = bm.reshape(b, L, SSD_GROUPS, SSD_STATE)
        cm = cm.reshape(b, L, SSD_GROUPS, SSD_STATE)
        if rope:
            bm, cm = _axial_rope(bm), _axial_rope(cm)
        dt = jax.nn.softplus(dt_raw.astype(jnp.float32).reshape(b, L, 2, SSD_GROUPS, SSD_HPG)
                             + dt_bias.astype(jnp.float32).reshape(2, SSD_GROUPS, SSD_HPG))
        return xs, bm, cm, dt

    xc, bc, cc, dtc = prep(xbc_c, dt_c, False)
    xl, bl, cl, dtl = prep(xbc_l, dt_l, True)
    a = -jnp.exp(a_log.astype(jnp.float32)).reshape(2, SSD_GROUPS, SSD_HPG)
    dsk = d_skip.astype(jnp.float32).reshape(SSD_GROUPS, SSD_HPG, 1)
    b = xl.shape[0]
    init = jnp.zeros((b, SSD_GROUPS, SSD_HPG, SSD_HEAD_DIM, SSD_STATE), jnp.float32)
    yc = dsk * xc.astype(jnp.float32)
    yl = dsk * xl.astype(jnp.float32)
    for direction in range(2):
        flip = (lambda t: t[:, ::-1]) if direction == 1 else (lambda t: t)

        def run(xs, bm, cm, dt, s0):
            dtd = dt[:, :, direction]
            y, s = _ssd_chunked(flip(xs * dtd[..., None]), flip(dtd * a[direction]), flip(bm), flip(cm), s0)
            return flip(y), s

        y_c, s_c = run(xc, bc, cc, dtc, init)
        y_l, _ = run(xl, bl, cl, dtl, s_c)
        yc = yc + y_c
        yl = yl + y_l
    return (yc.reshape(yc.shape[0], yc.shape[1], SSD_WIDTH).astype(out_dtype),
            yl.reshape(b, yl.shape[1], SSD_WIDTH).astype(out_dtype))


def _na_latent(q, k, v, k_ctx, v_ctx, rpb):
    b, L, h, d = q.shape
    rows = L // GRID_W
    win_r = min(NA_WIN_R, rows)
    nblk = GRID_W // NA_QBLK
    scale = d ** -0.5
    qcol = np.arange(GRID_W).reshape(nblk, NA_QBLK)
    qstart = np.clip(qcol - NA_WIN_C // 2, 0, GRID_W - NA_WIN_C)
    kstart = np.clip(qcol[:, 0] - NA_WIN_C // 2, 0, GRID_W - NA_KBLK)
    kcol = (kstart[:, None] + np.arange(NA_KBLK)[None]).astype(np.int32)
    col_mask = ((kcol[:, None, :] >= qstart[..., None]) &
                (kcol[:, None, :] < qstart[..., None] + NA_WIN_C))
    dc = np.clip(kcol[:, None, :] - qcol[..., None] + NA_WIN_C - 1, 0, 2 * NA_WIN_C - 2).astype(np.int32)
    kg = k.reshape(b, rows, GRID_W, h, d)
    vg = v.reshape(b, rows, GRID_W, h, d)
    qg = q.reshape(b, rows, nblk, NA_QBLK, h, d).transpose(1, 0, 2, 3, 4, 5)
    rpb32 = rpb.astype(jnp.float32)
    nwin = win_r * NA_KBLK

    def row_block(args):
        r, q_r = args
        start = jnp.clip(r - win_r // 2, 0, rows - win_r)
        k_r = lax.dynamic_slice_in_dim(kg, start, win_r, axis=1)[:, :, kcol]
        v_r = lax.dynamic_slice_in_dim(vg, start, win_r, axis=1)[:, :, kcol]
        dr = start + jnp.arange(win_r) - r + NA_WIN_R - 1
        bias = rpb32[:, dr][:, :, dc].transpose(0, 2, 3, 1, 4)
        s_w = jnp.einsum("bjqhd,bwjkhd->bhjqwk", q_r, k_r).astype(jnp.float32) * scale + bias[None]
        s_w = jnp.where(col_mask[:, :, None, :], s_w, -1e30)
        s_c = jnp.einsum("bjqhd,bchd->bhjqc", q_r, k_ctx).astype(jnp.float32) * scale
        s = jnp.concatenate([s_w.reshape(b, h, nblk, NA_QBLK, nwin), s_c], axis=-1)
        p = jax.nn.softmax(s, axis=-1).astype(v.dtype)
        p_w = p[..., :nwin].reshape(b, h, nblk, NA_QBLK, win_r, NA_KBLK)
        return (jnp.einsum("bhjqwk,bwjkhd->bjqhd", p_w, v_r)
                + jnp.einsum("bhjqc,bchd->bjqhd", p[..., nwin:], v_ctx))

    o = lax.map(row_block, (jnp.arange(rows), qg))
    return o.transpose(1, 0, 2, 3, 4, 5).reshape(b, L, h * d)


def _ctx_attention(q, k, v):
    b, n, h, d = q.shape
    s = jnp.einsum("bqhd,bkhd->bhqk", q, k).astype(jnp.float32) * (d ** -0.5)
    p = jax.nn.softmax(s, axis=-1).astype(v.dtype)
    return jnp.einsum("bhqk,bkhd->bqhd", p, v).reshape(b, n, h * d)


def _fourier_real(u):
    b, L, _ = u.shape
    ug = u.astype(jnp.float32).reshape(b, L, FOUR_GROUPS, FOUR_GROUP_DIM)
    f = jnp.fft.fft2(ug, axes=(1, 3), norm="ortho").real
    return f.reshape(b, L, FOUR_WIDTH).astype(u.dtype)


def _heads(t):
    return t.reshape(t.shape[0], t.shape[1], NA_HEADS, NA_HEAD_DIM)


def _split_main(p):
    offs = np.cumsum(MAIN_SPLITS)[:-1].tolist()
    return jnp.split(p, offs, axis=-1)


def _mm3(a, w, tm=1024, tn=1024):
    b, L, k = a.shape
    y = matmul(a.reshape(b * L, k), w, tm=min(tm, b * L), tn=tn)
    return y.reshape(b, L, -1)


def _layer(xc, xl, c, c_ctx, w_ada, b_ada, norm_w, w_in, rpb, four_w, conv_w, conv_b, dt_bias, a_log, d_skip,
           ssd_norm_w, wb_na, wb_four, wb_ssd, w_out, update_ctx):
    bsz, seq, d = xl.shape
    ctx_len = xc.shape[1]
    bf = jnp.bfloat16

    cc = jnp.concatenate([c, c_ctx[None], jnp.zeros((16 - bsz - 1, d), c.dtype)], axis=0)
    mod = matmul(jax.nn.silu(cc), w_ada.astype(bf), tm=16, tn=1024) + b_ada
    sh, sc, gt = jnp.split(mod, 3, axis=-1)
    sh_l, sc_l, g_l = sh[:bsz], sc[:bsz], gt[:bsz]
    sh_c, sc_c, g_c = sh[bsz:bsz + 1], sc[bsz:bsz + 1], gt[bsz]

    w_main = jnp.concatenate([w_in[:, :DT_OFFSET], w_in[:, DT_OFFSET + DT_WIDTH:]], axis=1).astype(bf)
    w_dt = jnp.pad(w_in[:, DT_OFFSET:DT_OFFSET + DT_WIDTH], ((0, 0), (0, DT_PAD - DT_WIDTH))).astype(bf)

    xl2d = xl.reshape(bsz * seq, d)
    xc2d = xc.reshape(bsz * ctx_len, d)
    pl2d, dt_l = in_projection(xl2d, norm_w, sc_l[:, None], sh_l[:, None], w_main, w_dt,
                               rows_per_mod=seq, tm=1024, tn=1536, out_dtype=jnp.float32)
    pc2d, dt_c = in_projection(xc2d, norm_w, sc_c[:, None], sh_c[:, None], w_main, w_dt,
                               rows_per_mod=bsz * ctx_len, tm=1024, tn=1536, out_dtype=jnp.float32)
    p_l = pl2d.reshape(bsz, seq, MAIN_WIDTH)
    p_c = pc2d.reshape(bsz, ctx_len, MAIN_WIDTH)
    dt_l = dt_l.reshape(bsz, seq, DT_PAD)
    dt_c = dt_c.reshape(bsz, ctx_len, DT_PAD)

    four_wb, wb_nab, wb_fourb, wb_ssdb, w_outb = (t.astype(bf) for t in (four_w, wb_na, wb_four, wb_ssd, w_out))

    a_l, a_c = na_attention(p_l, p_c, na_bias_table(rpb), with_ctx=update_ctx, out_dtype=bf)
    gfo_l = fourier_branch(p_l, four_wb, tm=512)
    s_c, s_l = ssd_mixer(p_l, p_c, dt_l, dt_c, conv_w, conv_b, dt_bias, a_log, d_skip)
    m_l = merge_branches(pl2d, a_l.reshape(bsz * seq, NA_WIDTH), gfo_l.reshape(bsz * seq, FOUR_WIDTH),
                         s_l.reshape(bsz * seq, SSD_WIDTH), ssd_norm_w, wb_nab, wb_fourb, wb_ssdb, tm=512)
    xl_new = out_projection(m_l, w_outb, xl2d, g_l[:, None], rows_per_mod=seq, tm=1024, tn=1024)
    xl_new = xl_new.reshape(bsz, seq, d)
    if update_ctx:
        gfo_c = fourier_branch(p_c, four_wb, tm=ctx_len)
        m_c = merge_branches(pc2d, a_c.reshape(bsz * ctx_len, NA_WIDTH), gfo_c.reshape(bsz * ctx_len, FOUR_WIDTH),
                             s_c.reshape(bsz * ctx_len, SSD_WIDTH), ssd_norm_w, wb_nab, wb_fourb, wb_ssdb, tm=512)
        xc = out_projection(m_c, w_outb, xc2d, g_c[None, None], rows_per_mod=bsz * ctx_len, tm=1024, tn=1024)
        xc = xc.reshape(bsz, ctx_len, d)
    return xc, xl_new


def kernel(x, c, ctx, c_ctx, w_ada, b_ada, norm_w, w_in, na_rpb, four_w, ssd_conv_w, ssd_conv_b, ssd_dt_bias,
           ssd_a_log, ssd_d, ssd_norm_w, wb_na, wb_four, wb_ssd, w_out, final_norm_w):
    xc, xl = ctx, x
    for l in range(DEPTH):
        xc, xl = _layer(xc, xl, c, c_ctx, w_ada[l], b_ada[l], norm_w[l], w_in[l], na_rpb[l], four_w[l],
                        ssd_conv_w[l], ssd_conv_b[l], ssd_dt_bias[l], ssd_a_log[l], ssd_d[l], ssd_norm_w[l],
                        wb_na[l], wb_four[l], wb_ssd[l], w_out[l], update_ctx=(l < DEPTH - 1))
    return _rmsnorm(xl, final_norm_w)
```

```python
import math
from functools import partial

import jax
import jax.numpy as jnp
import numpy as np
from jax import lax
from jax.experimental import pallas as pl
from jax.experimental.pallas import tpu as pltpu

D_MODEL = 2048
DEPTH = 2
GRID_W = 64
EPS = 1e-6
NA_HEADS = 8
NA_HEAD_DIM = 128
NA_WIDTH = NA_HEADS * NA_HEAD_DIM
NA_WIN_R = 8
NA_WIN_C = 16
NA_QBLK = 16
NA_KBLK = 32
FOUR_GROUPS = 4
FOUR_GROUP_DIM = 256
FOUR_WIDTH = FOUR_GROUPS * FOUR_GROUP_DIM
SSD_HEADS = 16
SSD_HEAD_DIM = 64
SSD_WIDTH = SSD_HEADS * SSD_HEAD_DIM
SSD_GROUPS = 4
SSD_HPG = SSD_HEADS // SSD_GROUPS
SSD_STATE = 128
SSD_CONV = 7
SSD_CHUNK = 128
SSD_CONV_CH = SSD_WIDTH + 2 * SSD_GROUPS * SSD_STATE
ROPE_BASE = 10000.0
DT_WIDTH = 2 * SSD_HEADS
MAIN_SPLITS = (NA_WIDTH, NA_WIDTH, NA_WIDTH, NA_WIDTH, FOUR_WIDTH, FOUR_WIDTH, SSD_CONV_CH, SSD_WIDTH,
               D_MODEL, D_MODEL, D_MODEL)
MAIN_WIDTH = sum(MAIN_SPLITS)
DT_OFFSET = sum(MAIN_SPLITS[:8])
DT_PAD = 128

V7X_VMEM_LIMIT_BYTES = 56 * 1024 * 1024


def _mm_kernel(a_ref, b_ref, o_ref):
    a = a_ref[...].astype(jnp.bfloat16)
    b = b_ref[...].astype(jnp.bfloat16)
    o_ref[...] = jnp.dot(a, b, preferred_element_type=jnp.float32).astype(o_ref.dtype)


def matmul(a, b, *, tm, tn, out_dtype=jnp.float32):
    m, k = a.shape
    _, n = b.shape
    assert m % tm == 0 and n % tn == 0, (a.shape, b.shape, tm, tn)
    return pl.pallas_call(
        _mm_kernel,
        grid=(m // tm, n // tn),
        in_specs=[pl.BlockSpec((tm, k), lambda i, j: (i, 0)),
                  pl.BlockSpec((k, tn), lambda i, j: (0, j))],
        out_specs=pl.BlockSpec((tm, tn), lambda i, j: (i, j)),
        out_shape=jax.ShapeDtypeStruct((m, n), out_dtype),
        compiler_params=pltpu.CompilerParams(dimension_semantics=("arbitrary", "arbitrary"),
                                             vmem_limit_bytes=V7X_VMEM_LIMIT_BYTES),
        name="matmul",
    )(a, b)


def _inproj_kernel(x_ref, nw_ref, sc_ref, sh_ref, wa_ref, wg_ref, wdt_ref, o_ref, dt_ref, h_ref, *, n_a):
    j = pl.program_id(1)

    @pl.when(j == 0)
    def _():
        x = x_ref[...]
        y = x * lax.rsqrt(jnp.mean(x * x, axis=-1, keepdims=True) + EPS) * nw_ref[...]
        h = (y * (1.0 + sc_ref[0]) + sh_ref[0]).astype(jnp.bfloat16)
        h_ref[...] = h
        dt_ref[...] = jnp.dot(h, wdt_ref[...], preferred_element_type=jnp.float32)

    @pl.when(j < n_a)
    def _():
        w = wa_ref[...].astype(jnp.bfloat16)
        o_ref[...] = jnp.dot(h_ref[...], w, preferred_element_type=jnp.float32).astype(o_ref.dtype)

    @pl.when(j >= n_a)
    def _():
        o_ref[...] = jnp.dot(h_ref[...], wg_ref[...], preferred_element_type=jnp.float32).astype(o_ref.dtype)


def in_projection(x2d, norm_w, scale, shift, w_in, w_gates, w_dt, *, rows_per_mod, tm, tn, out_dtype):
    m, d = x2d.shape
    assert DT_OFFSET % tn == 0 and w_gates.shape[1] % tn == 0
    n_a, n_g = DT_OFFSET // tn, w_gates.shape[1] // tn
    assert m % tm == 0 and rows_per_mod % tm == 0
    per = rows_per_mod // tm
    return pl.pallas_call(
        partial(_inproj_kernel, n_a=n_a),
        grid=(m // tm, n_a + n_g),
        in_specs=[pl.BlockSpec((tm, d), lambda i, j: (i, 0)),
                  pl.BlockSpec((1, d), lambda i, j: (0, 0)),
                  pl.BlockSpec((1, 1, d), lambda i, j: (i // per, 0, 0)),
                  pl.BlockSpec((1, 1, d), lambda i, j: (i // per, 0, 0)),
                  pl.BlockSpec((d, tn), lambda i, j: (0, jnp.minimum(j, n_a - 1))),
                  pl.BlockSpec((d, tn), lambda i, j: (0, jnp.maximum(j - n_a, 0))),
                  pl.BlockSpec((d, DT_PAD), lambda i, j: (0, 0))],
        out_specs=[pl.BlockSpec((tm, tn), lambda i, j: (i, j)),
                   pl.BlockSpec((tm, DT_PAD), lambda i, j: (i, 0))],
        out_shape=[jax.ShapeDtypeStruct((m, (n_a + n_g) * tn), out_dtype),
                   jax.ShapeDtypeStruct((m, DT_PAD), jnp.float32)],
        scratch_shapes=[pltpu.VMEM((tm, d), jnp.bfloat16)],
        compiler_params=pltpu.CompilerParams(dimension_semantics=("arbitrary", "arbitrary"),
                                             vmem_limit_bytes=V7X_VMEM_LIMIT_BYTES),
        name="in_projection",
    )(x2d, norm_w.reshape(1, d), scale, shift, w_in, w_gates, w_dt)


GRID_ROWS = 32
NA_WIN_TOKENS = NA_WIN_R * GRID_W
_NT = (((1,), (1,)), ((), ()))
NA_ROW_UNROLL = 8


def _na_kernel(*refs, with_ctx):
    if with_ctx:
        q_ref, k_ref, v_ref, kc_ref, vc_ref, bias_ref, qc_ref, o_ref, oc_ref = refs
    else:
        q_ref, k_ref, v_ref, kc_ref, vc_ref, bias_ref, o_ref = refs
    bf = jnp.bfloat16
    scale = NA_HEAD_DIM ** -0.5
    kc = kc_ref[0].astype(bf)
    vc = vc_ref[0].astype(bf)

    def rows(rb, carry):
        idx, scores, probs = [], [], []
        for i in range(NA_ROW_UNROLL):
            r = rb * NA_ROW_UNROLL + i
            start = jnp.clip(r - NA_WIN_R // 2, 0, GRID_ROWS - NA_WIN_R)
            q0 = pl.multiple_of(r * GRID_W, GRID_W)
            k0 = pl.multiple_of(start * GRID_W, GRID_W)
            q = q_ref[0, pl.ds(q0, GRID_W), :].astype(bf)
            kw = k_ref[0, pl.ds(k0, NA_WIN_TOKENS), :].astype(bf)
            s_w = lax.dot_general(q, kw, _NT, preferred_element_type=jnp.float32) * scale + bias_ref[0, r - start]
            s_c = lax.dot_general(q, kc, _NT, preferred_element_type=jnp.float32) * scale
            idx.append((q0, k0))
            scores.append((s_w, s_c))
        for s_w, s_c in scores:
            m = jnp.maximum(jnp.max(s_w, axis=-1, keepdims=True), jnp.max(s_c, axis=-1, keepdims=True))
            e_w = jnp.exp(s_w - m)
            e_c = jnp.exp(s_c - m)
            denom = jnp.sum(e_w, axis=-1, keepdims=True) + jnp.sum(e_c, axis=-1, keepdims=True)
            probs.append((e_w.astype(bf), e_c.astype(bf), denom))
        for (q0, k0), (e_w, e_c, denom) in zip(idx, probs):
            vw = v_ref[0, pl.ds(k0, NA_WIN_TOKENS), :].astype(bf)
            o = (jnp.dot(e_w, vw, preferred_element_type=jnp.float32)
                 + jnp.dot(e_c, vc, preferred_element_type=jnp.float32))
            o_ref[0, pl.ds(q0, GRID_W), :] = (o / denom).astype(o_ref.dtype)
        return carry

    lax.fori_loop(0, GRID_ROWS // NA_ROW_UNROLL, rows, 0)

    if with_ctx:
        s = lax.dot_general(qc_ref[0].astype(bf), kc, _NT, preferred_element_type=jnp.float32) * scale
        e = jnp.exp(s - jnp.max(s, axis=-1, keepdims=True))
        o = jnp.dot(e.astype(bf), vc, preferred_element_type=jnp.float32)
        oc_ref[0] = (o / jnp.sum(e, axis=-1, keepdims=True)).astype(oc_ref.dtype)


def na_bias_table(rpb):
    qcol = np.arange(GRID_W)[:, None]
    kcol = np.arange(GRID_W)[None, :]
    qstart = np.clip(qcol - NA_WIN_C // 2, 0, GRID_W - NA_WIN_C)
    mask = (kcol >= qstart) & (kcol < qstart + NA_WIN_C)
    dc = np.clip(kcol - qcol + NA_WIN_C - 1, 0, 2 * NA_WIN_C - 2)
    pick_col = (np.arange(2 * NA_WIN_C - 1)[:, None, None] == dc[None]).astype(np.float32)
    rows = jnp.stack([rpb.astype(jnp.float32)[:, NA_WIN_R - 1 - dl:2 * NA_WIN_R - 1 - dl] for dl in range(NA_WIN_R)],
                     axis=1)
    t = jnp.einsum("hdwc,cqk->hdqwk", rows, pick_col, precision=lax.Precision.HIGHEST)
    t = jnp.where(mask[None, None, :, None, :], t, -1e30)
    return t.reshape(rpb.shape[0], NA_WIN_R, GRID_W, NA_WIN_TOKENS)


def na_attention(p_lat, p_ctx, bias, *, with_ctx, out_dtype):
    bsz, seq, _ = p_lat.shape
    ctx_len = p_ctx.shape[1]
    assert seq == GRID_ROWS * GRID_W
    d, h = NA_HEAD_DIM, NA_HEADS
    in_specs = [pl.BlockSpec((1, seq, d), lambda b, i: (b, 0, i)),
                pl.BlockSpec((1, seq, d), lambda b, i: (b, 0, h + i)),
                pl.BlockSpec((1, seq, d), lambda b, i: (b, 0, 2 * h + i)),
                pl.BlockSpec((1, ctx_len, d), lambda b, i: (b, 0, h + i)),
                pl.BlockSpec((1, ctx_len, d), lambda b, i: (b, 0, 2 * h + i)),
                pl.BlockSpec((1, NA_WIN_R, GRID_W, NA_WIN_TOKENS), lambda b, i: (i, 0, 0, 0))]
    args = [p_lat, p_lat, p_lat, p_ctx, p_ctx, bias]
    out_specs = [pl.BlockSpec((1, seq, d), lambda b, i: (b, 0, i))]
    out_shape = [jax.ShapeDtypeStruct((bsz, seq, h * d), out_dtype)]
    if with_ctx:
        in_specs.append(pl.BlockSpec((1, ctx_len, d), lambda b, i: (b, 0, i)))
        args.append(p_ctx)
        out_specs.append(pl.BlockSpec((1, ctx_len, d), lambda b, i: (b, 0, i)))
        out_shape.append(jax.ShapeDtypeStruct((bsz, ctx_len, h * d), out_dtype))
    outs = pl.pallas_call(
        partial(_na_kernel, with_ctx=with_ctx),
        grid=(bsz, h),
        in_specs=in_specs, out_specs=out_specs, out_shape=out_shape,
        compiler_params=pltpu.CompilerParams(dimension_semantics=("arbitrary", "arbitrary"),
                                             vmem_limit_bytes=V7X_VMEM_LIMIT_BYTES),
        name="na_attention",
    )(*args)
    return (outs[0], outs[1]) if with_ctx else (outs[0], None)


def _dft_cos_sin(n):
    jk = np.outer(np.arange(n), np.arange(n)) % n
    ang = 2.0 * np.pi * jk / n
    return np.cos(ang) / np.sqrt(n), np.sin(ang) / np.sqrt(n)


def _fourier_kernel(u_ref, w1_ref, a_ref, fw_ref, zf_ref, o_ref, z_ref, *, seq, row_blk):
    bf = jnp.bfloat16
    gd = FOUR_GROUP_DIM

    @pl.when(pl.program_id(1) == 0)
    def _():
        for r0 in range(0, seq, row_blk):
            for g in range(FOUR_GROUPS):
                ug = u_ref[0, r0:r0 + row_blk, g * gd:(g + 1) * gd].astype(bf)
                z = jnp.dot(ug, w1_ref[...], preferred_element_type=jnp.float32)
                z_ref[r0:r0 + row_blk, g * gd:(g + 1) * gd] = z[:, :gd].astype(bf)
                z_ref[seq + r0:seq + r0 + row_blk, g * gd:(g + 1) * gd] = z[:, gd:].astype(bf)

    y = jnp.dot(a_ref[...], z_ref[...], preferred_element_type=jnp.float32)
    f = jnp.dot(y.astype(bf), fw_ref[...], preferred_element_type=jnp.float32)
    zf = zf_ref[0].astype(jnp.float32)
    o_ref[0] = (f * (zf * jax.nn.sigmoid(zf))).astype(o_ref.dtype)


def fourier_branch(p, four_w_bf, *, tm):
    bsz, seq, _ = p.shape
    assert seq % tm == 0
    cc, sc = _dft_cos_sin(FOUR_GROUP_DIM)
    cl, sl = _dft_cos_sin(seq)
    w1 = jnp.asarray(np.concatenate([cc, sc], axis=1), jnp.bfloat16)
    a = jnp.asarray(np.concatenate([cl, -sl], axis=1), jnp.bfloat16)
    fw = FOUR_WIDTH
    return pl.pallas_call(
        partial(_fourier_kernel, seq=seq, row_blk=min(seq, 512)),
        grid=(bsz, seq // tm),
        in_specs=[pl.BlockSpec((1, seq, fw), lambda b, i: (b, 0, 4)),
                  pl.BlockSpec((FOUR_GROUP_DIM, 2 * FOUR_GROUP_DIM), lambda b, i: (0, 0)),
                  pl.BlockSpec((tm, 2 * seq), lambda b, i: (i, 0)),
                  pl.BlockSpec((fw, fw), lambda b, i: (0, 0)),
                  pl.BlockSpec((1, tm, fw), lambda b, i: (b, i, 5))],
        out_specs=pl.BlockSpec((1, tm, fw), lambda b, i: (b, i, 0)),
        out_shape=jax.ShapeDtypeStruct((bsz, seq, fw), jnp.bfloat16),
        scratch_shapes=[pltpu.VMEM((2 * seq, fw), jnp.bfloat16)],
        compiler_params=pltpu.CompilerParams(dimension_semantics=("arbitrary", "arbitrary"),
                                             vmem_limit_bytes=V7X_VMEM_LIMIT_BYTES),
        name="fourier_branch",
    )(p, w1, a, four_w_bf, p)


def _merge_kernel(a_ref, zna_ref, gfo_ref, s_ref, zs_ref, nw_ref, wa_ref, wf_ref, ws_ref,
                  ga_ref, gf_ref, gs_ref, o_ref, ap_ref, sp_ref):
    bf = jnp.bfloat16

    @pl.when(pl.program_id(1) == 0)
    def _():
        zna = zna_ref[...].astype(jnp.float32)
        ap_ref[...] = (a_ref[...].astype(jnp.float32) * (zna * jax.nn.sigmoid(zna))).astype(bf)
        zs = zs_ref[...].astype(jnp.float32)
        t = s_ref[...].astype(jnp.float32) * (zs * jax.nn.sigmoid(zs))
        t = t * lax.rsqrt(jnp.mean(t * t, axis=-1, keepdims=True) + EPS) * nw_ref[...]
        sp_ref[...] = t.astype(bf)

    o_a = jnp.dot(ap_ref[...], wa_ref[...], preferred_element_type=jnp.float32)
    o_f = jnp.dot(gfo_ref[...], wf_ref[...], preferred_element_type=jnp.float32)
    o_s = jnp.dot(sp_ref[...], ws_ref[...], preferred_element_type=jnp.float32)
    m = (jax.nn.sigmoid(ga_ref[...].astype(jnp.float32)) * o_a
         + jax.nn.sigmoid(gf_ref[...].astype(jnp.float32)) * o_f
         + jax.nn.sigmoid(gs_ref[...].astype(jnp.float32)) * o_s)
    o_ref[...] = m.astype(o_ref.dtype)


def merge_branches(p2d, a2d, gfo2d, s2d, ssd_norm_w, wa, wf, ws, *, tm):
    m = p2d.shape[0]
    w, tn = NA_WIDTH, 1024
    assert m % tm == 0 and D_MODEL % tn == 0
    row = lambda i, j: (i, 0)
    wcol = lambda i, j: (0, j)
    return pl.pallas_call(
        _merge_kernel,
        grid=(m // tm, D_MODEL // tn),
        in_specs=[pl.BlockSpec((tm, w), row),
                  pl.BlockSpec((tm, w), lambda i, j: (i, 3)),
                  pl.BlockSpec((tm, w), row),
                  pl.BlockSpec((tm, w), row),
                  pl.BlockSpec((tm, w), lambda i, j: (i, 8)),
                  pl.BlockSpec((1, w), lambda i, j: (0, 0)),
                  pl.BlockSpec((w, tn), wcol), pl.BlockSpec((w, tn), wcol), pl.BlockSpec((w, tn), wcol),
                  pl.BlockSpec((tm, tn), lambda i, j: (i, 9 + j)),
                  pl.BlockSpec((tm, tn), lambda i, j: (i, 11 + j)),
                  pl.BlockSpec((tm, tn), lambda i, j: (i, 13 + j))],
        out_specs=pl.BlockSpec((tm, tn), lambda i, j: (i, j)),
        out_shape=jax.ShapeDtypeStruct((m, D_MODEL), jnp.bfloat16),
        scratch_shapes=[pltpu.VMEM((tm, w), jnp.bfloat16), pltpu.VMEM((tm, w), jnp.bfloat16)],
        compiler_params=pltpu.CompilerParams(dimension_semantics=("arbitrary", "arbitrary"),
                                             vmem_limit_bytes=V7X_VMEM_LIMIT_BYTES),
        name="merge_branches",
    )(a2d, p2d, gfo2d, s2d, p2d, ssd_norm_w.reshape(1, w), wa, wf, ws, p2d, p2d, p2d)


def _outproj_kernel(m_ref, w_ref, x_ref, g_ref, *rest, final_norm):
    y = jnp.dot(m_ref[...], w_ref[...], preferred_element_type=jnp.float32)
    r = x_ref[...] + g_ref[0] * y
    if final_norm:
        nw_ref, o_ref = rest
        o_ref[...] = r * lax.rsqrt(jnp.mean(r * r, axis=-1, keepdims=True) + EPS) * nw_ref[...]
    else:
        rest[0][...] = r


def out_projection(m2d, w_out_bf, x2d, gate, *, rows_per_mod, tm, tn, final_norm_w=None):
    m, d = x2d.shape
    assert m % tm == 0 and d % tn == 0 and rows_per_mod % tm == 0
    per = rows_per_mod // tm
    in_specs = [pl.BlockSpec((tm, d), lambda i, j: (i, 0)),
                pl.BlockSpec((d, tn), lambda i, j: (0, j)),
                pl.BlockSpec((tm, tn), lambda i, j: (i, j)),
                pl.BlockSpec((1, 1, tn), lambda i, j: (i // per, 0, j))]
    args = [m2d, w_out_bf, x2d, gate]
    if final_norm_w is not None:
        assert tn == d
        in_specs.append(pl.BlockSpec((1, d), lambda i, j: (0, 0)))
        args.append(final_norm_w.astype(jnp.float32).reshape(1, d))
    return pl.pallas_call(
        partial(_outproj_kernel, final_norm=final_norm_w is not None),
        grid=(m // tm, d // tn),
        in_specs=in_specs,
        out_specs=pl.BlockSpec((tm, tn), lambda i, j: (i, j)),
        out_shape=jax.ShapeDtypeStruct((m, d), jnp.float32),
        compiler_params=pltpu.CompilerParams(dimension_semantics=("arbitrary", "arbitrary"),
                                             vmem_limit_bytes=V7X_VMEM_LIMIT_BYTES),
        name="out_projection",
    )(*args)


SSD_GW = SSD_HPG * SSD_HEAD_DIM
CONV_PAD = 8


def _split3_bf16(a):
    rnd = lambda v: v.astype(jnp.bfloat16).astype(jnp.float32)
    a1 = rnd(a)
    a2 = rnd(a - a1)
    a3 = rnd((a - a1) - a2)
    return a1, a2, a3


def _head_lanes(cols, i0):
    t = cols.shape[0]
    lo = lax.broadcasted_iota(jnp.int32, (t, 128), 1) < SSD_HEAD_DIM
    pick = lambda i: jnp.broadcast_to(cols[:, i:i + 1], (t, 128))
    return jnp.concatenate([jnp.where(lo, pick(i0), pick(i0 + 1)),
                            jnp.where(lo, pick(i0 + 2), pick(i0 + 3))], axis=1)


def _ssd_kernel(xl_ref, bl_ref, cl_ref, xc_ref, bc_ref, cc_ref, dt_ref, bias_ref, alog_ref, dsk_ref,
                wx_ref, wb_ref, wc_ref, bx_ref, bb_ref, bcb_ref, cos_ref, sin_ref,
                yl_ref, yc_ref,
                padx, padb, padc, xs_s, b_s, c_s, bt_s, row_s, col_s, dec_s, y_s, st_s, *, seq, ctx_len):
    f32, bf = jnp.float32, jnp.bfloat16
    t = SSD_CHUNK
    nc_c, nc_l = ctx_len // t, seq // t
    nc = nc_c + nc_l

    lane = lax.broadcasted_iota(jnp.int32, (t, 128), 1)
    first_half = (lane % 64) < 32

    def conv_tile(pad, w_ref, b_ref, t0, l0):
        acc = jnp.broadcast_to(b_ref[:, l0:l0 + 128], (t, 128))
        win = pad[pl.ds(t0, t + 2 * CONV_PAD), l0:l0 + 128]
        for j in range(SSD_CONV):
            off = CONV_PAD - SSD_CONV // 2 + j
            acc = acc + win[off:off + t] * w_ref[j:j + 1, l0:l0 + 128]
        return acc * jax.nn.sigmoid(acc)

    def rope(u, t0):
        swapped = jnp.where(first_half, pltpu.roll(u, 96, axis=1), pltpu.roll(u, 32, axis=1))
        return u * cos_ref[pl.ds(t0, t), :] + swapped * sin_ref[pl.ds(t0, t), :]

    def conv_seq(x_ref, bm_ref, cm_ref, n, base, use_rope):
        for pad, src in ((padx, x_ref), (padb, bm_ref), (padc, cm_ref)):
            w = pad.shape[1]
            pad[0:CONV_PAD, :] = jnp.zeros((CONV_PAD, w), f32)
            pad[CONV_PAD + n:2 * CONV_PAD + n, :] = jnp.zeros((CONV_PAD, w), f32)
            pad[CONV_PAD:CONV_PAD + n, :] = src[0].astype(f32)

        def chunk(ci, carry):
            t0 = pl.multiple_of(ci * t, t)
            r0 = pl.multiple_of(base + ci * t, t)
            for l0 in range(0, SSD_GW, 128):
                xs_s[pl.ds(r0, t), l0:l0 + 128] = conv_tile(padx, wx_ref, bx_ref, t0, l0)
            ub = conv_tile(padb, wb_ref, bb_ref, t0, 0)
            uc = conv_tile(padc, wc_ref, bcb_ref, t0, 0)
            if use_rope:
                ub, uc = rope(ub, t0), rope(uc, t0)
            b_s[pl.ds(r0, t), :] = ub.astype(bf)
            bt_s[base // t + ci] = ub.T.astype(bf)
            c_s[pl.ds(r0, t), :] = uc.astype(bf)
            return carry

        lax.fori_loop(0, n // t, chunk, 0)

    conv_seq(xc_ref, bc_ref, cc_ref, ctx_len, 0, False)
    conv_seq(xl_ref, bl_ref, cl_ref, seq, ctx_len, True)

    i0 = lax.broadcasted_iota(jnp.int32, (t, t), 0)
    i1 = lax.broadcasted_iota(jnp.int32, (t, t), 1)
    tri_f = (i0 <= i1).astype(bf)
    tri_r = (i0 >= i1).astype(bf)
    nr = 2 * SSD_HPG
    lane8 = lax.broadcasted_iota(jnp.int32, (8, 128), 1)
    fwd_row = (lax.broadcasted_iota(jnp.int32, (nc * nr, 1), 0) % nr) < SSD_HPG
    x = dt_ref[0, 0].reshape(nc * nr, t) + jnp.tile(bias_ref[0], (nc, 1))
    dt = jnp.maximum(x, 0.0) + jnp.log1p(jnp.exp(-jnp.abs(x)))
    pieces = _split3_bf16(dt * jnp.tile(-jnp.exp(alog_ref[0]), (nc, 1)))
    pf = [jnp.dot(p.astype(bf), tri_f, preferred_element_type=f32) for p in pieces]
    pr = [jnp.dot(p.astype(bf), tri_r, preferred_element_type=f32) for p in pieces]
    cs = jnp.where(fwd_row, (pf[0] + pf[1]) + pf[2], (pr[0] + pr[1]) + pr[2])
    tot = jnp.where(fwd_row, cs[:, t - 1:t], cs[:, 0:1])
    e_in = jnp.exp(cs)
    e_out = dt * jnp.exp(tot - cs)
    decb = jnp.broadcast_to(jnp.exp(tot), (nc * nr, 128))
    for c in range(nc):
        sl = slice(c * nr, (c + 1) * nr)
        tile = jnp.concatenate([cs[sl], dt[sl], e_in[sl], e_out[sl], jnp.zeros((t - 4 * nr, t), f32)], axis=0)
        row_s[c] = tile[0:2 * nr]
        col_s[c * t:(c + 1) * t, :] = tile.T
        for d in range(2):
            r = c * nr + SSD_HPG * d
            halves = [jnp.where(lane8 < SSD_HEAD_DIM,
                                jnp.broadcast_to(decb[r + 2 * h:r + 2 * h + 1], (8, 128)),
                                jnp.broadcast_to(decb[r + 2 * h + 1:r + 2 * h + 2], (8, 128))) for h in range(2)]
            dec_s[2 * c + d] = jnp.concatenate(halves, axis=1)

    lower = i1 <= i0
    upper = i1 >= i0
    lane_blk = lax.broadcasted_iota(jnp.int32, (t, SSD_GW), 1) // SSD_HEAD_DIM
    neg_inf = jnp.float32(-jnp.inf)

    def diag_chunk(c, carry):
        r0 = pl.multiple_of(c * t, t)
        g = lax.dot_general(c_s[pl.ds(r0, t), :], b_s[pl.ds(r0, t), :], _NT, preferred_element_type=f32)
        rows = row_s[c]
        cols = col_s[pl.ds(r0, t), :]
        xs = xs_s[pl.ds(r0, t), :]
        xs_bf = xs.astype(bf)
        y = dsk_ref[0] * xs
        for k in range(SSD_HPG):
            kr = SSD_HPG + k
            lf = jnp.exp(jnp.where(lower, cols[:, k:k + 1] - rows[k:k + 1, :], neg_inf)) * rows[8 + k:9 + k, :]
            lr = jnp.exp(jnp.where(upper, cols[:, kr:kr + 1] - rows[kr:kr + 1, :], neg_inf)) * rows[8 + kr:9 + kr, :]
            mk = (g * (lf + lr)).astype(bf)
            xk = jnp.where(lane_blk == k, xs_bf, jnp.zeros_like(xs_bf))
            y = y + jnp.dot(mk, xk, preferred_element_type=f32)
        y_s[pl.ds(r0, t), :] = y
        return carry

    lax.fori_loop(0, nc, diag_chunk, 0)

    st_s[...] = jnp.zeros(st_s.shape, f32)

    def state_step(c, d):
        r0 = pl.multiple_of(c * t, t)
        cols = col_s[pl.ds(r0, t), :]
        st = st_s[d]
        y_off = jnp.dot(c_s[pl.ds(r0, t), :], st.astype(bf), preferred_element_type=f32)
        y_s[pl.ds(r0, t), :] = y_s[pl.ds(r0, t), :] + y_off * _head_lanes(cols, 16 + SSD_HPG * d)
        w = (xs_s[pl.ds(r0, t), :] * _head_lanes(cols, 24 + SSD_HPG * d)).astype(bf)
        st_s[d] = st * dec_s[2 * c + d][0:1, :] + jnp.dot(bt_s[c], w, preferred_element_type=f32)

    def ctx_step(i, carry):
        state_step(i, 0)
        state_step(nc_c - 1 - i, 1)
        return carry

    def lat_step(i, carry):
        state_step(nc_c + i, 0)
        state_step(nc - 1 - i, 1)
        return carry

    lax.fori_loop(0, nc_c, ctx_step, 0)
    lax.fori_loop(0, nc_l, lat_step, 0)

    yc_ref[0] = y_s[0:ctx_len, :]
    yl_ref[0] = y_s[ctx_len:ctx_len + seq, :]


def _rope_tables(seq):
    quarter = SSD_STATE // 4
    inv = ROPE_BASE ** (-np.arange(quarter, dtype=np.float64) / quarter)
    pos = np.arange(seq)
    ang_r = (pos // GRID_W)[:, None] * inv
    ang_c = (pos % GRID_W)[:, None] * inv
    cos = np.concatenate([np.cos(ang_r)] * 2 + [np.cos(ang_c)] * 2, axis=1)
    sin = np.concatenate([-np.sin(ang_r), np.sin(ang_r), -np.sin(ang_c), np.sin(ang_c)], axis=1)
    return jnp.asarray(cos, jnp.float32), jnp.asarray(sin, jnp.float32)


def ssd_mixer(p_lat, p_ctx, dt_lat, dt_ctx, conv_w, conv_b, dt_bias, a_log, d_skip):
    bsz, seq, _ = p_lat.shape
    ctx_len = p_ctx.shape[1]
    t, g, hpg = SSD_CHUNK, SSD_GROUPS, SSD_HPG
    assert seq % t == 0 and ctx_len % t == 0
    nc = (seq + ctx_len) // t
    dt = jnp.concatenate([dt_ctx[..., :DT_WIDTH], dt_lat[..., :DT_WIDTH]], axis=1)
    dt = dt.reshape(bsz, nc, t, 2, g, hpg).transpose(0, 4, 1, 3, 5, 2).reshape(bsz, g, nc, 2 * hpg, t)
    per_row = lambda v: jnp.broadcast_to(
        v.astype(jnp.float32).reshape(2, g, hpg).transpose(1, 0, 2).reshape(g, 2 * hpg, 1), (g, 2 * hpg, 128))
    dsk = jnp.repeat(d_skip.astype(jnp.float32).reshape(g, 1, hpg), SSD_HEAD_DIM, axis=2)
    cos, sin = _rope_tables(seq)
    x0 = sum(MAIN_SPLITS[:6]) // SSD_GW
    b0 = (sum(MAIN_SPLITS[:6]) + SSD_WIDTH) // 128
    c0 = b0 + g
    cw = conv_w.astype(jnp.float32)
    cb = conv_b.astype(jnp.float32).reshape(1, SSD_CONV_CH)
    wb0 = SSD_WIDTH // 128
    ltot = seq + ctx_len
    f32, bf = jnp.float32, jnp.bfloat16
    y_lat, y_ctx = pl.pallas_call(
        partial(_ssd_kernel, seq=seq, ctx_len=ctx_len),
        grid=(bsz, g),
        in_specs=[pl.BlockSpec((1, seq, SSD_GW), lambda b, i: (b, 0, x0 + i)),
                  pl.BlockSpec((1, seq, 128), lambda b, i: (b, 0, b0 + i)),
                  pl.BlockSpec((1, seq, 128), lambda b, i: (b, 0, c0 + i)),
                  pl.BlockSpec((1, ctx_len, SSD_GW), lambda b, i: (b, 0, x0 + i)),
                  pl.BlockSpec((1, ctx_len, 128), lambda b, i: (b, 0, b0 + i)),
                  pl.BlockSpec((1, ctx_len, 128), lambda b, i: (b, 0, c0 + i)),
                  pl.BlockSpec((1, 1, nc, 2 * hpg, t), lambda b, i: (b, i, 0, 0, 0)),
                  pl.BlockSpec((1, 2 * hpg, 128), lambda b, i: (i, 0, 0)),
                  pl.BlockSpec((1, 2 * hpg, 128), lambda b, i: (i, 0, 0)),
                  pl.BlockSpec((1, 1, SSD_GW), lambda b, i: (i, 0, 0)),
                  pl.BlockSpec((SSD_CONV, SSD_GW), lambda b, i: (0, i)),
                  pl.BlockSpec((SSD_CONV, 128), lambda b, i: (0, wb0 + i)),
                  pl.BlockSpec((SSD_CONV, 128), lambda b, i: (0, wb0 + g + i)),
                  pl.BlockSpec((1, SSD_GW), lambda b, i: (0, i)),
                  pl.BlockSpec((1, 128), lambda b, i: (0, wb0 + i)),
                  pl.BlockSpec((1, 128), lambda b, i: (0, wb0 + g + i)),
                  pl.BlockSpec((seq, 128), lambda b, i: (0, 0)),
                  pl.BlockSpec((seq, 128), lambda b, i: (0, 0))],
        out_specs=[pl.BlockSpec((1, seq, SSD_GW), lambda b, i: (b, 0, i)),
                   pl.BlockSpec((1, ctx_len, SSD_GW), lambda b, i: (b, 0, i))],
        out_shape=[jax.ShapeDtypeStruct((bsz, seq, SSD_WIDTH), f32),
                   jax.ShapeDtypeStruct((bsz, ctx_len, SSD_WIDTH), f32)],
        scratch_shapes=[pltpu.VMEM((seq + 2 * CONV_PAD, SSD_GW), f32),
                        pltpu.VMEM((seq + 2 * CONV_PAD, 128), f32),
                        pltpu.VMEM((seq + 2 * CONV_PAD, 128), f32),
                        pltpu.VMEM((ltot, SSD_GW), f32),
                        pltpu.VMEM((ltot, 128), bf),
                        pltpu.VMEM((ltot, 128), bf),
                        pltpu.VMEM((nc, 128, t), bf),
                        pltpu.VMEM((nc, 16, t), f32),
                        pltpu.VMEM((ltot, 128), f32),
                        pltpu.VMEM((2 * nc, 8, SSD_GW), f32),
                        pltpu.VMEM((ltot, SSD_GW), f32),
                        pltpu.VMEM((2, SSD_STATE, SSD_GW), f32)],
        compiler_params=pltpu.CompilerParams(dimension_semantics=("arbitrary", "arbitrary"),
                                             vmem_limit_bytes=V7X_VMEM_LIMIT_BYTES),
        name="ssd_mixer",
    )(p_lat, p_lat, p_lat, p_ctx, p_ctx, p_ctx, dt, per_row(dt_bias), per_row(a_log), dsk,
      cw, cw, cw, cb, cb, cb, cos, sin)
    return y_ctx, y_lat


ADA_ROWS = 16


def _layer(xc, xl, c, c_ctx, w_ada, b_ada, norm_w, w_in, rpb, four_w, conv_w, conv_b, dt_bias, a_log, d_skip,
           ssd_norm_w, wb_na, wb_four, wb_ssd, w_out, update_ctx, final_norm_w):
    bsz, seq, d = xl.shape
    ctx_len = xc.shape[1]
    bf = jnp.bfloat16
    assert bsz < ADA_ROWS

    cc = jnp.concatenate([c, c_ctx[None], jnp.zeros((ADA_ROWS - bsz - 1, d), c.dtype)], axis=0)
    mod = matmul(jax.nn.silu(cc), w_ada, tm=ADA_ROWS, tn=1024) + b_ada
    sh, sc, gt = jnp.split(mod, 3, axis=-1)
    sh_l, sc_l, g_l = sh[:bsz], sc[:bsz], gt[:bsz]
    sh_c, sc_c, g_c = sh[bsz:bsz + 1], sc[bsz:bsz + 1], gt[bsz]

    w_gates = w_in[:, DT_OFFSET + DT_WIDTH:].astype(bf)
    w_dt = jnp.pad(w_in[:, DT_OFFSET:DT_OFFSET + DT_WIDTH], ((0, 0), (0, DT_PAD - DT_WIDTH))).astype(bf)

    xl2d = xl.reshape(bsz * seq, d)
    xc2d = xc.reshape(bsz * ctx_len, d)
    pl2d, dt_l = in_projection(xl2d, norm_w, sc_l[:, None], sh_l[:, None], w_in, w_gates, w_dt,
                               rows_per_mod=seq, tm=1024, tn=1024, out_dtype=bf)
    pc2d, dt_c = in_projection(xc2d, norm_w, sc_c[:, None], sh_c[:, None], w_in, w_gates, w_dt,
                               rows_per_mod=bsz * ctx_len, tm=1024, tn=1024, out_dtype=bf)
    p_l = pl2d.reshape(bsz, seq, MAIN_WIDTH)
    p_c = pc2d.reshape(bsz, ctx_len, MAIN_WIDTH)
    dt_l = dt_l.reshape(bsz, seq, DT_PAD)
    dt_c = dt_c.reshape(bsz, ctx_len, DT_PAD)

    four_wb, wb_nab, wb_fourb, wb_ssdb, w_outb = (t.astype(bf) for t in (four_w, wb_na, wb_four, wb_ssd, w_out))

    a_l, a_c = na_attention(p_l, p_c, na_bias_table(rpb), with_ctx=update_ctx, out_dtype=bf)
    gfo_l = fourier_branch(p_l, four_wb, tm=512)
    s_c, s_l = ssd_mixer(p_l, p_c, dt_l, dt_c, conv_w, conv_b, dt_bias, a_log, d_skip)
    m_l = merge_branches(pl2d, a_l.reshape(bsz * seq, NA_WIDTH), gfo_l.reshape(bsz * seq, FOUR_WIDTH),
                         s_l.reshape(bsz * seq, SSD_WIDTH), ssd_norm_w, wb_nab, wb_fourb, wb_ssdb, tm=512)
    if final_norm_w is None:
        xl_new = out_projection(m_l, w_outb, xl2d, g_l[:, None], rows_per_mod=seq, tm=1024, tn=1024)
    else:
        xl_new = out_projection(m_l, w_outb, xl2d, g_l[:, None], rows_per_mod=seq, tm=512, tn=d,
                                final_norm_w=final_norm_w)
    xl_new = xl_new.reshape(bsz, seq, d)
    if update_ctx:
        gfo_c = fourier_branch(p_c, four_wb, tm=ctx_len)
        m_c = merge_branches(pc2d, a_c.reshape(bsz * ctx_len, NA_WIDTH), gfo_c.reshape(bsz * ctx_len, FOUR_WIDTH),
                             s_c.reshape(bsz * ctx_len, SSD_WIDTH), ssd_norm_w, wb_nab, wb_fourb, wb_ssdb, tm=512)
        xc = out_projection(m_c, w_outb, xc2d, g_c[None, None], rows_per_mod=bsz * ctx_len, tm=1024, tn=1024)
        xc = xc.reshape(bsz, ctx_len, d)
    return xc, xl_new


def kernel(x, c, ctx, c_ctx, w_ada, b_ada, norm_w, w_in, na_rpb, four_w, ssd_conv_w, ssd_conv_b, ssd_dt_bias,
           ssd_a_log, ssd_d, ssd_norm_w, wb_na, wb_four, wb_ssd, w_out, final_norm_w):
    xc, xl = ctx, x
    for l in range(DEPTH):
        xc, xl = _layer(xc, xl, c, c_ctx, w_ada[l], b_ada[l], norm_w[l], w_in[l], na_rpb[l], four_w[l],
                        ssd_conv_w[l], ssd_conv_b[l], ssd_dt_bias[l], ssd_a_log[l], ssd_d[l], ssd_norm_w[l],
                        wb_na[l], wb_four[l], wb_ssd[l], w_out[l], update_ctx=(l < DEPTH - 1),
                        final_norm_w=final_norm_w if l == DEPTH - 1 else None)
    return xl
```

```python
import math
from functools import partial

import jax
import jax.numpy as jnp
import numpy as np
from jax import lax
from jax.experimental import pallas as pl
from jax.experimental.pallas import tpu as pltpu

D_MODEL = 2048
DEPTH = 2
GRID_W = 64
EPS = 1e-6
NA_HEADS = 8
NA_HEAD_DIM = 128
NA_WIDTH = NA_HEADS * NA_HEAD_DIM
NA_WIN_R = 8
NA_WIN_C = 16
NA_QBLK = 16
NA_KBLK = 32
FOUR_GROUPS = 4
FOUR_GROUP_DIM = 256
FOUR_WIDTH = FOUR_GROUPS * FOUR_GROUP_DIM
SSD_HEADS = 16
SSD_HEAD_DIM = 64
SSD_WIDTH = SSD_HEADS * SSD_HEAD_DIM
SSD_GROUPS = 4
SSD_HPG = SSD_HEADS // SSD_GROUPS
SSD_STATE = 128
SSD_CONV = 7
SSD_CHUNK = 128
SSD_CONV_CH = SSD_WIDTH + 2 * SSD_GROUPS * SSD_STATE
ROPE_BASE = 10000.0
DT_WIDTH = 2 * SSD_HEADS
MAIN_SPLITS = (NA_WIDTH, NA_WIDTH, NA_WIDTH, NA_WIDTH, FOUR_WIDTH, FOUR_WIDTH, SSD_CONV_CH, SSD_WIDTH,
               D_MODEL, D_MODEL, D_MODEL)
MAIN_WIDTH = sum(MAIN_SPLITS)
DT_OFFSET = sum(MAIN_SPLITS[:8])
DT_PAD = 128

V7X_VMEM_LIMIT_BYTES = 56 * 1024 * 1024


def _mm_kernel(a_ref, b_ref, o_ref):
    a = a_ref[...].astype(jnp.bfloat16)
    b = b_ref[...].astype(jnp.bfloat16)
    o_ref[...] = jnp.dot(a, b, preferred_element_type=jnp.float32).astype(o_ref.dtype)


def matmul(a, b, *, tm, tn, out_dtype=jnp.float32):
    m, k = a.shape
    _, n = b.shape
    assert m % tm == 0 and n % tn == 0, (a.shape, b.shape, tm, tn)
    return pl.pallas_call(
        _mm_kernel,
        grid=(m // tm, n // tn),
        in_specs=[pl.BlockSpec((tm, k), lambda i, j: (i, 0)),
                  pl.BlockSpec((k, tn), lambda i, j: (0, j))],
        out_specs=pl.BlockSpec((tm, tn), lambda i, j: (i, j)),
        out_shape=jax.ShapeDtypeStruct((m, n), out_dtype),
        compiler_params=pltpu.CompilerParams(dimension_semantics=("arbitrary", "arbitrary"),
                                             vmem_limit_bytes=V7X_VMEM_LIMIT_BYTES),
        name="matmul",
    )(a, b)


def _inproj_kernel(x_ref, nw_ref, sc_ref, sh_ref, w_ref, wdt_ref, o_ref, dt_ref, h_ref):
    @pl.when(pl.program_id(1) == 0)
    def _():
        x = x_ref[...]
        y = x * lax.rsqrt(jnp.mean(x * x, axis=-1, keepdims=True) + EPS) * nw_ref[...]
        h = (y * (1.0 + sc_ref[0]) + sh_ref[0]).astype(jnp.bfloat16)
        h_ref[...] = h
        dt_ref[...] = jnp.dot(h, wdt_ref[...], preferred_element_type=jnp.float32)

    o_ref[...] = jnp.dot(h_ref[...], w_ref[...], preferred_element_type=jnp.float32).astype(o_ref.dtype)


def in_projection(x2d, norm_w, scale, shift, w_main, w_dt, *, rows_per_mod, tm, tn, out_dtype):
    m, d = x2d.shape
    n = w_main.shape[1]
    assert m % tm == 0 and n % tn == 0 and rows_per_mod % tm == 0
    per = rows_per_mod // tm
    return pl.pallas_call(
        _inproj_kernel,
        grid=(m // tm, n // tn),
        in_specs=[pl.BlockSpec((tm, d), lambda i, j: (i, 0)),
                  pl.BlockSpec((1, d), lambda i, j: (0, 0)),
                  pl.BlockSpec((1, 1, d), lambda i, j: (i // per, 0, 0)),
                  pl.BlockSpec((1, 1, d), lambda i, j: (i // per, 0, 0)),
                  pl.BlockSpec((d, tn), lambda i, j: (0, j)),
                  pl.BlockSpec((d, DT_PAD), lambda i, j: (0, 0))],
        out_specs=[pl.BlockSpec((tm, tn), lambda i, j: (i, j)),
                   pl.BlockSpec((tm, DT_PAD), lambda i, j: (i, 0))],
        out_shape=[jax.ShapeDtypeStruct((m, n), out_dtype),
                   jax.ShapeDtypeStruct((m, DT_PAD), jnp.float32)],
        scratch_shapes=[pltpu.VMEM((tm, d), jnp.bfloat16)],
        compiler_params=pltpu.CompilerParams(dimension_semantics=("arbitrary", "arbitrary"),
                                             vmem_limit_bytes=V7X_VMEM_LIMIT_BYTES),
        name="in_projection",
    )(x2d, norm_w.reshape(1, d), scale, shift, w_main, w_dt)


GRID_ROWS = 32
NA_WIN_TOKENS = NA_WIN_R * GRID_W
_NT = (((1,), (1,)), ((), ()))
NA_ROW_UNROLL = 8


def _na_kernel(*refs, with_ctx):
    if with_ctx:
        q_ref, k_ref, v_ref, kc_ref, vc_ref, bias_ref, qc_ref, o_ref, oc_ref = refs
    else:
        q_ref, k_ref, v_ref, kc_ref, vc_ref, bias_ref, o_ref = refs
    bf = jnp.bfloat16
    scale = NA_HEAD_DIM ** -0.5
    kc = kc_ref[0].astype(bf)
    vc = vc_ref[0].astype(bf)

    def rows(rb, carry):
        idx, scores, probs = [], [], []
        for i in range(NA_ROW_UNROLL):
            r = rb * NA_ROW_UNROLL + i
            start = jnp.clip(r - NA_WIN_R // 2, 0, GRID_ROWS - NA_WIN_R)
            q0 = pl.multiple_of(r * GRID_W, GRID_W)
            k0 = pl.multiple_of(start * GRID_W, GRID_W)
            q = q_ref[0, pl.ds(q0, GRID_W), :].astype(bf)
            kw = k_ref[0, pl.ds(k0, NA_WIN_TOKENS), :].astype(bf)
            s_w = lax.dot_general(q, kw, _NT, preferred_element_type=jnp.float32) * scale + bias_ref[0, r - start]
            s_c = lax.dot_general(q, kc, _NT, preferred_element_type=jnp.float32) * scale
            idx.append((q0, k0))
            scores.append((s_w, s_c))
        for s_w, s_c in scores:
            m = jnp.maximum(jnp.max(s_w, axis=-1, keepdims=True), jnp.max(s_c, axis=-1, keepdims=True))
            e_w = jnp.exp(s_w - m)
            e_c = jnp.exp(s_c - m)
            denom = jnp.sum(e_w, axis=-1, keepdims=True) + jnp.sum(e_c, axis=-1, keepdims=True)
            probs.append((e_w.astype(bf), e_c.astype(bf), denom))
        for (q0, k0), (e_w, e_c, denom) in zip(idx, probs):
            vw = v_ref[0, pl.ds(k0, NA_WIN_TOKENS), :].astype(bf)
            o = (jnp.dot(e_w, vw, preferred_element_type=jnp.float32)
                 + jnp.dot(e_c, vc, preferred_element_type=jnp.float32))
            o_ref[0, pl.ds(q0, GRID_W), :] = (o / denom).astype(o_ref.dtype)
        return carry

    lax.fori_loop(0, GRID_ROWS // NA_ROW_UNROLL, rows, 0)

    if with_ctx:
        s = lax.dot_general(qc_ref[0].astype(bf), kc, _NT, preferred_element_type=jnp.float32) * scale
        e = jnp.exp(s - jnp.max(s, axis=-1, keepdims=True))
        o = jnp.dot(e.astype(bf), vc, preferred_element_type=jnp.float32)
        oc_ref[0] = (o / jnp.sum(e, axis=-1, keepdims=True)).astype(oc_ref.dtype)


def na_bias_table(rpb):
    qcol = np.arange(GRID_W)[:, None]
    kcol = np.arange(GRID_W)[None, :]
    qstart = np.clip(qcol - NA_WIN_C // 2, 0, GRID_W - NA_WIN_C)
    mask = (kcol >= qstart) & (kcol < qstart + NA_WIN_C)
    dc = np.clip(kcol - qcol + NA_WIN_C - 1, 0, 2 * NA_WIN_C - 2)
    pick_col = (np.arange(2 * NA_WIN_C - 1)[:, None, None] == dc[None]).astype(np.float32)
    rows = jnp.stack([rpb.astype(jnp.float32)[:, NA_WIN_R - 1 - dl:2 * NA_WIN_R - 1 - dl] for dl in range(NA_WIN_R)],
                     axis=1)
    t = jnp.einsum("hdwc,cqk->hdqwk", rows, pick_col, precision=lax.Precision.HIGHEST)
    t = jnp.where(mask[None, None, :, None, :], t, -1e30)
    return t.reshape(rpb.shape[0], NA_WIN_R, GRID_W, NA_WIN_TOKENS)


def na_attention(p_lat, p_ctx, bias, *, with_ctx, out_dtype):
    bsz, seq, _ = p_lat.shape
    ctx_len = p_ctx.shape[1]
    assert seq == GRID_ROWS * GRID_W
    d, h = NA_HEAD_DIM, NA_HEADS
    in_specs = [pl.BlockSpec((1, seq, d), lambda b, i: (b, 0, i)),
                pl.BlockSpec((1, seq, d), lambda b, i: (b, 0, h + i)),
                pl.BlockSpec((1, seq, d), lambda b, i: (b, 0, 2 * h + i)),
                pl.BlockSpec((1, ctx_len, d), lambda b, i: (b, 0, h + i)),
                pl.BlockSpec((1, ctx_len, d), lambda b, i: (b, 0, 2 * h + i)),
                pl.BlockSpec((1, NA_WIN_R, GRID_W, NA_WIN_TOKENS), lambda b, i: (i, 0, 0, 0))]
    args = [p_lat, p_lat, p_lat, p_ctx, p_ctx, bias]
    out_specs = [pl.BlockSpec((1, seq, d), lambda b, i: (b, 0, i))]
    out_shape = [jax.ShapeDtypeStruct((bsz, seq, h * d), out_dtype)]
    if with_ctx:
        in_specs.append(pl.BlockSpec((1, ctx_len, d), lambda b, i: (b, 0, i)))
        args.append(p_ctx)
        out_specs.append(pl.BlockSpec((1, ctx_len, d), lambda b, i: (b, 0, i)))
        out_shape.append(jax.ShapeDtypeStruct((bsz, ctx_len, h * d), out_dtype))
    outs = pl.pallas_call(
        partial(_na_kernel, with_ctx=with_ctx),
        grid=(bsz, h),
        in_specs=in_specs, out_specs=out_specs, out_shape=out_shape,
        compiler_params=pltpu.CompilerParams(dimension_semantics=("arbitrary", "arbitrary"),
                                             vmem_limit_bytes=V7X_VMEM_LIMIT_BYTES),
        name="na_attention",
    )(*args)
    return (outs[0], outs[1]) if with_ctx else (outs[0], None)


def _dft_cos_sin(n):
    jk = np.outer(np.arange(n), np.arange(n)) % n
    ang = 2.0 * np.pi * jk / n
    return np.cos(ang) / np.sqrt(n), np.sin(ang) / np.sqrt(n)


def _fourier_kernel(u_ref, w1_ref, a_ref, fw_ref, zf_ref, o_ref, z_ref, *, seq, row_blk):
    bf = jnp.bfloat16
    gd = FOUR_GROUP_DIM

    @pl.when(pl.program_id(1) == 0)
    def _():
        for r0 in range(0, seq, row_blk):
            for g in range(FOUR_GROUPS):
                ug = u_ref[0, r0:r0 + row_blk, g * gd:(g + 1) * gd].astype(bf)
                z = jnp.dot(ug, w1_ref[...], preferred_element_type=jnp.float32)
                z_ref[r0:r0 + row_blk, g * gd:(g + 1) * gd] = z[:, :gd].astype(bf)
                z_ref[seq + r0:seq + r0 + row_blk, g * gd:(g + 1) * gd] = z[:, gd:].astype(bf)

    y = jnp.dot(a_ref[...], z_ref[...], preferred_element_type=jnp.float32)
    f = jnp.dot(y.astype(bf), fw_ref[...], preferred_element_type=jnp.float32)
    zf = zf_ref[0].astype(jnp.float32)
    o_ref[0] = (f * (zf * jax.nn.sigmoid(zf))).astype(o_ref.dtype)


def fourier_branch(p, four_w_bf, *, tm):
    bsz, seq, _ = p.shape
    assert seq % tm == 0
    cc, sc = _dft_cos_sin(FOUR_GROUP_DIM)
    cl, sl = _dft_cos_sin(seq)
    w1 = jnp.asarray(np.concatenate([cc, sc], axis=1), jnp.bfloat16)
    a = jnp.asarray(np.concatenate([cl, -sl], axis=1), jnp.bfloat16)
    fw = FOUR_WIDTH
    return pl.pallas_call(
        partial(_fourier_kernel, seq=seq, row_blk=min(seq, 512)),
        grid=(bsz, seq // tm),
        in_specs=[pl.BlockSpec((1, seq, fw), lambda b, i: (b, 0, 4)),
                  pl.BlockSpec((FOUR_GROUP_DIM, 2 * FOUR_GROUP_DIM), lambda b, i: (0, 0)),
                  pl.BlockSpec((tm, 2 * seq), lambda b, i: (i, 0)),
                  pl.BlockSpec((fw, fw), lambda b, i: (0, 0)),
                  pl.BlockSpec((1, tm, fw), lambda b, i: (b, i, 5))],
        out_specs=pl.BlockSpec((1, tm, fw), lambda b, i: (b, i, 0)),
        out_shape=jax.ShapeDtypeStruct((bsz, seq, fw), jnp.bfloat16),
        scratch_shapes=[pltpu.VMEM((2 * seq, fw), jnp.bfloat16)],
        compiler_params=pltpu.CompilerParams(dimension_semantics=("arbitrary", "arbitrary"),
                                             vmem_limit_bytes=V7X_VMEM_LIMIT_BYTES),
        name="fourier_branch",
    )(p, w1, a, four_w_bf, p)


def _merge_kernel(a_ref, zna_ref, gfo_ref, s_ref, zs_ref, nw_ref, wa_ref, wf_ref, ws_ref,
                  ga_ref, gf_ref, gs_ref, o_ref, ap_ref, sp_ref):
    bf = jnp.bfloat16

    @pl.when(pl.program_id(1) == 0)
    def _():
        zna = zna_ref[...].astype(jnp.float32)
        ap_ref[...] = (a_ref[...].astype(jnp.float32) * (zna * jax.nn.sigmoid(zna))).astype(bf)
        zs = zs_ref[...].astype(jnp.float32)
        t = s_ref[...].astype(jnp.float32) * (zs * jax.nn.sigmoid(zs))
        t = t * lax.rsqrt(jnp.mean(t * t, axis=-1, keepdims=True) + EPS) * nw_ref[...]
        sp_ref[...] = t.astype(bf)

    o_a = jnp.dot(ap_ref[...], wa_ref[...], preferred_element_type=jnp.float32)
    o_f = jnp.dot(gfo_ref[...], wf_ref[...], preferred_element_type=jnp.float32)
    o_s = jnp.dot(sp_ref[...], ws_ref[...], preferred_element_type=jnp.float32)
    m = (jax.nn.sigmoid(ga_ref[...].astype(jnp.float32)) * o_a
         + jax.nn.sigmoid(gf_ref[...].astype(jnp.float32)) * o_f
         + jax.nn.sigmoid(gs_ref[...].astype(jnp.float32)) * o_s)
    o_ref[...] = m.astype(o_ref.dtype)


def merge_branches(p2d, a2d, gfo2d, s2d, ssd_norm_w, wa, wf, ws, *, tm):
    m = p2d.shape[0]
    w, tn = NA_WIDTH, 1024
    assert m % tm == 0 and D_MODEL % tn == 0
    row = lambda i, j: (i, 0)
    wcol = lambda i, j: (0, j)
    return pl.pallas_call(
        _merge_kernel,
        grid=(m // tm, D_MODEL // tn),
        in_specs=[pl.BlockSpec((tm, w), row),
                  pl.BlockSpec((tm, w), lambda i, j: (i, 3)),
                  pl.BlockSpec((tm, w), row),
                  pl.BlockSpec((tm, w), row),
                  pl.BlockSpec((tm, w), lambda i, j: (i, 8)),
                  pl.BlockSpec((1, w), lambda i, j: (0, 0)),
                  pl.BlockSpec((w, tn), wcol), pl.BlockSpec((w, tn), wcol), pl.BlockSpec((w, tn), wcol),
                  pl.BlockSpec((tm, tn), lambda i, j: (i, 9 + j)),
                  pl.BlockSpec((tm, tn), lambda i, j: (i, 11 + j)),
                  pl.BlockSpec((tm, tn), lambda i, j: (i, 13 + j))],
        out_specs=pl.BlockSpec((tm, tn), lambda i, j: (i, j)),
        out_shape=jax.ShapeDtypeStruct((m, D_MODEL), jnp.bfloat16),
        scratch_shapes=[pltpu.VMEM((tm, w), jnp.bfloat16), pltpu.VMEM((tm, w), jnp.bfloat16)],
        compiler_params=pltpu.CompilerParams(dimension_semantics=("arbitrary", "arbitrary"),
                                             vmem_limit_bytes=V7X_VMEM_LIMIT_BYTES),
        name="merge_branches",
    )(a2d, p2d, gfo2d, s2d, p2d, ssd_norm_w.reshape(1, w), wa, wf, ws, p2d, p2d, p2d)


def _outproj_kernel(m_ref, w_ref, x_ref, g_ref, *rest, final_norm):
    y = jnp.dot(m_ref[...], w_ref[...], preferred_element_type=jnp.float32)
    r = x_ref[...] + g_ref[0] * y
    if final_norm:
        nw_ref, o_ref = rest
        o_ref[...] = r * lax.rsqrt(jnp.mean(r * r, axis=-1, keepdims=True) + EPS) * nw_ref[...]
    else:
        rest[0][...] = r


def out_projection(m2d, w_out_bf, x2d, gate, *, rows_per_mod, tm, tn, final_norm_w=None):
    m, d = x2d.shape
    assert m % tm == 0 and d % tn == 0 and rows_per_mod % tm == 0
    per = rows_per_mod // tm
    in_specs = [pl.BlockSpec((tm, d), lambda i, j: (i, 0)),
                pl.BlockSpec((d, tn), lambda i, j: (0, j)),
                pl.BlockSpec((tm, tn), lambda i, j: (i, j)),
                pl.BlockSpec((1, 1, tn), lambda i, j: (i // per, 0, j))]
    args = [m2d, w_out_bf, x2d, gate]
    if final_norm_w is not None:
        assert tn == d
        in_specs.append(pl.BlockSpec((1, d), lambda i, j: (0, 0)))
        args.append(final_norm_w.astype(jnp.float32).reshape(1, d))
    return pl.pallas_call(
        partial(_outproj_kernel, final_norm=final_norm_w is not None),
        grid=(m // tm, d // tn),
        in_specs=in_specs,
        out_specs=pl.BlockSpec((tm, tn), lambda i, j: (i, j)),
        out_shape=jax.ShapeDtypeStruct((m, d), jnp.float32),
        compiler_params=pltpu.CompilerParams(dimension_semantics=("arbitrary", "arbitrary"),
                                             vmem_limit_bytes=V7X_VMEM_LIMIT_BYTES),
        name="out_projection",
    )(*args)


SSD_GW = SSD_HPG * SSD_HEAD_DIM
CONV_PAD = 8
SSD_CHUNK_UNROLL = 2


def _split3_bf16(a):
    rnd = lambda v: v.astype(jnp.bfloat16).astype(jnp.float32)
    a1 = rnd(a)
    a2 = rnd(a - a1)
    a3 = rnd((a - a1) - a2)
    return a1, a2, a3


def _head_rows(rows, r0, n):
    return jnp.concatenate([jnp.broadcast_to(rows[r0 + k:r0 + k + 1, :], (SSD_HEAD_DIM, n))
                            for k in range(SSD_HPG)], axis=0)


def _ssd_kernel(xl_ref, bl_ref, cl_ref, xc_ref, bc_ref, cc_ref, dt_ref, bias_ref, alog_ref, dsk_ref,
                wx_ref, wb_ref, wc_ref, bx_ref, bb_ref, bcb_ref, cos_ref, sin_ref,
                yl_ref, yc_ref,
                padx, padb, padc, xt_s, b_s, c_s, row_s, col_s, dec_s, yt_s, p_s, sp_s, st_s, *, seq, ctx_len):
    f32, bf = jnp.float32, jnp.bfloat16
    t = SSD_CHUNK
    nc_c, nc_l = ctx_len // t, seq // t
    nc = nc_c + nc_l

    lane = lax.broadcasted_iota(jnp.int32, (t, 128), 1)
    first_half = (lane % 64) < 32

    def conv_tile(pad, w_ref, b_ref, t0, l0):
        acc = jnp.broadcast_to(b_ref[:, l0:l0 + 128], (t, 128))
        for j in range(SSD_CONV):
            off = CONV_PAD - SSD_CONV // 2 + j
            acc = acc + pad[pl.ds(t0 + off, t), :] * w_ref[j:j + 1, l0:l0 + 128]
        return acc * jax.nn.sigmoid(acc)

    def rope(u, t0):
        swapped = jnp.where(first_half, pltpu.roll(u, 96, axis=1), pltpu.roll(u, 32, axis=1))
        return u * cos_ref[pl.ds(t0, t), :] + swapped * sin_ref[pl.ds(t0, t), :]

    def conv_seq(x_ref, bm_ref, cm_ref, n, base, use_rope):
        slabs = ([(padx.at[h], x_ref, h * 128) for h in range(SSD_GW // 128)]
                 + [(padb, bm_ref, 0), (padc, cm_ref, 0)])
        for pad, src, l0 in slabs:
            pad[0:CONV_PAD, :] = jnp.zeros((CONV_PAD, 128), f32)
            pad[CONV_PAD + n:2 * CONV_PAD + n, :] = jnp.zeros((CONV_PAD, 128), f32)
            pad[CONV_PAD:CONV_PAD + n, :] = src[0, :, l0:l0 + 128].astype(f32)

        def chunk(ci, carry):
            t0 = pl.multiple_of(ci * t, t)
            r0 = pl.multiple_of(base + ci * t, t)
            for h in range(SSD_GW // 128):
                l0 = h * 128
                xt_s[base // t + ci, l0:l0 + 128, :] = conv_tile(padx.at[h], wx_ref, bx_ref, t0, l0).T
            ub = conv_tile(padb, wb_ref, bb_ref, t0, 0)
            uc = conv_tile(padc, wc_ref, bcb_ref, t0, 0)
            if use_rope:
                ub, uc = rope(ub, t0), rope(uc, t0)
            b_s[pl.ds(r0, t), :] = ub.astype(bf)
            c_s[pl.ds(r0, t), :] = uc.astype(bf)
            return carry

        lax.fori_loop(0, n // t, chunk, 0)

    conv_seq(xc_ref, bc_ref, cc_ref, ctx_len, 0, False)
    conv_seq(xl_ref, bl_ref, cl_ref, seq, ctx_len, True)

    i0 = lax.broadcasted_iota(jnp.int32, (t, t), 0)
    i1 = lax.broadcasted_iota(jnp.int32, (t, t), 1)
    tri_f = (i0 <= i1).astype(bf)
    tri_r = (i0 >= i1).astype(bf)
    nr = 2 * SSD_HPG
    fwd_row = (lax.broadcasted_iota(jnp.int32, (nc * nr, 1), 0) % nr) < SSD_HPG
    x = dt_ref[0, 0].reshape(nc * nr, t) + jnp.tile(bias_ref[0], (nc, 1))
    dt = jnp.maximum(x, 0.0) + jnp.log1p(jnp.exp(-jnp.abs(x)))
    pieces = _split3_bf16(dt * jnp.tile(-jnp.exp(alog_ref[0]), (nc, 1)))
    pf = [jnp.dot(p.astype(bf), tri_f, preferred_element_type=f32) for p in pieces]
    pr = [jnp.dot(p.astype(bf), tri_r, preferred_element_type=f32) for p in pieces]
    cs = jnp.where(fwd_row, (pf[0] + pf[1]) + pf[2], (pr[0] + pr[1]) + pr[2])
    tot = jnp.where(fwd_row, cs[:, t - 1:t], cs[:, 0:1])
    e_in = jnp.exp(cs)
    e_out = dt * jnp.exp(tot - cs)
    decb = jnp.broadcast_to(jnp.exp(tot), (nc * nr, 128))
    for c in range(nc):
        sl = slice(c * nr, (c + 1) * nr)
        tile = jnp.concatenate([cs[sl], dt[sl], e_in[sl], e_out[sl], jnp.zeros((t - 4 * nr, t), f32)], axis=0)
        row_s[c] = tile[0:4 * nr]
        col_s[c * t:(c + 1) * t, :] = tile.T
        dec_s[c] = decb[sl]

    lower = i1 <= i0
    upper = i1 >= i0
    neg_inf = jnp.float32(-jnp.inf)

    def chunks_a(i, carry):
        ids = [i * SSD_CHUNK_UNROLL + u for u in range(SSD_CHUNK_UNROLL)]
        r0s = [pl.multiple_of(c * t, t) for c in ids]
        cms = [c_s[pl.ds(r0, t), :] for r0 in r0s]
        bms = [b_s[pl.ds(r0, t), :] for r0 in r0s]
        gs = [lax.dot_general(cm, bm, _NT, preferred_element_type=f32) for cm, bm in zip(cms, bms)]
        for c, r0, bm, g in zip(ids, r0s, bms, gs):
            rows = row_s[c]
            cols = col_s[pl.ds(r0, t), :]
            xt = xt_s[c]
            xt_bf = xt.astype(bf)
            parts = []
            for k in range(SSD_HPG):
                kr = SSD_HPG + k
                lf = jnp.exp(jnp.where(lower, cols[:, k:k + 1] - rows[k:k + 1, :], neg_inf)) * rows[8 + k:9 + k, :]
                lr = (jnp.exp(jnp.where(upper, cols[:, kr:kr + 1] - rows[kr:kr + 1, :], neg_inf))
                      * rows[8 + kr:9 + kr, :])
                mk = (g * (lf + lr)).astype(bf)
                xk = xt_bf[k * SSD_HEAD_DIM:(k + 1) * SSD_HEAD_DIM, :]
                parts.append(lax.dot_general(xk, mk, _NT, preferred_element_type=f32))
            yt_s[c] = dsk_ref[0] * xt + jnp.concatenate(parts, axis=0)
            for d in range(2):
                wt = (xt * _head_rows(rows, 24 + SSD_HPG * d, t)).astype(bf)
                p_s[d, c] = jnp.dot(wt, bm, preferred_element_type=f32)
        return carry

    lax.fori_loop(0, nc // SSD_CHUNK_UNROLL, chunks_a, 0)

    st_s[...] = jnp.zeros(st_s.shape, f32)

    def state_step(c, d):
        st = st_s[d]
        sp_s[d, c] = st.astype(bf)
        st_s[d] = st * _head_rows(dec_s[c], SSD_HPG * d, SSD_STATE) + p_s[d, c]

    def ctx_step(i, carry):
        state_step(i, 0)
        state_step(nc_c - 1 - i, 1)
        return carry

    def lat_step(i, carry):
        state_step(nc_c + i, 0)
        state_step(nc - 1 - i, 1)
        return carry

    lax.fori_loop(0, nc_c, ctx_step, 0)
    lax.fori_loop(0, nc_l, lat_step, 0)

    def chunks_c(first, out_ref, o_first):
        ids = [first + u for u in range(SSD_CHUNK_UNROLL)]
        cms = [c_s[pl.ds(pl.multiple_of(c * t, t), t), :] for c in ids]
        offs = [[lax.dot_general(sp_s[d, c], cm, _NT, preferred_element_type=f32) for d in range(2)]
                for c, cm in zip(ids, cms)]
        for u, (c, off) in enumerate(zip(ids, offs)):
            rows = row_s[c]
            yt = yt_s[c]
            for d in range(2):
                yt = yt + off[d] * _head_rows(rows, 16 + SSD_HPG * d, t)
            for l0 in range(0, SSD_GW, 128):
                out_ref[0, pl.ds(pl.multiple_of((o_first + u) * t, t), t), l0:l0 + 128] = yt[l0:l0 + 128, :].T

    def ctx_out(i, carry):
        chunks_c(i * SSD_CHUNK_UNROLL, yc_ref, i * SSD_CHUNK_UNROLL)
        return carry

    def lat_out(i, carry):
        chunks_c(nc_c + i * SSD_CHUNK_UNROLL, yl_ref, i * SSD_CHUNK_UNROLL)
        return carry

    lax.fori_loop(0, nc_c // SSD_CHUNK_UNROLL, ctx_out, 0)
    lax.fori_loop(0, nc_l // SSD_CHUNK_UNROLL, lat_out, 0)


def _rope_tables(seq):
    quarter = SSD_STATE // 4
    inv = ROPE_BASE ** (-np.arange(quarter, dtype=np.float64) / quarter)
    pos = np.arange(seq)
    ang_r = (pos // GRID_W)[:, None] * inv
    ang_c = (pos % GRID_W)[:, None] * inv
    cos = np.concatenate([np.cos(ang_r)] * 2 + [np.cos(ang_c)] * 2, axis=1)
    sin = np.concatenate([-np.sin(ang_r), np.sin(ang_r), -np.sin(ang_c), np.sin(ang_c)], axis=1)
    return jnp.asarray(cos, jnp.float32), jnp.asarray(sin, jnp.float32)


def ssd_mixer(p_lat, p_ctx, dt_lat, dt_ctx, conv_w, conv_b, dt_bias, a_log, d_skip):
    bsz, seq, _ = p_lat.shape
    ctx_len = p_ctx.shape[1]
    t, g, hpg = SSD_CHUNK, SSD_GROUPS, SSD_HPG
    assert seq % t == 0 and ctx_len % t == 0
    nc = (seq + ctx_len) // t
    dt = jnp.concatenate([dt_ctx[..., :DT_WIDTH], dt_lat[..., :DT_WIDTH]], axis=1)
    dt = dt.reshape(bsz, nc, t, 2, g, hpg).transpose(0, 4, 1, 3, 5, 2).reshape(bsz, g, nc, 2 * hpg, t)
    per_row = lambda v: jnp.broadcast_to(
        v.astype(jnp.float32).reshape(2, g, hpg).transpose(1, 0, 2).reshape(g, 2 * hpg, 1), (g, 2 * hpg, 128))
    dsk = jnp.broadcast_to(jnp.repeat(d_skip.astype(jnp.float32).reshape(g, hpg), SSD_HEAD_DIM, axis=1)[:, :, None],
                           (g, SSD_GW, 128))
    cos, sin = _rope_tables(seq)
    x0 = sum(MAIN_SPLITS[:6]) // SSD_GW
    b0 = (sum(MAIN_SPLITS[:6]) + SSD_WIDTH) // 128
    c0 = b0 + g
    cw = conv_w.astype(jnp.float32)
    cb = conv_b.astype(jnp.float32).reshape(1, SSD_CONV_CH)
    wb0 = SSD_WIDTH // 128
    ltot = seq + ctx_len
    f32, bf = jnp.float32, jnp.bfloat16
    y_lat, y_ctx = pl.pallas_call(
        partial(_ssd_kernel, seq=seq, ctx_len=ctx_len),
        grid=(bsz, g),
        in_specs=[pl.BlockSpec((1, seq, SSD_GW), lambda b, i: (b, 0, x0 + i)),
                  pl.BlockSpec((1, seq, 128), lambda b, i: (b, 0, b0 + i)),
                  pl.BlockSpec((1, seq, 128), lambda b, i: (b, 0, c0 + i)),
                  pl.BlockSpec((1, ctx_len, SSD_GW), lambda b, i: (b, 0, x0 + i)),
                  pl.BlockSpec((1, ctx_len, 128), lambda b, i: (b, 0, b0 + i)),
                  pl.BlockSpec((1, ctx_len, 128), lambda b, i: (b, 0, c0 + i)),
                  pl.BlockSpec((1, 1, nc, 2 * hpg, t), lambda b, i: (b, i, 0, 0, 0)),
                  pl.BlockSpec((1, 2 * hpg, 128), lambda b, i: (i, 0, 0)),
                  pl.BlockSpec((1, 2 * hpg, 128), lambda b, i: (i, 0, 0)),
                  pl.BlockSpec((1, SSD_GW, 128), lambda b, i: (i, 0, 0)),
                  pl.BlockSpec((SSD_CONV, SSD_GW), lambda b, i: (0, i)),
                  pl.BlockSpec((SSD_CONV, 128), lambda b, i: (0, wb0 + i)),
                  pl.BlockSpec((SSD_CONV, 128), lambda b, i: (0, wb0 + g + i)),
                  pl.BlockSpec((1, SSD_GW), lambda b, i: (0, i)),
                  pl.BlockSpec((1, 128), lambda b, i: (0, wb0 + i)),
                  pl.BlockSpec((1, 128), lambda b, i: (0, wb0 + g + i)),
                  pl.BlockSpec((seq, 128), lambda b, i: (0, 0)),
                  pl.BlockSpec((seq, 128), lambda b, i: (0, 0))],
        out_specs=[pl.BlockSpec((1, seq, SSD_GW), lambda b, i: (b, 0, i)),
                   pl.BlockSpec((1, ctx_len, SSD_GW), lambda b, i: (b, 0, i))],
        out_shape=[jax.ShapeDtypeStruct((bsz, seq, SSD_WIDTH), f32),
                   jax.ShapeDtypeStruct((bsz, ctx_len, SSD_WIDTH), f32)],
        scratch_shapes=[pltpu.VMEM((SSD_GW // 128, seq + 2 * CONV_PAD, 128), f32),
                        pltpu.VMEM((seq + 2 * CONV_PAD, 128), f32),
                        pltpu.VMEM((seq + 2 * CONV_PAD, 128), f32),
                        pltpu.VMEM((nc, SSD_GW, t), f32),
                        pltpu.VMEM((ltot, 128), bf),
                        pltpu.VMEM((ltot, 128), bf),
                        pltpu.VMEM((nc, 4 * 2 * hpg, t), f32),
                        pltpu.VMEM((ltot, 128), f32),
                        pltpu.VMEM((nc, 2 * hpg, 128), f32),
                        pltpu.VMEM((nc, SSD_GW, t), f32),
                        pltpu.VMEM((2, nc, SSD_GW, SSD_STATE), f32),
                        pltpu.VMEM((2, nc, SSD_GW, SSD_STATE), bf),
                        pltpu.VMEM((2, SSD_GW, SSD_STATE), f32)],
        compiler_params=pltpu.CompilerParams(dimension_semantics=("arbitrary", "arbitrary"),
                                             vmem_limit_bytes=V7X_VMEM_LIMIT_BYTES),
        name="ssd_mixer",
    )(p_lat, p_lat, p_lat, p_ctx, p_ctx, p_ctx, dt, per_row(dt_bias), per_row(a_log), dsk,
      cw, cw, cw, cb, cb, cb, cos, sin)
    return y_ctx, y_lat


ADA_ROWS = 16


def _layer(xc, xl, c, c_ctx, w_ada, b_ada, norm_w, w_in, rpb, four_w, conv_w, conv_b, dt_bias, a_log, d_skip,
           ssd_norm_w, wb_na, wb_four, wb_ssd, w_out, update_ctx, final_norm_w):
    bsz, seq, d = xl.shape
    ctx_len = xc.shape[1]
    bf = jnp.bfloat16
    assert bsz < ADA_ROWS

    cc = jnp.concatenate([c, c_ctx[None], jnp.zeros((ADA_ROWS - bsz - 1, d), c.dtype)], axis=0)
    mod = matmul(jax.nn.silu(cc), w_ada, tm=ADA_ROWS, tn=1024) + b_ada
    sh, sc, gt = jnp.split(mod, 3, axis=-1)
    sh_l, sc_l, g_l = sh[:bsz], sc[:bsz], gt[:bsz]
    sh_c, sc_c, g_c = sh[bsz:bsz + 1], sc[bsz:bsz + 1], gt[bsz]

    w_main = jnp.concatenate([w_in[:, :DT_OFFSET].astype(bf), w_in[:, DT_OFFSET + DT_WIDTH:].astype(bf)], axis=1)
    w_dt =jnp.pad(w_in[:, DT_OFFSET:DT_OFFSET + DT_WIDTH], ((0, 0), (0, DT_PAD - DT_WIDTH))).astype(bf)

    xl2d = xl.reshape(bsz * seq, d)
    xc2d = xc.reshape(bsz * ctx_len, d)
    pl2d, dt_l = in_projection(xl2d, norm_w, sc_l[:, None], sh_l[:, None], w_main, w_dt,
                               rows_per_mod=seq, tm=1024, tn=1536, out_dtype=bf)
    pc2d, dt_c = in_projection(xc2d, norm_w, sc_c[:, None], sh_c[:, None], w_main, w_dt,
                               rows_per_mod=bsz * ctx_len, tm=1024, tn=1536, out_dtype=bf)
    p_l = pl2d.reshape(bsz, seq, MAIN_WIDTH)
    p_c = pc2d.reshape(bsz, ctx_len, MAIN_WIDTH)
    dt_l = dt_l.reshape(bsz, seq, DT_PAD)
    dt_c = dt_c.reshape(bsz, ctx_len, DT_PAD)

    four_wb, wb_nab, wb_fourb, wb_ssdb, w_outb = (t.astype(bf) for t in (four_w, wb_na, wb_four, wb_ssd, w_out))

    a_l, a_c = na_attention(p_l, p_c, na_bias_table(rpb), with_ctx=update_ctx, out_dtype=bf)
    gfo_l = fourier_branch(p_l, four_wb, tm=512)
    s_c, s_l = ssd_mixer(p_l, p_c, dt_l, dt_c, conv_w, conv_b, dt_bias, a_log, d_skip)
    m_l = merge_branches(pl2d, a_l.reshape(bsz * seq, NA_WIDTH), gfo_l.reshape(bsz * seq, FOUR_WIDTH),
                         s_l.reshape(bsz * seq, SSD_WIDTH), ssd_norm_w, wb_nab, wb_fourb, wb_ssdb, tm=512)
    if final_norm_w is None:
        xl_new = out_projection(m_l, w_outb, xl2d, g_l[:, None], rows_per_mod=seq, tm=1024, tn=1024)
    else:
        xl_new = out_projection(m_l, w_outb, xl2d, g_l[:, None], rows_per_mod=seq, tm=512, tn=d,
                                final_norm_w=final_norm_w)
    xl_new = xl_new.reshape(bsz, seq, d)
    if update_ctx:
        gfo_c = fourier_branch(p_c, four_wb, tm=ctx_len)
        m_c = merge_branches(pc2d, a_c.reshape(bsz * ctx_len, NA_WIDTH), gfo_c.reshape(bsz * ctx_len, FOUR_WIDTH),
                             s_c.reshape(bsz * ctx_len, SSD_WIDTH), ssd_norm_w, wb_nab, wb_fourb, wb_ssdb, tm=512)
        xc = out_projection(m_c, w_outb, xc2d, g_c[None, None], rows_per_mod=bsz * ctx_len, tm=1024, tn=1024)
        xc = xc.reshape(bsz, ctx_len, d)
    return xc, xl_new


def kernel(x, c, ctx, c_ctx, w_ada, b_ada, norm_w, w_in, na_rpb, four_w, ssd_conv_w, ssd_conv_b, ssd_dt_bias,
           ssd_a_log, ssd_d, ssd_norm_w, wb_na, wb_four, wb_ssd, w_out, final_norm_w):
    xc, xl = ctx, x
    for l in range(DEPTH):
        xc, xl = _layer(xc, xl, c, c_ctx, w_ada[l], b_ada[l], norm_w[l], w_in[l], na_rpb[l], four_w[l],
                        ssd_conv_w[l], ssd_conv_b[l], ssd_dt_bias[l], ssd_a_log[l], ssd_d[l], ssd_norm_w[l],
                        wb_na[l], wb_four[l], wb_ssd[l], w_out[l], update_ctx=(l < DEPTH - 1),
                        final_norm_w=final_norm_w if l == DEPTH - 1 else None)
    return xl
```

```python
import math
from functools import partial

import jax
import jax.numpy as jnp
import numpy as np
from jax import lax
from jax.experimental import pallas as pl
from jax.experimental.pallas import tpu as pltpu

D_MODEL = 2048
DEPTH = 2
GRID_W = 64
EPS = 1e-6
NA_HEADS = 8
NA_HEAD_DIM = 128
NA_WIDTH = NA_HEADS * NA_HEAD_DIM
NA_WIN_R = 8
NA_WIN_C = 16
NA_QBLK = 16
NA_KBLK = 32
FOUR_GROUPS = 4
FOUR_GROUP_DIM = 256
FOUR_WIDTH = FOUR_GROUPS * FOUR_GROUP_DIM
SSD_HEADS = 16
SSD_HEAD_DIM = 64
SSD_WIDTH = SSD_HEADS * SSD_HEAD_DIM
SSD_GROUPS = 4
SSD_HPG = SSD_HEADS // SSD_GROUPS
SSD_STATE = 128
SSD_CONV = 7
SSD_CHUNK = 128
SSD_CONV_CH = SSD_WIDTH + 2 * SSD_GROUPS * SSD_STATE
ROPE_BASE = 10000.0
DT_WIDTH = 2 * SSD_HEADS
MAIN_SPLITS = (NA_WIDTH, NA_WIDTH, NA_WIDTH, NA_WIDTH, FOUR_WIDTH, FOUR_WIDTH, SSD_CONV_CH, SSD_WIDTH,
               D_MODEL, D_MODEL, D_MODEL)
MAIN_WIDTH = sum(MAIN_SPLITS)
DT_OFFSET = sum(MAIN_SPLITS[:8])
DT_PAD = 128

V7X_VMEM_LIMIT_BYTES = 56 * 1024 * 1024


def _mm_kernel(a_ref, b_ref, o_ref):
    a = a_ref[...].astype(jnp.bfloat16)
    b = b_ref[...].astype(jnp.bfloat16)
    o_ref[...] = jnp.dot(a, b, preferred_element_type=jnp.float32).astype(o_ref.dtype)


def matmul(a, b, *, tm, tn, out_dtype=jnp.float32):
    m, k = a.shape
    _, n = b.shape
    assert m % tm == 0 and n % tn == 0, (a.shape, b.shape, tm, tn)
    return pl.pallas_call(
        _mm_kernel,
        grid=(m // tm, n // tn),
        in_specs=[pl.BlockSpec((tm, k), lambda i, j: (i, 0)),
                  pl.BlockSpec((k, tn), lambda i, j: (0, j))],
        out_specs=pl.BlockSpec((tm, tn), lambda i, j: (i, j)),
        out_shape=jax.ShapeDtypeStruct((m, n), out_dtype),
        compiler_params=pltpu.CompilerParams(dimension_semantics=("arbitrary", "arbitrary"),
                                             vmem_limit_bytes=V7X_VMEM_LIMIT_BYTES),
        name="matmul",
    )(a, b)


def _inproj_kernel(x_ref, nw_ref, sc_ref, sh_ref, w_ref, wdt_ref, o_ref, dt_ref, h_ref):
    @pl.when(pl.program_id(1) == 0)
    def _():
        x = x_ref[...]
        y = x * lax.rsqrt(jnp.mean(x * x, axis=-1, keepdims=True) + EPS) * nw_ref[...]
        h = (y * (1.0 + sc_ref[0]) + sh_ref[0]).astype(jnp.bfloat16)
        h_ref[...] = h
        dt_ref[...] = jnp.dot(h, wdt_ref[...], preferred_element_type=jnp.float32)

    o_ref[...] = jnp.dot(h_ref[...], w_ref[...], preferred_element_type=jnp.float32).astype(o_ref.dtype)


def in_projection(x2d, norm_w, scale, shift, w_main, w_dt, *, rows_per_mod, tm, tn, out_dtype):
    m, d = x2d.shape
    n = w_main.shape[1]
    assert m % tm == 0 and n % tn == 0 and rows_per_mod % tm == 0
    per = rows_per_mod // tm
    return pl.pallas_call(
        _inproj_kernel,
        grid=(m // tm, n // tn),
        in_specs=[pl.BlockSpec((tm, d), lambda i, j: (i, 0)),
                  pl.BlockSpec((1, d), lambda i, j: (0, 0)),
                  pl.BlockSpec((1, 1, d), lambda i, j: (i // per, 0, 0)),
                  pl.BlockSpec((1, 1, d), lambda i, j: (i // per, 0, 0)),
                  pl.BlockSpec((d, tn), lambda i, j: (0, j)),
                  pl.BlockSpec((d, DT_PAD), lambda i, j: (0, 0))],
        out_specs=[pl.BlockSpec((tm, tn), lambda i, j: (i, j)),
                   pl.BlockSpec((tm, DT_PAD), lambda i, j: (i, 0))],
        out_shape=[jax.ShapeDtypeStruct((m, n), out_dtype),
                   jax.ShapeDtypeStruct((m, DT_PAD), jnp.float32)],
        scratch_shapes=[pltpu.VMEM((tm, d), jnp.bfloat16)],
        compiler_params=pltpu.CompilerParams(dimension_semantics=("arbitrary", "arbitrary"),
                                             vmem_limit_bytes=V7X_VMEM_LIMIT_BYTES),
        name="in_projection",
    )(x2d, norm_w.reshape(1, d), scale, shift, w_main, w_dt)


GRID_ROWS = 32
NA_WIN_TOKENS = NA_WIN_R * GRID_W
_NT = (((1,), (1,)), ((), ()))
NA_ROW_UNROLL = 8


def _na_kernel(*refs, with_ctx):
    if with_ctx:
        q_ref, k_ref, v_ref, kc_ref, vc_ref, bias_ref, qc_ref, o_ref, oc_ref = refs
    else:
        q_ref, k_ref, v_ref, kc_ref, vc_ref, bias_ref, o_ref = refs
    bf = jnp.bfloat16
    scale = NA_HEAD_DIM ** -0.5
    kc = kc_ref[0].astype(bf)
    vc = vc_ref[0].astype(bf)

    def rows(rb, carry):
        idx, scores, probs = [], [], []
        for i in range(NA_ROW_UNROLL):
            r = rb * NA_ROW_UNROLL + i
            start = jnp.clip(r - NA_WIN_R // 2, 0, GRID_ROWS - NA_WIN_R)
            q0 = pl.multiple_of(r * GRID_W, GRID_W)
            k0 = pl.multiple_of(start * GRID_W, GRID_W)
            q = q_ref[0, pl.ds(q0, GRID_W), :].astype(bf)
            kw = k_ref[0, pl.ds(k0, NA_WIN_TOKENS), :].astype(bf)
            s_w = lax.dot_general(q, kw, _NT, preferred_element_type=jnp.float32) * scale + bias_ref[0, r - start]
            s_c = lax.dot_general(q, kc, _NT, preferred_element_type=jnp.float32) * scale
            idx.append((q0, k0))
            scores.append((s_w, s_c))
        for s_w, s_c in scores:
            m = jnp.maximum(jnp.max(s_w, axis=-1, keepdims=True), jnp.max(s_c, axis=-1, keepdims=True))
            e_w = jnp.exp(s_w - m)
            e_c = jnp.exp(s_c - m)
            denom = jnp.sum(e_w, axis=-1, keepdims=True) + jnp.sum(e_c, axis=-1, keepdims=True)
            probs.append((e_w.astype(bf), e_c.astype(bf), denom))
        for (q0, k0), (e_w, e_c, denom) in zip(idx, probs):
            vw = v_ref[0, pl.ds(k0, NA_WIN_TOKENS), :].astype(bf)
            o = (jnp.dot(e_w, vw, preferred_element_type=jnp.float32)
                 + jnp.dot(e_c, vc, preferred_element_type=jnp.float32))
            o_ref[0, pl.ds(q0, GRID_W), :] = (o / denom).astype(o_ref.dtype)
        return carry

    lax.fori_loop(0, GRID_ROWS // NA_ROW_UNROLL, rows, 0)

    if with_ctx:
        s = lax.dot_general(qc_ref[0].astype(bf), kc, _NT, preferred_element_type=jnp.float32) * scale
        e = jnp.exp(s - jnp.max(s, axis=-1, keepdims=True))
        o = jnp.dot(e.astype(bf), vc, preferred_element_type=jnp.float32)
        oc_ref[0] = (o / jnp.sum(e, axis=-1, keepdims=True)).astype(oc_ref.dtype)


def na_bias_table(rpb):
    qcol = np.arange(GRID_W)[:, None]
    kcol = np.arange(GRID_W)[None, :]
    qstart = np.clip(qcol - NA_WIN_C // 2, 0, GRID_W - NA_WIN_C)
    mask = (kcol >= qstart) & (kcol < qstart + NA_WIN_C)
    dc = np.clip(kcol - qcol + NA_WIN_C - 1, 0, 2 * NA_WIN_C - 2)
    pick_col = (np.arange(2 * NA_WIN_C - 1)[:, None, None] == dc[None]).astype(np.float32)
    rows = jnp.stack([rpb.astype(jnp.float32)[:, NA_WIN_R - 1 - dl:2 * NA_WIN_R - 1 - dl] for dl in range(NA_WIN_R)],
                     axis=1)
    t = jnp.einsum("hdwc,cqk->hdqwk", rows, pick_col, precision=lax.Precision.HIGHEST)
    t = jnp.where(mask[None, None, :, None, :], t, -1e30)
    return t.reshape(rpb.shape[0], NA_WIN_R, GRID_W, NA_WIN_TOKENS)


def na_attention(p_lat, p_ctx, bias, *, with_ctx, out_dtype):
    bsz, seq, _ = p_lat.shape
    ctx_len = p_ctx.shape[1]
    assert seq == GRID_ROWS * GRID_W
    d, h = NA_HEAD_DIM, NA_HEADS
    in_specs = [pl.BlockSpec((1, seq, d), lambda b, i: (b, 0, i)),
                pl.BlockSpec((1, seq, d), lambda b, i: (b, 0, h + i)),
                pl.BlockSpec((1, seq, d), lambda b, i: (b, 0, 2 * h + i)),
                pl.BlockSpec((1, ctx_len, d), lambda b, i: (b, 0, h + i)),
                pl.BlockSpec((1, ctx_len, d), lambda b, i: (b, 0, 2 * h + i)),
                pl.BlockSpec((1, NA_WIN_R, GRID_W, NA_WIN_TOKENS), lambda b, i: (i, 0, 0, 0))]
    args = [p_lat, p_lat, p_lat, p_ctx, p_ctx, bias]
    out_specs = [pl.BlockSpec((1, seq, d), lambda b, i: (b, 0, i))]
    out_shape = [jax.ShapeDtypeStruct((bsz, seq, h * d), out_dtype)]
    if with_ctx:
        in_specs.append(pl.BlockSpec((1, ctx_len, d), lambda b, i: (b, 0, i)))
        args.append(p_ctx)
        out_specs.append(pl.BlockSpec((1, ctx_len, d), lambda b, i: (b, 0, i)))
        out_shape.append(jax.ShapeDtypeStruct((bsz, ctx_len, h * d), out_dtype))
    outs = pl.pallas_call(
        partial(_na_kernel, with_ctx=with_ctx),
        grid=(bsz, h),
        in_specs=in_specs, out_specs=out_specs, out_shape=out_shape,
        compiler_params=pltpu.CompilerParams(dimension_semantics=("arbitrary", "arbitrary"),
                                             vmem_limit_bytes=V7X_VMEM_LIMIT_BYTES),
        name="na_attention",
    )(*args)
    return (outs[0], outs[1]) if with_ctx else (outs[0], None)


def _dft_cos_sin(n):
    jk = np.outer(np.arange(n), np.arange(n)) % n
    ang = 2.0 * np.pi * jk / n
    return np.cos(ang) / np.sqrt(n), np.sin(ang) / np.sqrt(n)


def _fourier_kernel(u_ref, w1_ref, a_ref, fw_ref, zf_ref, o_ref, z_ref, *, seq, row_blk):
    bf = jnp.bfloat16
    gd = FOUR_GROUP_DIM

    @pl.when(pl.program_id(1) == 0)
    def _():
        for r0 in range(0, seq, row_blk):
            for g in range(FOUR_GROUPS):
                ug = u_ref[0, r0:r0 + row_blk, g * gd:(g + 1) * gd].astype(bf)
                z = jnp.dot(ug, w1_ref[...], preferred_element_type=jnp.float32)
                z_ref[r0:r0 + row_blk, g * gd:(g + 1) * gd] = z[:, :gd].astype(bf)
                z_ref[seq + r0:seq + r0 + row_blk, g * gd:(g + 1) * gd] = z[:, gd:].astype(bf)

    y = jnp.dot(a_ref[...], z_ref[...], preferred_element_type=jnp.float32)
    f = jnp.dot(y.astype(bf), fw_ref[...], preferred_element_type=jnp.float32)
    zf = zf_ref[0].astype(jnp.float32)
    o_ref[0] = (f * (zf * jax.nn.sigmoid(zf))).astype(o_ref.dtype)


def fourier_branch(p, four_w_bf, *, tm):
    bsz, seq, _ = p.shape
    assert seq % tm == 0
    cc, sc = _dft_cos_sin(FOUR_GROUP_DIM)
    cl, sl = _dft_cos_sin(seq)
    w1 = jnp.asarray(np.concatenate([cc, sc], axis=1), jnp.bfloat16)
    a = jnp.asarray(np.concatenate([cl, -sl], axis=1), jnp.bfloat16)
    fw = FOUR_WIDTH
    return pl.pallas_call(
        partial(_fourier_kernel, seq=seq, row_blk=min(seq, 512)),
        grid=(bsz, seq // tm),
        in_specs=[pl.BlockSpec((1, seq, fw), lambda b, i: (b, 0, 4)),
                  pl.BlockSpec((FOUR_GROUP_DIM, 2 * FOUR_GROUP_DIM), lambda b, i: (0, 0)),
                  pl.BlockSpec((tm, 2 * seq), lambda b, i: (i, 0)),
                  pl.BlockSpec((fw, fw), lambda b, i: (0, 0)),
                  pl.BlockSpec((1, tm, fw), lambda b, i: (b, i, 5))],
        out_specs=pl.BlockSpec((1, tm, fw), lambda b, i: (b, i, 0)),
        out_shape=jax.ShapeDtypeStruct((bsz, seq, fw), jnp.bfloat16),
        scratch_shapes=[pltpu.VMEM((2 * seq, fw), jnp.bfloat16)],
        compiler_params=pltpu.CompilerParams(dimension_semantics=("arbitrary", "arbitrary"),
                                             vmem_limit_bytes=V7X_VMEM_LIMIT_BYTES),
        name="fourier_branch",
    )(p, w1, a, four_w_bf, p)


MERGE_TN = 1024


def _merge_kernel(a_ref, zna_ref, gfo_ref, s_ref, zs_ref, nw_ref, wa_ref, wf_ref, ws_ref, *rest):
    n_half = D_MODEL // MERGE_TN
    gate_refs, (o_ref, ap_ref, sp_ref) = rest[:3 * n_half], rest[3 * n_half:]
    bf = jnp.bfloat16
    zna = zna_ref[...].astype(jnp.float32)
    ap_ref[...] = (a_ref[...].astype(jnp.float32) * (zna * jax.nn.sigmoid(zna))).astype(bf)
    zs = zs_ref[...].astype(jnp.float32)
    t = s_ref[...].astype(jnp.float32) * (zs * jax.nn.sigmoid(zs))
    t = t * lax.rsqrt(jnp.mean(t * t, axis=-1, keepdims=True) + EPS) * nw_ref[...]
    sp_ref[...] = t.astype(bf)
    for h in range(n_half):
        cols = slice(h * MERGE_TN, (h + 1) * MERGE_TN)
        ga_ref, gf_ref, gs_ref = gate_refs[h], gate_refs[n_half + h], gate_refs[2 * n_half + h]
        o_a = jnp.dot(ap_ref[...], wa_ref[:, cols], preferred_element_type=jnp.float32)
        o_f = jnp.dot(gfo_ref[...], wf_ref[:, cols], preferred_element_type=jnp.float32)
        o_s = jnp.dot(sp_ref[...], ws_ref[:, cols], preferred_element_type=jnp.float32)
        m = (jax.nn.sigmoid(ga_ref[...].astype(jnp.float32)) * o_a
             + jax.nn.sigmoid(gf_ref[...].astype(jnp.float32)) * o_f
             + jax.nn.sigmoid(gs_ref[...].astype(jnp.float32)) * o_s)
        o_ref[:, cols] = m.astype(o_ref.dtype)


def merge_branches(p2d, a2d, gfo2d, s2d, ssd_norm_w, wa, wf, ws, *, tm):
    m = p2d.shape[0]
    w, tn = NA_WIDTH, MERGE_TN
    n_half = D_MODEL // tn
    assert m % tm == 0
    row = lambda i: (i, 0)
    resident = lambda shape: pl.BlockSpec(shape, lambda i: (0, 0), pipeline_mode=pl.Buffered(1))
    gate0 = DT_OFFSET // tn
    gate_specs = [pl.BlockSpec((tm, tn), partial(lambda i, blk: (i, blk), blk=gate0 + g * n_half + h))
                  for g in range(3) for h in range(n_half)]
    return pl.pallas_call(
        _merge_kernel,
        grid=(m // tm,),
        in_specs=[pl.BlockSpec((tm, w), row),
                  pl.BlockSpec((tm, w), lambda i: (i, 3)),
                  pl.BlockSpec((tm, w), row),
                  pl.BlockSpec((tm, w), row),
                  pl.BlockSpec((tm, w), lambda i: (i, 8)),
                  resident((1, w)),
                  resident((w, D_MODEL)), resident((w, D_MODEL)), resident((w, D_MODEL))] + gate_specs,
        out_specs=pl.BlockSpec((tm, D_MODEL), row),
        out_shape=jax.ShapeDtypeStruct((m, D_MODEL), jnp.bfloat16),
        scratch_shapes=[pltpu.VMEM((tm, w), jnp.bfloat16), pltpu.VMEM((tm, w), jnp.bfloat16)],
        compiler_params=pltpu.CompilerParams(dimension_semantics=("arbitrary",),
                                             vmem_limit_bytes=V7X_VMEM_LIMIT_BYTES),
        name="merge_branches",
    )(a2d, p2d, gfo2d, s2d, p2d, ssd_norm_w.reshape(1, w), wa, wf, ws, *([p2d] * (3 * n_half)))


def _outproj_kernel(m_ref, w_ref, x_ref, g_ref, *rest, final_norm):
    y = jnp.dot(m_ref[...], w_ref[...], preferred_element_type=jnp.float32)
    r = x_ref[...] + g_ref[0] * y
    if final_norm:
        nw_ref, o_ref = rest
        o_ref[...] = r * lax.rsqrt(jnp.mean(r * r, axis=-1, keepdims=True) + EPS) * nw_ref[...]
    else:
        rest[0][...] = r


def out_projection(m2d, w_out_bf, x2d, gate, *, rows_per_mod, tm, tn, final_norm_w=None):
    m, d = x2d.shape
    assert m % tm == 0 and d % tn == 0 and rows_per_mod % tm == 0
    per = rows_per_mod // tm
    in_specs = [pl.BlockSpec((tm, d), lambda i, j: (i, 0)),
                pl.BlockSpec((d, tn), lambda i, j: (0, j)),
                pl.BlockSpec((tm, tn), lambda i, j: (i, j)),
                pl.BlockSpec((1, 1, tn), lambda i, j: (i // per, 0, j))]
    args = [m2d, w_out_bf, x2d, gate]
    if final_norm_w is not None:
        assert tn == d
        in_specs.append(pl.BlockSpec((1, d), lambda i, j: (0, 0)))
        args.append(final_norm_w.astype(jnp.float32).reshape(1, d))
    return pl.pallas_call(
        partial(_outproj_kernel, final_norm=final_norm_w is not None),
        grid=(m // tm, d // tn),
        in_specs=in_specs,
        out_specs=pl.BlockSpec((tm, tn), lambda i, j: (i, j)),
        out_shape=jax.ShapeDtypeStruct((m, d), jnp.float32),
        compiler_params=pltpu.CompilerParams(dimension_semantics=("arbitrary", "arbitrary"),
                                             vmem_limit_bytes=V7X_VMEM_LIMIT_BYTES),
        name="out_projection",
    )(*args)


SSD_GW = SSD_HPG * SSD_HEAD_DIM
CONV_PAD = 8
SSD_CHUNK_UNROLL = 2


def _split3_bf16(a):
    rnd = lambda v: v.astype(jnp.bfloat16).astype(jnp.float32)
    a1 = rnd(a)
    a2 = rnd(a - a1)
    a3 = rnd((a - a1) - a2)
    return a1, a2, a3


def _head_rows(rows, r0, n):
    return jnp.concatenate([jnp.broadcast_to(rows[r0 + k:r0 + k + 1, :], (SSD_HEAD_DIM, n))
                            for k in range(SSD_HPG)], axis=0)


def _ssd_kernel(xl_ref, bl_ref, cl_ref, xc_ref, bc_ref, cc_ref, dt_ref, bias_ref, alog_ref, dsk_ref,
                wx_ref, wb_ref, wc_ref, bx_ref, bb_ref, bcb_ref, cos_ref, sin_ref,
                yl_ref, yc_ref,
                padx, padb, padc, xt_s, b_s, c_s, row_s, col_s, dec_s, yt_s, p_s, sp_s, st_s, *, seq, ctx_len):
    f32, bf = jnp.float32, jnp.bfloat16
    t = SSD_CHUNK
    nc_c, nc_l = ctx_len // t, seq // t
    nc = nc_c + nc_l

    lane = lax.broadcasted_iota(jnp.int32, (t, 128), 1)
    first_half = (lane % 64) < 32

    def conv_tile(pad, w_ref, b_ref, t0, l0):
        acc = jnp.broadcast_to(b_ref[:, l0:l0 + 128], (t, 128))
        for j in range(SSD_CONV):
            off = CONV_PAD - SSD_CONV // 2 + j
            acc = acc + pad[pl.ds(t0 + off, t), :] * w_ref[j:j + 1, l0:l0 + 128]
        return acc * jax.nn.sigmoid(acc)

    def rope(u, t0):
        swapped = jnp.where(first_half, pltpu.roll(u, 96, axis=1), pltpu.roll(u, 32, axis=1))
        return u * cos_ref[pl.ds(t0, t), :] + swapped * sin_ref[pl.ds(t0, t), :]

    def conv_seq(x_ref, bm_ref, cm_ref, n, base, use_rope):
        slabs = ([(padx.at[h], x_ref, h * 128) for h in range(SSD_GW // 128)]
                 + [(padb, bm_ref, 0), (padc, cm_ref, 0)])
        for pad, src, l0 in slabs:
            pad[0:CONV_PAD, :] = jnp.zeros((CONV_PAD, 128), f32)
            pad[CONV_PAD + n:2 * CONV_PAD + n, :] = jnp.zeros((CONV_PAD, 128), f32)
            pad[CONV_PAD:CONV_PAD + n, :] = src[0, :, l0:l0 + 128].astype(f32)

        def chunk(ci, carry):
            t0 = pl.multiple_of(ci * t, t)
            r0 = pl.multiple_of(base + ci * t, t)
            for h in range(SSD_GW // 128):
                l0 = h * 128
                xt_s[base // t + ci, l0:l0 + 128, :] = conv_tile(padx.at[h], wx_ref, bx_ref, t0, l0).T
            ub = conv_tile(padb, wb_ref, bb_ref, t0, 0)
            uc = conv_tile(padc, wc_ref, bcb_ref, t0, 0)
            if use_rope:
                ub, uc = rope(ub, t0), rope(uc, t0)
            b_s[pl.ds(r0, t), :] = ub.astype(bf)
            c_s[pl.ds(r0, t), :] = uc.astype(bf)
            return carry

        lax.fori_loop(0, n // t, chunk, 0)

    conv_seq(xc_ref, bc_ref, cc_ref, ctx_len, 0, False)
    conv_seq(xl_ref, bl_ref, cl_ref, seq, ctx_len, True)

    i0 = lax.broadcasted_iota(jnp.int32, (t, t), 0)
    i1 = lax.broadcasted_iota(jnp.int32, (t, t), 1)
    tri_f = (i0 <= i1).astype(bf)
    tri_r = (i0 >= i1).astype(bf)
    nr = 2 * SSD_HPG
    fwd_row = (lax.broadcasted_iota(jnp.int32, (nc * nr, 1), 0) % nr) < SSD_HPG
    x = dt_ref[0, 0].reshape(nc * nr, t) + jnp.tile(bias_ref[0], (nc, 1))
    dt = jnp.maximum(x, 0.0) + jnp.log1p(jnp.exp(-jnp.abs(x)))
    pieces = _split3_bf16(dt * jnp.tile(-jnp.exp(alog_ref[0]), (nc, 1)))
    pf = [jnp.dot(p.astype(bf), tri_f, preferred_element_type=f32) for p in pieces]
    pr = [jnp.dot(p.astype(bf), tri_r, preferred_element_type=f32) for p in pieces]
    cs = jnp.where(fwd_row, (pf[0] + pf[1]) + pf[2], (pr[0] + pr[1]) + pr[2])
    tot = jnp.where(fwd_row, cs[:, t - 1:t], cs[:, 0:1])
    e_in = jnp.exp(cs)
    e_out = dt * jnp.exp(tot - cs)
    decb = jnp.broadcast_to(jnp.exp(tot), (nc * nr, 128))
    for c in range(nc):
        sl = slice(c * nr, (c + 1) * nr)
        tile = jnp.concatenate([cs[sl], dt[sl], e_in[sl], e_out[sl], jnp.zeros((t - 4 * nr, t), f32)], axis=0)
        row_s[c] = tile[0:4 * nr]
        col_s[c * t:(c + 1) * t, :] = tile.T
        dec_s[c] = decb[sl]

    lower = i1 <= i0
    upper = i1 >= i0
    neg_inf = jnp.float32(-jnp.inf)

    def chunks_a(i, carry):
        ids = [i * SSD_CHUNK_UNROLL + u for u in range(SSD_CHUNK_UNROLL)]
        r0s = [pl.multiple_of(c * t, t) for c in ids]
        cms = [c_s[pl.ds(r0, t), :] for r0 in r0s]
        bms = [b_s[pl.ds(r0, t), :] for r0 in r0s]
        gs = [lax.dot_general(cm, bm, _NT, preferred_element_type=f32) for cm, bm in zip(cms, bms)]
        for c, r0, bm, g in zip(ids, r0s, bms, gs):
            rows = row_s[c]
            cols = col_s[pl.ds(r0, t), :]
            xt = xt_s[c]
            xt_bf = xt.astype(bf)
            parts = []
            for k in range(SSD_HPG):
                kr = SSD_HPG + k
                lf = jnp.exp(jnp.where(lower, cols[:, k:k + 1] - rows[k:k + 1, :], neg_inf)) * rows[8 + k:9 + k, :]
                lr = (jnp.exp(jnp.where(upper, cols[:, kr:kr + 1] - rows[kr:kr + 1, :], neg_inf))
                      * rows[8 + kr:9 + kr, :])
                mk = (g * (lf + lr)).astype(bf)
                xk = xt_bf[k * SSD_HEAD_DIM:(k + 1) * SSD_HEAD_DIM, :]
                parts.append(lax.dot_general(xk, mk, _NT, preferred_element_type=f32))
            yt_s[c] = dsk_ref[0] * xt + jnp.concatenate(parts, axis=0)
            for d in range(2):
                wt = (xt * _head_rows(rows, 24 + SSD_HPG * d, t)).astype(bf)
                p_s[d, c] = jnp.dot(wt, bm, preferred_element_type=f32)
        return carry

    lax.fori_loop(0, nc // SSD_CHUNK_UNROLL, chunks_a, 0)

    st_s[...] = jnp.zeros(st_s.shape, f32)

    def state_step(c, d):
        st = st_s[d]
        sp_s[d, c] = st.astype(bf)
        st_s[d] = st * _head_rows(dec_s[c], SSD_HPG * d, SSD_STATE) + p_s[d, c]

    def ctx_step(i, carry):
        state_step(i, 0)
        state_step(nc_c - 1 - i, 1)
        return carry

    def lat_step(i, carry):
        state_step(nc_c + i, 0)
        state_step(nc - 1 - i, 1)
        return carry

    lax.fori_loop(0, nc_c, ctx_step, 0)
    lax.fori_loop(0, nc_l, lat_step, 0)

    def chunks_c(first, out_ref, o_first):
        ids = [first + u for u in range(SSD_CHUNK_UNROLL)]
        cms = [c_s[pl.ds(pl.multiple_of(c * t, t), t), :] for c in ids]
        offs = [[lax.dot_general(sp_s[d, c], cm, _NT, preferred_element_type=f32) for d in range(2)]
                for c, cm in zip(ids, cms)]
        for u, (c, off) in enumerate(zip(ids, offs)):
            rows = row_s[c]
            yt = yt_s[c]
            for d in range(2):
                yt = yt + off[d] * _head_rows(rows, 16 + SSD_HPG * d, t)
            for l0 in range(0, SSD_GW, 128):
                out_ref[0, pl.ds(pl.multiple_of((o_first + u) * t, t), t), l0:l0 + 128] = yt[l0:l0 + 128, :].T

    def ctx_out(i, carry):
        chunks_c(i * SSD_CHUNK_UNROLL, yc_ref, i * SSD_CHUNK_UNROLL)
        return carry

    def lat_out(i, carry):
        chunks_c(nc_c + i * SSD_CHUNK_UNROLL, yl_ref, i * SSD_CHUNK_UNROLL)
        return carry

    lax.fori_loop(0, nc_c // SSD_CHUNK_UNROLL, ctx_out, 0)
    lax.fori_loop(0, nc_l // SSD_CHUNK_UNROLL, lat_out, 0)


def _rope_tables(seq):
    quarter = SSD_STATE // 4
    inv = ROPE_BASE ** (-np.arange(quarter, dtype=np.float64) / quarter)
    pos = np.arange(seq)
    ang_r = (pos // GRID_W)[:, None] * inv
    ang_c = (pos % GRID_W)[:, None] * inv
    cos = np.concatenate([np.cos(ang_r)] * 2 + [np.cos(ang_c)] * 2, axis=1)
    sin = np.concatenate([-np.sin(ang_r), np.sin(ang_r), -np.sin(ang_c), np.sin(ang_c)], axis=1)
    return jnp.asarray(cos, jnp.float32), jnp.asarray(sin, jnp.float32)


def ssd_mixer(p_lat, p_ctx, dt_lat, dt_ctx, conv_w, conv_b, dt_bias, a_log, d_skip):
    bsz, seq, _ = p_lat.shape
    ctx_len = p_ctx.shape[1]
    t, g, hpg = SSD_CHUNK, SSD_GROUPS, SSD_HPG
    assert seq % t == 0 and ctx_len % t == 0
    nc = (seq + ctx_len) // t
    dt = jnp.concatenate([dt_ctx[..., :DT_WIDTH], dt_lat[..., :DT_WIDTH]], axis=1)
    dt = dt.reshape(bsz, nc, t, 2, g, hpg).transpose(0, 4, 1, 3, 5, 2).reshape(bsz, g, nc, 2 * hpg, t)
    per_row = lambda v: jnp.broadcast_to(
        v.astype(jnp.float32).reshape(2, g, hpg).transpose(1, 0, 2).reshape(g, 2 * hpg, 1), (g, 2 * hpg, 128))
    dsk = jnp.broadcast_to(jnp.repeat(d_skip.astype(jnp.float32).reshape(g, hpg), SSD_HEAD_DIM, axis=1)[:, :, None],
                           (g, SSD_GW, 128))
    cos, sin = _rope_tables(seq)
    x0 = sum(MAIN_SPLITS[:6]) // SSD_GW
    b0 = (sum(MAIN_SPLITS[:6]) + SSD_WIDTH) // 128
    c0 = b0 + g
    cw = conv_w.astype(jnp.float32)
    cb = conv_b.astype(jnp.float32).reshape(1, SSD_CONV_CH)
    wb0 = SSD_WIDTH // 128
    ltot = seq + ctx_len
    f32, bf = jnp.float32, jnp.bfloat16
    y_lat, y_ctx = pl.pallas_call(
        partial(_ssd_kernel, seq=seq, ctx_len=ctx_len),
        grid=(bsz, g),
        in_specs=[pl.BlockSpec((1, seq, SSD_GW), lambda b, i: (b, 0, x0 + i)),
                  pl.BlockSpec((1, seq, 128), lambda b, i: (b, 0, b0 + i)),
                  pl.BlockSpec((1, seq, 128), lambda b, i: (b, 0, c0 + i)),
                  pl.BlockSpec((1, ctx_len, SSD_GW), lambda b, i: (b, 0, x0 + i)),
                  pl.BlockSpec((1, ctx_len, 128), lambda b, i: (b, 0, b0 + i)),
                  pl.BlockSpec((1, ctx_len, 128), lambda b, i: (b, 0, c0 + i)),
                  pl.BlockSpec((1, 1, nc, 2 * hpg, t), lambda b, i: (b, i, 0, 0, 0)),
                  pl.BlockSpec((1, 2 * hpg, 128), lambda b, i: (i, 0, 0)),
                  pl.BlockSpec((1, 2 * hpg, 128), lambda b, i: (i, 0, 0)),
                  pl.BlockSpec((1, SSD_GW, 128), lambda b, i: (i, 0, 0)),
                  pl.BlockSpec((SSD_CONV, SSD_GW), lambda b, i: (0, i)),
                  pl.BlockSpec((SSD_CONV, 128), lambda b, i: (0, wb0 + i)),
                  pl.BlockSpec((SSD_CONV, 128), lambda b, i: (0, wb0 + g + i)),
                  pl.BlockSpec((1, SSD_GW), lambda b, i: (0, i)),
                  pl.BlockSpec((1, 128), lambda b, i: (0, wb0 + i)),
                  pl.BlockSpec((1, 128), lambda b, i: (0, wb0 + g + i)),
                  pl.BlockSpec((seq, 128), lambda b, i: (0, 0)),
                  pl.BlockSpec((seq, 128), lambda b, i: (0, 0))],
        out_specs=[pl.BlockSpec((1, seq, SSD_GW), lambda b, i: (b, 0, i)),
                   pl.BlockSpec((1, ctx_len, SSD_GW), lambda b, i: (b, 0, i))],
        out_shape=[jax.ShapeDtypeStruct((bsz, seq, SSD_WIDTH), f32),
                   jax.ShapeDtypeStruct((bsz, ctx_len, SSD_WIDTH), f32)],
        scratch_shapes=[pltpu.VMEM((SSD_GW // 128, seq + 2 * CONV_PAD, 128), f32),
                        pltpu.VMEM((seq + 2 * CONV_PAD, 128), f32),
                        pltpu.VMEM((seq + 2 * CONV_PAD, 128), f32),
                        pltpu.VMEM((nc, SSD_GW, t), f32),
                        pltpu.VMEM((ltot, 128), bf),
                        pltpu.VMEM((ltot, 128), bf),
                        pltpu.VMEM((nc, 4 * 2 * hpg, t), f32),
                        pltpu.VMEM((ltot, 128), f32),
                        pltpu.VMEM((nc, 2 * hpg, 128), f32),
                        pltpu.VMEM((nc, SSD_GW, t), f32),
                        pltpu.VMEM((2, nc, SSD_GW, SSD_STATE), f32),
                        pltpu.VMEM((2, nc, SSD_GW, SSD_STATE), bf),
                        pltpu.VMEM((2, SSD_GW, SSD_STATE), f32)],
        compiler_params=pltpu.CompilerParams(dimension_semantics=("arbitrary", "arbitrary"),
                                             vmem_limit_bytes=V7X_VMEM_LIMIT_BYTES),
        name="ssd_mixer",
    )(p_lat, p_lat, p_lat, p_ctx, p_ctx, p_ctx, dt, per_row(dt_bias), per_row(a_log), dsk,
      cw, cw, cw, cb, cb, cb, cos, sin)
    return y_ctx, y_lat


INPROJ_TN = 2560
ADA_ROWS = 16


def _layer(xc, xl, c, c_ctx, w_ada, b_ada, norm_w, w_in, rpb, four_w, conv_w, conv_b, dt_bias, a_log, d_skip,
           ssd_norm_w, wb_na, wb_four, wb_ssd, w_out, update_ctx, final_norm_w):
    bsz, seq, d = xl.shape
    ctx_len = xc.shape[1]
    bf = jnp.bfloat16
    assert bsz < ADA_ROWS

    cc = jnp.concatenate([c, c_ctx[None], jnp.zeros((ADA_ROWS - bsz - 1, d), c.dtype)], axis=0)
    mod = matmul(jax.nn.silu(cc), w_ada, tm=ADA_ROWS, tn=1024) + b_ada
    sh, sc, gt = jnp.split(mod, 3, axis=-1)
    sh_l, sc_l, g_l = sh[:bsz], sc[:bsz], gt[:bsz]
    sh_c, sc_c, g_c = sh[bsz:bsz + 1], sc[bsz:bsz + 1], gt[bsz]

    w_main = jnp.concatenate([w_in[:, :DT_OFFSET].astype(bf), w_in[:, DT_OFFSET + DT_WIDTH:].astype(bf)], axis=1)
    w_dt =jnp.pad(w_in[:, DT_OFFSET:DT_OFFSET + DT_WIDTH], ((0, 0), (0, DT_PAD - DT_WIDTH))).astype(bf)

    xl2d = xl.reshape(bsz * seq, d)
    xc2d = xc.reshape(bsz * ctx_len, d)
    pl2d, dt_l = in_projection(xl2d, norm_w, sc_l[:, None], sh_l[:, None], w_main, w_dt,
                               rows_per_mod=seq, tm=1024, tn=INPROJ_TN, out_dtype=bf)
    pc2d, dt_c = in_projection(xc2d, norm_w, sc_c[:, None], sh_c[:, None], w_main, w_dt,
                               rows_per_mod=bsz * ctx_len, tm=1024, tn=INPROJ_TN, out_dtype=bf)
    p_l = pl2d.reshape(bsz, seq, MAIN_WIDTH)
    p_c = pc2d.reshape(bsz, ctx_len, MAIN_WIDTH)
    dt_l = dt_l.reshape(bsz, seq, DT_PAD)
    dt_c = dt_c.reshape(bsz, ctx_len, DT_PAD)

    four_wb, wb_nab, wb_fourb, wb_ssdb, w_outb = (t.astype(bf) for t in (four_w, wb_na, wb_four, wb_ssd, w_out))

    a_l, a_c = na_attention(p_l, p_c, na_bias_table(rpb), with_ctx=update_ctx, out_dtype=bf)
    gfo_l = fourier_branch(p_l, four_wb, tm=512)
    s_c, s_l = ssd_mixer(p_l, p_c, dt_l, dt_c, conv_w, conv_b, dt_bias, a_log, d_skip)
    m_l = merge_branches(pl2d, a_l.reshape(bsz * seq, NA_WIDTH), gfo_l.reshape(bsz * seq, FOUR_WIDTH),
                         s_l.reshape(bsz * seq, SSD_WIDTH), ssd_norm_w, wb_nab, wb_fourb, wb_ssdb, tm=512)
    if final_norm_w is None:
        xl_new = out_projection(m_l, w_outb, xl2d, g_l[:, None], rows_per_mod=seq, tm=1024, tn=1024)
    else:
        xl_new = out_projection(m_l, w_outb, xl2d, g_l[:, None], rows_per_mod=seq, tm=512, tn=d,
                                final_norm_w=final_norm_w)
    xl_new = xl_new.reshape(bsz, seq, d)
    if update_ctx:
        gfo_c = fourier_branch(p_c, four_wb, tm=ctx_len)
        m_c = merge_branches(pc2d, a_c.reshape(bsz * ctx_len, NA_WIDTH), gfo_c.reshape(bsz * ctx_len, FOUR_WIDTH),
                             s_c.reshape(bsz * ctx_len, SSD_WIDTH), ssd_norm_w, wb_nab, wb_fourb, wb_ssdb, tm=512)
        xc = out_projection(m_c, w_outb, xc2d, g_c[None, None], rows_per_mod=bsz * ctx_len, tm=1024, tn=1024)
        xc = xc.reshape(bsz, ctx_len, d)
    return xc, xl_new


def kernel(x, c, ctx, c_ctx, w_ada, b_ada, norm_w, w_in, na_rpb, four_w, ssd_conv_w, ssd_conv_b, ssd_dt_bias,
           ssd_a_log, ssd_d, ssd_norm_w, wb_na, wb_four, wb_ssd, w_out, final_norm_w):
    xc, xl = ctx, x
    for l in range(DEPTH):
        xc, xl = _layer(xc, xl, c, c_ctx, w_ada[l], b_ada[l], norm_w[l], w_in[l], na_rpb[l], four_w[l],
                        ssd_conv_w[l], ssd_conv_b[l], ssd_dt_bias[l], ssd_a_log[l], ssd_d[l], ssd_norm_w[l],
                        wb_na[l], wb_four[l], wb_ssd[l], w_out[l], update_ctx=(l < DEPTH - 1),
                        final_norm_w=final_norm_w if l == DEPTH - 1 else None)
    return xl
```

```python
import math
from functools import partial
from typing import NamedTuple

import jax
import jax.numpy as jnp
import numpy as np
from jax import lax
from jax.experimental import pallas as pl
from jax.experimental.pallas import tpu as pltpu

D_MODEL = 2048
DEPTH = 2
GRID_W = 64
EPS = 1e-6
NA_HEADS = 8
NA_HEAD_DIM = 128
NA_WIDTH = NA_HEADS * NA_HEAD_DIM
NA_WIN_R = 8
NA_WIN_C = 16
LOG2E = math.log2(math.e)
FOUR_GROUPS = 4
FOUR_GROUP_DIM = 256
FOUR_WIDTH = FOUR_GROUPS * FOUR_GROUP_DIM
SSD_HEADS = 16
SSD_HEAD_DIM = 64
SSD_WIDTH = SSD_HEADS * SSD_HEAD_DIM
SSD_GROUPS = 4
SSD_HPG = SSD_HEADS // SSD_GROUPS
SSD_STATE = 128
SSD_CONV = 7
SSD_CHUNK = 128
SSD_CONV_CH = SSD_WIDTH + 2 * SSD_GROUPS * SSD_STATE
ROPE_BASE = 10000.0
DT_WIDTH = 2 * SSD_HEADS
MAIN_SPLITS = (NA_WIDTH, NA_WIDTH, NA_WIDTH, NA_WIDTH, FOUR_WIDTH, FOUR_WIDTH, SSD_CONV_CH, SSD_WIDTH,
               D_MODEL, D_MODEL, D_MODEL)
MAIN_WIDTH = sum(MAIN_SPLITS)
DT_OFFSET = sum(MAIN_SPLITS[:8])
DT_PAD = 128


class ProjCols(NamedTuple):
    k: int
    v: int
    xbc: int


FULL_COLS = ProjCols(k=NA_WIDTH, v=2 * NA_WIDTH, xbc=sum(MAIN_SPLITS[:6]))
KV_XBC_COLS = ProjCols(k=0, v=NA_WIDTH, xbc=2 * NA_WIDTH)

V7X_VMEM_LIMIT_BYTES = 56 * 1024 * 1024


def _mm_kernel(a_ref, b_ref, o_ref):
    a = a_ref[...].astype(jnp.bfloat16)
    b = b_ref[...].astype(jnp.bfloat16)
    o_ref[...] = jnp.dot(a, b, preferred_element_type=jnp.float32).astype(o_ref.dtype)


def matmul(a, b, layer, *, tm, tn, out_dtype=jnp.float32):
    m, k = a.shape
    _, _, n = b.shape
    assert m % tm == 0 and n % tn == 0, (a.shape, b.shape, tm, tn)
    return pl.pallas_call(
        _mm_kernel,
        grid=(m // tm, n // tn),
        in_specs=[pl.BlockSpec((tm, k), lambda i, j: (i, 0)),
                  pl.BlockSpec((None, k, tn), lambda i, j: (layer, 0, j))],
        out_specs=pl.BlockSpec((tm, tn), lambda i, j: (i, j)),
        out_shape=jax.ShapeDtypeStruct((m, n), out_dtype),
        compiler_params=pltpu.CompilerParams(dimension_semantics=("arbitrary", "arbitrary"),
                                             vmem_limit_bytes=V7X_VMEM_LIMIT_BYTES),
        name="matmul",
    )(a, b)


def _inproj_kernel(x_ref, nw_ref, sc_ref, sh_ref, w_ref, wdt_ref, o_ref, dt_ref, h_ref):
    @pl.when(pl.program_id(1) == 0)
    def _():
        x = x_ref[...]
        y = x * lax.rsqrt(jnp.mean(x * x, axis=-1, keepdims=True) + EPS) * nw_ref[...]
        h = (y * (1.0 + sc_ref[0]) + sh_ref[0]).astype(jnp.bfloat16)
        h_ref[...] = h
        dt_ref[...] = jnp.dot(h, wdt_ref[...], preferred_element_type=jnp.float32)

    o_ref[...] = jnp.dot(h_ref[...], w_ref[...], preferred_element_type=jnp.float32).astype(o_ref.dtype)


def in_projection(x2d, norm_w, scale, shift, w_main, w_dt, *, rows_per_mod, tm, tn, out_dtype):
    m, d = x2d.shape
    n = w_main.shape[1]
    assert m % tm == 0 and n % tn == 0 and rows_per_mod % tm == 0
    per = rows_per_mod // tm
    return pl.pallas_call(
        _inproj_kernel,
        grid=(m // tm, n // tn),
        in_specs=[pl.BlockSpec((tm, d), lambda i, j: (i, 0)),
                  pl.BlockSpec((1, d), lambda i, j: (0, 0)),
                  pl.BlockSpec((1, 1, d), lambda i, j: (i // per, 0, 0)),
                  pl.BlockSpec((1, 1, d), lambda i, j: (i // per, 0, 0)),
                  pl.BlockSpec((d, tn), lambda i, j: (0, j)),
                  pl.BlockSpec((d, DT_PAD), lambda i, j: (0, 0))],
        out_specs=[pl.BlockSpec((tm, tn), lambda i, j: (i, j)),
                   pl.BlockSpec((tm, DT_PAD), lambda i, j: (i, 0))],
        out_shape=[jax.ShapeDtypeStruct((m, n), out_dtype),
                   jax.ShapeDtypeStruct((m, DT_PAD), jnp.float32)],
        scratch_shapes=[pltpu.VMEM((tm, d), jnp.bfloat16)],
        compiler_params=pltpu.CompilerParams(dimension_semantics=("arbitrary", "arbitrary"),
                                             vmem_limit_bytes=V7X_VMEM_LIMIT_BYTES),
        name="in_projection",
    )(x2d, norm_w.reshape(1, d), scale, shift, w_main, w_dt)


GRID_ROWS = 32
NA_WIN_TOKENS = NA_WIN_R * GRID_W
_NT = (((1,), (1,)), ((), ()))
NA_ROW_UNROLL = 8


def _na_kernel(*refs, with_ctx):
    if with_ctx:
        q_ref, k_ref, v_ref, kc_ref, vc_ref, bias_ref, qc_ref, o_ref, oc_ref = refs
    else:
        q_ref, k_ref, v_ref, kc_ref, vc_ref, bias_ref, o_ref = refs
    bf = jnp.bfloat16
    scale = NA_HEAD_DIM ** -0.5
    scale2 = scale * LOG2E
    kc = kc_ref[0].astype(bf)
    vc = vc_ref[0].astype(bf)
    nq = NA_ROW_UNROLL * GRID_W

    def lane_tiles(x, op):
        acc = x[:, 0:128]
        for l0 in range(128, x.shape[1], 128):
            acc = op(acc, x[:, l0:l0 + 128])
        return acc

    def rows(rb, carry):
        qb0 = pl.multiple_of(rb * nq, nq)
        q_all = q_ref[0, pl.ds(qb0, nq), :].astype(bf)
        s_c_all = lax.dot_general(q_all, kc, _NT, preferred_element_type=jnp.float32) * scale2
        k0s, scores, probs = [], [], []
        for i in range(NA_ROW_UNROLL):
            r = rb * NA_ROW_UNROLL + i
            start = jnp.clip(r - NA_WIN_R // 2, 0, GRID_ROWS - NA_WIN_R)
            k0 = pl.multiple_of(start * GRID_W, GRID_W)
            q = q_all[i * GRID_W:(i + 1) * GRID_W]
            kw = k_ref[0, pl.ds(k0, NA_WIN_TOKENS), :].astype(bf)
            s_w = lax.dot_general(q, kw, _NT, preferred_element_type=jnp.float32) * scale2 + bias_ref[0, r - start]
            k0s.append(k0)
            scores.append((s_w, s_c_all[i * GRID_W:(i + 1) * GRID_W]))
        for s_w, s_c in scores:
            m = jnp.max(jnp.maximum(lane_tiles(s_w, jnp.maximum), lane_tiles(s_c, jnp.maximum)),
                        axis=-1, keepdims=True)
            e_w = jnp.exp2(s_w - m)
            e_c = jnp.exp2(s_c - m)
            denom = jnp.sum(lane_tiles(e_w, jnp.add) + lane_tiles(e_c, jnp.add), axis=-1, keepdims=True)
            probs.append((e_w.astype(bf), e_c.astype(bf), denom))
        o_c_all = jnp.dot(jnp.concatenate([p[1] for p in probs], axis=0), vc, preferred_element_type=jnp.float32)
        for i, (k0, (e_w, _, denom)) in enumerate(zip(k0s, probs)):
            vw = v_ref[0, pl.ds(k0, NA_WIN_TOKENS), :].astype(bf)
            o = jnp.dot(e_w, vw, preferred_element_type=jnp.float32) + o_c_all[i * GRID_W:(i + 1) * GRID_W]
            o_ref[0, pl.ds(qb0 + i * GRID_W, GRID_W), :] = (o / denom).astype(o_ref.dtype)
        return carry

    lax.fori_loop(0, GRID_ROWS // NA_ROW_UNROLL, rows, 0)

    if with_ctx:
        s = lax.dot_general(qc_ref[0].astype(bf), kc, _NT, preferred_element_type=jnp.float32) * scale
        e = jnp.exp(s - jnp.max(s, axis=-1, keepdims=True))
        o = jnp.dot(e.astype(bf), vc, preferred_element_type=jnp.float32)
        oc_ref[0] = (o / jnp.sum(e, axis=-1, keepdims=True)).astype(oc_ref.dtype)


def na_bias_table(rpb):
    qcol = np.arange(GRID_W)[:, None]
    kcol = np.arange(GRID_W)[None, :]
    qstart = np.clip(qcol - NA_WIN_C // 2, 0, GRID_W - NA_WIN_C)
    mask = (kcol >= qstart) & (kcol < qstart + NA_WIN_C)
    dc = np.clip(kcol - qcol + NA_WIN_C - 1, 0, 2 * NA_WIN_C - 2)
    pick_col = (np.arange(2 * NA_WIN_C - 1)[:, None, None] == dc[None]).astype(np.float32)
    rows = jnp.stack([rpb.astype(jnp.float32)[:, NA_WIN_R - 1 - dl:2 * NA_WIN_R - 1 - dl] for dl in range(NA_WIN_R)],
                     axis=1)
    t = jnp.einsum("hdwc,cqk->hdqwk", rows, pick_col, precision=lax.Precision.HIGHEST)
    t = jnp.where(mask[None, None, :, None, :], t * LOG2E, -1e30)
    return t.reshape(rpb.shape[0], NA_WIN_R, GRID_W, NA_WIN_TOKENS)


def na_attention(p_lat, p_ctx, bias, *, with_ctx, out_dtype, ctx_cols=FULL_COLS):
    bsz, seq, _ = p_lat.shape
    ctx_len = p_ctx.shape[1]
    assert seq == GRID_ROWS * GRID_W
    d, h = NA_HEAD_DIM, NA_HEADS
    in_specs = [pl.BlockSpec((1, seq, d), lambda b, i: (b, 0, i)),
                pl.BlockSpec((1, seq, d), lambda b, i: (b, 0, h + i)),
                pl.BlockSpec((1, seq, d), lambda b, i: (b, 0, 2 * h + i)),
                pl.BlockSpec((1, ctx_len, d), lambda b, i: (b, 0, ctx_cols.k // d + i)),
                pl.BlockSpec((1, ctx_len, d), lambda b, i: (b, 0, ctx_cols.v // d + i)),
                pl.BlockSpec((1, NA_WIN_R, GRID_W, NA_WIN_TOKENS), lambda b, i: (i, 0, 0, 0))]
    args = [p_lat, p_lat, p_lat, p_ctx, p_ctx, bias]
    out_specs = [pl.BlockSpec((1, seq, d), lambda b, i: (b, 0, i))]
    out_shape = [jax.ShapeDtypeStruct((bsz, seq, h * d), out_dtype)]
    if with_ctx:
        in_specs.append(pl.BlockSpec((1, ctx_len, d), lambda b, i: (b, 0, i)))
        args.append(p_ctx)
        out_specs.append(pl.BlockSpec((1, ctx_len, d), lambda b, i: (b, 0, i)))
        out_shape.append(jax.ShapeDtypeStruct((bsz, ctx_len, h * d), out_dtype))
    outs = pl.pallas_call(
        partial(_na_kernel, with_ctx=with_ctx),
        grid=(bsz, h),
        in_specs=in_specs, out_specs=out_specs, out_shape=out_shape,
        compiler_params=pltpu.CompilerParams(dimension_semantics=("arbitrary", "arbitrary"),
                                             vmem_limit_bytes=V7X_VMEM_LIMIT_BYTES),
        name="na_attention",
    )(*args)
    return (outs[0], outs[1]) if with_ctx else (outs[0], None)


def _dft_cos_sin(n):
    jk = np.outer(np.arange(n), np.arange(n)) % n
    ang = 2.0 * np.pi * jk / n
    return np.cos(ang) / np.sqrt(n), np.sin(ang) / np.sqrt(n)


def _fourier_kernel(u_ref, w1_ref, a_ref, fw_ref, zf_ref, o_ref, z_ref, *, seq, row_blk):
    bf = jnp.bfloat16
    gd = FOUR_GROUP_DIM

    @pl.when(pl.program_id(1) == 0)
    def _():
        for r0 in range(0, seq, row_blk):
            for g in range(FOUR_GROUPS):
                ug = u_ref[0, r0:r0 + row_blk, g * gd:(g + 1) * gd].astype(bf)
                z = jnp.dot(ug, w1_ref[...], preferred_element_type=jnp.float32)
                z_ref[r0:r0 + row_blk, g * gd:(g + 1) * gd] = z[:, :gd].astype(bf)
                z_ref[seq + r0:seq + r0 + row_blk, g * gd:(g + 1) * gd] = z[:, gd:].astype(bf)

    y = jnp.dot(a_ref[...], z_ref[...], preferred_element_type=jnp.float32)
    f = jnp.dot(y.astype(bf), fw_ref[...], preferred_element_type=jnp.float32)
    zf = zf_ref[0].astype(jnp.float32)
    o_ref[0] = (f * (zf * jax.nn.sigmoid(zf))).astype(o_ref.dtype)


def fourier_branch(p, four_w_bf, *, tm):
    bsz, seq, _ = p.shape
    assert seq % tm == 0
    cc, sc = _dft_cos_sin(FOUR_GROUP_DIM)
    cl, sl = _dft_cos_sin(seq)
    w1 = jnp.asarray(np.concatenate([cc, sc], axis=1), jnp.bfloat16)
    a = jnp.asarray(np.concatenate([cl, -sl], axis=1), jnp.bfloat16)
    fw = FOUR_WIDTH
    return pl.pallas_call(
        partial(_fourier_kernel, seq=seq, row_blk=min(seq, 512)),
        grid=(bsz, seq // tm),
        in_specs=[pl.BlockSpec((1, seq, fw), lambda b, i: (b, 0, 4)),
                  pl.BlockSpec((FOUR_GROUP_DIM, 2 * FOUR_GROUP_DIM), lambda b, i: (0, 0)),
                  pl.BlockSpec((tm, 2 * seq), lambda b, i: (i, 0)),
                  pl.BlockSpec((fw, fw), lambda b, i: (0, 0)),
                  pl.BlockSpec((1, tm, fw), lambda b, i: (b, i, 5))],
        out_specs=pl.BlockSpec((1, tm, fw), lambda b, i: (b, i, 0)),
        out_shape=jax.ShapeDtypeStruct((bsz, seq, fw), jnp.bfloat16),
        scratch_shapes=[pltpu.VMEM((2 * seq, fw), jnp.bfloat16)],
        compiler_params=pltpu.CompilerParams(dimension_semantics=("arbitrary", "arbitrary"),
                                             vmem_limit_bytes=V7X_VMEM_LIMIT_BYTES),
        name="fourier_branch",
    )(p, w1, a, four_w_bf, p)


MERGE_TN = 1024


def _merge_kernel(a_ref, zna_ref, gfo_ref, s_ref, zs_ref, nw_ref, wa_ref, wf_ref, ws_ref, *rest):
    n_half = D_MODEL // MERGE_TN
    gate_refs, (o_ref, ap_ref, sp_ref) = rest[:3 * n_half], rest[3 * n_half:]
    bf = jnp.bfloat16
    zna = zna_ref[...].astype(jnp.float32)
    ap_ref[...] = (a_ref[...].astype(jnp.float32) * (zna * jax.nn.sigmoid(zna))).astype(bf)
    zs = zs_ref[...].astype(jnp.float32)
    t = s_ref[...].astype(jnp.float32) * (zs * jax.nn.sigmoid(zs))
    t = t * lax.rsqrt(jnp.mean(t * t, axis=-1, keepdims=True) + EPS) * nw_ref[...]
    sp_ref[...] = t.astype(bf)
    for h in range(n_half):
        cols = slice(h * MERGE_TN, (h + 1) * MERGE_TN)
        ga_ref, gf_ref, gs_ref = gate_refs[h], gate_refs[n_half + h], gate_refs[2 * n_half + h]
        o_a = jnp.dot(ap_ref[...], wa_ref[:, cols], preferred_element_type=jnp.float32)
        o_f = jnp.dot(gfo_ref[...], wf_ref[:, cols], preferred_element_type=jnp.float32)
        o_s = jnp.dot(sp_ref[...], ws_ref[:, cols], preferred_element_type=jnp.float32)
        m = (jax.nn.sigmoid(ga_ref[...].astype(jnp.float32)) * o_a
             + jax.nn.sigmoid(gf_ref[...].astype(jnp.float32)) * o_f
             + jax.nn.sigmoid(gs_ref[...].astype(jnp.float32)) * o_s)
        o_ref[:, cols] = m.astype(o_ref.dtype)


def merge_branches(p2d, a2d, gfo2d, s2d, ssd_norm_w, wa, wf, ws, *, tm):
    m = p2d.shape[0]
    w, tn = NA_WIDTH, MERGE_TN
    n_half = D_MODEL // tn
    assert m % tm == 0
    row = lambda i: (i, 0)
    resident = lambda shape: pl.BlockSpec(shape, lambda i: (0, 0), pipeline_mode=pl.Buffered(1))
    gate0 = DT_OFFSET // tn
    gate_specs = [pl.BlockSpec((tm, tn), partial(lambda i, blk: (i, blk), blk=gate0 + g * n_half + h))
                  for g in range(3) for h in range(n_half)]
    return pl.pallas_call(
        _merge_kernel,
        grid=(m // tm,),
        in_specs=[pl.BlockSpec((tm, w), row),
                  pl.BlockSpec((tm, w), lambda i: (i, 3)),
                  pl.BlockSpec((tm, w), row),
                  pl.BlockSpec((tm, w), row),
                  pl.BlockSpec((tm, w), lambda i: (i, 8)),
                  resident((1, w)),
                  resident((w, D_MODEL)), resident((w, D_MODEL)), resident((w, D_MODEL))] + gate_specs,
        out_specs=pl.BlockSpec((tm, D_MODEL), row),
        out_shape=jax.ShapeDtypeStruct((m, D_MODEL), jnp.bfloat16),
        scratch_shapes=[pltpu.VMEM((tm, w), jnp.bfloat16), pltpu.VMEM((tm, w), jnp.bfloat16)],
        compiler_params=pltpu.CompilerParams(dimension_semantics=("arbitrary",),
                                             vmem_limit_bytes=V7X_VMEM_LIMIT_BYTES),
        name="merge_branches",
    )(a2d, p2d, gfo2d, s2d, p2d, ssd_norm_w.reshape(1, w), wa, wf, ws, *([p2d] * (3 * n_half)))


def _outproj_kernel(m_ref, w_ref, x_ref, g_ref, *rest, final_norm):
    y = jnp.dot(m_ref[...], w_ref[...], preferred_element_type=jnp.float32)
    r = x_ref[...] + g_ref[0] * y
    if final_norm:
        nw_ref, o_ref = rest
        o_ref[...] = r * lax.rsqrt(jnp.mean(r * r, axis=-1, keepdims=True) + EPS) * nw_ref[...]
    else:
        rest[0][...] = r


def out_projection(m2d, w_out_bf, x2d, gate, *, rows_per_mod, tm, tn, final_norm_w=None):
    m, d = x2d.shape
    assert m % tm == 0 and d % tn == 0 and rows_per_mod % tm == 0
    per = rows_per_mod // tm
    in_specs = [pl.BlockSpec((tm, d), lambda i, j: (i, 0)),
                pl.BlockSpec((d, tn), lambda i, j: (0, j)),
                pl.BlockSpec((tm, tn), lambda i, j: (i, j)),
                pl.BlockSpec((1, 1, tn), lambda i, j: (i // per, 0, j))]
    args = [m2d, w_out_bf, x2d, gate]
    if final_norm_w is not None:
        assert tn == d
        in_specs.append(pl.BlockSpec((1, d), lambda i, j: (0, 0)))
        args.append(final_norm_w.astype(jnp.float32).reshape(1, d))
    return pl.pallas_call(
        partial(_outproj_kernel, final_norm=final_norm_w is not None),
        grid=(m // tm, d // tn),
        in_specs=in_specs,
        out_specs=pl.BlockSpec((tm, tn), lambda i, j: (i, j)),
        out_shape=jax.ShapeDtypeStruct((m, d), jnp.float32),
        compiler_params=pltpu.CompilerParams(dimension_semantics=("arbitrary", "arbitrary"),
                                             vmem_limit_bytes=V7X_VMEM_LIMIT_BYTES),
        name="out_projection",
    )(*args)


SSD_GW = SSD_HPG * SSD_HEAD_DIM
CONV_PAD = 8
SSD_CHUNK_UNROLL = 2


def _split3_bf16(a):
    rnd = lambda v: v.astype(jnp.bfloat16).astype(jnp.float32)
    a1 = rnd(a)
    a2 = rnd(a - a1)
    a3 = rnd((a - a1) - a2)
    return a1, a2, a3


def _head_rows(rows, r0, n):
    return jnp.concatenate([jnp.broadcast_to(rows[r0 + k:r0 + k + 1, :], (SSD_HEAD_DIM, n))
                            for k in range(SSD_HPG)], axis=0)


def _ssd_kernel(xl_ref, bl_ref, cl_ref, xc_ref, bc_ref, cc_ref, dtl_ref, dtc_ref, bias_ref, alog_ref, dsk_ref,
                wx_ref, wb_ref, wc_ref, bx_ref, bb_ref, bcb_ref, cos_ref, sin_ref,
                yl_ref, yc_ref,
                padx, padb, padc, xt_s, b_s, c_s, dtt_s, row_s, col_s, dec_s, yt_s, p_s, sp_s, st_s, *, seq, ctx_len):
    f32, bf = jnp.float32, jnp.bfloat16
    t = SSD_CHUNK
    nc_c, nc_l = ctx_len // t, seq // t
    nc = nc_c + nc_l

    lane = lax.broadcasted_iota(jnp.int32, (t, 128), 1)
    first_half = (lane % 64) < 32

    def conv_tile(pad, w_ref, b_ref, t0, l0):
        acc = jnp.broadcast_to(b_ref[:, l0:l0 + 128], (t, 128))
        for j in range(SSD_CONV):
            off = CONV_PAD - SSD_CONV // 2 + j
            acc = acc + pad[pl.ds(t0 + off, t), :] * w_ref[j:j + 1, l0:l0 + 128]
        return acc * jax.nn.sigmoid(acc)

    def rope(u, t0):
        swapped = jnp.where(first_half, pltpu.roll(u, 96, axis=1), pltpu.roll(u, 32, axis=1))
        return u * cos_ref[pl.ds(t0, t), :] + swapped * sin_ref[pl.ds(t0, t), :]

    def conv_seq(x_ref, bm_ref, cm_ref, n, base, use_rope):
        slabs = ([(padx.at[h], x_ref, h * 128) for h in range(SSD_GW // 128)]
                 + [(padb, bm_ref, 0), (padc, cm_ref, 0)])
        for pad, src, l0 in slabs:
            pad[0:CONV_PAD, :] = jnp.zeros((CONV_PAD, 128), f32)
            pad[CONV_PAD + n:2 * CONV_PAD + n, :] = jnp.zeros((CONV_PAD, 128), f32)
            pad[CONV_PAD:CONV_PAD + n, :] = src[0, :, l0:l0 + 128].astype(f32)

        def chunk(ci, carry):
            t0 = pl.multiple_of(ci * t, t)
            r0 = pl.multiple_of(base + ci * t, t)
            for h in range(SSD_GW // 128):
                l0 = h * 128
                xt_s[base // t + ci, l0:l0 + 128, :] = conv_tile(padx.at[h], wx_ref, bx_ref, t0, l0).T
            ub = conv_tile(padb, wb_ref, bb_ref, t0, 0)
            uc = conv_tile(padc, wc_ref, bcb_ref, t0, 0)
            if use_rope:
                ub, uc = rope(ub, t0), rope(uc, t0)
            b_s[pl.ds(r0, t), :] = ub.astype(bf)
            c_s[pl.ds(r0, t), :] = uc.astype(bf)
            return carry

        lax.fori_loop(0, n // t, chunk, 0)

    conv_seq(xc_ref, bc_ref, cc_ref, ctx_len, 0, False)
    conv_seq(xl_ref, bl_ref, cl_ref, seq, ctx_len, True)

    i0 = lax.broadcasted_iota(jnp.int32, (t, t), 0)
    i1 = lax.broadcasted_iota(jnp.int32, (t, t), 1)
    tri_f = (i0 <= i1).astype(bf)
    tri_r = (i0 >= i1).astype(bf)
    nr = 2 * SSD_HPG
    fwd_row = (lax.broadcasted_iota(jnp.int32, (nc * nr, 1), 0) % nr) < SSD_HPG
    g8 = pl.multiple_of(pl.program_id(1) * nr, nr)
    raw = []
    for c in range(nc):
        src, c0 = (dtc_ref, c) if c < nc_c else (dtl_ref, c - nc_c)
        dtt_s[c] = src[0, c0 * t:(c0 + 1) * t, :].T
        raw.append(dtt_s[c, pl.ds(g8, nr), :])
    x = jnp.concatenate(raw, axis=0) + jnp.tile(bias_ref[0], (nc, 1))
    dt = jnp.maximum(x, 0.0) + jnp.log1p(jnp.exp(-jnp.abs(x)))
    pieces = _split3_bf16(dt * jnp.tile(-jnp.exp(alog_ref[0]), (nc, 1)))
    pf = [jnp.dot(p.astype(bf), tri_f, preferred_element_type=f32) for p in pieces]
    pr = [jnp.dot(p.astype(bf), tri_r, preferred_element_type=f32) for p in pieces]
    cs = jnp.where(fwd_row, (pf[0] + pf[1]) + pf[2], (pr[0] + pr[1]) + pr[2])
    tot = jnp.where(fwd_row, cs[:, t - 1:t], cs[:, 0:1])
    e_in = jnp.exp(cs)
    e_out = dt * jnp.exp(tot - cs)
    decb = jnp.broadcast_to(jnp.exp(tot), (nc * nr, 128))
    for c in range(nc):
        sl = slice(c * nr, (c + 1) * nr)
        tile = jnp.concatenate([cs[sl], dt[sl], e_in[sl], e_out[sl], jnp.zeros((t - 4 * nr, t), f32)], axis=0)
        row_s[c] = tile[0:4 * nr]
        col_s[c * t:(c + 1) * t, :] = tile.T
        dec_s[c] = decb[sl]

    lower = i1 <= i0
    upper = i1 >= i0
    neg_inf = jnp.float32(-jnp.inf)

    def chunks_a(i, carry):
        ids = [i * SSD_CHUNK_UNROLL + u for u in range(SSD_CHUNK_UNROLL)]
        r0s = [pl.multiple_of(c * t, t) for c in ids]
        cms = [c_s[pl.ds(r0, t), :] for r0 in r0s]
        bms = [b_s[pl.ds(r0, t), :] for r0 in r0s]
        gs = [lax.dot_general(cm, bm, _NT, preferred_element_type=f32) for cm, bm in zip(cms, bms)]
        for c, r0, bm, g in zip(ids, r0s, bms, gs):
            rows = row_s[c]
            cols = col_s[pl.ds(r0, t), :]
            xt = xt_s[c]
            xt_bf = xt.astype(bf)
            parts = []
            for k in range(SSD_HPG):
                kr = SSD_HPG + k
                lf = jnp.exp(jnp.where(lower, cols[:, k:k + 1] - rows[k:k + 1, :], neg_inf)) * rows[8 + k:9 + k, :]
                lr = (jnp.exp(jnp.where(upper, cols[:, kr:kr + 1] - rows[kr:kr + 1, :], neg_inf))
                      * rows[8 + kr:9 + kr, :])
                mk = (g * (lf + lr)).astype(bf)
                xk = xt_bf[k * SSD_HEAD_DIM:(k + 1) * SSD_HEAD_DIM, :]
                parts.append(lax.dot_general(xk, mk, _NT, preferred_element_type=f32))
            yt_s[c] = dsk_ref[0] * xt + jnp.concatenate(parts, axis=0)
            for d in range(2):
                wt = (xt * _head_rows(rows, 24 + SSD_HPG * d, t)).astype(bf)
                p_s[d, c] = jnp.dot(wt, bm, preferred_element_type=f32)
        return carry

    lax.fori_loop(0, nc // SSD_CHUNK_UNROLL, chunks_a, 0)

    st_s[...] = jnp.zeros(st_s.shape, f32)

    def state_step(c, d):
        st = st_s[d]
        sp_s[d, c] = st.astype(bf)
        st_s[d] = st * _head_rows(dec_s[c], SSD_HPG * d, SSD_STATE) + p_s[d, c]

    def ctx_step(i, carry):
        state_step(i, 0)
        state_step(nc_c - 1 - i, 1)
        return carry

    def lat_step(i, carry):
        state_step(nc_c + i, 0)
        state_step(nc - 1 - i, 1)
        return carry

    lax.fori_loop(0, nc_c, ctx_step, 0)
    lax.fori_loop(0, nc_l, lat_step, 0)

    def chunks_c(first, out_ref, o_first):
        ids = [first + u for u in range(SSD_CHUNK_UNROLL)]
        cms = [c_s[pl.ds(pl.multiple_of(c * t, t), t), :] for c in ids]
        offs = [[lax.dot_general(sp_s[d, c], cm, _NT, preferred_element_type=f32) for d in range(2)]
                for c, cm in zip(ids, cms)]
        for u, (c, off) in enumerate(zip(ids, offs)):
            rows = row_s[c]
            yt = yt_s[c]
            for d in range(2):
                yt = yt + off[d] * _head_rows(rows, 16 + SSD_HPG * d, t)
            for l0 in range(0, SSD_GW, 128):
                out_ref[0, pl.ds(pl.multiple_of((o_first + u) * t, t), t), l0:l0 + 128] = yt[l0:l0 + 128, :].T

    def ctx_out(i, carry):
        chunks_c(i * SSD_CHUNK_UNROLL, yc_ref, i * SSD_CHUNK_UNROLL)
        return carry

    def lat_out(i, carry):
        chunks_c(nc_c + i * SSD_CHUNK_UNROLL, yl_ref, i * SSD_CHUNK_UNROLL)
        return carry

    lax.fori_loop(0, nc_c // SSD_CHUNK_UNROLL, ctx_out, 0)
    lax.fori_loop(0, nc_l // SSD_CHUNK_UNROLL, lat_out, 0)


def _rope_tables(seq):
    quarter = SSD_STATE // 4
    inv = ROPE_BASE ** (-np.arange(quarter, dtype=np.float64) / quarter)
    pos = np.arange(seq)
    ang_r = (pos // GRID_W)[:, None] * inv
    ang_c = (pos % GRID_W)[:, None] * inv
    cos = np.concatenate([np.cos(ang_r)] * 2 + [np.cos(ang_c)] * 2, axis=1)
    sin = np.concatenate([-np.sin(ang_r), np.sin(ang_r), -np.sin(ang_c), np.sin(ang_c)], axis=1)
    return jnp.asarray(cos, jnp.float32), jnp.asarray(sin, jnp.float32)


def ssd_mixer(p_lat, p_ctx, dt_lat, dt_ctx, conv_w, conv_b, dt_bias, a_log, d_skip, ctx_cols=FULL_COLS):
    bsz, seq, _ = p_lat.shape
    ctx_len = p_ctx.shape[1]
    t, g, hpg = SSD_CHUNK, SSD_GROUPS, SSD_HPG
    assert seq % t == 0 and ctx_len % t == 0
    nc = (seq + ctx_len) // t
    per_row = lambda v: jnp.broadcast_to(
        v.astype(jnp.float32).reshape(2, g, hpg).transpose(1, 0, 2).reshape(g, 2 * hpg, 1), (g, 2 * hpg, 128))
    dsk = jnp.broadcast_to(jnp.repeat(d_skip.astype(jnp.float32).reshape(g, hpg), SSD_HEAD_DIM, axis=1)[:, :, None],
                           (g, SSD_GW, 128))
    cos, sin = _rope_tables(seq)
    x0 = sum(MAIN_SPLITS[:6]) // SSD_GW
    b0 = (sum(MAIN_SPLITS[:6]) + SSD_WIDTH) // 128
    c0 = b0 + g
    cw = conv_w.astype(jnp.float32)
    cb = conv_b.astype(jnp.float32).reshape(1, SSD_CONV_CH)
    wb0 = SSD_WIDTH // 128
    ltot = seq + ctx_len
    f32, bf = jnp.float32, jnp.bfloat16
    y_lat, y_ctx = pl.pallas_call(
        partial(_ssd_kernel, seq=seq, ctx_len=ctx_len),
        grid=(bsz, g),
        in_specs=[pl.BlockSpec((1, seq, SSD_GW), lambda b, i: (b, 0, x0 + i)),
                  pl.BlockSpec((1, seq, 128), lambda b, i: (b, 0, b0 + i)),
                  pl.BlockSpec((1, seq, 128), lambda b, i: (b, 0, c0 + i)),
                  pl.BlockSpec((1, ctx_len, SSD_GW), lambda b, i: (b, 0, ctx_cols.xbc // SSD_GW + i)),
                  pl.BlockSpec((1, ctx_len, 128), lambda b, i: (b, 0, (ctx_cols.xbc + SSD_WIDTH) // 128 + i)),
                  pl.BlockSpec((1, ctx_len, 128), lambda b, i: (b, 0, (ctx_cols.xbc + SSD_WIDTH) // 128 + g + i)),
                  pl.BlockSpec((1, seq, DT_PAD), lambda b, i: (b, 0, 0)),
                  pl.BlockSpec((1, ctx_len, DT_PAD), lambda b, i: (b, 0, 0)),
                  pl.BlockSpec((1, 2 * hpg, 128), lambda b, i: (i, 0, 0)),
                  pl.BlockSpec((1, 2 * hpg, 128), lambda b, i: (i, 0, 0)),
                  pl.BlockSpec((1, SSD_GW, 128), lambda b, i: (i, 0, 0)),
                  pl.BlockSpec((SSD_CONV, SSD_GW), lambda b, i: (0, i)),
                  pl.BlockSpec((SSD_CONV, 128), lambda b, i: (0, wb0 + i)),
                  pl.BlockSpec((SSD_CONV, 128), lambda b, i: (0, wb0 + g + i)),
                  pl.BlockSpec((1, SSD_GW), lambda b, i: (0, i)),
                  pl.BlockSpec((1, 128), lambda b, i: (0, wb0 + i)),
                  pl.BlockSpec((1, 128), lambda b, i: (0, wb0 + g + i)),
                  pl.BlockSpec((seq, 128), lambda b, i: (0, 0)),
                  pl.BlockSpec((seq, 128), lambda b, i: (0, 0))],
        out_specs=[pl.BlockSpec((1, seq, SSD_GW), lambda b, i: (b, 0, i)),
                   pl.BlockSpec((1, ctx_len, SSD_GW), lambda b, i: (b, 0, i))],
        out_shape=[jax.ShapeDtypeStruct((bsz, seq, SSD_WIDTH), f32),
                   jax.ShapeDtypeStruct((bsz, ctx_len, SSD_WIDTH), f32)],
        scratch_shapes=[pltpu.VMEM((SSD_GW // 128, seq + 2 * CONV_PAD, 128), f32),
                        pltpu.VMEM((seq + 2 * CONV_PAD, 128), f32),
                        pltpu.VMEM((seq + 2 * CONV_PAD, 128), f32),
                        pltpu.VMEM((nc, SSD_GW, t), f32),
                        pltpu.VMEM((ltot, 128), bf),
                        pltpu.VMEM((ltot, 128), bf),
                        pltpu.VMEM((nc, DT_PAD, t), f32),
                        pltpu.VMEM((nc, 4 * 2 * hpg, t), f32),
                        pltpu.VMEM((ltot, 128), f32),
                        pltpu.VMEM((nc, 2 * hpg, 128), f32),
                        pltpu.VMEM((nc, SSD_GW, t), f32),
                        pltpu.VMEM((2, nc, SSD_GW, SSD_STATE), f32),
                        pltpu.VMEM((2, nc, SSD_GW, SSD_STATE), bf),
                        pltpu.VMEM((2, SSD_GW, SSD_STATE), f32)],
        compiler_params=pltpu.CompilerParams(dimension_semantics=("arbitrary", "arbitrary"),
                                             vmem_limit_bytes=V7X_VMEM_LIMIT_BYTES),
        name="ssd_mixer",
    )(p_lat, p_lat, p_lat, p_ctx, p_ctx, p_ctx, dt_lat, dt_ctx, per_row(dt_bias), per_row(a_log), dsk,
      cw, cw, cw, cb, cb, cb, cos, sin)
    return y_ctx, y_lat


INPROJ_TN = 2560
ADA_ROWS = 16


def _layer(xc, xl, c, c_ctx, w_ada_all, layer, b_ada, norm_w, w_in, rpb, four_w, conv_w, conv_b, dt_bias, a_log,
           d_skip, ssd_norm_w, wb_na, wb_four, wb_ssd, w_out, update_ctx, final_norm_w):
    bsz, seq, d = xl.shape
    ctx_len = xc.shape[1]
    bf = jnp.bfloat16
    assert bsz < ADA_ROWS

    cc = jnp.concatenate([c, c_ctx[None], jnp.zeros((ADA_ROWS - bsz - 1, d), c.dtype)], axis=0)
    mod = matmul(jax.nn.silu(cc), w_ada_all, layer, tm=ADA_ROWS, tn=1024) + b_ada
    sh, sc, gt = jnp.split(mod, 3, axis=-1)
    sh_l, sc_l, g_l = sh[:bsz], sc[:bsz], gt[:bsz]
    sh_c, sc_c, g_c = sh[bsz:bsz + 1], sc[bsz:bsz + 1], gt[bsz]

    w_main = jnp.concatenate([w_in[:, :DT_OFFSET].astype(bf), w_in[:, DT_OFFSET + DT_WIDTH:].astype(bf)], axis=1)
    w_dt = w_in[:, DT_OFFSET:DT_OFFSET + DT_WIDTH].reshape(d, 2, SSD_GROUPS, SSD_HPG).transpose(0, 2, 1, 3)
    w_dt = jnp.pad(w_dt.reshape(d, DT_WIDTH), ((0, 0), (0, DT_PAD - DT_WIDTH))).astype(bf)

    xl2d = xl.reshape(bsz * seq, d)
    xc2d = xc.reshape(bsz * ctx_len, d)
    pl2d, dt_l = in_projection(xl2d, norm_w, sc_l[:, None], sh_l[:, None], w_main, w_dt,
                               rows_per_mod=seq, tm=1024, tn=INPROJ_TN, out_dtype=bf)
    if update_ctx:
        ctx_cols, w_ctx, ctx_tn = FULL_COLS, w_main, INPROJ_TN
    else:
        ctx_cols = KV_XBC_COLS
        w_ctx = jnp.concatenate([w_main[:, FULL_COLS.k:FULL_COLS.v + NA_WIDTH],
                                 w_main[:, FULL_COLS.xbc:FULL_COLS.xbc + SSD_CONV_CH]], axis=1)
        ctx_tn = w_ctx.shape[1] // 2
    pc2d, dt_c = in_projection(xc2d, norm_w, sc_c[:, None], sh_c[:, None], w_ctx, w_dt,
                               rows_per_mod=bsz * ctx_len, tm=1024, tn=ctx_tn, out_dtype=bf)
    p_l = pl2d.reshape(bsz, seq, MAIN_WIDTH)
    p_c = pc2d.reshape(bsz, ctx_len, w_ctx.shape[1])
    dt_l = dt_l.reshape(bsz, seq, DT_PAD)
    dt_c = dt_c.reshape(bsz, ctx_len, DT_PAD)

    four_wb, wb_nab, wb_fourb, wb_ssdb, w_outb = (t.astype(bf) for t in (four_w, wb_na, wb_four, wb_ssd, w_out))

    a_l, a_c = na_attention(p_l, p_c, na_bias_table(rpb), with_ctx=update_ctx, out_dtype=bf, ctx_cols=ctx_cols)
    gfo_l = fourier_branch(p_l, four_wb, tm=512)
    s_c, s_l = ssd_mixer(p_l, p_c, dt_l, dt_c, conv_w, conv_b, dt_bias, a_log, d_skip, ctx_cols=ctx_cols)
    m_l = merge_branches(pl2d, a_l.reshape(bsz * seq, NA_WIDTH), gfo_l.reshape(bsz * seq, FOUR_WIDTH),
                         s_l.reshape(bsz * seq, SSD_WIDTH), ssd_norm_w, wb_nab, wb_fourb, wb_ssdb, tm=512)
    if final_norm_w is None:
        xl_new = out_projection(m_l, w_outb, xl2d, g_l[:, None], rows_per_mod=seq, tm=1024, tn=1024)
    else:
        xl_new = out_projection(m_l, w_outb, xl2d, g_l[:, None], rows_per_mod=seq, tm=512, tn=d,
                                final_norm_w=final_norm_w)
    xl_new = xl_new.reshape(bsz, seq, d)
    if update_ctx:
        gfo_c = fourier_branch(p_c, four_wb, tm=ctx_len)
        m_c = merge_branches(pc2d, a_c.reshape(bsz * ctx_len, NA_WIDTH), gfo_c.reshape(bsz * ctx_len, FOUR_WIDTH),
                             s_c.reshape(bsz * ctx_len, SSD_WIDTH), ssd_norm_w, wb_nab, wb_fourb, wb_ssdb, tm=512)
        xc = out_projection(m_c, w_outb, xc2d, g_c[None, None], rows_per_mod=bsz * ctx_len, tm=1024, tn=1024)
        xc = xc.reshape(bsz, ctx_len, d)
    return xc, xl_new


def kernel(x, c, ctx, c_ctx, w_ada, b_ada, norm_w, w_in, na_rpb, four_w, ssd_conv_w, ssd_conv_b, ssd_dt_bias,
           ssd_a_log, ssd_d, ssd_norm_w, wb_na, wb_four, wb_ssd, w_out, final_norm_w):
    xc, xl = ctx, x
    for l in range(DEPTH):
        xc, xl = _layer(xc, xl, c, c_ctx, w_ada, l, b_ada[l], norm_w[l], w_in[l], na_rpb[l], four_w[l],
                        ssd_conv_w[l], ssd_conv_b[l], ssd_dt_bias[l], ssd_a_log[l], ssd_d[l], ssd_norm_w[l],
                        wb_na[l], wb_four[l], wb_ssd[l], w_out[l], update_ctx=(l < DEPTH - 1),
                        final_norm_w=final_norm_w if l == DEPTH - 1 else None)
    return xl
```

```python
import math
from functools import partial
from typing import NamedTuple

import jax
import jax.numpy as jnp
import numpy as np
from jax import lax
from jax.experimental import pallas as pl
from jax.experimental.pallas import tpu as pltpu

D_MODEL = 2048
DEPTH = 2
GRID_W = 64
EPS = 1e-6
NA_HEADS = 8
NA_HEAD_DIM = 128
NA_WIDTH = NA_HEADS * NA_HEAD_DIM
NA_WIN_R = 8
NA_WIN_C = 16
LOG2E = math.log2(math.e)
FOUR_GROUPS = 4
FOUR_GROUP_DIM = 256
FOUR_WIDTH = FOUR_GROUPS * FOUR_GROUP_DIM
SSD_HEADS = 16
SSD_HEAD_DIM = 64
SSD_WIDTH = SSD_HEADS * SSD_HEAD_DIM
SSD_GROUPS = 4
SSD_HPG = SSD_HEADS // SSD_GROUPS
SSD_STATE = 128
SSD_CONV = 7
SSD_CHUNK = 128
SSD_CONV_CH = SSD_WIDTH + 2 * SSD_GROUPS * SSD_STATE
ROPE_BASE = 10000.0
DT_WIDTH = 2 * SSD_HEADS
MAIN_SPLITS = (NA_WIDTH, NA_WIDTH, NA_WIDTH, NA_WIDTH, FOUR_WIDTH, FOUR_WIDTH, SSD_CONV_CH, SSD_WIDTH,
               D_MODEL, D_MODEL, D_MODEL)
MAIN_WIDTH = sum(MAIN_SPLITS)
DT_OFFSET = sum(MAIN_SPLITS[:8])
DT_PAD = 128


class ProjCols(NamedTuple):
    k: int
    v: int
    xbc: int


FULL_COLS = ProjCols(k=NA_WIDTH, v=2 * NA_WIDTH, xbc=sum(MAIN_SPLITS[:6]))
KV_XBC_COLS = ProjCols(k=0, v=NA_WIDTH, xbc=2 * NA_WIDTH)

V7X_VMEM_LIMIT_BYTES = 56 * 1024 * 1024


def _mm_kernel(a_ref, b_ref, o_ref):
    a = a_ref[...].astype(jnp.bfloat16)
    b = b_ref[...].astype(jnp.bfloat16)
    o_ref[...] = jnp.dot(a, b, preferred_element_type=jnp.float32).astype(o_ref.dtype)


def matmul(a, b, layer, *, tm, tn, out_dtype=jnp.float32):
    m, k = a.shape
    _, _, n = b.shape
    assert m % tm == 0 and n % tn == 0, (a.shape, b.shape, tm, tn)
    return pl.pallas_call(
        _mm_kernel,
        grid=(m // tm, n // tn),
        in_specs=[pl.BlockSpec((tm, k), lambda i, j: (i, 0)),
                  pl.BlockSpec((None, k, tn), lambda i, j: (layer, 0, j))],
        out_specs=pl.BlockSpec((tm, tn), lambda i, j: (i, j)),
        out_shape=jax.ShapeDtypeStruct((m, n), out_dtype),
        compiler_params=pltpu.CompilerParams(dimension_semantics=("arbitrary", "arbitrary"),
                                             vmem_limit_bytes=V7X_VMEM_LIMIT_BYTES),
        name="matmul",
    )(a, b)


REPACK_TN = 1024


def _repack_kernel(a_ref, nxt_ref, o_ref):
    j = pl.program_id(1)

    @pl.when(j < DT_OFFSET // REPACK_TN)
    def _():
        o_ref[...] = a_ref[...].astype(o_ref.dtype)

    @pl.when(j >= DT_OFFSET // REPACK_TN)
    def _():
        shifted = jnp.concatenate([a_ref[:, DT_WIDTH:], nxt_ref[:, :DT_WIDTH]], axis=1)
        o_ref[...] = shifted.astype(o_ref.dtype)


def repack_w_in(w_in):
    depth, d, _ = w_in.shape
    tn = REPACK_TN
    return pl.pallas_call(
        _repack_kernel,
        grid=(depth, MAIN_WIDTH // tn),
        in_specs=[pl.BlockSpec((None, d, tn), lambda l, j: (l, 0, j)),
                  pl.BlockSpec((None, d, 128), lambda l, j: (l, 0, (j + 1) * (tn // 128)))],
        out_specs=pl.BlockSpec((None, d, tn), lambda l, j: (l, 0, j)),
        out_shape=jax.ShapeDtypeStruct((depth, d, MAIN_WIDTH), jnp.bfloat16),
        compiler_params=pltpu.CompilerParams(dimension_semantics=("arbitrary", "arbitrary"),
                                             vmem_limit_bytes=V7X_VMEM_LIMIT_BYTES),
        name="repack_w_in",
    )(w_in, w_in)


def _inproj_kernel(x_ref, nw_ref, sc_ref, sh_ref, w_ref, wdt_ref, o_ref, dt_ref, h_ref):
    @pl.when(pl.program_id(1) == 0)
    def _():
        x = x_ref[...]
        y = x * lax.rsqrt(jnp.mean(x * x, axis=-1, keepdims=True) + EPS) * nw_ref[...]
        h = (y * (1.0 + sc_ref[0]) + sh_ref[0]).astype(jnp.bfloat16)
        h_ref[...] = h
        dt_ref[...] = jnp.dot(h, wdt_ref[...], preferred_element_type=jnp.float32)

    o_ref[...] = jnp.dot(h_ref[...], w_ref[...], preferred_element_type=jnp.float32).astype(o_ref.dtype)


def in_projection(x2d, norm_w, scale, shift, w_main, layer, w_dt, *, rows_per_mod, tm, tn, out_dtype):
    m, d = x2d.shape
    n = w_main.shape[2]
    assert m % tm == 0 and n % tn == 0 and rows_per_mod % tm == 0
    per = rows_per_mod // tm
    return pl.pallas_call(
        _inproj_kernel,
        grid=(m // tm, n // tn),
        in_specs=[pl.BlockSpec((tm, d), lambda i, j: (i, 0)),
                  pl.BlockSpec((1, d), lambda i, j: (0, 0)),
                  pl.BlockSpec((1, 1, d), lambda i, j: (i // per, 0, 0)),
                  pl.BlockSpec((1, 1, d), lambda i, j: (i // per, 0, 0)),
                  pl.BlockSpec((None, d, tn), lambda i, j: (layer, 0, j)),
                  pl.BlockSpec((d, DT_PAD), lambda i, j: (0, 0))],
        out_specs=[pl.BlockSpec((tm, tn), lambda i, j: (i, j)),
                   pl.BlockSpec((tm, DT_PAD), lambda i, j: (i, 0))],
        out_shape=[jax.ShapeDtypeStruct((m, n), out_dtype),
                   jax.ShapeDtypeStruct((m, DT_PAD), jnp.float32)],
        scratch_shapes=[pltpu.VMEM((tm, d), jnp.bfloat16)],
        compiler_params=pltpu.CompilerParams(dimension_semantics=("arbitrary", "arbitrary"),
                                             vmem_limit_bytes=V7X_VMEM_LIMIT_BYTES),
        name="in_projection",
    )(x2d, norm_w.reshape(1, d), scale, shift, w_main, w_dt)


GRID_ROWS = 32
NA_WIN_TOKENS = NA_WIN_R * GRID_W
_NT = (((1,), (1,)), ((), ()))
NA_ROW_UNROLL = 8


def _na_kernel(*refs, with_ctx):
    if with_ctx:
        q_ref, k_ref, v_ref, kc_ref, vc_ref, bias_ref, qc_ref, o_ref, oc_ref = refs
    else:
        q_ref, k_ref, v_ref, kc_ref, vc_ref, bias_ref, o_ref = refs
    bf = jnp.bfloat16
    scale = NA_HEAD_DIM ** -0.5
    scale2 = scale * LOG2E
    kc = kc_ref[0].astype(bf)
    vc = vc_ref[0].astype(bf)
    nq = NA_ROW_UNROLL * GRID_W

    def lane_tiles(x, op):
        acc = x[:, 0:128]
        for l0 in range(128, x.shape[1], 128):
            acc = op(acc, x[:, l0:l0 + 128])
        return acc

    def rows(rb, carry):
        qb0 = pl.multiple_of(rb * nq, nq)
        q_all = q_ref[0, pl.ds(qb0, nq), :].astype(bf)
        s_c_all = lax.dot_general(q_all, kc, _NT, preferred_element_type=jnp.float32) * scale2
        k0s, scores, probs = [], [], []
        for i in range(NA_ROW_UNROLL):
            r = rb * NA_ROW_UNROLL + i
            start = jnp.clip(r - NA_WIN_R // 2, 0, GRID_ROWS - NA_WIN_R)
            k0 = pl.multiple_of(start * GRID_W, GRID_W)
            q = q_all[i * GRID_W:(i + 1) * GRID_W]
            kw = k_ref[0, pl.ds(k0, NA_WIN_TOKENS), :].astype(bf)
            s_w = lax.dot_general(q, kw, _NT, preferred_element_type=jnp.float32) * scale2 + bias_ref[0, r - start]
            k0s.append(k0)
            scores.append((s_w, s_c_all[i * GRID_W:(i + 1) * GRID_W]))
        for s_w, s_c in scores:
            m = jnp.max(jnp.maximum(lane_tiles(s_w, jnp.maximum), lane_tiles(s_c, jnp.maximum)),
                        axis=-1, keepdims=True)
            e_w = jnp.exp2(s_w - m)
            e_c = jnp.exp2(s_c - m)
            denom = jnp.sum(lane_tiles(e_w, jnp.add) + lane_tiles(e_c, jnp.add), axis=-1, keepdims=True)
            probs.append((e_w.astype(bf), e_c.astype(bf), denom))
        o_c_all = jnp.dot(jnp.concatenate([p[1] for p in probs], axis=0), vc, preferred_element_type=jnp.float32)
        for i, (k0, (e_w, _, denom)) in enumerate(zip(k0s, probs)):
            vw = v_ref[0, pl.ds(k0, NA_WIN_TOKENS), :].astype(bf)
            o = jnp.dot(e_w, vw, preferred_element_type=jnp.float32) + o_c_all[i * GRID_W:(i + 1) * GRID_W]
            o_ref[0, pl.ds(qb0 + i * GRID_W, GRID_W), :] = (o / denom).astype(o_ref.dtype)
        return carry

    lax.fori_loop(0, GRID_ROWS // NA_ROW_UNROLL, rows, 0)

    if with_ctx:
        s = lax.dot_general(qc_ref[0].astype(bf), kc, _NT, preferred_element_type=jnp.float32) * scale
        e = jnp.exp(s - jnp.max(s, axis=-1, keepdims=True))
        o = jnp.dot(e.astype(bf), vc, preferred_element_type=jnp.float32)
        oc_ref[0] = (o / jnp.sum(e, axis=-1, keepdims=True)).astype(oc_ref.dtype)


def na_bias_table(rpb):
    qcol = np.arange(GRID_W)[:, None]
    kcol = np.arange(GRID_W)[None, :]
    qstart = np.clip(qcol - NA_WIN_C // 2, 0, GRID_W - NA_WIN_C)
    mask = (kcol >= qstart) & (kcol < qstart + NA_WIN_C)
    dc = np.clip(kcol - qcol + NA_WIN_C - 1, 0, 2 * NA_WIN_C - 2)
    pick_col = (np.arange(2 * NA_WIN_C - 1)[:, None, None] == dc[None]).astype(np.float32)
    rows = jnp.stack([rpb.astype(jnp.float32)[:, NA_WIN_R - 1 - dl:2 * NA_WIN_R - 1 - dl] for dl in range(NA_WIN_R)],
                     axis=1)
    t = jnp.einsum("hdwc,cqk->hdqwk", rows, pick_col, precision=lax.Precision.HIGHEST)
    t = jnp.where(mask[None, None, :, None, :], t * LOG2E, -1e30)
    return t.reshape(rpb.shape[0], NA_WIN_R, GRID_W, NA_WIN_TOKENS)


def na_attention(p_lat, p_ctx, bias, *, with_ctx, out_dtype, ctx_cols=FULL_COLS):
    bsz, seq, _ = p_lat.shape
    ctx_len = p_ctx.shape[1]
    assert seq == GRID_ROWS * GRID_W
    d, h = NA_HEAD_DIM, NA_HEADS
    in_specs = [pl.BlockSpec((1, seq, d), lambda b, i: (b, 0, i)),
                pl.BlockSpec((1, seq, d), lambda b, i: (b, 0, h + i)),
                pl.BlockSpec((1, seq, d), lambda b, i: (b, 0, 2 * h + i)),
                pl.BlockSpec((1, ctx_len, d), lambda b, i: (b, 0, ctx_cols.k // d + i)),
                pl.BlockSpec((1, ctx_len, d), lambda b, i: (b, 0, ctx_cols.v // d + i)),
                pl.BlockSpec((1, NA_WIN_R, GRID_W, NA_WIN_TOKENS), lambda b, i: (i, 0, 0, 0))]
    args = [p_lat, p_lat, p_lat, p_ctx, p_ctx, bias]
    out_specs = [pl.BlockSpec((1, seq, d), lambda b, i: (b, 0, i))]
    out_shape = [jax.ShapeDtypeStruct((bsz, seq, h * d), out_dtype)]
    if with_ctx:
        in_specs.append(pl.BlockSpec((1, ctx_len, d), lambda b, i: (b, 0, i)))
        args.append(p_ctx)
        out_specs.append(pl.BlockSpec((1, ctx_len, d), lambda b, i: (b, 0, i)))
        out_shape.append(jax.ShapeDtypeStruct((bsz, ctx_len, h * d), out_dtype))
    outs = pl.pallas_call(
        partial(_na_kernel, with_ctx=with_ctx),
        grid=(bsz, h),
        in_specs=in_specs, out_specs=out_specs, out_shape=out_shape,
        compiler_params=pltpu.CompilerParams(dimension_semantics=("arbitrary", "arbitrary"),
                                             vmem_limit_bytes=V7X_VMEM_LIMIT_BYTES),
        name="na_attention",
    )(*args)
    return (outs[0], outs[1]) if with_ctx else (outs[0], None)


def _dft_cos_sin(n):
    jk = np.outer(np.arange(n), np.arange(n)) % n
    ang = 2.0 * np.pi * jk / n
    return np.cos(ang) / np.sqrt(n), np.sin(ang) / np.sqrt(n)


def _fourier_kernel(u_ref, w1_ref, a_ref, fw_ref, zf_ref, o_ref, z_ref, *, seq, row_blk):
    bf = jnp.bfloat16
    gd = FOUR_GROUP_DIM

    @pl.when(pl.program_id(1) == 0)
    def _():
        for r0 in range(0, seq, row_blk):
            for g in range(FOUR_GROUPS):
                ug = u_ref[0, r0:r0 + row_blk, g * gd:(g + 1) * gd].astype(bf)
                z = jnp.dot(ug, w1_ref[...], preferred_element_type=jnp.float32)
                z_ref[r0:r0 + row_blk, g * gd:(g + 1) * gd] = z[:, :gd].astype(bf)
                z_ref[seq + r0:seq + r0 + row_blk, g * gd:(g + 1) * gd] = z[:, gd:].astype(bf)

    y = jnp.dot(a_ref[...], z_ref[...], preferred_element_type=jnp.float32)
    f = jnp.dot(y.astype(bf), fw_ref[...], preferred_element_type=jnp.float32)
    zf = zf_ref[0].astype(jnp.float32)
    o_ref[0] = (f * (zf * jax.nn.sigmoid(zf))).astype(o_ref.dtype)


def fourier_branch(p, four_w_bf, *, tm):
    bsz, seq, _ = p.shape
    assert seq % tm == 0
    cc, sc = _dft_cos_sin(FOUR_GROUP_DIM)
    cl, sl = _dft_cos_sin(seq)
    w1 = jnp.asarray(np.concatenate([cc, sc], axis=1), jnp.bfloat16)
    a = jnp.asarray(np.concatenate([cl, -sl], axis=1), jnp.bfloat16)
    fw = FOUR_WIDTH
    return pl.pallas_call(
        partial(_fourier_kernel, seq=seq, row_blk=min(seq, 512)),
        grid=(bsz, seq // tm),
        in_specs=[pl.BlockSpec((1, seq, fw), lambda b, i: (b, 0, 4)),
                  pl.BlockSpec((FOUR_GROUP_DIM, 2 * FOUR_GROUP_DIM), lambda b, i: (0, 0)),
                  pl.BlockSpec((tm, 2 * seq), lambda b, i: (i, 0)),
                  pl.BlockSpec((fw, fw), lambda b, i: (0, 0)),
                  pl.BlockSpec((1, tm, fw), lambda b, i: (b, i, 5))],
        out_specs=pl.BlockSpec((1, tm, fw), lambda b, i: (b, i, 0)),
        out_shape=jax.ShapeDtypeStruct((bsz, seq, fw), jnp.bfloat16),
        scratch_shapes=[pltpu.VMEM((2 * seq, fw), jnp.bfloat16)],
        compiler_params=pltpu.CompilerParams(dimension_semantics=("arbitrary", "arbitrary"),
                                             vmem_limit_bytes=V7X_VMEM_LIMIT_BYTES),
        name="fourier_branch",
    )(p, w1, a, four_w_bf, p)


MERGE_TN = 1024


def _merge_kernel(a_ref, zna_ref, gfo_ref, s_ref, zs_ref, nw_ref, wa_ref, wf_ref, ws_ref, *rest):
    n_half = D_MODEL // MERGE_TN
    gate_refs, (o_ref, ap_ref, sp_ref) = rest[:3 * n_half], rest[3 * n_half:]
    bf = jnp.bfloat16
    zna = zna_ref[...].astype(jnp.float32)
    ap_ref[...] = (a_ref[...].astype(jnp.float32) * (zna * jax.nn.sigmoid(zna))).astype(bf)
    zs = zs_ref[...].astype(jnp.float32)
    t = s_ref[...].astype(jnp.float32) * (zs * jax.nn.sigmoid(zs))
    t = t * lax.rsqrt(jnp.mean(t * t, axis=-1, keepdims=True) + EPS) * nw_ref[...]
    sp_ref[...] = t.astype(bf)
    for h in range(n_half):
        cols = slice(h * MERGE_TN, (h + 1) * MERGE_TN)
        ga_ref, gf_ref, gs_ref = gate_refs[h], gate_refs[n_half + h], gate_refs[2 * n_half + h]
        o_a = jnp.dot(ap_ref[...], wa_ref[:, cols], preferred_element_type=jnp.float32)
        o_f = jnp.dot(gfo_ref[...], wf_ref[:, cols], preferred_element_type=jnp.float32)
        o_s = jnp.dot(sp_ref[...], ws_ref[:, cols], preferred_element_type=jnp.float32)
        m = (jax.nn.sigmoid(ga_ref[...].astype(jnp.float32)) * o_a
             + jax.nn.sigmoid(gf_ref[...].astype(jnp.float32)) * o_f
             + jax.nn.sigmoid(gs_ref[...].astype(jnp.float32)) * o_s)
        o_ref[:, cols] = m.astype(o_ref.dtype)


def merge_branches(p2d, a2d, gfo2d, s2d, ssd_norm_w, wa, wf, ws, *, tm):
    m = p2d.shape[0]
    w, tn = NA_WIDTH, MERGE_TN
    n_half = D_MODEL // tn
    assert m % tm == 0
    row = lambda i: (i, 0)
    resident = lambda shape: pl.BlockSpec(shape, lambda i: (0, 0), pipeline_mode=pl.Buffered(1))
    gate0 = DT_OFFSET // tn
    gate_specs = [pl.BlockSpec((tm, tn), partial(lambda i, blk: (i, blk), blk=gate0 + g * n_half + h))
                  for g in range(3) for h in range(n_half)]
    return pl.pallas_call(
        _merge_kernel,
        grid=(m // tm,),
        in_specs=[pl.BlockSpec((tm, w), row),
                  pl.BlockSpec((tm, w), lambda i: (i, 3)),
                  pl.BlockSpec((tm, w), row),
                  pl.BlockSpec((tm, w), row),
                  pl.BlockSpec((tm, w), lambda i: (i, 8)),
                  resident((1, w)),
                  resident((w, D_MODEL)), resident((w, D_MODEL)), resident((w, D_MODEL))] + gate_specs,
        out_specs=pl.BlockSpec((tm, D_MODEL), row),
        out_shape=jax.ShapeDtypeStruct((m, D_MODEL), jnp.bfloat16),
        scratch_shapes=[pltpu.VMEM((tm, w), jnp.bfloat16), pltpu.VMEM((tm, w), jnp.bfloat16)],
        compiler_params=pltpu.CompilerParams(dimension_semantics=("arbitrary",),
                                             vmem_limit_bytes=V7X_VMEM_LIMIT_BYTES),
        name="merge_branches",
    )(a2d, p2d, gfo2d, s2d, p2d, ssd_norm_w.reshape(1, w), wa, wf, ws, *([p2d] * (3 * n_half)))


def _outproj_kernel(m_ref, w_ref, x_ref, g_ref, *rest, final_norm):
    y = jnp.dot(m_ref[...], w_ref[...], preferred_element_type=jnp.float32)
    r = x_ref[...] + g_ref[0] * y
    if final_norm:
        nw_ref, o_ref = rest
        o_ref[...] = r * lax.rsqrt(jnp.mean(r * r, axis=-1, keepdims=True) + EPS) * nw_ref[...]
    else:
        rest[0][...] = r


def out_projection(m2d, w_out_bf, x2d, gate, *, rows_per_mod, tm, tn, final_norm_w=None):
    m, d = x2d.shape
    assert m % tm == 0 and d % tn == 0 and rows_per_mod % tm == 0
    per = rows_per_mod // tm
    in_specs = [pl.BlockSpec((tm, d), lambda i, j: (i, 0)),
                pl.BlockSpec((d, tn), lambda i, j: (0, j)),
                pl.BlockSpec((tm, tn), lambda i, j: (i, j)),
                pl.BlockSpec((1, 1, tn), lambda i, j: (i // per, 0, j))]
    args = [m2d, w_out_bf, x2d, gate]
    if final_norm_w is not None:
        assert tn == d
        in_specs.append(pl.BlockSpec((1, d), lambda i, j: (0, 0)))
        args.append(final_norm_w.astype(jnp.float32).reshape(1, d))
    return pl.pallas_call(
        partial(_outproj_kernel, final_norm=final_norm_w is not None),
        grid=(m // tm, d // tn),
        in_specs=in_specs,
        out_specs=pl.BlockSpec((tm, tn), lambda i, j: (i, j)),
        out_shape=jax.ShapeDtypeStruct((m, d), jnp.float32),
        compiler_params=pltpu.CompilerParams(dimension_semantics=("arbitrary", "arbitrary"),
                                             vmem_limit_bytes=V7X_VMEM_LIMIT_BYTES),
        name="out_projection",
    )(*args)


SSD_GW = SSD_HPG * SSD_HEAD_DIM
CONV_PAD = 8
SSD_CHUNK_UNROLL = 2


def _split3_bf16(a):
    rnd = lambda v: v.astype(jnp.bfloat16).astype(jnp.float32)
    a1 = rnd(a)
    a2 = rnd(a - a1)
    a3 = rnd((a - a1) - a2)
    return a1, a2, a3


def _head_rows(rows, r0, n):
    return jnp.concatenate([jnp.broadcast_to(rows[r0 + k:r0 + k + 1, :], (SSD_HEAD_DIM, n))
                            for k in range(SSD_HPG)], axis=0)


def _ssd_kernel(xl_ref, bl_ref, cl_ref, xc_ref, bc_ref, cc_ref, dtl_ref, dtc_ref, bias_ref, alog_ref, dsk_ref,
                wx_ref, wb_ref, wc_ref, bx_ref, bb_ref, bcb_ref, cos_ref, sin_ref,
                yl_ref, yc_ref,
                padx, padb, padc, xt_s, b_s, c_s, dtt_s, row_s, col_s, dec_s, yt_s, p_s, sp_s, st_s, *, seq, ctx_len):
    f32, bf = jnp.float32, jnp.bfloat16
    t = SSD_CHUNK
    nc_c, nc_l = ctx_len // t, seq // t
    nc = nc_c + nc_l

    lane = lax.broadcasted_iota(jnp.int32, (t, 128), 1)
    first_half = (lane % 64) < 32

    def conv_tile(pad, w_ref, b_ref, t0, l0):
        acc = jnp.broadcast_to(b_ref[:, l0:l0 + 128], (t, 128))
        for j in range(SSD_CONV):
            off = CONV_PAD - SSD_CONV // 2 + j
            acc = acc + pad[pl.ds(t0 + off, t), :] * w_ref[j:j + 1, l0:l0 + 128]
        return acc * jax.nn.sigmoid(acc)

    def rope(u, t0):
        swapped = jnp.where(first_half, pltpu.roll(u, 96, axis=1), pltpu.roll(u, 32, axis=1))
        return u * cos_ref[pl.ds(t0, t), :] + swapped * sin_ref[pl.ds(t0, t), :]

    def conv_seq(x_ref, bm_ref, cm_ref, n, base, use_rope):
        slabs = ([(padx.at[h], x_ref, h * 128) for h in range(SSD_GW // 128)]
                 + [(padb, bm_ref, 0), (padc, cm_ref, 0)])
        for pad, src, l0 in slabs:
            pad[0:CONV_PAD, :] = jnp.zeros((CONV_PAD, 128), f32)
            pad[CONV_PAD + n:2 * CONV_PAD + n, :] = jnp.zeros((CONV_PAD, 128), f32)
            pad[CONV_PAD:CONV_PAD + n, :] = src[0, :, l0:l0 + 128].astype(f32)

        def chunk(ci, carry):
            t0 = pl.multiple_of(ci * t, t)
            r0 = pl.multiple_of(base + ci * t, t)
            for h in range(SSD_GW // 128):
                l0 = h * 128
                xt_s[base // t + ci, l0:l0 + 128, :] = conv_tile(padx.at[h], wx_ref, bx_ref, t0, l0).T
            ub = conv_tile(padb, wb_ref, bb_ref, t0, 0)
            uc = conv_tile(padc, wc_ref, bcb_ref, t0, 0)
            if use_rope:
                ub, uc = rope(ub, t0), rope(uc, t0)
            b_s[pl.ds(r0, t), :] = ub.astype(bf)
            c_s[pl.ds(r0, t), :] = uc.astype(bf)
            return carry

        lax.fori_loop(0, n // t, chunk, 0)

    conv_seq(xc_ref, bc_ref, cc_ref, ctx_len, 0, False)
    conv_seq(xl_ref, bl_ref, cl_ref, seq, ctx_len, True)

    i0 = lax.broadcasted_iota(jnp.int32, (t, t), 0)
    i1 = lax.broadcasted_iota(jnp.int32, (t, t), 1)
    tri_f = (i0 <= i1).astype(bf)
    tri_r = (i0 >= i1).astype(bf)
    nr = 2 * SSD_HPG
    fwd_row = (lax.broadcasted_iota(jnp.int32, (nc * nr, 1), 0) % nr) < SSD_HPG
    g8 = pl.multiple_of(pl.program_id(1) * nr, nr)
    raw = []
    for c in range(nc):
        src, c0 = (dtc_ref, c) if c < nc_c else (dtl_ref, c - nc_c)
        dtt_s[c] = src[0, c0 * t:(c0 + 1) * t, :].T
        raw.append(dtt_s[c, pl.ds(g8, nr), :])
    x = jnp.concatenate(raw, axis=0) + jnp.tile(bias_ref[0], (nc, 1))
    dt = jnp.maximum(x, 0.0) + jnp.log1p(jnp.exp(-jnp.abs(x)))
    pieces = _split3_bf16(dt * jnp.tile(-jnp.exp(alog_ref[0]), (nc, 1)))
    pf = [jnp.dot(p.astype(bf), tri_f, preferred_element_type=f32) for p in pieces]
    pr = [jnp.dot(p.astype(bf), tri_r, preferred_element_type=f32) for p in pieces]
    cs = jnp.where(fwd_row, (pf[0] + pf[1]) + pf[2], (pr[0] + pr[1]) + pr[2])
    tot = jnp.where(fwd_row, cs[:, t - 1:t], cs[:, 0:1])
    e_in = jnp.exp(cs)
    e_out = dt * jnp.exp(tot - cs)
    decb = jnp.broadcast_to(jnp.exp(tot), (nc * nr, 128))
    for c in range(nc):
        sl = slice(c * nr, (c + 1) * nr)
        tile = jnp.concatenate([cs[sl], dt[sl], e_in[sl], e_out[sl], jnp.zeros((t - 4 * nr, t), f32)], axis=0)
        row_s[c] = tile[0:4 * nr]
        col_s[c * t:(c + 1) * t, :] = tile.T
        dec_s[c] = decb[sl]

    lower = i1 <= i0
    upper = i1 >= i0
    neg_inf = jnp.float32(-jnp.inf)

    def chunks_a(i, carry):
        ids = [i * SSD_CHUNK_UNROLL + u for u in range(SSD_CHUNK_UNROLL)]
        r0s = [pl.multiple_of(c * t, t) for c in ids]
        cms = [c_s[pl.ds(r0, t), :] for r0 in r0s]
        bms = [b_s[pl.ds(r0, t), :] for r0 in r0s]
        gs = [lax.dot_general(cm, bm, _NT, preferred_element_type=f32) for cm, bm in zip(cms, bms)]
        for c, r0, bm, g in zip(ids, r0s, bms, gs):
            rows = row_s[c]
            cols = col_s[pl.ds(r0, t), :]
            xt = xt_s[c]
            xt_bf = xt.astype(bf)
            parts = []
            for k in range(SSD_HPG):
                kr = SSD_HPG + k
                lf = jnp.exp(jnp.where(lower, cols[:, k:k + 1] - rows[k:k + 1, :], neg_inf)) * rows[8 + k:9 + k, :]
                lr = (jnp.exp(jnp.where(upper, cols[:, kr:kr + 1] - rows[kr:kr + 1, :], neg_inf))
                      * rows[8 + kr:9 + kr, :])
                mk = (g * (lf + lr)).astype(bf)
                xk = xt_bf[k * SSD_HEAD_DIM:(k + 1) * SSD_HEAD_DIM, :]
                parts.append(lax.dot_general(xk, mk, _NT, preferred_element_type=f32))
            yt_s[c] = dsk_ref[0] * xt + jnp.concatenate(parts, axis=0)
            for d in range(2):
                wt = (xt * _head_rows(rows, 24 + SSD_HPG * d, t)).astype(bf)
                p_s[d, c] = jnp.dot(wt, bm, preferred_element_type=f32)
        return carry

    lax.fori_loop(0, nc // SSD_CHUNK_UNROLL, chunks_a, 0)

    st_s[...] = jnp.zeros(st_s.shape, f32)

    def state_step(c, d):
        st = st_s[d]
        sp_s[d, c] = st.astype(bf)
        st_s[d] = st * _head_rows(dec_s[c], SSD_HPG * d, SSD_STATE) + p_s[d, c]

    def ctx_step(i, carry):
        state_step(i, 0)
        state_step(nc_c - 1 - i, 1)
        return carry

    def lat_step(i, carry):
        state_step(nc_c + i, 0)
        state_step(nc - 1 - i, 1)
        return carry

    lax.fori_loop(0, nc_c, ctx_step, 0)
    lax.fori_loop(0, nc_l, lat_step, 0)

    def chunks_c(first, out_ref, o_first):
        ids = [first + u for u in range(SSD_CHUNK_UNROLL)]
        cms = [c_s[pl.ds(pl.multiple_of(c * t, t), t), :] for c in ids]
        offs = [[lax.dot_general(sp_s[d, c], cm, _NT, preferred_element_type=f32) for d in range(2)]
                for c, cm in zip(ids, cms)]
        for u, (c, off) in enumerate(zip(ids, offs)):
            rows = row_s[c]
            yt = yt_s[c]
            for d in range(2):
                yt = yt + off[d] * _head_rows(rows, 16 + SSD_HPG * d, t)
            for l0 in range(0, SSD_GW, 128):
                out_ref[0, pl.ds(pl.multiple_of((o_first + u) * t, t), t), l0:l0 + 128] = yt[l0:l0 + 128, :].T

    def ctx_out(i, carry):
        chunks_c(i * SSD_CHUNK_UNROLL, yc_ref, i * SSD_CHUNK_UNROLL)
        return carry

    def lat_out(i, carry):
        chunks_c(nc_c + i * SSD_CHUNK_UNROLL, yl_ref, i * SSD_CHUNK_UNROLL)
        return carry

    lax.fori_loop(0, nc_c // SSD_CHUNK_UNROLL, ctx_out, 0)
    lax.fori_loop(0, nc_l // SSD_CHUNK_UNROLL, lat_out, 0)


def _rope_tables(seq):
    quarter = SSD_STATE // 4
    inv = ROPE_BASE ** (-np.arange(quarter, dtype=np.float64) / quarter)
    pos = np.arange(seq)
    ang_r = (pos // GRID_W)[:, None] * inv
    ang_c = (pos % GRID_W)[:, None] * inv
    cos = np.concatenate([np.cos(ang_r)] * 2 + [np.cos(ang_c)] * 2, axis=1)
    sin = np.concatenate([-np.sin(ang_r), np.sin(ang_r), -np.sin(ang_c), np.sin(ang_c)], axis=1)
    return jnp.asarray(cos, jnp.float32), jnp.asarray(sin, jnp.float32)


def ssd_mixer(p_lat, p_ctx, dt_lat, dt_ctx, conv_w, conv_b, dt_bias, a_log, d_skip, ctx_cols=FULL_COLS):
    bsz, seq, _ = p_lat.shape
    ctx_len = p_ctx.shape[1]
    t, g, hpg = SSD_CHUNK, SSD_GROUPS, SSD_HPG
    assert seq % t == 0 and ctx_len % t == 0
    nc = (seq + ctx_len) // t
    per_row = lambda v: jnp.broadcast_to(
        v.astype(jnp.float32).reshape(2, g, hpg).transpose(1, 0, 2).reshape(g, 2 * hpg, 1), (g, 2 * hpg, 128))
    dsk = jnp.broadcast_to(jnp.repeat(d_skip.astype(jnp.float32).reshape(g, hpg), SSD_HEAD_DIM, axis=1)[:, :, None],
                           (g, SSD_GW, 128))
    cos, sin = _rope_tables(seq)
    x0 = sum(MAIN_SPLITS[:6]) // SSD_GW
    b0 = (sum(MAIN_SPLITS[:6]) + SSD_WIDTH) // 128
    c0 = b0 + g
    cw = conv_w.astype(jnp.float32)
    cb = conv_b.astype(jnp.float32).reshape(1, SSD_CONV_CH)
    wb0 = SSD_WIDTH // 128
    ltot = seq + ctx_len
    f32, bf = jnp.float32, jnp.bfloat16
    y_lat, y_ctx = pl.pallas_call(
        partial(_ssd_kernel, seq=seq, ctx_len=ctx_len),
        grid=(bsz, g),
        in_specs=[pl.BlockSpec((1, seq, SSD_GW), lambda b, i: (b, 0, x0 + i)),
                  pl.BlockSpec((1, seq, 128), lambda b, i: (b, 0, b0 + i)),
                  pl.BlockSpec((1, seq, 128), lambda b, i: (b, 0, c0 + i)),
                  pl.BlockSpec((1, ctx_len, SSD_GW), lambda b, i: (b, 0, ctx_cols.xbc // SSD_GW + i)),
                  pl.BlockSpec((1, ctx_len, 128), lambda b, i: (b, 0, (ctx_cols.xbc + SSD_WIDTH) // 128 + i)),
                  pl.BlockSpec((1, ctx_len, 128), lambda b, i: (b, 0, (ctx_cols.xbc + SSD_WIDTH) // 128 + g + i)),
                  pl.BlockSpec((1, seq, DT_PAD), lambda b, i: (b, 0, 0)),
                  pl.BlockSpec((1, ctx_len, DT_PAD), lambda b, i: (b, 0, 0)),
                  pl.BlockSpec((1, 2 * hpg, 128), lambda b, i: (i, 0, 0)),
                  pl.BlockSpec((1, 2 * hpg, 128), lambda b, i: (i, 0, 0)),
                  pl.BlockSpec((1, SSD_GW, 128), lambda b, i: (i, 0, 0)),
                  pl.BlockSpec((SSD_CONV, SSD_GW), lambda b, i: (0, i)),
                  pl.BlockSpec((SSD_CONV, 128), lambda b, i: (0, wb0 + i)),
                  pl.BlockSpec((SSD_CONV, 128), lambda b, i: (0, wb0 + g + i)),
                  pl.BlockSpec((1, SSD_GW), lambda b, i: (0, i)),
                  pl.BlockSpec((1, 128), lambda b, i: (0, wb0 + i)),
                  pl.BlockSpec((1, 128), lambda b, i: (0, wb0 + g + i)),
                  pl.BlockSpec((seq, 128), lambda b, i: (0, 0)),
                  pl.BlockSpec((seq, 128), lambda b, i: (0, 0))],
        out_specs=[pl.BlockSpec((1, seq, SSD_GW), lambda b, i: (b, 0, i)),
                   pl.BlockSpec((1, ctx_len, SSD_GW), lambda b, i: (b, 0, i))],
        out_shape=[jax.ShapeDtypeStruct((bsz, seq, SSD_WIDTH), f32),
                   jax.ShapeDtypeStruct((bsz, ctx_len, SSD_WIDTH), f32)],
        scratch_shapes=[pltpu.VMEM((SSD_GW // 128, seq + 2 * CONV_PAD, 128), f32),
                        pltpu.VMEM((seq + 2 * CONV_PAD, 128), f32),
                        pltpu.VMEM((seq + 2 * CONV_PAD, 128), f32),
                        pltpu.VMEM((nc, SSD_GW, t), f32),
                        pltpu.VMEM((ltot, 128), bf),
                        pltpu.VMEM((ltot, 128), bf),
                        pltpu.VMEM((nc, DT_PAD, t), f32),
                        pltpu.VMEM((nc, 4 * 2 * hpg, t), f32),
                        pltpu.VMEM((ltot, 128), f32),
                        pltpu.VMEM((nc, 2 * hpg, 128), f32),
                        pltpu.VMEM((nc, SSD_GW, t), f32),
                        pltpu.VMEM((2, nc, SSD_GW, SSD_STATE), f32),
                        pltpu.VMEM((2, nc, SSD_GW, SSD_STATE), bf),
                        pltpu.VMEM((2, SSD_GW, SSD_STATE), f32)],
        compiler_params=pltpu.CompilerParams(dimension_semantics=("arbitrary", "arbitrary"),
                                             vmem_limit_bytes=V7X_VMEM_LIMIT_BYTES),
        name="ssd_mixer",
    )(p_lat, p_lat, p_lat, p_ctx, p_ctx, p_ctx, dt_lat, dt_ctx, per_row(dt_bias), per_row(a_log), dsk,
      cw, cw, cw, cb, cb, cb, cos, sin)
    return y_ctx, y_lat


INPROJ_TN = 2560
ADA_ROWS = 16


def _layer(xc, xl, c, c_ctx, w_ada_all, layer, b_ada, norm_w, w_in, w_main_all, rpb, four_w, conv_w, conv_b, dt_bias, a_log,
           d_skip, ssd_norm_w, wb_na, wb_four, wb_ssd, w_out, update_ctx, final_norm_w):
    bsz, seq, d = xl.shape
    ctx_len = xc.shape[1]
    bf = jnp.bfloat16
    assert bsz < ADA_ROWS

    cc = jnp.concatenate([c, c_ctx[None], jnp.zeros((ADA_ROWS - bsz - 1, d), c.dtype)], axis=0)
    mod = matmul(jax.nn.silu(cc), w_ada_all, layer, tm=ADA_ROWS, tn=1024) + b_ada
    sh, sc, gt = jnp.split(mod, 3, axis=-1)
    sh_l, sc_l, g_l = sh[:bsz], sc[:bsz], gt[:bsz]
    sh_c, sc_c, g_c = sh[bsz:bsz + 1], sc[bsz:bsz + 1], gt[bsz]

    w_dt = w_in[:, DT_OFFSET:DT_OFFSET + DT_WIDTH].reshape(d, 2, SSD_GROUPS, SSD_HPG).transpose(0, 2, 1, 3)
    w_dt = jnp.pad(w_dt.reshape(d, DT_WIDTH), ((0, 0), (0, DT_PAD - DT_WIDTH))).astype(bf)

    xl2d = xl.reshape(bsz * seq, d)
    xc2d = xc.reshape(bsz * ctx_len, d)
    pl2d, dt_l = in_projection(xl2d, norm_w, sc_l[:, None], sh_l[:, None], w_main_all, layer, w_dt,
                               rows_per_mod=seq, tm=1024, tn=INPROJ_TN, out_dtype=bf)
    if update_ctx:
        ctx_cols, w_ctx, ctx_layer, ctx_tn = FULL_COLS, w_main_all, layer, INPROJ_TN
    else:
        ctx_cols, ctx_layer = KV_XBC_COLS, 0
        w_ctx = jnp.concatenate([w_main_all[layer, :, FULL_COLS.k:FULL_COLS.v + NA_WIDTH],
                                 w_main_all[layer, :, FULL_COLS.xbc:FULL_COLS.xbc + SSD_CONV_CH]], axis=1)[None]
        ctx_tn = w_ctx.shape[2] // 2
    pc2d, dt_c = in_projection(xc2d, norm_w, sc_c[:, None], sh_c[:, None], w_ctx, ctx_layer, w_dt,
                               rows_per_mod=bsz * ctx_len, tm=1024, tn=ctx_tn, out_dtype=bf)
    p_l = pl2d.reshape(bsz, seq, MAIN_WIDTH)
    p_c = pc2d.reshape(bsz, ctx_len, w_ctx.shape[2])
    dt_l = dt_l.reshape(bsz, seq, DT_PAD)
    dt_c = dt_c.reshape(bsz, ctx_len, DT_PAD)

    four_wb, wb_nab, wb_fourb, wb_ssdb, w_outb = (t.astype(bf) for t in (four_w, wb_na, wb_four, wb_ssd, w_out))

    a_l, a_c = na_attention(p_l, p_c, na_bias_table(rpb), with_ctx=update_ctx, out_dtype=bf, ctx_cols=ctx_cols)
    gfo_l = fourier_branch(p_l, four_wb, tm=512)
    s_c, s_l = ssd_mixer(p_l, p_c, dt_l, dt_c, conv_w, conv_b, dt_bias, a_log, d_skip, ctx_cols=ctx_cols)
    m_l = merge_branches(pl2d, a_l.reshape(bsz * seq, NA_WIDTH), gfo_l.reshape(bsz * seq, FOUR_WIDTH),
                         s_l.reshape(bsz * seq, SSD_WIDTH), ssd_norm_w, wb_nab, wb_fourb, wb_ssdb, tm=512)
    if final_norm_w is None:
        xl_new = out_projection(m_l, w_outb, xl2d, g_l[:, None], rows_per_mod=seq, tm=1024, tn=1024)
    else:
        xl_new = out_projection(m_l, w_outb, xl2d, g_l[:, None], rows_per_mod=seq, tm=512, tn=d,
                                final_norm_w=final_norm_w)
    xl_new = xl_new.reshape(bsz, seq, d)
    if update_ctx:
        gfo_c = fourier_branch(p_c, four_wb, tm=ctx_len)
        m_c = merge_branches(pc2d, a_c.reshape(bsz * ctx_len, NA_WIDTH), gfo_c.reshape(bsz * ctx_len, FOUR_WIDTH),
                             s_c.reshape(bsz * ctx_len, SSD_WIDTH), ssd_norm_w, wb_nab, wb_fourb, wb_ssdb, tm=512)
        xc = out_projection(m_c, w_outb, xc2d, g_c[None, None], rows_per_mod=bsz * ctx_len, tm=1024, tn=1024)
        xc = xc.reshape(bsz, ctx_len, d)
    return xc, xl_new


def kernel(x, c, ctx, c_ctx, w_ada, b_ada, norm_w, w_in, na_rpb, four_w, ssd_conv_w, ssd_conv_b, ssd_dt_bias,
           ssd_a_log, ssd_d, ssd_norm_w, wb_na, wb_four, wb_ssd, w_out, final_norm_w):
    xc, xl = ctx, x
    w_main_all = repack_w_in(w_in)
    for l in range(DEPTH):
        xc, xl = _layer(xc, xl, c, c_ctx, w_ada, l, b_ada[l], norm_w[l], w_in[l], w_main_all, na_rpb[l], four_w[l],
                        ssd_conv_w[l], ssd_conv_b[l], ssd_dt_bias[l], ssd_a_log[l], ssd_d[l], ssd_norm_w[l],
                        wb_na[l], wb_four[l], wb_ssd[l], w_out[l], update_ctx=(l < DEPTH - 1),
                        final_norm_w=final_norm_w if l == DEPTH - 1 else None)
    return xl
```

```python
import math
from functools import partial
from typing import NamedTuple

import jax
import jax.numpy as jnp
import numpy as np
from jax import lax
from jax.experimental import pallas as pl
from jax.experimental.pallas import tpu as pltpu

D_MODEL = 2048
DEPTH = 2
GRID_W = 64
EPS = 1e-6
NA_HEADS = 8
NA_HEAD_DIM = 128
NA_WIDTH = NA_HEADS * NA_HEAD_DIM
NA_WIN_R = 8
NA_WIN_C = 16
LOG2E = math.log2(math.e)
FOUR_GROUPS = 4
FOUR_GROUP_DIM = 256
FOUR_WIDTH = FOUR_GROUPS * FOUR_GROUP_DIM
SSD_HEADS = 16
SSD_HEAD_DIM = 64
SSD_WIDTH = SSD_HEADS * SSD_HEAD_DIM
SSD_GROUPS = 4
SSD_HPG = SSD_HEADS // SSD_GROUPS
SSD_STATE = 128
SSD_CONV = 7
SSD_CHUNK = 128
SSD_CONV_CH = SSD_WIDTH + 2 * SSD_GROUPS * SSD_STATE
ROPE_BASE = 10000.0
DT_WIDTH = 2 * SSD_HEADS
MAIN_SPLITS = (NA_WIDTH, NA_WIDTH, NA_WIDTH, NA_WIDTH, FOUR_WIDTH, FOUR_WIDTH, SSD_CONV_CH, SSD_WIDTH,
               D_MODEL, D_MODEL, D_MODEL)
MAIN_WIDTH = sum(MAIN_SPLITS)
DT_OFFSET = sum(MAIN_SPLITS[:8])
DT_PAD = 128


class ProjCols(NamedTuple):
    k: int
    v: int
    xbc: int


FULL_COLS = ProjCols(k=NA_WIDTH, v=2 * NA_WIDTH, xbc=sum(MAIN_SPLITS[:6]))
KV_XBC_COLS = ProjCols(k=0, v=NA_WIDTH, xbc=2 * NA_WIDTH)

V7X_VMEM_LIMIT_BYTES = 56 * 1024 * 1024


def _mm_kernel(a_ref, b_ref, o_ref):
    a = a_ref[...].astype(jnp.bfloat16)
    b = b_ref[...].astype(jnp.bfloat16)
    o_ref[...] = jnp.dot(a, b, preferred_element_type=jnp.float32).astype(o_ref.dtype)


def matmul(a, b, layer, *, tm, tn, out_dtype=jnp.float32):
    m, k = a.shape
    _, _, n = b.shape
    assert m % tm == 0 and n % tn == 0, (a.shape, b.shape, tm, tn)
    return pl.pallas_call(
        _mm_kernel,
        grid=(m // tm, n // tn),
        in_specs=[pl.BlockSpec((tm, k), lambda i, j: (i, 0)),
                  pl.BlockSpec((None, k, tn), lambda i, j: (layer, 0, j))],
        out_specs=pl.BlockSpec((tm, tn), lambda i, j: (i, j)),
        out_shape=jax.ShapeDtypeStruct((m, n), out_dtype),
        compiler_params=pltpu.CompilerParams(dimension_semantics=("arbitrary", "arbitrary"),
                                             vmem_limit_bytes=V7X_VMEM_LIMIT_BYTES),
        name="matmul",
    )(a, b)


REPACK_TN = 1024


def _repack_kernel(a_ref, nxt_ref, o_ref, odt_ref):
    j = pl.program_id(1)
    first_below = DT_OFFSET // REPACK_TN

    @pl.when(j < first_below)
    def _():
        o_ref[...] = a_ref[...].astype(o_ref.dtype)

    @pl.when(j >= first_below)
    def _():
        o_ref[...] = jnp.concatenate([a_ref[DT_WIDTH:, :], nxt_ref[...]], axis=0).astype(o_ref.dtype)

    @pl.when(j == first_below)
    def _():
        odt_ref[...] = a_ref[:DT_PAD, :].astype(odt_ref.dtype)


def repack_w_in(w_in):
    depth, d, _ = w_in.shape
    tn = REPACK_TN
    w_t = jnp.swapaxes(w_in, 1, 2)
    return pl.pallas_call(
        _repack_kernel,
        grid=(depth, MAIN_WIDTH // tn),
        in_specs=[pl.BlockSpec((None, tn, d), lambda l, j: (l, j, 0)),
                  pl.BlockSpec((None, DT_WIDTH, d), lambda l, j: (l, (j + 1) * (tn // DT_WIDTH), 0))],
        out_specs=[pl.BlockSpec((None, tn, d), lambda l, j: (l, j, 0)),
                   pl.BlockSpec((None, DT_PAD, d), lambda l, j: (l, 0, 0))],
        out_shape=[jax.ShapeDtypeStruct((depth, MAIN_WIDTH, d), jnp.bfloat16),
                   jax.ShapeDtypeStruct((depth, DT_PAD, d), jnp.bfloat16)],
        compiler_params=pltpu.CompilerParams(dimension_semantics=("arbitrary", "arbitrary"),
                                             vmem_limit_bytes=V7X_VMEM_LIMIT_BYTES),
        name="repack_w_in",
    )(w_t, w_t)


def _inproj_kernel(x_ref, nw_ref, sc_ref, sh_ref, w_ref, wdt_ref, o_ref, dt_ref, h_ref):
    @pl.when(pl.program_id(1) == 0)
    def _():
        x = x_ref[...]
        y = x * lax.rsqrt(jnp.mean(x * x, axis=-1, keepdims=True) + EPS) * nw_ref[...]
        h = (y * (1.0 + sc_ref[0]) + sh_ref[0]).astype(jnp.bfloat16)
        h_ref[...] = h
        dt_ref[...] = lax.dot_general(h, wdt_ref[...], _NT, preferred_element_type=jnp.float32)

    o_ref[...] = lax.dot_general(h_ref[...], w_ref[...], _NT, preferred_element_type=jnp.float32).astype(o_ref.dtype)


def in_projection(x2d, norm_w, scale, shift, w_main_t, layer, w_dt_t, dt_layer, *, rows_per_mod, tm, tn, out_dtype):
    m, d = x2d.shape
    n = w_main_t.shape[1]
    assert m % tm == 0 and n % tn == 0 and rows_per_mod % tm == 0
    per = rows_per_mod // tm
    return pl.pallas_call(
        _inproj_kernel,
        grid=(m // tm, n // tn),
        in_specs=[pl.BlockSpec((tm, d), lambda i, j: (i, 0)),
                  pl.BlockSpec((1, d), lambda i, j: (0, 0)),
                  pl.BlockSpec((1, 1, d), lambda i, j: (i // per, 0, 0)),
                  pl.BlockSpec((1, 1, d), lambda i, j: (i // per, 0, 0)),
                  pl.BlockSpec((None, tn, d), lambda i, j: (layer, j, 0)),
                  pl.BlockSpec((None, DT_PAD, d), lambda i, j: (dt_layer, 0, 0))],
        out_specs=[pl.BlockSpec((tm, tn), lambda i, j: (i, j)),
                   pl.BlockSpec((tm, DT_PAD), lambda i, j: (i, 0))],
        out_shape=[jax.ShapeDtypeStruct((m, n), out_dtype),
                   jax.ShapeDtypeStruct((m, DT_PAD), jnp.float32)],
        scratch_shapes=[pltpu.VMEM((tm, d), jnp.bfloat16)],
        compiler_params=pltpu.CompilerParams(dimension_semantics=("arbitrary", "arbitrary"),
                                             vmem_limit_bytes=V7X_VMEM_LIMIT_BYTES),
        name="in_projection",
    )(x2d, norm_w.reshape(1, d), scale, shift, w_main_t, w_dt_t)


GRID_ROWS = 32
NA_WIN_TOKENS = NA_WIN_R * GRID_W
_NT = (((1,), (1,)), ((), ()))
NA_ROW_UNROLL = 8


def _na_kernel(*refs, with_ctx):
    if with_ctx:
        q_ref, k_ref, v_ref, kc_ref, vc_ref, bias_ref, qc_ref, o_ref, oc_ref = refs
    else:
        q_ref, k_ref, v_ref, kc_ref, vc_ref, bias_ref, o_ref = refs
    bf = jnp.bfloat16
    scale = NA_HEAD_DIM ** -0.5
    scale2 = scale * LOG2E
    kc = kc_ref[0].astype(bf)
    vc = vc_ref[0].astype(bf)
    nq = NA_ROW_UNROLL * GRID_W

    def lane_tiles(x, op):
        acc = x[:, 0:128]
        for l0 in range(128, x.shape[1], 128):
            acc = op(acc, x[:, l0:l0 + 128])
        return acc

    def rows(rb, carry):
        qb0 = pl.multiple_of(rb * nq, nq)
        q_all = q_ref[0, pl.ds(qb0, nq), :].astype(bf)
        s_c_all = lax.dot_general(q_all, kc, _NT, preferred_element_type=jnp.float32) * scale2
        k0s, scores, probs = [], [], []
        for i in range(NA_ROW_UNROLL):
            r = rb * NA_ROW_UNROLL + i
            start = jnp.clip(r - NA_WIN_R // 2, 0, GRID_ROWS - NA_WIN_R)
            k0 = pl.multiple_of(start * GRID_W, GRID_W)
            q = q_all[i * GRID_W:(i + 1) * GRID_W]
            kw = k_ref[0, pl.ds(k0, NA_WIN_TOKENS), :].astype(bf)
            s_w = lax.dot_general(q, kw, _NT, preferred_element_type=jnp.float32) * scale2 + bias_ref[0, r - start]
            k0s.append(k0)
            scores.append((s_w, s_c_all[i * GRID_W:(i + 1) * GRID_W]))
        for s_w, s_c in scores:
            m = jnp.max(jnp.maximum(lane_tiles(s_w, jnp.maximum), lane_tiles(s_c, jnp.maximum)),
                        axis=-1, keepdims=True)
            e_w = jnp.exp2(s_w - m)
            e_c = jnp.exp2(s_c - m)
            denom = jnp.sum(lane_tiles(e_w, jnp.add) + lane_tiles(e_c, jnp.add), axis=-1, keepdims=True)
            probs.append((e_w.astype(bf), e_c.astype(bf), denom))
        o_c_all = jnp.dot(jnp.concatenate([p[1] for p in probs], axis=0), vc, preferred_element_type=jnp.float32)
        for i, (k0, (e_w, _, denom)) in enumerate(zip(k0s, probs)):
            vw = v_ref[0, pl.ds(k0, NA_WIN_TOKENS), :].astype(bf)
            o = jnp.dot(e_w, vw, preferred_element_type=jnp.float32) + o_c_all[i * GRID_W:(i + 1) * GRID_W]
            o_ref[0, pl.ds(qb0 + i * GRID_W, GRID_W), :] = (o / denom).astype(o_ref.dtype)
        return carry

    lax.fori_loop(0, GRID_ROWS // NA_ROW_UNROLL, rows, 0)

    if with_ctx:
        s = lax.dot_general(qc_ref[0].astype(bf), kc, _NT, preferred_element_type=jnp.float32) * scale
        e = jnp.exp(s - jnp.max(s, axis=-1, keepdims=True))
        o = jnp.dot(e.astype(bf), vc, preferred_element_type=jnp.float32)
        oc_ref[0] = (o / jnp.sum(e, axis=-1, keepdims=True)).astype(oc_ref.dtype)


def na_bias_table(rpb):
    qcol = np.arange(GRID_W)[:, None]
    kcol = np.arange(GRID_W)[None, :]
    qstart = np.clip(qcol - NA_WIN_C // 2, 0, GRID_W - NA_WIN_C)
    mask = (kcol >= qstart) & (kcol < qstart + NA_WIN_C)
    dc = np.clip(kcol - qcol + NA_WIN_C - 1, 0, 2 * NA_WIN_C - 2)
    pick_col = (np.arange(2 * NA_WIN_C - 1)[:, None, None] == dc[None]).astype(np.float32)
    rows = jnp.stack([rpb.astype(jnp.float32)[:, NA_WIN_R - 1 - dl:2 * NA_WIN_R - 1 - dl] for dl in range(NA_WIN_R)],
                     axis=1)
    t = jnp.einsum("hdwc,cqk->hdqwk", rows, pick_col, precision=lax.Precision.HIGHEST)
    t = jnp.where(mask[None, None, :, None, :], t * LOG2E, -1e30)
    return t.reshape(rpb.shape[0], NA_WIN_R, GRID_W, NA_WIN_TOKENS)


def na_attention(p_lat, p_ctx, bias, *, with_ctx, out_dtype, ctx_cols=FULL_COLS):
    bsz, seq, _ = p_lat.shape
    ctx_len = p_ctx.shape[1]
    assert seq == GRID_ROWS * GRID_W
    d, h = NA_HEAD_DIM, NA_HEADS
    in_specs = [pl.BlockSpec((1, seq, d), lambda b, i: (b, 0, i)),
                pl.BlockSpec((1, seq, d), lambda b, i: (b, 0, h + i)),
                pl.BlockSpec((1, seq, d), lambda b, i: (b, 0, 2 * h + i)),
                pl.BlockSpec((1, ctx_len, d), lambda b, i: (b, 0, ctx_cols.k // d + i)),
                pl.BlockSpec((1, ctx_len, d), lambda b, i: (b, 0, ctx_cols.v // d + i)),
                pl.BlockSpec((1, NA_WIN_R, GRID_W, NA_WIN_TOKENS), lambda b, i: (i, 0, 0, 0))]
    args = [p_lat, p_lat, p_lat, p_ctx, p_ctx, bias]
    out_specs = [pl.BlockSpec((1, seq, d), lambda b, i: (b, 0, i))]
    out_shape = [jax.ShapeDtypeStruct((bsz, seq, h * d), out_dtype)]
    if with_ctx:
        in_specs.append(pl.BlockSpec((1, ctx_len, d), lambda b, i: (b, 0, i)))
        args.append(p_ctx)
        out_specs.append(pl.BlockSpec((1, ctx_len, d), lambda b, i: (b, 0, i)))
        out_shape.append(jax.ShapeDtypeStruct((bsz, ctx_len, h * d), out_dtype))
    outs = pl.pallas_call(
        partial(_na_kernel, with_ctx=with_ctx),
        grid=(bsz, h),
        in_specs=in_specs, out_specs=out_specs, out_shape=out_shape,
        compiler_params=pltpu.CompilerParams(dimension_semantics=("arbitrary", "arbitrary"),
                                             vmem_limit_bytes=V7X_VMEM_LIMIT_BYTES),
        name="na_attention",
    )(*args)
    return (outs[0], outs[1]) if with_ctx else (outs[0], None)


def _dft_cos_sin(n):
    jk = np.outer(np.arange(n), np.arange(n)) % n
    ang = 2.0 * np.pi * jk / n
    return np.cos(ang) / np.sqrt(n), np.sin(ang) / np.sqrt(n)


def _fourier_kernel(u_ref, w1_ref, a_ref, fw_ref, zf_ref, o_ref, z_ref, *, seq, row_blk):
    bf = jnp.bfloat16
    gd = FOUR_GROUP_DIM

    @pl.when(pl.program_id(1) == 0)
    def _():
        for r0 in range(0, seq, row_blk):
            for g in range(FOUR_GROUPS):
                ug = u_ref[0, r0:r0 + row_blk, g * gd:(g + 1) * gd].astype(bf)
                z = jnp.dot(ug, w1_ref[...], preferred_element_type=jnp.float32)
                z_ref[r0:r0 + row_blk, g * gd:(g + 1) * gd] = z[:, :gd].astype(bf)
                z_ref[seq + r0:seq + r0 + row_blk, g * gd:(g + 1) * gd] = z[:, gd:].astype(bf)

    y = jnp.dot(a_ref[...], z_ref[...], preferred_element_type=jnp.float32)
    f = jnp.dot(y.astype(bf), fw_ref[...], preferred_element_type=jnp.float32)
    zf = zf_ref[0].astype(jnp.float32)
    o_ref[0] = (f * (zf * jax.nn.sigmoid(zf))).astype(o_ref.dtype)


def fourier_branch(p, four_w_bf, *, tm):
    bsz, seq, _ = p.shape
    assert seq % tm == 0
    cc, sc = _dft_cos_sin(FOUR_GROUP_DIM)
    cl, sl = _dft_cos_sin(seq)
    w1 = jnp.asarray(np.concatenate([cc, sc], axis=1), jnp.bfloat16)
    a = jnp.asarray(np.concatenate([cl, -sl], axis=1), jnp.bfloat16)
    fw = FOUR_WIDTH
    return pl.pallas_call(
        partial(_fourier_kernel, seq=seq, row_blk=min(seq, 512)),
        grid=(bsz, seq // tm),
        in_specs=[pl.BlockSpec((1, seq, fw), lambda b, i: (b, 0, 4)),
                  pl.BlockSpec((FOUR_GROUP_DIM, 2 * FOUR_GROUP_DIM), lambda b, i: (0, 0)),
                  pl.BlockSpec((tm, 2 * seq), lambda b, i: (i, 0)),
                  pl.BlockSpec((fw, fw), lambda b, i: (0, 0)),
                  pl.BlockSpec((1, tm, fw), lambda b, i: (b, i, 5))],
        out_specs=pl.BlockSpec((1, tm, fw), lambda b, i: (b, i, 0)),
        out_shape=jax.ShapeDtypeStruct((bsz, seq, fw), jnp.bfloat16),
        scratch_shapes=[pltpu.VMEM((2 * seq, fw), jnp.bfloat16)],
        compiler_params=pltpu.CompilerParams(dimension_semantics=("arbitrary", "arbitrary"),
                                             vmem_limit_bytes=V7X_VMEM_LIMIT_BYTES),
        name="fourier_branch",
    )(p, w1, a, four_w_bf, p)


MERGE_TN = 1024


def _merge_kernel(a_ref, zna_ref, gfo_ref, s_ref, zs_ref, nw_ref, wa_ref, wf_ref, ws_ref, *rest):
    n_half = D_MODEL // MERGE_TN
    gate_refs, (o_ref, ap_ref, sp_ref) = rest[:3 * n_half], rest[3 * n_half:]
    bf = jnp.bfloat16
    zna = zna_ref[...].astype(jnp.float32)
    ap_ref[...] = (a_ref[...].astype(jnp.float32) * (zna * jax.nn.sigmoid(zna))).astype(bf)
    zs = zs_ref[...].astype(jnp.float32)
    t = s_ref[...].astype(jnp.float32) * (zs * jax.nn.sigmoid(zs))
    t = t * lax.rsqrt(jnp.mean(t * t, axis=-1, keepdims=True) + EPS) * nw_ref[...]
    sp_ref[...] = t.astype(bf)
    for h in range(n_half):
        cols = slice(h * MERGE_TN, (h + 1) * MERGE_TN)
        ga_ref, gf_ref, gs_ref = gate_refs[h], gate_refs[n_half + h], gate_refs[2 * n_half + h]
        o_a = jnp.dot(ap_ref[...], wa_ref[:, cols], preferred_element_type=jnp.float32)
        o_f = jnp.dot(gfo_ref[...], wf_ref[:, cols], preferred_element_type=jnp.float32)
        o_s = jnp.dot(sp_ref[...], ws_ref[:, cols], preferred_element_type=jnp.float32)
        m = (jax.nn.sigmoid(ga_ref[...].astype(jnp.float32)) * o_a
             + jax.nn.sigmoid(gf_ref[...].astype(jnp.float32)) * o_f
             + jax.nn.sigmoid(gs_ref[...].astype(jnp.float32)) * o_s)
        o_ref[:, cols] = m.astype(o_ref.dtype)


def merge_branches(p2d, a2d, gfo2d, s2d, ssd_norm_w, wa, wf, ws, *, tm):
    m = p2d.shape[0]
    w, tn = NA_WIDTH, MERGE_TN
    n_half = D_MODEL // tn
    assert m % tm == 0
    row = lambda i: (i, 0)
    resident = lambda shape: pl.BlockSpec(shape, lambda i: (0, 0), pipeline_mode=pl.Buffered(1))
    gate0 = DT_OFFSET // tn
    gate_specs = [pl.BlockSpec((tm, tn), partial(lambda i, blk: (i, blk), blk=gate0 + g * n_half + h))
                  for g in range(3) for h in range(n_half)]
    return pl.pallas_call(
        _merge_kernel,
        grid=(m // tm,),
        in_specs=[pl.BlockSpec((tm, w), row),
                  pl.BlockSpec((tm, w), lambda i: (i, 3)),
                  pl.BlockSpec((tm, w), row),
                  pl.BlockSpec((tm, w), row),
                  pl.BlockSpec((tm, w), lambda i: (i, 8)),
                  resident((1, w)),
                  resident((w, D_MODEL)), resident((w, D_MODEL)), resident((w, D_MODEL))] + gate_specs,
        out_specs=pl.BlockSpec((tm, D_MODEL), row),
        out_shape=jax.ShapeDtypeStruct((m, D_MODEL), jnp.bfloat16),
        scratch_shapes=[pltpu.VMEM((tm, w), jnp.bfloat16), pltpu.VMEM((tm, w), jnp.bfloat16)],
        compiler_params=pltpu.CompilerParams(dimension_semantics=("arbitrary",),
                                             vmem_limit_bytes=V7X_VMEM_LIMIT_BYTES),
        name="merge_branches",
    )(a2d, p2d, gfo2d, s2d, p2d, ssd_norm_w.reshape(1, w), wa, wf, ws, *([p2d] * (3 * n_half)))


def _outproj_kernel(m_ref, w_ref, x_ref, g_ref, *rest, final_norm):
    y = jnp.dot(m_ref[...], w_ref[...], preferred_element_type=jnp.float32)
    r = x_ref[...] + g_ref[0] * y
    if final_norm:
        nw_ref, o_ref = rest
        o_ref[...] = r * lax.rsqrt(jnp.mean(r * r, axis=-1, keepdims=True) + EPS) * nw_ref[...]
    else:
        rest[0][...] = r


def out_projection(m2d, w_out_bf, x2d, gate, *, rows_per_mod, tm, tn, final_norm_w=None):
    m, d = x2d.shape
    assert m % tm == 0 and d % tn == 0 and rows_per_mod % tm == 0
    per = rows_per_mod // tm
    in_specs = [pl.BlockSpec((tm, d), lambda i, j: (i, 0)),
                pl.BlockSpec((d, tn), lambda i, j: (0, j)),
                pl.BlockSpec((tm, tn), lambda i, j: (i, j)),
                pl.BlockSpec((1, 1, tn), lambda i, j: (i // per, 0, j))]
    args = [m2d, w_out_bf, x2d, gate]
    if final_norm_w is not None:
        assert tn == d
        in_specs.append(pl.BlockSpec((1, d), lambda i, j: (0, 0)))
        args.append(final_norm_w.astype(jnp.float32).reshape(1, d))
    return pl.pallas_call(
        partial(_outproj_kernel, final_norm=final_norm_w is not None),
        grid=(m // tm, d // tn),
        in_specs=in_specs,
        out_specs=pl.BlockSpec((tm, tn), lambda i, j: (i, j)),
        out_shape=jax.ShapeDtypeStruct((m, d), jnp.float32),
        compiler_params=pltpu.CompilerParams(dimension_semantics=("arbitrary", "arbitrary"),
                                             vmem_limit_bytes=V7X_VMEM_LIMIT_BYTES),
        name="out_projection",
    )(*args)


SSD_GW = SSD_HPG * SSD_HEAD_DIM
CONV_PAD = 8
SSD_CHUNK_UNROLL = 2


def _split3_bf16(a):
    rnd = lambda v: v.astype(jnp.bfloat16).astype(jnp.float32)
    a1 = rnd(a)
    a2 = rnd(a - a1)
    a3 = rnd((a - a1) - a2)
    return a1, a2, a3


def _head_rows(rows, r0, n):
    return jnp.concatenate([jnp.broadcast_to(rows[r0 + k:r0 + k + 1, :], (SSD_HEAD_DIM, n))
                            for k in range(SSD_HPG)], axis=0)


def _ssd_kernel(xl_ref, bl_ref, cl_ref, xc_ref, bc_ref, cc_ref, dtl_ref, dtc_ref, bias_ref, alog_ref, dsk_ref,
                wx_ref, wb_ref, wc_ref, bx_ref, bb_ref, bcb_ref, cos_ref, sin_ref,
                yl_ref, yc_ref,
                padx, padb, padc, xt_s, b_s, c_s, dtt_s, row_s, col_s, dec_s, yt_s, p_s, sp_s, st_s, *, seq, ctx_len):
    f32, bf = jnp.float32, jnp.bfloat16
    t = SSD_CHUNK
    nc_c, nc_l = ctx_len // t, seq // t
    nc = nc_c + nc_l

    lane = lax.broadcasted_iota(jnp.int32, (t, 128), 1)
    first_half = (lane % 64) < 32

    def conv_tile(pad, w_ref, b_ref, t0, l0):
        acc = jnp.broadcast_to(b_ref[:, l0:l0 + 128], (t, 128))
        for j in range(SSD_CONV):
            off = CONV_PAD - SSD_CONV // 2 + j
            acc = acc + pad[pl.ds(t0 + off, t), :] * w_ref[j:j + 1, l0:l0 + 128]
        return acc * jax.nn.sigmoid(acc)

    def rope(u, t0):
        swapped = jnp.where(first_half, pltpu.roll(u, 96, axis=1), pltpu.roll(u, 32, axis=1))
        return u * cos_ref[pl.ds(t0, t), :] + swapped * sin_ref[pl.ds(t0, t), :]

    def conv_seq(x_ref, bm_ref, cm_ref, n, base, use_rope):
        slabs = ([(padx.at[h], x_ref, h * 128) for h in range(SSD_GW // 128)]
                 + [(padb, bm_ref, 0), (padc, cm_ref, 0)])
        for pad, src, l0 in slabs:
            pad[0:CONV_PAD, :] = jnp.zeros((CONV_PAD, 128), f32)
            pad[CONV_PAD + n:2 * CONV_PAD + n, :] = jnp.zeros((CONV_PAD, 128), f32)
            pad[CONV_PAD:CONV_PAD + n, :] = src[0, :, l0:l0 + 128].astype(f32)

        def chunk(ci, carry):
            t0 = pl.multiple_of(ci * t, t)
            r0 = pl.multiple_of(base + ci * t, t)
            for h in range(SSD_GW // 128):
                l0 = h * 128
                xt_s[base // t + ci, l0:l0 + 128, :] = conv_tile(padx.at[h], wx_ref, bx_ref, t0, l0).T
            ub = conv_tile(padb, wb_ref, bb_ref, t0, 0)
            uc = conv_tile(padc, wc_ref, bcb_ref, t0, 0)
            if use_rope:
                ub, uc = rope(ub, t0), rope(uc, t0)
            b_s[pl.ds(r0, t), :] = ub.astype(bf)
            c_s[pl.ds(r0, t), :] = uc.astype(bf)
            return carry

        lax.fori_loop(0, n // t, chunk, 0)

    conv_seq(xc_ref, bc_ref, cc_ref, ctx_len, 0, False)
    conv_seq(xl_ref, bl_ref, cl_ref, seq, ctx_len, True)

    i0 = lax.broadcasted_iota(jnp.int32, (t, t), 0)
    i1 = lax.broadcasted_iota(jnp.int32, (t, t), 1)
    tri_f = (i0 <= i1).astype(bf)
    tri_r = (i0 >= i1).astype(bf)
    nr = 2 * SSD_HPG
    fwd_row = (lax.broadcasted_iota(jnp.int32, (nc * nr, 1), 0) % nr) < SSD_HPG
    g4 = pl.program_id(1) * SSD_HPG
    raw = []
    for c in range(nc):
        src, c0 = (dtc_ref, c) if c < nc_c else (dtl_ref, c - nc_c)
        dtt_s[c] = src[0, c0 * t:(c0 + 1) * t, :].T
        raw += [dtt_s[c, pl.ds(g4, SSD_HPG), :], dtt_s[c, pl.ds(SSD_HEADS + g4, SSD_HPG), :]]
    x = jnp.concatenate(raw, axis=0) + jnp.tile(bias_ref[0], (nc, 1))
    dt = jnp.maximum(x, 0.0) + jnp.log1p(jnp.exp(-jnp.abs(x)))
    pieces = _split3_bf16(dt * jnp.tile(-jnp.exp(alog_ref[0]), (nc, 1)))
    pf = [jnp.dot(p.astype(bf), tri_f, preferred_element_type=f32) for p in pieces]
    pr = [jnp.dot(p.astype(bf), tri_r, preferred_element_type=f32) for p in pieces]
    cs = jnp.where(fwd_row, (pf[0] + pf[1]) + pf[2], (pr[0] + pr[1]) + pr[2])
    tot = jnp.where(fwd_row, cs[:, t - 1:t], cs[:, 0:1])
    e_in = jnp.exp(cs)
    e_out = dt * jnp.exp(tot - cs)
    decb = jnp.broadcast_to(jnp.exp(tot), (nc * nr, 128))
    for c in range(nc):
        sl = slice(c * nr, (c + 1) * nr)
        tile = jnp.concatenate([cs[sl], dt[sl], e_in[sl], e_out[sl], jnp.zeros((t - 4 * nr, t), f32)], axis=0)
        row_s[c] = tile[0:4 * nr]
        col_s[c * t:(c + 1) * t, :] = tile.T
        dec_s[c] = decb[sl]

    lower = i1 <= i0
    upper = i1 >= i0
    neg_inf = jnp.float32(-jnp.inf)

    def chunks_a(i, carry):
        ids = [i * SSD_CHUNK_UNROLL + u for u in range(SSD_CHUNK_UNROLL)]
        r0s = [pl.multiple_of(c * t, t) for c in ids]
        cms = [c_s[pl.ds(r0, t), :] for r0 in r0s]
        bms = [b_s[pl.ds(r0, t), :] for r0 in r0s]
        gs = [lax.dot_general(cm, bm, _NT, preferred_element_type=f32) for cm, bm in zip(cms, bms)]
        for c, r0, bm, g in zip(ids, r0s, bms, gs):
            rows = row_s[c]
            cols = col_s[pl.ds(r0, t), :]
            xt = xt_s[c]
            xt_bf = xt.astype(bf)
            parts = []
            for k in range(SSD_HPG):
                kr = SSD_HPG + k
                lf = jnp.exp(jnp.where(lower, cols[:, k:k + 1] - rows[k:k + 1, :], neg_inf)) * rows[8 + k:9 + k, :]
                lr = (jnp.exp(jnp.where(upper, cols[:, kr:kr + 1] - rows[kr:kr + 1, :], neg_inf))
                      * rows[8 + kr:9 + kr, :])
                mk = (g * (lf + lr)).astype(bf)
                xk = xt_bf[k * SSD_HEAD_DIM:(k + 1) * SSD_HEAD_DIM, :]
                parts.append(lax.dot_general(xk, mk, _NT, preferred_element_type=f32))
            yt_s[c] = dsk_ref[0] * xt + jnp.concatenate(parts, axis=0)
            for d in range(2):
                wt = (xt * _head_rows(rows, 24 + SSD_HPG * d, t)).astype(bf)
                p_s[d, c] = jnp.dot(wt, bm, preferred_element_type=f32)
        return carry

    lax.fori_loop(0, nc // SSD_CHUNK_UNROLL, chunks_a, 0)

    st_s[...] = jnp.zeros(st_s.shape, f32)

    def state_step(c, d):
        st = st_s[d]
        sp_s[d, c] = st.astype(bf)
        st_s[d] = st * _head_rows(dec_s[c], SSD_HPG * d, SSD_STATE) + p_s[d, c]

    def ctx_step(i, carry):
        state_step(i, 0)
        state_step(nc_c - 1 - i, 1)
        return carry

    def lat_step(i, carry):
        state_step(nc_c + i, 0)
        state_step(nc - 1 - i, 1)
        return carry

    lax.fori_loop(0, nc_c, ctx_step, 0)
    lax.fori_loop(0, nc_l, lat_step, 0)

    def chunks_c(first, out_ref, o_first):
        ids = [first + u for u in range(SSD_CHUNK_UNROLL)]
        cms = [c_s[pl.ds(pl.multiple_of(c * t, t), t), :] for c in ids]
        offs = [[lax.dot_general(sp_s[d, c], cm, _NT, preferred_element_type=f32) for d in range(2)]
                for c, cm in zip(ids, cms)]
        for u, (c, off) in enumerate(zip(ids, offs)):
            rows = row_s[c]
            yt = yt_s[c]
            for d in range(2):
                yt = yt + off[d] * _head_rows(rows, 16 + SSD_HPG * d, t)
            for l0 in range(0, SSD_GW, 128):
                out_ref[0, pl.ds(pl.multiple_of((o_first + u) * t, t), t), l0:l0 + 128] = yt[l0:l0 + 128, :].T

    def ctx_out(i, carry):
        chunks_c(i * SSD_CHUNK_UNROLL, yc_ref, i * SSD_CHUNK_UNROLL)
        return carry

    def lat_out(i, carry):
        chunks_c(nc_c + i * SSD_CHUNK_UNROLL, yl_ref, i * SSD_CHUNK_UNROLL)
        return carry

    lax.fori_loop(0, nc_c // SSD_CHUNK_UNROLL, ctx_out, 0)
    lax.fori_loop(0, nc_l // SSD_CHUNK_UNROLL, lat_out, 0)


def _rope_tables(seq):
    quarter = SSD_STATE // 4
    inv = ROPE_BASE ** (-np.arange(quarter, dtype=np.float64) / quarter)
    pos = np.arange(seq)
    ang_r = (pos // GRID_W)[:, None] * inv
    ang_c = (pos % GRID_W)[:, None] * inv
    cos = np.concatenate([np.cos(ang_r)] * 2 + [np.cos(ang_c)] * 2, axis=1)
    sin = np.concatenate([-np.sin(ang_r), np.sin(ang_r), -np.sin(ang_c), np.sin(ang_c)], axis=1)
    return jnp.asarray(cos, jnp.float32), jnp.asarray(sin, jnp.float32)


def ssd_mixer(p_lat, p_ctx, dt_lat, dt_ctx, conv_w, conv_b, dt_bias, a_log, d_skip, ctx_cols=FULL_COLS):
    bsz, seq, _ = p_lat.shape
    ctx_len = p_ctx.shape[1]
    t, g, hpg = SSD_CHUNK, SSD_GROUPS, SSD_HPG
    assert seq % t == 0 and ctx_len % t == 0
    nc = (seq + ctx_len) // t
    per_row = lambda v: jnp.broadcast_to(
        v.astype(jnp.float32).reshape(2, g, hpg).transpose(1, 0, 2).reshape(g, 2 * hpg, 1), (g, 2 * hpg, 128))
    dsk = jnp.broadcast_to(jnp.repeat(d_skip.astype(jnp.float32).reshape(g, hpg), SSD_HEAD_DIM, axis=1)[:, :, None],
                           (g, SSD_GW, 128))
    cos, sin = _rope_tables(seq)
    x0 = sum(MAIN_SPLITS[:6]) // SSD_GW
    b0 = (sum(MAIN_SPLITS[:6]) + SSD_WIDTH) // 128
    c0 = b0 + g
    cw = conv_w.astype(jnp.float32)
    cb = conv_b.astype(jnp.float32).reshape(1, SSD_CONV_CH)
    wb0 = SSD_WIDTH // 128
    ltot = seq + ctx_len
    f32, bf = jnp.float32, jnp.bfloat16
    y_lat, y_ctx = pl.pallas_call(
        partial(_ssd_kernel, seq=seq, ctx_len=ctx_len),
        grid=(bsz, g),
        in_specs=[pl.BlockSpec((1, seq, SSD_GW), lambda b, i: (b, 0, x0 + i)),
                  pl.BlockSpec((1, seq, 128), lambda b, i: (b, 0, b0 + i)),
                  pl.BlockSpec((1, seq, 128), lambda b, i: (b, 0, c0 + i)),
                  pl.BlockSpec((1, ctx_len, SSD_GW), lambda b, i: (b, 0, ctx_cols.xbc // SSD_GW + i)),
                  pl.BlockSpec((1, ctx_len, 128), lambda b, i: (b, 0, (ctx_cols.xbc + SSD_WIDTH) // 128 + i)),
                  pl.BlockSpec((1, ctx_len, 128), lambda b, i: (b, 0, (ctx_cols.xbc + SSD_WIDTH) // 128 + g + i)),
                  pl.BlockSpec((1, seq, DT_PAD), lambda b, i: (b, 0, 0)),
                  pl.BlockSpec((1, ctx_len, DT_PAD), lambda b, i: (b, 0, 0)),
                  pl.BlockSpec((1, 2 * hpg, 128), lambda b, i: (i, 0, 0)),
                  pl.BlockSpec((1, 2 * hpg, 128), lambda b, i: (i, 0, 0)),
                  pl.BlockSpec((1, SSD_GW, 128), lambda b, i: (i, 0, 0)),
                  pl.BlockSpec((SSD_CONV, SSD_GW), lambda b, i: (0, i)),
                  pl.BlockSpec((SSD_CONV, 128), lambda b, i: (0, wb0 + i)),
                  pl.BlockSpec((SSD_CONV, 128), lambda b, i: (0, wb0 + g + i)),
                  pl.BlockSpec((1, SSD_GW), lambda b, i: (0, i)),
                  pl.BlockSpec((1, 128), lambda b, i: (0, wb0 + i)),
                  pl.BlockSpec((1, 128), lambda b, i: (0, wb0 + g + i)),
                  pl.BlockSpec((seq, 128), lambda b, i: (0, 0)),
                  pl.BlockSpec((seq, 128), lambda b, i: (0, 0))],
        out_specs=[pl.BlockSpec((1, seq, SSD_GW), lambda b, i: (b, 0, i)),
                   pl.BlockSpec((1, ctx_len, SSD_GW), lambda b, i: (b, 0, i))],
        out_shape=[jax.ShapeDtypeStruct((bsz, seq, SSD_WIDTH), f32),
                   jax.ShapeDtypeStruct((bsz, ctx_len, SSD_WIDTH), f32)],
        scratch_shapes=[pltpu.VMEM((SSD_GW // 128, seq + 2 * CONV_PAD, 128), f32),
                        pltpu.VMEM((seq + 2 * CONV_PAD, 128), f32),
                        pltpu.VMEM((seq + 2 * CONV_PAD, 128), f32),
                        pltpu.VMEM((nc, SSD_GW, t), f32),
                        pltpu.VMEM((ltot, 128), bf),
                        pltpu.VMEM((ltot, 128), bf),
                        pltpu.VMEM((nc, DT_PAD, t), f32),
                        pltpu.VMEM((nc, 4 * 2 * hpg, t), f32),
                        pltpu.VMEM((ltot, 128), f32),
                        pltpu.VMEM((nc, 2 * hpg, 128), f32),
                        pltpu.VMEM((nc, SSD_GW, t), f32),
                        pltpu.VMEM((2, nc, SSD_GW, SSD_STATE), f32),
                        pltpu.VMEM((2, nc, SSD_GW, SSD_STATE), bf),
                        pltpu.VMEM((2, SSD_GW, SSD_STATE), f32)],
        compiler_params=pltpu.CompilerParams(dimension_semantics=("arbitrary", "arbitrary"),
                                             vmem_limit_bytes=V7X_VMEM_LIMIT_BYTES),
        name="ssd_mixer",
    )(p_lat, p_lat, p_lat, p_ctx, p_ctx, p_ctx, dt_lat, dt_ctx, per_row(dt_bias), per_row(a_log), dsk,
      cw, cw, cw, cb, cb, cb, cos, sin)
    return y_ctx, y_lat


INPROJ_TN = 2560
ADA_ROWS = 16


def _layer(xc, xl, c, c_ctx, w_ada_all, layer, b_ada, norm_w, w_main_all, w_dt_all, rpb, four_w, conv_w, conv_b, dt_bias, a_log,
           d_skip, ssd_norm_w, wb_na, wb_four, wb_ssd, w_out, update_ctx, final_norm_w):
    bsz, seq, d = xl.shape
    ctx_len = xc.shape[1]
    bf = jnp.bfloat16
    assert bsz < ADA_ROWS

    cc = jnp.concatenate([c, c_ctx[None], jnp.zeros((ADA_ROWS - bsz - 1, d), c.dtype)], axis=0)
    mod = matmul(jax.nn.silu(cc), w_ada_all, layer, tm=ADA_ROWS, tn=1024) + b_ada
    sh, sc, gt = jnp.split(mod, 3, axis=-1)
    sh_l, sc_l, g_l = sh[:bsz], sc[:bsz], gt[:bsz]
    sh_c, sc_c, g_c = sh[bsz:bsz + 1], sc[bsz:bsz + 1], gt[bsz]

    xl2d = xl.reshape(bsz * seq, d)
    xc2d = xc.reshape(bsz * ctx_len, d)
    pl2d, dt_l = in_projection(xl2d, norm_w, sc_l[:, None], sh_l[:, None], w_main_all, layer, w_dt_all, layer,
                               rows_per_mod=seq, tm=1024, tn=INPROJ_TN, out_dtype=bf)
    if update_ctx:
        ctx_cols, w_ctx, ctx_layer, ctx_tn = FULL_COLS, w_main_all, layer, INPROJ_TN
    else:
        ctx_cols, ctx_layer = KV_XBC_COLS, 0
        w_ctx = jnp.concatenate([w_main_all[layer, FULL_COLS.k:FULL_COLS.v + NA_WIDTH],
                                 w_main_all[layer, FULL_COLS.xbc:FULL_COLS.xbc + SSD_CONV_CH]], axis=0)[None]
        ctx_tn = w_ctx.shape[1] // 2
    pc2d, dt_c = in_projection(xc2d, norm_w, sc_c[:, None], sh_c[:, None], w_ctx, ctx_layer, w_dt_all, layer,
                               rows_per_mod=bsz * ctx_len, tm=1024, tn=ctx_tn, out_dtype=bf)
    p_l = pl2d.reshape(bsz, seq, MAIN_WIDTH)
    p_c = pc2d.reshape(bsz, ctx_len, w_ctx.shape[1])
    dt_l = dt_l.reshape(bsz, seq, DT_PAD)
    dt_c = dt_c.reshape(bsz, ctx_len, DT_PAD)

    four_wb, wb_nab, wb_fourb, wb_ssdb, w_outb = (t.astype(bf) for t in (four_w, wb_na, wb_four, wb_ssd, w_out))

    a_l, a_c = na_attention(p_l, p_c, na_bias_table(rpb), with_ctx=update_ctx, out_dtype=bf, ctx_cols=ctx_cols)
    gfo_l = fourier_branch(p_l, four_wb, tm=512)
    s_c, s_l = ssd_mixer(p_l, p_c, dt_l, dt_c, conv_w, conv_b, dt_bias, a_log, d_skip, ctx_cols=ctx_cols)
    m_l = merge_branches(pl2d, a_l.reshape(bsz * seq, NA_WIDTH), gfo_l.reshape(bsz * seq, FOUR_WIDTH),
                         s_l.reshape(bsz * seq, SSD_WIDTH), ssd_norm_w, wb_nab, wb_fourb, wb_ssdb, tm=512)
    if final_norm_w is None:
        xl_new = out_projection(m_l, w_outb, xl2d, g_l[:, None], rows_per_mod=seq, tm=1024, tn=1024)
    else:
        xl_new = out_projection(m_l, w_outb, xl2d, g_l[:, None], rows_per_mod=seq, tm=512, tn=d,
                                final_norm_w=final_norm_w)
    xl_new = xl_new.reshape(bsz, seq, d)
    if update_ctx:
        gfo_c = fourier_branch(p_c, four_wb, tm=ctx_len)
        m_c = merge_branches(pc2d, a_c.reshape(bsz * ctx_len, NA_WIDTH), gfo_c.reshape(bsz * ctx_len, FOUR_WIDTH),
                             s_c.reshape(bsz * ctx_len, SSD_WIDTH), ssd_norm_w, wb_nab, wb_fourb, wb_ssdb, tm=512)
        xc = out_projection(m_c, w_outb, xc2d, g_c[None, None], rows_per_mod=bsz * ctx_len, tm=1024, tn=1024)
        xc = xc.reshape(bsz, ctx_len, d)
    return xc, xl_new


def kernel(x, c, ctx, c_ctx, w_ada, b_ada, norm_w, w_in, na_rpb, four_w, ssd_conv_w, ssd_conv_b, ssd_dt_bias,
           ssd_a_log, ssd_d, ssd_norm_w, wb_na, wb_four, wb_ssd, w_out, final_norm_w):
    xc, xl = ctx, x
    w_main_all, w_dt_all = repack_w_in(w_in)
    for l in range(DEPTH):
        xc, xl = _layer(xc, xl, c, c_ctx, w_ada, l, b_ada[l], norm_w[l], w_main_all, w_dt_all, na_rpb[l], four_w[l],
                        ssd_conv_w[l], ssd_conv_b[l], ssd_dt_bias[l], ssd_a_log[l], ssd_d[l], ssd_norm_w[l],
                        wb_na[l], wb_four[l], wb_ssd[l], w_out[l], update_ctx=(l < DEPTH - 1),
                        final_norm_w=final_norm_w if l == DEPTH - 1 else None)
    return xl
```

```python
import math
from functools import partial
from typing import NamedTuple

import jax
import jax.numpy as jnp
import numpy as np
from jax import lax
from jax.experimental import pallas as pl
from jax.experimental.pallas import tpu as pltpu

D_MODEL = 2048
DEPTH = 2
GRID_W = 64
EPS = 1e-6
NA_HEADS = 8
NA_HEAD_DIM = 128
NA_WIDTH = NA_HEADS * NA_HEAD_DIM
NA_WIN_R = 8
NA_WIN_C = 16
LOG2E = math.log2(math.e)
FOUR_GROUPS = 4
FOUR_GROUP_DIM = 256
FOUR_WIDTH = FOUR_GROUPS * FOUR_GROUP_DIM
SSD_HEADS = 16
SSD_HEAD_DIM = 64
SSD_WIDTH = SSD_HEADS * SSD_HEAD_DIM
SSD_GROUPS = 4
SSD_HPG = SSD_HEADS // SSD_GROUPS
SSD_STATE = 128
SSD_CONV = 7
SSD_CHUNK = 128
SSD_CONV_CH = SSD_WIDTH + 2 * SSD_GROUPS * SSD_STATE
ROPE_BASE = 10000.0
DT_WIDTH = 2 * SSD_HEADS
MAIN_SPLITS = (NA_WIDTH, NA_WIDTH, NA_WIDTH, NA_WIDTH, FOUR_WIDTH, FOUR_WIDTH, SSD_CONV_CH, SSD_WIDTH,
               D_MODEL, D_MODEL, D_MODEL)
MAIN_WIDTH = sum(MAIN_SPLITS)
DT_OFFSET = sum(MAIN_SPLITS[:8])
DT_PAD = 128


class ProjCols(NamedTuple):
    k: int
    v: int
    xbc: int


FULL_COLS = ProjCols(k=NA_WIDTH, v=2 * NA_WIDTH, xbc=sum(MAIN_SPLITS[:6]))
KV_XBC_COLS = ProjCols(k=0, v=NA_WIDTH, xbc=2 * NA_WIDTH)

V7X_VMEM_LIMIT_BYTES = 56 * 1024 * 1024


def _mm_kernel(a_ref, b_ref, o_ref):
    a = a_ref[...].astype(jnp.bfloat16)
    b = b_ref[...].astype(jnp.bfloat16)
    o_ref[...] = jnp.dot(a, b, preferred_element_type=jnp.float32).astype(o_ref.dtype)


def matmul(a, b, layer, *, tm, tn, out_dtype=jnp.float32):
    m, k = a.shape
    _, _, n = b.shape
    assert m % tm == 0 and n % tn == 0, (a.shape, b.shape, tm, tn)
    return pl.pallas_call(
        _mm_kernel,
        grid=(m // tm, n // tn),
        in_specs=[pl.BlockSpec((tm, k), lambda i, j: (i, 0)),
                  pl.BlockSpec((None, k, tn), lambda i, j: (layer, 0, j))],
        out_specs=pl.BlockSpec((tm, tn), lambda i, j: (i, j)),
        out_shape=jax.ShapeDtypeStruct((m, n), out_dtype),
        compiler_params=pltpu.CompilerParams(dimension_semantics=("arbitrary", "arbitrary"),
                                             vmem_limit_bytes=V7X_VMEM_LIMIT_BYTES),
        name="matmul",
    )(a, b)


REPACK_TN = 1024


def _repack_kernel(a_ref, nxt_ref, o_ref, odt_ref):
    j = pl.program_id(1)
    first_below = DT_OFFSET // REPACK_TN

    @pl.when(j < first_below)
    def _():
        o_ref[...] = a_ref[...].astype(o_ref.dtype)

    @pl.when(j >= first_below)
    def _():
        o_ref[...] = jnp.concatenate([a_ref[DT_WIDTH:, :], nxt_ref[...]], axis=0).astype(o_ref.dtype)

    @pl.when(j == first_below)
    def _():
        odt_ref[...] = a_ref[:DT_PAD, :].astype(odt_ref.dtype)


def repack_w_in(w_in):
    depth, d, _ = w_in.shape
    tn = REPACK_TN
    w_t = jnp.swapaxes(w_in, 1, 2)
    return pl.pallas_call(
        _repack_kernel,
        grid=(depth, MAIN_WIDTH // tn),
        in_specs=[pl.BlockSpec((None, tn, d), lambda l, j: (l, j, 0)),
                  pl.BlockSpec((None, DT_WIDTH, d), lambda l, j: (l, (j + 1) * (tn // DT_WIDTH), 0))],
        out_specs=[pl.BlockSpec((None, tn, d), lambda l, j: (l, j, 0)),
                   pl.BlockSpec((None, DT_PAD, d), lambda l, j: (l, 0, 0))],
        out_shape=[jax.ShapeDtypeStruct((depth, MAIN_WIDTH, d), jnp.bfloat16),
                   jax.ShapeDtypeStruct((depth, DT_PAD, d), jnp.bfloat16)],
        compiler_params=pltpu.CompilerParams(dimension_semantics=("arbitrary", "arbitrary"),
                                             vmem_limit_bytes=V7X_VMEM_LIMIT_BYTES),
        name="repack_w_in",
    )(w_t, w_t)


def _inproj_kernel(x_ref, nw_ref, sc_ref, sh_ref, w_ref, wdt_ref, o_ref, dt_ref, h_ref):
    @pl.when(pl.program_id(1) == 0)
    def _():
        x = x_ref[...]
        y = x * lax.rsqrt(jnp.mean(x * x, axis=-1, keepdims=True) + EPS) * nw_ref[...]
        h = (y * (1.0 + sc_ref[0]) + sh_ref[0]).astype(jnp.bfloat16)
        h_ref[...] = h
        dt_ref[...] = lax.dot_general(h, wdt_ref[...], _NT, preferred_element_type=jnp.float32)

    o_ref[...] = lax.dot_general(h_ref[...], w_ref[...], _NT, preferred_element_type=jnp.float32).astype(o_ref.dtype)


def in_projection(x2d, norm_w, scale, shift, w_main_t, layer, w_dt_t, dt_layer, *, rows_per_mod, tm, tn, out_dtype):
    m, d = x2d.shape
    n = w_main_t.shape[1]
    assert m % tm == 0 and n % tn == 0 and rows_per_mod % tm == 0
    per = rows_per_mod // tm
    return pl.pallas_call(
        _inproj_kernel,
        grid=(m // tm, n // tn),
        in_specs=[pl.BlockSpec((tm, d), lambda i, j: (i, 0)),
                  pl.BlockSpec((1, d), lambda i, j: (0, 0)),
                  pl.BlockSpec((1, 1, d), lambda i, j: (i // per, 0, 0)),
                  pl.BlockSpec((1, 1, d), lambda i, j: (i // per, 0, 0)),
                  pl.BlockSpec((None, tn, d), lambda i, j: (layer, j, 0)),
                  pl.BlockSpec((None, DT_PAD, d), lambda i, j: (dt_layer, 0, 0))],
        out_specs=[pl.BlockSpec((tm, tn), lambda i, j: (i, j)),
                   pl.BlockSpec((tm, DT_PAD), lambda i, j: (i, 0))],
        out_shape=[jax.ShapeDtypeStruct((m, n), out_dtype),
                   jax.ShapeDtypeStruct((m, DT_PAD), jnp.float32)],
        scratch_shapes=[pltpu.VMEM((tm, d), jnp.bfloat16)],
        compiler_params=pltpu.CompilerParams(dimension_semantics=("arbitrary", "arbitrary"),
                                             vmem_limit_bytes=V7X_VMEM_LIMIT_BYTES),
        name="in_projection",
    )(x2d, norm_w.reshape(1, d), scale, shift, w_main_t, w_dt_t)


GRID_ROWS = 32
NA_WIN_TOKENS = NA_WIN_R * GRID_W
_NT = (((1,), (1,)), ((), ()))
NA_ROW_UNROLL = 16


def _na_kernel(*refs, with_ctx):
    if with_ctx:
        q_ref, k_ref, v_ref, kc_ref, vc_ref, bias_ref, qc_ref, o_ref, oc_ref = refs
    else:
        q_ref, k_ref, v_ref, kc_ref, vc_ref, bias_ref, o_ref = refs
    bf = jnp.bfloat16
    scale = NA_HEAD_DIM ** -0.5
    scale2 = scale * LOG2E
    kc = kc_ref[0].astype(bf)
    vc = vc_ref[0].astype(bf)
    nq = NA_ROW_UNROLL * GRID_W

    def lane_tiles(x, op):
        acc = x[:, 0:128]
        for l0 in range(128, x.shape[1], 128):
            acc = op(acc, x[:, l0:l0 + 128])
        return acc

    def rows(rb, carry):
        qb0 = pl.multiple_of(rb * nq, nq)
        q_all = q_ref[0, pl.ds(qb0, nq), :].astype(bf)
        s_c_all = lax.dot_general(q_all, kc, _NT, preferred_element_type=jnp.float32) * scale2
        k0s, scores, probs = [], [], []
        for i in range(NA_ROW_UNROLL):
            r = rb * NA_ROW_UNROLL + i
            start = jnp.clip(r - NA_WIN_R // 2, 0, GRID_ROWS - NA_WIN_R)
            k0 = pl.multiple_of(start * GRID_W, GRID_W)
            q = q_all[i * GRID_W:(i + 1) * GRID_W]
            kw = k_ref[0, pl.ds(k0, NA_WIN_TOKENS), :].astype(bf)
            s_w = lax.dot_general(q, kw, _NT, preferred_element_type=jnp.float32) * scale2 + bias_ref[0, r - start]
            k0s.append(k0)
            scores.append((s_w, s_c_all[i * GRID_W:(i + 1) * GRID_W]))
        for s_w, s_c in scores:
            m = jnp.max(jnp.maximum(lane_tiles(s_w, jnp.maximum), lane_tiles(s_c, jnp.maximum)),
                        axis=-1, keepdims=True)
            e_w = jnp.exp2(s_w - m)
            e_c = jnp.exp2(s_c - m)
            denom = jnp.sum(lane_tiles(e_w, jnp.add) + lane_tiles(e_c, jnp.add), axis=-1, keepdims=True)
            probs.append((e_w.astype(bf), e_c.astype(bf), denom))
        o_c_all = jnp.dot(jnp.concatenate([p[1] for p in probs], axis=0), vc, preferred_element_type=jnp.float32)
        for i, (k0, (e_w, _, denom)) in enumerate(zip(k0s, probs)):
            vw = v_ref[0, pl.ds(k0, NA_WIN_TOKENS), :].astype(bf)
            o = jnp.dot(e_w, vw, preferred_element_type=jnp.float32) + o_c_all[i * GRID_W:(i + 1) * GRID_W]
            o_ref[0, pl.ds(qb0 + i * GRID_W, GRID_W), :] = (o / denom).astype(o_ref.dtype)
        return carry

    lax.fori_loop(0, GRID_ROWS // NA_ROW_UNROLL, rows, 0)

    if with_ctx:
        s = lax.dot_general(qc_ref[0].astype(bf), kc, _NT, preferred_element_type=jnp.float32) * scale
        e = jnp.exp(s - jnp.max(s, axis=-1, keepdims=True))
        o = jnp.dot(e.astype(bf), vc, preferred_element_type=jnp.float32)
        oc_ref[0] = (o / jnp.sum(e, axis=-1, keepdims=True)).astype(oc_ref.dtype)


def na_bias_table(rpb):
    qcol = np.arange(GRID_W)[:, None]
    kcol = np.arange(GRID_W)[None, :]
    qstart = np.clip(qcol - NA_WIN_C // 2, 0, GRID_W - NA_WIN_C)
    mask = (kcol >= qstart) & (kcol < qstart + NA_WIN_C)
    dc = np.clip(kcol - qcol + NA_WIN_C - 1, 0, 2 * NA_WIN_C - 2)
    pick_col = (np.arange(2 * NA_WIN_C - 1)[:, None, None] == dc[None]).astype(np.float32)
    rows = jnp.stack([rpb.astype(jnp.float32)[:, NA_WIN_R - 1 - dl:2 * NA_WIN_R - 1 - dl] for dl in range(NA_WIN_R)],
                     axis=1)
    t = jnp.einsum("hdwc,cqk->hdqwk", rows, pick_col, precision=lax.Precision.HIGHEST)
    t = jnp.where(mask[None, None, :, None, :], t * LOG2E, -1e30)
    return t.reshape(rpb.shape[0], NA_WIN_R, GRID_W, NA_WIN_TOKENS)


def na_attention(p_lat, p_ctx, bias, *, with_ctx, out_dtype, ctx_cols=FULL_COLS):
    bsz, seq, _ = p_lat.shape
    ctx_len = p_ctx.shape[1]
    assert seq == GRID_ROWS * GRID_W
    d, h = NA_HEAD_DIM, NA_HEADS
    in_specs = [pl.BlockSpec((1, seq, d), lambda b, i: (b, 0, i)),
                pl.BlockSpec((1, seq, d), lambda b, i: (b, 0, h + i)),
                pl.BlockSpec((1, seq, d), lambda b, i: (b, 0, 2 * h + i)),
                pl.BlockSpec((1, ctx_len, d), lambda b, i: (b, 0, ctx_cols.k // d + i)),
                pl.BlockSpec((1, ctx_len, d), lambda b, i: (b, 0, ctx_cols.v // d + i)),
                pl.BlockSpec((1, NA_WIN_R, GRID_W, NA_WIN_TOKENS), lambda b, i: (i, 0, 0, 0))]
    args = [p_lat, p_lat, p_lat, p_ctx, p_ctx, bias]
    out_specs = [pl.BlockSpec((1, seq, d), lambda b, i: (b, 0, i))]
    out_shape = [jax.ShapeDtypeStruct((bsz, seq, h * d), out_dtype)]
    if with_ctx:
        in_specs.append(pl.BlockSpec((1, ctx_len, d), lambda b, i: (b, 0, i)))
        args.append(p_ctx)
        out_specs.append(pl.BlockSpec((1, ctx_len, d), lambda b, i: (b, 0, i)))
        out_shape.append(jax.ShapeDtypeStruct((bsz, ctx_len, h * d), out_dtype))
    outs = pl.pallas_call(
        partial(_na_kernel, with_ctx=with_ctx),
        grid=(bsz, h),
        in_specs=in_specs, out_specs=out_specs, out_shape=out_shape,
        compiler_params=pltpu.CompilerParams(dimension_semantics=("arbitrary", "arbitrary"),
                                             vmem_limit_bytes=V7X_VMEM_LIMIT_BYTES),
        name="na_attention",
    )(*args)
    return (outs[0], outs[1]) if with_ctx else (outs[0], None)


def _dft_cos_sin(n):
    jk = np.outer(np.arange(n), np.arange(n)) % n
    ang = 2.0 * np.pi * jk / n
    return np.cos(ang) / np.sqrt(n), np.sin(ang) / np.sqrt(n)


def _fourier_kernel(u_ref, w1_ref, a_ref, fw_ref, zf_ref, o_ref, z_ref, *, seq, row_blk):
    bf = jnp.bfloat16
    gd = FOUR_GROUP_DIM

    @pl.when(pl.program_id(1) == 0)
    def _():
        for r0 in range(0, seq, row_blk):
            for g in range(FOUR_GROUPS):
                ug = u_ref[0, r0:r0 + row_blk, g * gd:(g + 1) * gd].astype(bf)
                z = jnp.dot(ug, w1_ref[...], preferred_element_type=jnp.float32)
                z_ref[r0:r0 + row_blk, g * gd:(g + 1) * gd] = z[:, :gd].astype(bf)
                z_ref[seq + r0:seq + r0 + row_blk, g * gd:(g + 1) * gd] = z[:, gd:].astype(bf)

    y = jnp.dot(a_ref[...], z_ref[...], preferred_element_type=jnp.float32)
    f = jnp.dot(y.astype(bf), fw_ref[...], preferred_element_type=jnp.float32)
    zf = zf_ref[0].astype(jnp.float32)
    o_ref[0] = (f * (zf * jax.nn.sigmoid(zf))).astype(o_ref.dtype)


def fourier_branch(p, four_w_bf, *, tm):
    bsz, seq, _ = p.shape
    assert seq % tm == 0
    cc, sc = _dft_cos_sin(FOUR_GROUP_DIM)
    cl, sl = _dft_cos_sin(seq)
    w1 = jnp.asarray(np.concatenate([cc, sc], axis=1), jnp.bfloat16)
    a = jnp.asarray(np.concatenate([cl, -sl], axis=1), jnp.bfloat16)
    fw = FOUR_WIDTH
    return pl.pallas_call(
        partial(_fourier_kernel, seq=seq, row_blk=min(seq, 512)),
        grid=(bsz, seq // tm),
        in_specs=[pl.BlockSpec((1, seq, fw), lambda b, i: (b, 0, 4)),
                  pl.BlockSpec((FOUR_GROUP_DIM, 2 * FOUR_GROUP_DIM), lambda b, i: (0, 0)),
                  pl.BlockSpec((tm, 2 * seq), lambda b, i: (i, 0)),
                  pl.BlockSpec((fw, fw), lambda b, i: (0, 0)),
                  pl.BlockSpec((1, tm, fw), lambda b, i: (b, i, 5))],
        out_specs=pl.BlockSpec((1, tm, fw), lambda b, i: (b, i, 0)),
        out_shape=jax.ShapeDtypeStruct((bsz, seq, fw), jnp.bfloat16),
        scratch_shapes=[pltpu.VMEM((2 * seq, fw), jnp.bfloat16)],
        compiler_params=pltpu.CompilerParams(dimension_semantics=("arbitrary", "arbitrary"),
                                             vmem_limit_bytes=V7X_VMEM_LIMIT_BYTES),
        name="fourier_branch",
    )(p, w1, a, four_w_bf, p)


MERGE_TN = 1024


def _merge_kernel(a_ref, zna_ref, gfo_ref, s_ref, zs_ref, nw_ref, wa_ref, wf_ref, ws_ref, *rest):
    n_half = D_MODEL // MERGE_TN
    gate_refs, (o_ref, ap_ref, sp_ref) = rest[:3 * n_half], rest[3 * n_half:]
    bf = jnp.bfloat16
    zna = zna_ref[...].astype(jnp.float32)
    ap_ref[...] = (a_ref[...].astype(jnp.float32) * (zna * jax.nn.sigmoid(zna))).astype(bf)
    zs = zs_ref[...].astype(jnp.float32)
    t = s_ref[...].astype(jnp.float32) * (zs * jax.nn.sigmoid(zs))
    t = t * lax.rsqrt(jnp.mean(t * t, axis=-1, keepdims=True) + EPS) * nw_ref[...]
    sp_ref[...] = t.astype(bf)
    for h in range(n_half):
        cols = slice(h * MERGE_TN, (h + 1) * MERGE_TN)
        ga_ref, gf_ref, gs_ref = gate_refs[h], gate_refs[n_half + h], gate_refs[2 * n_half + h]
        o_a = jnp.dot(ap_ref[...], wa_ref[:, cols], preferred_element_type=jnp.float32)
        o_f = jnp.dot(gfo_ref[...], wf_ref[:, cols], preferred_element_type=jnp.float32)
        o_s = jnp.dot(sp_ref[...], ws_ref[:, cols], preferred_element_type=jnp.float32)
        m = (jax.nn.sigmoid(ga_ref[...].astype(jnp.float32)) * o_a
             + jax.nn.sigmoid(gf_ref[...].astype(jnp.float32)) * o_f
             + jax.nn.sigmoid(gs_ref[...].astype(jnp.float32)) * o_s)
        o_ref[:, cols] = m.astype(o_ref.dtype)


def merge_branches(p2d, a2d, gfo2d, s2d, ssd_norm_w, wa, wf, ws, *, tm):
    m = p2d.shape[0]
    w, tn = NA_WIDTH, MERGE_TN
    n_half = D_MODEL // tn
    assert m % tm == 0
    row = lambda i: (i, 0)
    resident = lambda shape: pl.BlockSpec(shape, lambda i: (0, 0), pipeline_mode=pl.Buffered(1))
    gate0 = DT_OFFSET // tn
    gate_specs = [pl.BlockSpec((tm, tn), partial(lambda i, blk: (i, blk), blk=gate0 + g * n_half + h))
                  for g in range(3) for h in range(n_half)]
    return pl.pallas_call(
        _merge_kernel,
        grid=(m // tm,),
        in_specs=[pl.BlockSpec((tm, w), row),
                  pl.BlockSpec((tm, w), lambda i: (i, 3)),
                  pl.BlockSpec((tm, w), row),
                  pl.BlockSpec((tm, w), row),
                  pl.BlockSpec((tm, w), lambda i: (i, 8)),
                  resident((1, w)),
                  resident((w, D_MODEL)), resident((w, D_MODEL)), resident((w, D_MODEL))] + gate_specs,
        out_specs=pl.BlockSpec((tm, D_MODEL), row),
        out_shape=jax.ShapeDtypeStruct((m, D_MODEL), jnp.bfloat16),
        scratch_shapes=[pltpu.VMEM((tm, w), jnp.bfloat16), pltpu.VMEM((tm, w), jnp.bfloat16)],
        compiler_params=pltpu.CompilerParams(dimension_semantics=("arbitrary",),
                                             vmem_limit_bytes=V7X_VMEM_LIMIT_BYTES),
        name="merge_branches",
    )(a2d, p2d, gfo2d, s2d, p2d, ssd_norm_w.reshape(1, w), wa, wf, ws, *([p2d] * (3 * n_half)))


def _outproj_kernel(m_ref, w_ref, x_ref, g_ref, *rest, final_norm):
    y = jnp.dot(m_ref[...], w_ref[...], preferred_element_type=jnp.float32)
    r = x_ref[...] + g_ref[0] * y
    if final_norm:
        nw_ref, o_ref = rest
        o_ref[...] = r * lax.rsqrt(jnp.mean(r * r, axis=-1, keepdims=True) + EPS) * nw_ref[...]
    else:
        rest[0][...] = r


def out_projection(m2d, w_out_bf, x2d, gate, *, rows_per_mod, tm, tn, final_norm_w=None):
    m, d = x2d.shape
    assert m % tm == 0 and d % tn == 0 and rows_per_mod % tm == 0
    per = rows_per_mod // tm
    in_specs = [pl.BlockSpec((tm, d), lambda i, j: (i, 0)),
                pl.BlockSpec((d, tn), lambda i, j: (0, j)),
                pl.BlockSpec((tm, tn), lambda i, j: (i, j)),
                pl.BlockSpec((1, 1, tn), lambda i, j: (i // per, 0, j))]
    args = [m2d, w_out_bf, x2d, gate]
    if final_norm_w is not None:
        assert tn == d
        in_specs.append(pl.BlockSpec((1, d), lambda i, j: (0, 0)))
        args.append(final_norm_w.astype(jnp.float32).reshape(1, d))
    return pl.pallas_call(
        partial(_outproj_kernel, final_norm=final_norm_w is not None),
        grid=(m // tm, d // tn),
        in_specs=in_specs,
        out_specs=pl.BlockSpec((tm, tn), lambda i, j: (i, j)),
        out_shape=jax.ShapeDtypeStruct((m, d), jnp.float32),
        compiler_params=pltpu.CompilerParams(dimension_semantics=("arbitrary", "arbitrary"),
                                             vmem_limit_bytes=V7X_VMEM_LIMIT_BYTES),
        name="out_projection",
    )(*args)


SSD_GW = SSD_HPG * SSD_HEAD_DIM
CONV_PAD = 8
SSD_CHUNK_UNROLL = 2
SSD_A_UNROLL = 3


def _split3_bf16(a):
    rnd = lambda v: v.astype(jnp.bfloat16).astype(jnp.float32)
    a1 = rnd(a)
    a2 = rnd(a - a1)
    a3 = rnd((a - a1) - a2)
    return a1, a2, a3


def _head_rows(rows, r0, n):
    return jnp.concatenate([jnp.broadcast_to(rows[r0 + k:r0 + k + 1, :], (SSD_HEAD_DIM, n))
                            for k in range(SSD_HPG)], axis=0)


def _ssd_kernel(xl_ref, bl_ref, cl_ref, xc_ref, bc_ref, cc_ref, dtl_ref, dtc_ref, bias_ref, alog_ref, dsk_ref,
                wx_ref, wb_ref, wc_ref, bx_ref, bb_ref, bcb_ref, cos_ref, sin_ref,
                yl_ref, yc_ref,
                padx, padb, padc, xt_s, b_s, c_s, dtt_s, row_s, col_s, dec_s, yt_s, p_s, sp_s, st_s, *, seq, ctx_len):
    f32, bf = jnp.float32, jnp.bfloat16
    t = SSD_CHUNK
    nc_c, nc_l = ctx_len // t, seq // t
    nc = nc_c + nc_l

    lane = lax.broadcasted_iota(jnp.int32, (t, 128), 1)
    first_half = (lane % 64) < 32

    def conv_tile(pad, w_ref, b_ref, t0, l0):
        acc = jnp.broadcast_to(b_ref[:, l0:l0 + 128], (t, 128))
        for j in range(SSD_CONV):
            off = CONV_PAD - SSD_CONV // 2 + j
            acc = acc + pad[pl.ds(t0 + off, t), :] * w_ref[j:j + 1, l0:l0 + 128]
        return acc * jax.nn.sigmoid(acc)

    def rope(u, t0):
        swapped = jnp.where(first_half, pltpu.roll(u, 96, axis=1), pltpu.roll(u, 32, axis=1))
        return u * cos_ref[pl.ds(t0, t), :] + swapped * sin_ref[pl.ds(t0, t), :]

    def conv_seq(x_ref, bm_ref, cm_ref, n, base, use_rope):
        slabs = ([(padx.at[h], x_ref, h * 128) for h in range(SSD_GW // 128)]
                 + [(padb, bm_ref, 0), (padc, cm_ref, 0)])
        for pad, src, l0 in slabs:
            pad[0:CONV_PAD, :] = jnp.zeros((CONV_PAD, 128), f32)
            pad[CONV_PAD + n:2 * CONV_PAD + n, :] = jnp.zeros((CONV_PAD, 128), f32)
            pad[CONV_PAD:CONV_PAD + n, :] = src[0, :, l0:l0 + 128].astype(f32)

        def chunk(ci, carry):
            t0 = pl.multiple_of(ci * t, t)
            r0 = pl.multiple_of(base + ci * t, t)
            for h in range(SSD_GW // 128):
                l0 = h * 128
                xt_s[base // t + ci, l0:l0 + 128, :] = conv_tile(padx.at[h], wx_ref, bx_ref, t0, l0).T
            ub = conv_tile(padb, wb_ref, bb_ref, t0, 0)
            uc = conv_tile(padc, wc_ref, bcb_ref, t0, 0)
            if use_rope:
                ub, uc = rope(ub, t0), rope(uc, t0)
            b_s[pl.ds(r0, t), :] = ub.astype(bf)
            c_s[pl.ds(r0, t), :] = uc.astype(bf)
            return carry

        lax.fori_loop(0, n // t, chunk, 0)

    conv_seq(xc_ref, bc_ref, cc_ref, ctx_len, 0, False)
    conv_seq(xl_ref, bl_ref, cl_ref, seq, ctx_len, True)

    i0 = lax.broadcasted_iota(jnp.int32, (t, t), 0)
    i1 = lax.broadcasted_iota(jnp.int32, (t, t), 1)
    tri_f = (i0 <= i1).astype(bf)
    tri_r = (i0 >= i1).astype(bf)
    nr = 2 * SSD_HPG
    fwd_row = (lax.broadcasted_iota(jnp.int32, (nc * nr, 1), 0) % nr) < SSD_HPG
    g4 = pl.program_id(1) * SSD_HPG
    raw = []
    for c in range(nc):
        src, c0 = (dtc_ref, c) if c < nc_c else (dtl_ref, c - nc_c)
        dtt_s[c] = src[0, c0 * t:(c0 + 1) * t, :].T
        raw += [dtt_s[c, pl.ds(g4, SSD_HPG), :], dtt_s[c, pl.ds(SSD_HEADS + g4, SSD_HPG), :]]
    x = jnp.concatenate(raw, axis=0) + jnp.tile(bias_ref[0], (nc, 1))
    dt = jnp.maximum(x, 0.0) + jnp.log1p(jnp.exp(-jnp.abs(x)))
    pieces = _split3_bf16(dt * jnp.tile(-jnp.exp(alog_ref[0]), (nc, 1)))
    pf = [jnp.dot(p.astype(bf), tri_f, preferred_element_type=f32) for p in pieces]
    pr = [jnp.dot(p.astype(bf), tri_r, preferred_element_type=f32) for p in pieces]
    cs = jnp.where(fwd_row, (pf[0] + pf[1]) + pf[2], (pr[0] + pr[1]) + pr[2])
    tot = jnp.where(fwd_row, cs[:, t - 1:t], cs[:, 0:1])
    e_in = jnp.exp(cs)
    e_out = dt * jnp.exp(tot - cs)
    u = cs - jnp.log(dt)
    decb = jnp.broadcast_to(jnp.exp(tot), (nc * nr, 128))
    for c in range(nc):
        sl = slice(c * nr, (c + 1) * nr)
        tile = jnp.concatenate([cs[sl], dt[sl], e_in[sl], e_out[sl], u[sl], jnp.zeros((t - 5 * nr, t), f32)], axis=0)
        row_s[c] = tile[0:5 * nr]
        col_s[c * t:(c + 1) * t, :] = tile.T
        dec_s[c] = decb[sl]

    lower = i1 <= i0
    diag = i1 == i0

    def chunks_a(i, carry):
        ids = [i * SSD_A_UNROLL + j for j in range(SSD_A_UNROLL)]
        r0s = [pl.multiple_of(c * t, t) for c in ids]
        cms = [c_s[pl.ds(r0, t), :] for r0 in r0s]
        bms = [b_s[pl.ds(r0, t), :] for r0 in r0s]
        gs = [lax.dot_general(cm, bm, _NT, preferred_element_type=f32) for cm, bm in zip(cms, bms)]
        for c, r0, bm, g in zip(ids, r0s, bms, gs):
            rows = row_s[c]
            cols = col_s[pl.ds(r0, t), :]
            xt = xt_s[c]
            xt_bf = xt.astype(bf)
            parts = []
            for k in range(SSD_HPG):
                kr = SSD_HPG + k
                w = jnp.exp(jnp.where(lower, cols[:, k:k + 1] - rows[32 + k:33 + k, :],
                                      cols[:, kr:kr + 1] - rows[32 + kr:33 + kr, :]))
                mk = (g * (w + jnp.where(diag, rows[8 + kr:9 + kr, :], 0.0))).astype(bf)
                xk = xt_bf[k * SSD_HEAD_DIM:(k + 1) * SSD_HEAD_DIM, :]
                parts.append(lax.dot_general(xk, mk, _NT, preferred_element_type=f32))
            yt_s[c] = dsk_ref[0] * xt + jnp.concatenate(parts, axis=0)
            for d in range(2):
                wt = (xt * _head_rows(rows, 24 + SSD_HPG * d, t)).astype(bf)
                p_s[d, c] = jnp.dot(wt, bm, preferred_element_type=f32)
        return carry

    lax.fori_loop(0, nc // SSD_A_UNROLL, chunks_a, 0)

    st_s[...] = jnp.zeros(st_s.shape, f32)

    def state_step(c, d):
        st = st_s[d]
        sp_s[d, c] = st.astype(bf)
        st_s[d] = st * _head_rows(dec_s[c], SSD_HPG * d, SSD_STATE) + p_s[d, c]

    def ctx_step(i, carry):
        state_step(i, 0)
        state_step(nc_c - 1 - i, 1)
        return carry

    def lat_step(i, carry):
        state_step(nc_c + i, 0)
        state_step(nc - 1 - i, 1)
        return carry

    lax.fori_loop(0, nc_c, ctx_step, 0)
    lax.fori_loop(0, nc_l, lat_step, 0)

    def chunks_c(first, out_ref, o_first):
        ids = [first + u for u in range(SSD_CHUNK_UNROLL)]
        cms = [c_s[pl.ds(pl.multiple_of(c * t, t), t), :] for c in ids]
        offs = [[lax.dot_general(sp_s[d, c], cm, _NT, preferred_element_type=f32) for d in range(2)]
                for c, cm in zip(ids, cms)]
        for u, (c, off) in enumerate(zip(ids, offs)):
            rows = row_s[c]
            yt = yt_s[c]
            for d in range(2):
                yt = yt + off[d] * _head_rows(rows, 16 + SSD_HPG * d, t)
            for l0 in range(0, SSD_GW, 128):
                out_ref[0, pl.ds(pl.multiple_of((o_first + u) * t, t), t), l0:l0 + 128] = yt[l0:l0 + 128, :].T

    def ctx_out(i, carry):
        chunks_c(i * SSD_CHUNK_UNROLL, yc_ref, i * SSD_CHUNK_UNROLL)
        return carry

    def lat_out(i, carry):
        chunks_c(nc_c + i * SSD_CHUNK_UNROLL, yl_ref, i * SSD_CHUNK_UNROLL)
        return carry

    lax.fori_loop(0, nc_c // SSD_CHUNK_UNROLL, ctx_out, 0)
    lax.fori_loop(0, nc_l // SSD_CHUNK_UNROLL, lat_out, 0)


def _rope_tables(seq):
    quarter = SSD_STATE // 4
    inv = ROPE_BASE ** (-np.arange(quarter, dtype=np.float64) / quarter)
    pos = np.arange(seq)
    ang_r = (pos // GRID_W)[:, None] * inv
    ang_c = (pos % GRID_W)[:, None] * inv
    cos = np.concatenate([np.cos(ang_r)] * 2 + [np.cos(ang_c)] * 2, axis=1)
    sin = np.concatenate([-np.sin(ang_r), np.sin(ang_r), -np.sin(ang_c), np.sin(ang_c)], axis=1)
    return jnp.asarray(cos, jnp.float32), jnp.asarray(sin, jnp.float32)


def ssd_mixer(p_lat, p_ctx, dt_lat, dt_ctx, conv_w, conv_b, dt_bias, a_log, d_skip, ctx_cols=FULL_COLS):
    bsz, seq, _ = p_lat.shape
    ctx_len = p_ctx.shape[1]
    t, g, hpg = SSD_CHUNK, SSD_GROUPS, SSD_HPG
    assert seq % t == 0 and ctx_len % t == 0
    nc = (seq + ctx_len) // t
    per_row = lambda v: jnp.broadcast_to(
        v.astype(jnp.float32).reshape(2, g, hpg).transpose(1, 0, 2).reshape(g, 2 * hpg, 1), (g, 2 * hpg, 128))
    dsk = jnp.broadcast_to(jnp.repeat(d_skip.astype(jnp.float32).reshape(g, hpg), SSD_HEAD_DIM, axis=1)[:, :, None],
                           (g, SSD_GW, 128))
    cos, sin = _rope_tables(seq)
    x0 = sum(MAIN_SPLITS[:6]) // SSD_GW
    b0 = (sum(MAIN_SPLITS[:6]) + SSD_WIDTH) // 128
    c0 = b0 + g
    cw = conv_w.astype(jnp.float32)
    cb = conv_b.astype(jnp.float32).reshape(1, SSD_CONV_CH)
    wb0 = SSD_WIDTH // 128
    ltot = seq + ctx_len
    f32, bf = jnp.float32, jnp.bfloat16
    y_lat, y_ctx = pl.pallas_call(
        partial(_ssd_kernel, seq=seq, ctx_len=ctx_len),
        grid=(bsz, g),
        in_specs=[pl.BlockSpec((1, seq, SSD_GW), lambda b, i: (b, 0, x0 + i)),
                  pl.BlockSpec((1, seq, 128), lambda b, i: (b, 0, b0 + i)),
                  pl.BlockSpec((1, seq, 128), lambda b, i: (b, 0, c0 + i)),
                  pl.BlockSpec((1, ctx_len, SSD_GW), lambda b, i: (b, 0, ctx_cols.xbc // SSD_GW + i)),
                  pl.BlockSpec((1, ctx_len, 128), lambda b, i: (b, 0, (ctx_cols.xbc + SSD_WIDTH) // 128 + i)),
                  pl.BlockSpec((1, ctx_len, 128), lambda b, i: (b, 0, (ctx_cols.xbc + SSD_WIDTH) // 128 + g + i)),
                  pl.BlockSpec((1, seq, DT_PAD), lambda b, i: (b, 0, 0)),
                  pl.BlockSpec((1, ctx_len, DT_PAD), lambda b, i: (b, 0, 0)),
                  pl.BlockSpec((1, 2 * hpg, 128), lambda b, i: (i, 0, 0)),
                  pl.BlockSpec((1, 2 * hpg, 128), lambda b, i: (i, 0, 0)),
                  pl.BlockSpec((1, SSD_GW, 128), lambda b, i: (i, 0, 0)),
                  pl.BlockSpec((SSD_CONV, SSD_GW), lambda b, i: (0, i)),
                  pl.BlockSpec((SSD_CONV, 128), lambda b, i: (0, wb0 + i)),
                  pl.BlockSpec((SSD_CONV, 128), lambda b, i: (0, wb0 + g + i)),
                  pl.BlockSpec((1, SSD_GW), lambda b, i: (0, i)),
                  pl.BlockSpec((1, 128), lambda b, i: (0, wb0 + i)),
                  pl.BlockSpec((1, 128), lambda b, i: (0, wb0 + g + i)),
                  pl.BlockSpec((seq, 128), lambda b, i: (0, 0)),
                  pl.BlockSpec((seq, 128), lambda b, i: (0, 0))],
        out_specs=[pl.BlockSpec((1, seq, SSD_GW), lambda b, i: (b, 0, i)),
                   pl.BlockSpec((1, ctx_len, SSD_GW), lambda b, i: (b, 0, i))],
        out_shape=[jax.ShapeDtypeStruct((bsz, seq, SSD_WIDTH), f32),
                   jax.ShapeDtypeStruct((bsz, ctx_len, SSD_WIDTH), f32)],
        scratch_shapes=[pltpu.VMEM((SSD_GW // 128, seq + 2 * CONV_PAD, 128), f32),
                        pltpu.VMEM((seq + 2 * CONV_PAD, 128), f32),
                        pltpu.VMEM((seq + 2 * CONV_PAD, 128), f32),
                        pltpu.VMEM((nc, SSD_GW, t), f32),
                        pltpu.VMEM((ltot, 128), bf),
                        pltpu.VMEM((ltot, 128), bf),
                        pltpu.VMEM((nc, DT_PAD, t), f32),
                        pltpu.VMEM((nc, 5 * 2 * hpg, t), f32),
                        pltpu.VMEM((ltot, 128), f32),
                        pltpu.VMEM((nc, 2 * hpg, 128), f32),
                        pltpu.VMEM((nc, SSD_GW, t), f32),
                        pltpu.VMEM((2, nc, SSD_GW, SSD_STATE), f32),
                        pltpu.VMEM((2, nc, SSD_GW, SSD_STATE), bf),
                        pltpu.VMEM((2, SSD_GW, SSD_STATE), f32)],
        compiler_params=pltpu.CompilerParams(dimension_semantics=("arbitrary", "arbitrary"),
                                             vmem_limit_bytes=V7X_VMEM_LIMIT_BYTES),
        name="ssd_mixer",
    )(p_lat, p_lat, p_lat, p_ctx, p_ctx, p_ctx, dt_lat, dt_ctx, per_row(dt_bias), per_row(a_log), dsk,
      cw, cw, cw, cb, cb, cb, cos, sin)
    return y_ctx, y_lat


INPROJ_TN = 2560
ADA_ROWS = 16


def _layer(xc, xl, c, c_ctx, w_ada_all, layer, b_ada, norm_w, w_main_all, w_dt_all, rpb, four_w, conv_w, conv_b, dt_bias, a_log,
           d_skip, ssd_norm_w, wb_na, wb_four, wb_ssd, w_out, update_ctx, final_norm_w):
    bsz, seq, d = xl.shape
    ctx_len = xc.shape[1]
    bf = jnp.bfloat16
    assert bsz < ADA_ROWS

    cc = jnp.concatenate([c, c_ctx[None], jnp.zeros((ADA_ROWS - bsz - 1, d), c.dtype)], axis=0)
    mod = matmul(jax.nn.silu(cc), w_ada_all, layer, tm=ADA_ROWS, tn=1024) + b_ada
    sh, sc, gt = jnp.split(mod, 3, axis=-1)
    sh_l, sc_l, g_l = sh[:bsz], sc[:bsz], gt[:bsz]
    sh_c, sc_c, g_c = sh[bsz:bsz + 1], sc[bsz:bsz + 1], gt[bsz]

    xl2d = xl.reshape(bsz * seq, d)
    xc2d = xc.reshape(bsz * ctx_len, d)
    pl2d, dt_l = in_projection(xl2d, norm_w, sc_l[:, None], sh_l[:, None], w_main_all, layer, w_dt_all, layer,
                               rows_per_mod=seq, tm=1024, tn=INPROJ_TN, out_dtype=bf)
    if update_ctx:
        ctx_cols, w_ctx, ctx_layer, ctx_tn = FULL_COLS, w_main_all, layer, INPROJ_TN
    else:
        ctx_cols, ctx_layer = KV_XBC_COLS, 0
        w_ctx = jnp.concatenate([w_main_all[layer, FULL_COLS.k:FULL_COLS.v + NA_WIDTH],
                                 w_main_all[layer, FULL_COLS.xbc:FULL_COLS.xbc + SSD_CONV_CH]], axis=0)[None]
        ctx_tn = w_ctx.shape[1] // 2
    pc2d, dt_c = in_projection(xc2d, norm_w, sc_c[:, None], sh_c[:, None], w_ctx, ctx_layer, w_dt_all, layer,
                               rows_per_mod=bsz * ctx_len, tm=1024, tn=ctx_tn, out_dtype=bf)
    p_l = pl2d.reshape(bsz, seq, MAIN_WIDTH)
    p_c = pc2d.reshape(bsz, ctx_len, w_ctx.shape[1])
    dt_l = dt_l.reshape(bsz, seq, DT_PAD)
    dt_c = dt_c.reshape(bsz, ctx_len, DT_PAD)

    four_wb, wb_nab, wb_fourb, wb_ssdb, w_outb = (t.astype(bf) for t in (four_w, wb_na, wb_four, wb_ssd, w_out))

    a_l, a_c = na_attention(p_l, p_c, na_bias_table(rpb), with_ctx=update_ctx, out_dtype=bf, ctx_cols=ctx_cols)
    gfo_l = fourier_branch(p_l, four_wb, tm=512)
    s_c, s_l = ssd_mixer(p_l, p_c, dt_l, dt_c, conv_w, conv_b, dt_bias, a_log, d_skip, ctx_cols=ctx_cols)
    m_l = merge_branches(pl2d, a_l.reshape(bsz * seq, NA_WIDTH), gfo_l.reshape(bsz * seq, FOUR_WIDTH),
                         s_l.reshape(bsz * seq, SSD_WIDTH), ssd_norm_w, wb_nab, wb_fourb, wb_ssdb, tm=512)
    if final_norm_w is None:
        xl_new = out_projection(m_l, w_outb, xl2d, g_l[:, None], rows_per_mod=seq, tm=1024, tn=1024)
    else:
        xl_new = out_projection(m_l, w_outb, xl2d, g_l[:, None], rows_per_mod=seq, tm=512, tn=d,
                                final_norm_w=final_norm_w)
    xl_new = xl_new.reshape(bsz, seq, d)
    if update_ctx:
        gfo_c = fourier_branch(p_c, four_wb, tm=ctx_len)
        m_c = merge_branches(pc2d, a_c.reshape(bsz * ctx_len, NA_WIDTH), gfo_c.reshape(bsz * ctx_len, FOUR_WIDTH),
                             s_c.reshape(bsz * ctx_len, SSD_WIDTH), ssd_norm_w, wb_nab, wb_fourb, wb_ssdb, tm=512)
        xc = out_projection(m_c, w_outb, xc2d, g_c[None, None], rows_per_mod=bsz * ctx_len, tm=1024, tn=1024)
        xc = xc.reshape(bsz, ctx_len, d)
    return xc, xl_new


def kernel(x, c, ctx, c_ctx, w_ada, b_ada, norm_w, w_in, na_rpb, four_w, ssd_conv_w, ssd_conv_b, ssd_dt_bias,
           ssd_a_log, ssd_d, ssd_norm_w, wb_na, wb_four, wb_ssd, w_out, final_norm_w):
    xc, xl = ctx, x
    w_main_all, w_dt_all = repack_w_in(w_in)
    for l in range(DEPTH):
        xc, xl = _layer(xc, xl, c, c_ctx, w_ada, l, b_ada[l], norm_w[l], w_main_all, w_dt_all, na_rpb[l], four_w[l],
                        ssd_conv_w[l], ssd_conv_b[l], ssd_dt_bias[l], ssd_a_log[l], ssd_d[l], ssd_norm_w[l],
                        wb_na[l], wb_four[l], wb_ssd[l], w_out[l], update_ctx=(l < DEPTH - 1),
                        final_norm_w=final_norm_w if l == DEPTH - 1 else None)
    return xl
```

```python
import math
from functools import partial
from typing import NamedTuple

import jax
import jax.numpy as jnp
import numpy as np
from jax import lax
from jax.experimental import pallas as pl
from jax.experimental.pallas import tpu as pltpu

D_MODEL = 2048
DEPTH = 2
GRID_W = 64
EPS = 1e-6
NA_HEADS = 8
NA_HEAD_DIM = 128
NA_WIDTH = NA_HEADS * NA_HEAD_DIM
NA_WIN_R = 8
NA_WIN_C = 16
LOG2E = math.log2(math.e)
FOUR_GROUPS = 4
FOUR_GROUP_DIM = 256
FOUR_WIDTH = FOUR_GROUPS * FOUR_GROUP_DIM
SSD_HEADS = 16
SSD_HEAD_DIM = 64
SSD_WIDTH = SSD_HEADS * SSD_HEAD_DIM
SSD_GROUPS = 4
SSD_HPG = SSD_HEADS // SSD_GROUPS
SSD_STATE = 128
SSD_CONV = 7
SSD_CHUNK = 128
SSD_CONV_CH = SSD_WIDTH + 2 * SSD_GROUPS * SSD_STATE
ROPE_BASE = 10000.0
DT_WIDTH = 2 * SSD_HEADS
MAIN_SPLITS = (NA_WIDTH, NA_WIDTH, NA_WIDTH, NA_WIDTH, FOUR_WIDTH, FOUR_WIDTH, SSD_CONV_CH, SSD_WIDTH,
               D_MODEL, D_MODEL, D_MODEL)
MAIN_WIDTH = sum(MAIN_SPLITS)
DT_OFFSET = sum(MAIN_SPLITS[:8])
DT_PAD = 128


class ProjCols(NamedTuple):
    k: int
    v: int
    xbc: int


FULL_COLS = ProjCols(k=NA_WIDTH, v=2 * NA_WIDTH, xbc=sum(MAIN_SPLITS[:6]))
KV_XBC_COLS = ProjCols(k=0, v=NA_WIDTH, xbc=2 * NA_WIDTH)

LANES = 128
V7X_VMEM_LIMIT_BYTES = 56 * 1024 * 1024


def _mm_kernel(a_ref, b_ref, o_ref):
    a = a_ref[...].astype(jnp.bfloat16)
    b = b_ref[...].astype(jnp.bfloat16)
    o_ref[...] = jnp.dot(a, b, preferred_element_type=jnp.float32).astype(o_ref.dtype)


def matmul(a, b, layer, *, tm, tn, out_dtype=jnp.float32):
    m, k = a.shape
    _, _, n = b.shape
    assert m % tm == 0 and n % tn == 0, (a.shape, b.shape, tm, tn)
    return pl.pallas_call(
        _mm_kernel,
        grid=(m // tm, n // tn),
        in_specs=[pl.BlockSpec((tm, k), lambda i, j: (i, 0)),
                  pl.BlockSpec((None, k, tn), lambda i, j: (layer, 0, j))],
        out_specs=pl.BlockSpec((tm, tn), lambda i, j: (i, j)),
        out_shape=jax.ShapeDtypeStruct((m, n), out_dtype),
        compiler_params=pltpu.CompilerParams(dimension_semantics=("arbitrary", "arbitrary"),
                                             vmem_limit_bytes=V7X_VMEM_LIMIT_BYTES),
        name="matmul",
    )(a, b)


REPACK_TN = 1024


def _repack_kernel(a_ref, nxt_ref, o_ref, odt_ref):
    j = pl.program_id(1)
    first_below = DT_OFFSET // REPACK_TN

    @pl.when(j < first_below)
    def _():
        o_ref[...] = a_ref[...].astype(o_ref.dtype)

    @pl.when(j >= first_below)
    def _():
        o_ref[...] = jnp.concatenate([a_ref[DT_WIDTH:, :], nxt_ref[...]], axis=0).astype(o_ref.dtype)

    @pl.when(j == first_below)
    def _():
        odt_ref[...] = a_ref[:DT_PAD, :].astype(odt_ref.dtype)


def repack_w_in(w_in):
    depth, d, _ = w_in.shape
    tn = REPACK_TN
    w_t = jnp.swapaxes(w_in, 1, 2)
    return pl.pallas_call(
        _repack_kernel,
        grid=(depth, MAIN_WIDTH // tn),
        in_specs=[pl.BlockSpec((None, tn, d), lambda l, j: (l, j, 0)),
                  pl.BlockSpec((None, DT_WIDTH, d), lambda l, j: (l, (j + 1) * (tn // DT_WIDTH), 0))],
        out_specs=[pl.BlockSpec((None, tn, d), lambda l, j: (l, j, 0)),
                   pl.BlockSpec((None, DT_PAD, d), lambda l, j: (l, 0, 0))],
        out_shape=[jax.ShapeDtypeStruct((depth, MAIN_WIDTH, d), jnp.bfloat16),
                   jax.ShapeDtypeStruct((depth, DT_PAD, d), jnp.bfloat16)],
        compiler_params=pltpu.CompilerParams(dimension_semantics=("arbitrary", "arbitrary"),
                                             vmem_limit_bytes=V7X_VMEM_LIMIT_BYTES),
        name="repack_w_in",
    )(w_t, w_t)


def _inproj_kernel(x_ref, nw_ref, sc_ref, sh_ref, w_ref, wdt_ref, o_ref, dt_ref, h_ref):
    @pl.when(pl.program_id(1) == 0)
    def _():
        x = x_ref[...]
        y = x * lax.rsqrt(jnp.mean(x * x, axis=-1, keepdims=True) + EPS) * nw_ref[...]
        h = (y * (1.0 + sc_ref[0]) + sh_ref[0]).astype(jnp.bfloat16)
        h_ref[...] = h
        dt_ref[...] = lax.dot_general(h, wdt_ref[...], _NT, preferred_element_type=jnp.float32)

    o_ref[...] = lax.dot_general(h_ref[...], w_ref[...], _NT, preferred_element_type=jnp.float32).astype(o_ref.dtype)


def in_projection(x2d, norm_w, scale, shift, w_main_t, layer, w_dt_t, dt_layer, *, rows_per_mod, tm, tn, out_dtype):
    m, d = x2d.shape
    n = w_main_t.shape[1]
    assert m % tm == 0 and n % tn == 0 and rows_per_mod % tm == 0
    per = rows_per_mod // tm
    return pl.pallas_call(
        _inproj_kernel,
        grid=(m // tm, n // tn),
        in_specs=[pl.BlockSpec((tm, d), lambda i, j: (i, 0)),
                  pl.BlockSpec((1, d), lambda i, j: (0, 0)),
                  pl.BlockSpec((1, 1, d), lambda i, j: (i // per, 0, 0)),
                  pl.BlockSpec((1, 1, d), lambda i, j: (i // per, 0, 0)),
                  pl.BlockSpec((None, tn, d), lambda i, j: (layer, j, 0)),
                  pl.BlockSpec((None, DT_PAD, d), lambda i, j: (dt_layer, 0, 0))],
        out_specs=[pl.BlockSpec((tm, tn), lambda i, j: (i, j)),
                   pl.BlockSpec((tm, DT_PAD), lambda i, j: (i, 0))],
        out_shape=[jax.ShapeDtypeStruct((m, n), out_dtype),
                   jax.ShapeDtypeStruct((m, DT_PAD), jnp.float32)],
        scratch_shapes=[pltpu.VMEM((tm, d), jnp.bfloat16)],
        compiler_params=pltpu.CompilerParams(dimension_semantics=("arbitrary", "arbitrary"),
                                             vmem_limit_bytes=V7X_VMEM_LIMIT_BYTES),
        name="in_projection",
    )(x2d, norm_w.reshape(1, d), scale, shift, w_main_t, w_dt_t)


GRID_ROWS = 32
NA_WIN_TOKENS = NA_WIN_R * GRID_W
_NT = (((1,), (1,)), ((), ()))
NA_ROW_UNROLL = 16


def _na_kernel(*refs, with_ctx):
    if with_ctx:
        q_ref, k_ref, v_ref, kc_ref, vc_ref, bias_ref, qc_ref, o_ref, oc_ref = refs
    else:
        q_ref, k_ref, v_ref, kc_ref, vc_ref, bias_ref, o_ref = refs
    bf = jnp.bfloat16
    scale = NA_HEAD_DIM ** -0.5
    scale2 = scale * LOG2E
    kc = kc_ref[0].astype(bf)
    vc = vc_ref[0].astype(bf)
    nq = NA_ROW_UNROLL * GRID_W

    def lane_tiles(x, op):
        acc = x[:, 0:LANES]
        for l0 in range(LANES, x.shape[1], LANES):
            acc = op(acc, x[:, l0:l0 + LANES])
        return acc

    def rows(rb, carry):
        qb0 = pl.multiple_of(rb * nq, nq)
        q_all = q_ref[0, pl.ds(qb0, nq), :].astype(bf)
        s_c_all = lax.dot_general(q_all, kc, _NT, preferred_element_type=jnp.float32) * scale2
        k0s, scores, probs = [], [], []
        for i in range(NA_ROW_UNROLL):
            r = rb * NA_ROW_UNROLL + i
            start = jnp.clip(r - NA_WIN_R // 2, 0, GRID_ROWS - NA_WIN_R)
            k0 = pl.multiple_of(start * GRID_W, GRID_W)
            q = q_all[i * GRID_W:(i + 1) * GRID_W]
            kw = k_ref[0, pl.ds(k0, NA_WIN_TOKENS), :].astype(bf)
            s_w = lax.dot_general(q, kw, _NT, preferred_element_type=jnp.float32) * scale2 + bias_ref[0, r - start]
            k0s.append(k0)
            scores.append((s_w, s_c_all[i * GRID_W:(i + 1) * GRID_W]))
        for s_w, s_c in scores:
            m = jnp.max(jnp.maximum(lane_tiles(s_w, jnp.maximum), lane_tiles(s_c, jnp.maximum)),
                        axis=-1, keepdims=True)
            e_w = jnp.exp2(s_w - m)
            e_c = jnp.exp2(s_c - m)
            denom = jnp.sum(lane_tiles(e_w, jnp.add) + lane_tiles(e_c, jnp.add), axis=-1, keepdims=True)
            probs.append((e_w.astype(bf), e_c.astype(bf), denom))
        o_c_all = jnp.dot(jnp.concatenate([p[1] for p in probs], axis=0), vc, preferred_element_type=jnp.float32)
        for i, (k0, (e_w, _, denom)) in enumerate(zip(k0s, probs)):
            vw = v_ref[0, pl.ds(k0, NA_WIN_TOKENS), :].astype(bf)
            o = jnp.dot(e_w, vw, preferred_element_type=jnp.float32) + o_c_all[i * GRID_W:(i + 1) * GRID_W]
            o_ref[0, pl.ds(qb0 + i * GRID_W, GRID_W), :] = (o / denom).astype(o_ref.dtype)
        return carry

    lax.fori_loop(0, GRID_ROWS // NA_ROW_UNROLL, rows, 0)

    if with_ctx:
        s = lax.dot_general(qc_ref[0].astype(bf), kc, _NT, preferred_element_type=jnp.float32) * scale
        e = jnp.exp(s - jnp.max(s, axis=-1, keepdims=True))
        o = jnp.dot(e.astype(bf), vc, preferred_element_type=jnp.float32)
        oc_ref[0] = (o / jnp.sum(e, axis=-1, keepdims=True)).astype(oc_ref.dtype)


def na_bias_table(rpb):
    qcol = np.arange(GRID_W)[:, None]
    kcol = np.arange(GRID_W)[None, :]
    qstart = np.clip(qcol - NA_WIN_C // 2, 0, GRID_W - NA_WIN_C)
    mask = (kcol >= qstart) & (kcol < qstart + NA_WIN_C)
    dc = np.clip(kcol - qcol + NA_WIN_C - 1, 0, 2 * NA_WIN_C - 2)
    pick_col = (np.arange(2 * NA_WIN_C - 1)[:, None, None] == dc[None]).astype(np.float32)
    rows = jnp.stack([rpb.astype(jnp.float32)[:, NA_WIN_R - 1 - dl:2 * NA_WIN_R - 1 - dl] for dl in range(NA_WIN_R)],
                     axis=1)
    t = jnp.einsum("hdwc,cqk->hdqwk", rows, pick_col, precision=lax.Precision.HIGHEST)
    t = jnp.where(mask[None, None, :, None, :], t * LOG2E, -1e30)
    return t.reshape(rpb.shape[0], NA_WIN_R, GRID_W, NA_WIN_TOKENS)


def na_attention(p_lat, p_ctx, bias, *, with_ctx, out_dtype, ctx_cols=FULL_COLS):
    bsz, seq, _ = p_lat.shape
    ctx_len = p_ctx.shape[1]
    assert seq == GRID_ROWS * GRID_W
    d, h = NA_HEAD_DIM, NA_HEADS
    in_specs = [pl.BlockSpec((1, seq, d), lambda b, i: (b, 0, i)),
                pl.BlockSpec((1, seq, d), lambda b, i: (b, 0, h + i)),
                pl.BlockSpec((1, seq, d), lambda b, i: (b, 0, 2 * h + i)),
                pl.BlockSpec((1, ctx_len, d), lambda b, i: (b, 0, ctx_cols.k // d + i)),
                pl.BlockSpec((1, ctx_len, d), lambda b, i: (b, 0, ctx_cols.v // d + i)),
                pl.BlockSpec((1, NA_WIN_R, GRID_W, NA_WIN_TOKENS), lambda b, i: (i, 0, 0, 0))]
    args = [p_lat, p_lat, p_lat, p_ctx, p_ctx, bias]
    out_specs = [pl.BlockSpec((1, seq, d), lambda b, i: (b, 0, i))]
    out_shape = [jax.ShapeDtypeStruct((bsz, seq, h * d), out_dtype)]
    if with_ctx:
        in_specs.append(pl.BlockSpec((1, ctx_len, d), lambda b, i: (b, 0, i)))
        args.append(p_ctx)
        out_specs.append(pl.BlockSpec((1, ctx_len, d), lambda b, i: (b, 0, i)))
        out_shape.append(jax.ShapeDtypeStruct((bsz, ctx_len, h * d), out_dtype))
    outs = pl.pallas_call(
        partial(_na_kernel, with_ctx=with_ctx),
        grid=(bsz, h),
        in_specs=in_specs, out_specs=out_specs, out_shape=out_shape,
        compiler_params=pltpu.CompilerParams(dimension_semantics=("arbitrary", "arbitrary"),
                                             vmem_limit_bytes=V7X_VMEM_LIMIT_BYTES),
        name="na_attention",
    )(*args)
    return (outs[0], outs[1]) if with_ctx else (outs[0], None)


def _dft_cos_sin(n):
    jk = np.outer(np.arange(n), np.arange(n)) % n
    ang = 2.0 * np.pi * jk / n
    return np.cos(ang) / np.sqrt(n), np.sin(ang) / np.sqrt(n)


FOUR_EXT = 8


def _fourier_kernel(u_ref, cc_ref, sc_ref, a_ref, flip_ref, fw_ref, zlo_ref, zhi_ref, o_ref, *, tm):
    bf = jnp.bfloat16
    gd, ext = FOUR_GROUP_DIM, tm + FOUR_EXT
    pos = jnp.dot(a_ref[...], u_ref[0].astype(bf), preferred_element_type=jnp.float32).astype(bf)
    even, odd = [], []
    for g in range(FOUR_GROUPS):
        cols = slice(g * gd, (g + 1) * gd)
        even.append(jnp.dot(pos[:ext, cols], cc_ref[...], preferred_element_type=jnp.float32))
        odd.append(jnp.dot(pos[ext:, cols], sc_ref[...], preferred_element_type=jnp.float32))
    even = jnp.concatenate(even, axis=1)
    odd = jnp.concatenate(odd, axis=1)
    y_lo = (even[:tm] - odd[:tm]).astype(bf)
    y_hi = jnp.dot(flip_ref[...], (even + odd).astype(bf), preferred_element_type=jnp.float32).astype(bf)
    for half, (y, z_ref) in enumerate(((y_lo, zlo_ref), (y_hi, zhi_ref))):
        f = jnp.dot(y, fw_ref[...], preferred_element_type=jnp.float32)
        zf = z_ref[0].astype(jnp.float32)
        o_ref[0, half] = (f * (zf * jax.nn.sigmoid(zf))).astype(o_ref.dtype)


def fourier_branch(p, four_w_bf, layer, *, tm):
    bsz, seq, _ = p.shape
    half = seq // 2
    assert half % tm == 0
    nt2, ext = half // tm, tm + FOUR_EXT
    cc, sc = _dft_cos_sin(FOUR_GROUP_DIM)
    cl, sl = _dft_cos_sin(seq)
    a = np.stack([np.concatenate([cl[i * tm:i * tm + ext], sl[i * tm:i * tm + ext]], axis=0) for i in range(nt2)])
    flip = np.zeros((tm, ext))
    flip[np.arange(tm), tm - np.arange(tm)] = 1.0
    fw, gd = FOUR_WIDTH, FOUR_GROUP_DIM
    bf = jnp.bfloat16
    out = pl.pallas_call(
        partial(_fourier_kernel, tm=tm),
        grid=(bsz, nt2),
        in_specs=[pl.BlockSpec((1, seq, fw), lambda b, i: (b, 0, 4)),
                  pl.BlockSpec((gd, gd), lambda b, i: (0, 0)),
                  pl.BlockSpec((gd, gd), lambda b, i: (0, 0)),
                  pl.BlockSpec((None, 2 * ext, seq), lambda b, i: (i, 0, 0)),
                  pl.BlockSpec((tm, ext), lambda b, i: (0, 0)),
                  pl.BlockSpec((None, fw, fw), lambda b, i: (layer, 0, 0)),
                  pl.BlockSpec((1, tm, fw), lambda b, i: (b, i, 5)),
                  pl.BlockSpec((1, tm, fw), lambda b, i: (b, 2 * nt2 - 1 - i, 5))],
        out_specs=pl.BlockSpec((1, 2, tm, fw), lambda b, i: (b, 0, i, 0)),
        out_shape=jax.ShapeDtypeStruct((bsz, 2, half, fw), bf),
        compiler_params=pltpu.CompilerParams(dimension_semantics=("arbitrary", "arbitrary"),
                                             vmem_limit_bytes=V7X_VMEM_LIMIT_BYTES),
        name="fourier_branch",
    )(p, jnp.asarray(cc, bf), jnp.asarray(sc, bf), jnp.asarray(a, bf), jnp.asarray(flip, bf), four_w_bf, p, p)
    return out.reshape(bsz, seq, fw), nt2


def fourier_row_block(t, nt2):
    return jnp.where(t < nt2, t, 3 * nt2 - 1 - t)


MERGE_TN = 1024


def _merge_kernel(a_ref, zna_ref, gfo_ref, s_ref, zs_ref, nw_ref, wa_ref, wf_ref, ws_ref, *rest):
    n_half = D_MODEL // MERGE_TN
    gate_refs, (o_ref, ap_ref, sp_ref) = rest[:3 * n_half], rest[3 * n_half:]
    bf = jnp.bfloat16
    zna = zna_ref[...].astype(jnp.float32)
    ap_ref[...] = (a_ref[...].astype(jnp.float32) * (zna * jax.nn.sigmoid(zna))).astype(bf)
    zs = zs_ref[...].astype(jnp.float32)
    t = s_ref[...].astype(jnp.float32) * (zs * jax.nn.sigmoid(zs))
    t = t * lax.rsqrt(jnp.mean(t * t, axis=-1, keepdims=True) + EPS) * nw_ref[...]
    sp_ref[...] = t.astype(bf)
    for h in range(n_half):
        cols = slice(h * MERGE_TN, (h + 1) * MERGE_TN)
        ga_ref, gf_ref, gs_ref = gate_refs[h], gate_refs[n_half + h], gate_refs[2 * n_half + h]
        o_a = jnp.dot(ap_ref[...], wa_ref[:, cols], preferred_element_type=jnp.float32)
        o_f = jnp.dot(gfo_ref[...], wf_ref[:, cols], preferred_element_type=jnp.float32)
        o_s = jnp.dot(sp_ref[...], ws_ref[:, cols], preferred_element_type=jnp.float32)
        m = (jax.nn.sigmoid(ga_ref[...].astype(jnp.float32)) * o_a
             + jax.nn.sigmoid(gf_ref[...].astype(jnp.float32)) * o_f
             + jax.nn.sigmoid(gs_ref[...].astype(jnp.float32)) * o_s)
        o_ref[:, cols] = m.astype(o_ref.dtype)


def merge_branches(p2d, a2d, gfo2d, gfo_nt2, s2d, ssd_norm_w, wa, wf, ws, layer, *, tm):
    m = p2d.shape[0]
    w, tn = NA_WIDTH, MERGE_TN
    n_half = D_MODEL // tn
    assert m % tm == 0
    row = lambda i: (i, 0)
    if gfo_nt2 is None:
        gfo_row = row
    else:
        assert p2d.shape[0] % (2 * gfo_nt2 * tm) == 0
        per_seq = 2 * gfo_nt2
        gfo_row = lambda i: ((i // per_seq) * per_seq + fourier_row_block(i % per_seq, gfo_nt2), 0)
    resident = lambda shape, idx: pl.BlockSpec(shape, lambda i: idx, pipeline_mode=pl.Buffered(1))
    w_spec = resident((None, w, D_MODEL), (layer, 0, 0))
    gate0 = DT_OFFSET // tn
    gate_specs = [pl.BlockSpec((tm, tn), partial(lambda i, blk: (i, blk), blk=gate0 + g * n_half + h))
                  for g in range(3) for h in range(n_half)]
    return pl.pallas_call(
        _merge_kernel,
        grid=(m // tm,),
        in_specs=[pl.BlockSpec((tm, w), row),
                  pl.BlockSpec((tm, w), lambda i: (i, 3)),
                  pl.BlockSpec((tm, w), gfo_row),
                  pl.BlockSpec((tm, w), row),
                  pl.BlockSpec((tm, w), lambda i: (i, 8)),
                  resident((1, w), (0, 0)), w_spec, w_spec, w_spec] + gate_specs,
        out_specs=pl.BlockSpec((tm, D_MODEL), row),
        out_shape=jax.ShapeDtypeStruct((m, D_MODEL), jnp.bfloat16),
        scratch_shapes=[pltpu.VMEM((tm, w), jnp.bfloat16), pltpu.VMEM((tm, w), jnp.bfloat16)],
        compiler_params=pltpu.CompilerParams(dimension_semantics=("arbitrary",),
                                             vmem_limit_bytes=V7X_VMEM_LIMIT_BYTES),
        name="merge_branches",
    )(a2d, p2d, gfo2d, s2d, p2d, ssd_norm_w.reshape(1, w), wa, wf, ws, *([p2d] * (3 * n_half)))


def _outproj_kernel(m_ref, w_ref, x_ref, g_ref, *rest, final_norm):
    y = jnp.dot(m_ref[...], w_ref[...], preferred_element_type=jnp.float32)
    r = x_ref[...] + g_ref[0] * y
    if final_norm:
        nw_ref, o_ref = rest
        o_ref[...] = r * lax.rsqrt(jnp.mean(r * r, axis=-1, keepdims=True) + EPS) * nw_ref[...]
    else:
        rest[0][...] = r


def out_projection(m2d, w_out_bf, layer, x2d, gate, *, rows_per_mod, tm, tn, final_norm_w=None):
    m, d = x2d.shape
    assert m % tm == 0 and d % tn == 0 and rows_per_mod % tm == 0
    per = rows_per_mod // tm
    in_specs = [pl.BlockSpec((tm, d), lambda i, j: (i, 0)),
                pl.BlockSpec((None, d, tn), lambda i, j: (layer, 0, j)),
                pl.BlockSpec((tm, tn), lambda i, j: (i, j)),
                pl.BlockSpec((1, 1, tn), lambda i, j: (i // per, 0, j))]
    args = [m2d, w_out_bf, x2d, gate]
    if final_norm_w is not None:
        assert tn == d
        in_specs.append(pl.BlockSpec((1, d), lambda i, j: (0, 0)))
        args.append(final_norm_w.astype(jnp.float32).reshape(1, d))
    return pl.pallas_call(
        partial(_outproj_kernel, final_norm=final_norm_w is not None),
        grid=(m // tm, d // tn),
        in_specs=in_specs,
        out_specs=pl.BlockSpec((tm, tn), lambda i, j: (i, j)),
        out_shape=jax.ShapeDtypeStruct((m, d), jnp.float32),
        compiler_params=pltpu.CompilerParams(dimension_semantics=("arbitrary", "arbitrary"),
                                             vmem_limit_bytes=V7X_VMEM_LIMIT_BYTES),
        name="out_projection",
    )(*args)


SSD_GW = SSD_HPG * SSD_HEAD_DIM
CONV_PAD = 8
SSD_CHUNK_UNROLL = 2
SSD_C_UNROLL = 4
SSD_A_UNROLL = 3


def _split3_bf16(a):
    rnd = lambda v: v.astype(jnp.bfloat16).astype(jnp.float32)
    a1 = rnd(a)
    a2 = rnd(a - a1)
    a3 = rnd((a - a1) - a2)
    return a1, a2, a3


def _head_rows(rows, r0, n):
    return jnp.concatenate([jnp.broadcast_to(rows[r0 + k:r0 + k + 1, :], (SSD_HEAD_DIM, n))
                            for k in range(SSD_HPG)], axis=0)


def _ssd_kernel(xl_ref, bl_ref, cl_ref, xc_ref, bc_ref, cc_ref, dtl_ref, dtc_ref, bias_ref, alog_ref, dsk_ref,
                wx_ref, wb_ref, wc_ref, bx_ref, bb_ref, bcb_ref, cos_ref, sin_ref,
                yl_ref, yc_ref,
                padx, padb, padc, xt_s, b_s, c_s, dtt_s, row_s, col_s, dec_s, yt_s, p_s, sp_s, st_s, *, seq, ctx_len):
    f32, bf = jnp.float32, jnp.bfloat16
    t = SSD_CHUNK
    nc_c, nc_l = ctx_len // t, seq // t
    nc = nc_c + nc_l

    lane = lax.broadcasted_iota(jnp.int32, (t, LANES), 1)
    first_half = (lane % 64) < 32

    def conv_tile(pad, w_ref, b_ref, t0, l0):
        acc = jnp.broadcast_to(b_ref[:, l0:l0 + LANES], (t, LANES))
        for j in range(SSD_CONV):
            off = CONV_PAD - SSD_CONV // 2 + j
            acc = acc + pad[pl.ds(t0 + off, t), :] * w_ref[j:j + 1, l0:l0 + LANES]
        return acc * jax.nn.sigmoid(acc)

    def rope(u, t0):
        swapped = jnp.where(first_half, pltpu.roll(u, 96, axis=1), pltpu.roll(u, 32, axis=1))
        return u * cos_ref[pl.ds(t0, t), :] + swapped * sin_ref[pl.ds(t0, t), :]

    def conv_seq(x_ref, bm_ref, cm_ref, n, base, use_rope):
        slabs = ([(padx.at[h], x_ref, h * LANES) for h in range(SSD_GW // LANES)]
                 + [(padb, bm_ref, 0), (padc, cm_ref, 0)])
        for pad, src, l0 in slabs:
            pad[0:CONV_PAD, :] = jnp.zeros((CONV_PAD, LANES), f32)
            pad[CONV_PAD + n:2 * CONV_PAD + n, :] = jnp.zeros((CONV_PAD, LANES), f32)
            pad[CONV_PAD:CONV_PAD + n, :] = src[0, :, l0:l0 + LANES].astype(f32)

        def chunks(i, carry):
            for j in range(SSD_CHUNK_UNROLL):
                ci = i * SSD_CHUNK_UNROLL + j
                t0 = pl.multiple_of(ci * t, t)
                r0 = pl.multiple_of(base + ci * t, t)
                for h in range(SSD_GW // LANES):
                    l0 = h * LANES
                    xt_s[base // t + ci, l0:l0 + LANES, :] = conv_tile(padx.at[h], wx_ref, bx_ref, t0, l0).T
                ub = conv_tile(padb, wb_ref, bb_ref, t0, 0)
                uc = conv_tile(padc, wc_ref, bcb_ref, t0, 0)
                if use_rope:
                    ub, uc = rope(ub, t0), rope(uc, t0)
                b_s[pl.ds(r0, t), :] = ub.astype(bf)
                c_s[pl.ds(r0, t), :] = uc.astype(bf)
            return carry

        lax.fori_loop(0, n // t // SSD_CHUNK_UNROLL, chunks, 0)

    conv_seq(xc_ref, bc_ref, cc_ref, ctx_len, 0, False)
    conv_seq(xl_ref, bl_ref, cl_ref, seq, ctx_len, True)

    i0 = lax.broadcasted_iota(jnp.int32, (t, t), 0)
    i1 = lax.broadcasted_iota(jnp.int32, (t, t), 1)
    tri_f = (i0 <= i1).astype(bf)
    tri_r = (i0 >= i1).astype(bf)
    nr = 2 * SSD_HPG
    fwd_row = (lax.broadcasted_iota(jnp.int32, (nc * nr, 1), 0) % nr) < SSD_HPG
    g4 = pl.program_id(1) * SSD_HPG
    raw = []
    for c in range(nc):
        src, c0 = (dtc_ref, c) if c < nc_c else (dtl_ref, c - nc_c)
        dtt_s[c] = src[0, c0 * t:(c0 + 1) * t, :].T
        raw += [dtt_s[c, pl.ds(g4, SSD_HPG), :], dtt_s[c, pl.ds(SSD_HEADS + g4, SSD_HPG), :]]
    x = jnp.concatenate(raw, axis=0) + jnp.tile(bias_ref[0], (nc, 1))
    dt = jnp.maximum(x, 0.0) + jnp.log1p(jnp.exp(-jnp.abs(x)))
    pieces = _split3_bf16(dt * jnp.tile(-jnp.exp(alog_ref[0]), (nc, 1)))
    pf = [jnp.dot(p.astype(bf), tri_f, preferred_element_type=f32) for p in pieces]
    pr = [jnp.dot(p.astype(bf), tri_r, preferred_element_type=f32) for p in pieces]
    cs = jnp.where(fwd_row, (pf[0] + pf[1]) + pf[2], (pr[0] + pr[1]) + pr[2])
    tot = jnp.where(fwd_row, cs[:, t - 1:t], cs[:, 0:1])
    e_in = jnp.exp(cs)
    e_out = dt * jnp.exp(tot - cs)
    u = cs - jnp.log(dt)
    decb = jnp.broadcast_to(jnp.exp(tot), (nc * nr, LANES))
    for c in range(nc):
        sl = slice(c * nr, (c + 1) * nr)
        tile = jnp.concatenate([cs[sl], dt[sl], e_in[sl], e_out[sl], u[sl], jnp.zeros((t - 5 * nr, t), f32)], axis=0)
        row_s[c] = tile[0:5 * nr]
        col_s[c * t:(c + 1) * t, :] = tile.T
        dec_s[c] = decb[sl]

    lower = i1 <= i0
    diag = i1 == i0

    def chunks_a(i, carry):
        ids = [i * SSD_A_UNROLL + j for j in range(SSD_A_UNROLL)]
        r0s = [pl.multiple_of(c * t, t) for c in ids]
        cms = [c_s[pl.ds(r0, t), :] for r0 in r0s]
        bms = [b_s[pl.ds(r0, t), :] for r0 in r0s]
        gs = [lax.dot_general(cm, bm, _NT, preferred_element_type=f32) for cm, bm in zip(cms, bms)]
        for c, r0, bm, g in zip(ids, r0s, bms, gs):
            rows = row_s[c]
            cols = col_s[pl.ds(r0, t), :]
            xt = xt_s[c]
            xt_bf = xt.astype(bf)
            parts = []
            for k in range(SSD_HPG):
                kr = SSD_HPG + k
                w = jnp.exp(jnp.where(lower, cols[:, k:k + 1] - rows[32 + k:33 + k, :],
                                      cols[:, kr:kr + 1] - rows[32 + kr:33 + kr, :]))
                mk = (g * (w + jnp.where(diag, rows[8 + kr:9 + kr, :], 0.0))).astype(bf)
                xk = xt_bf[k * SSD_HEAD_DIM:(k + 1) * SSD_HEAD_DIM, :]
                parts.append(lax.dot_general(xk, mk, _NT, preferred_element_type=f32))
            yt_s[c] = dsk_ref[0] * xt + jnp.concatenate(parts, axis=0)
            for d in range(2):
                wt = (xt * _head_rows(rows, 24 + SSD_HPG * d, t)).astype(bf)
                p_s[d, c] = jnp.dot(wt, bm, preferred_element_type=f32)
        return carry

    lax.fori_loop(0, nc // SSD_A_UNROLL, chunks_a, 0)

    st_s[...] = jnp.zeros(st_s.shape, f32)

    def state_step(c, d):
        st = st_s[d]
        sp_s[d, c] = st.astype(bf)
        st_s[d] = st * _head_rows(dec_s[c], SSD_HPG * d, SSD_STATE) + p_s[d, c]

    def ctx_step(i, carry):
        state_step(i, 0)
        state_step(nc_c - 1 - i, 1)
        return carry

    def lat_step(i, carry):
        state_step(nc_c + i, 0)
        state_step(nc - 1 - i, 1)
        return carry

    lax.fori_loop(0, nc_c, ctx_step, 0)
    lax.fori_loop(0, nc_l, lat_step, 0)

    def chunks_c(first, out_ref, o_first, count):
        ids = [first + u for u in range(count)]
        cms = [c_s[pl.ds(pl.multiple_of(c * t, t), t), :] for c in ids]
        offs = [[lax.dot_general(sp_s[d, c], cm, _NT, preferred_element_type=f32) for d in range(2)]
                for c, cm in zip(ids, cms)]
        for u, (c, off) in enumerate(zip(ids, offs)):
            rows = row_s[c]
            yt = yt_s[c]
            for d in range(2):
                yt = yt + off[d] * _head_rows(rows, 16 + SSD_HPG * d, t)
            for l0 in range(0, SSD_GW, LANES):
                out_ref[0, pl.ds(pl.multiple_of((o_first + u) * t, t), t), l0:l0 + LANES] = yt[l0:l0 + LANES, :].T

    def ctx_out(i, carry):
        chunks_c(i * SSD_CHUNK_UNROLL, yc_ref, i * SSD_CHUNK_UNROLL, SSD_CHUNK_UNROLL)
        return carry

    def lat_out(i, carry):
        chunks_c(nc_c + i * SSD_C_UNROLL, yl_ref, i * SSD_C_UNROLL, SSD_C_UNROLL)
        return carry

    lax.fori_loop(0, nc_c // SSD_CHUNK_UNROLL, ctx_out, 0)
    lax.fori_loop(0, nc_l // SSD_C_UNROLL, lat_out, 0)


def _rope_tables(seq):
    quarter = SSD_STATE // 4
    inv = ROPE_BASE ** (-np.arange(quarter, dtype=np.float64) / quarter)
    pos = np.arange(seq)
    ang_r = (pos // GRID_W)[:, None] * inv
    ang_c = (pos % GRID_W)[:, None] * inv
    cos = np.concatenate([np.cos(ang_r)] * 2 + [np.cos(ang_c)] * 2, axis=1)
    sin = np.concatenate([-np.sin(ang_r), np.sin(ang_r), -np.sin(ang_c), np.sin(ang_c)], axis=1)
    return jnp.asarray(cos, jnp.float32), jnp.asarray(sin, jnp.float32)


def ssd_mixer(p_lat, p_ctx, dt_lat, dt_ctx, conv_w, conv_b, dt_bias, a_log, d_skip, ctx_cols=FULL_COLS):
    bsz, seq, _ = p_lat.shape
    ctx_len = p_ctx.shape[1]
    t, g, hpg = SSD_CHUNK, SSD_GROUPS, SSD_HPG
    assert seq % t == 0 and ctx_len % t == 0
    nc = (seq + ctx_len) // t
    per_row = lambda v: jnp.broadcast_to(
        v.astype(jnp.float32).reshape(2, g, hpg).transpose(1, 0, 2).reshape(g, 2 * hpg, 1), (g, 2 * hpg, LANES))
    dsk = jnp.broadcast_to(jnp.repeat(d_skip.astype(jnp.float32).reshape(g, hpg), SSD_HEAD_DIM, axis=1)[:, :, None],
                           (g, SSD_GW, LANES))
    cos, sin = _rope_tables(seq)
    x0 = sum(MAIN_SPLITS[:6]) // SSD_GW
    b0 = (sum(MAIN_SPLITS[:6]) + SSD_WIDTH) // SSD_STATE
    c0 = b0 + g
    cw = conv_w.astype(jnp.float32)
    cb = conv_b.astype(jnp.float32).reshape(1, SSD_CONV_CH)
    wb0 = SSD_WIDTH // SSD_STATE
    ltot = seq + ctx_len
    f32, bf = jnp.float32, jnp.bfloat16
    y_lat, y_ctx = pl.pallas_call(
        partial(_ssd_kernel, seq=seq, ctx_len=ctx_len),
        grid=(bsz, g),
        in_specs=[pl.BlockSpec((1, seq, SSD_GW), lambda b, i: (b, 0, x0 + i)),
                  pl.BlockSpec((1, seq, SSD_STATE), lambda b, i: (b, 0, b0 + i)),
                  pl.BlockSpec((1, seq, SSD_STATE), lambda b, i: (b, 0, c0 + i)),
                  pl.BlockSpec((1, ctx_len, SSD_GW), lambda b, i: (b, 0, ctx_cols.xbc // SSD_GW + i)),
                  pl.BlockSpec((1, ctx_len, SSD_STATE), lambda b, i: (b, 0, (ctx_cols.xbc + SSD_WIDTH) // SSD_STATE + i)),
                  pl.BlockSpec((1, ctx_len, SSD_STATE), lambda b, i: (b, 0, (ctx_cols.xbc + SSD_WIDTH) // SSD_STATE + g + i)),
                  pl.BlockSpec((1, seq, DT_PAD), lambda b, i: (b, 0, 0)),
                  pl.BlockSpec((1, ctx_len, DT_PAD), lambda b, i: (b, 0, 0)),
                  pl.BlockSpec((1, 2 * hpg, LANES), lambda b, i: (i, 0, 0)),
                  pl.BlockSpec((1, 2 * hpg, LANES), lambda b, i: (i, 0, 0)),
                  pl.BlockSpec((1, SSD_GW, LANES), lambda b, i: (i, 0, 0)),
                  pl.BlockSpec((SSD_CONV, SSD_GW), lambda b, i: (0, i)),
                  pl.BlockSpec((SSD_CONV, SSD_STATE), lambda b, i: (0, wb0 + i)),
                  pl.BlockSpec((SSD_CONV, SSD_STATE), lambda b, i: (0, wb0 + g + i)),
                  pl.BlockSpec((1, SSD_GW), lambda b, i: (0, i)),
                  pl.BlockSpec((1, SSD_STATE), lambda b, i: (0, wb0 + i)),
                  pl.BlockSpec((1, SSD_STATE), lambda b, i: (0, wb0 + g + i)),
                  pl.BlockSpec((seq, SSD_STATE), lambda b, i: (0, 0)),
                  pl.BlockSpec((seq, SSD_STATE), lambda b, i: (0, 0))],
        out_specs=[pl.BlockSpec((1, seq, SSD_GW), lambda b, i: (b, 0, i)),
                   pl.BlockSpec((1, ctx_len, SSD_GW), lambda b, i: (b, 0, i))],
        out_shape=[jax.ShapeDtypeStruct((bsz, seq, SSD_WIDTH), f32),
                   jax.ShapeDtypeStruct((bsz, ctx_len, SSD_WIDTH), f32)],
        scratch_shapes=[pltpu.VMEM((SSD_GW // LANES, seq + 2 * CONV_PAD, LANES), f32),
                        pltpu.VMEM((seq + 2 * CONV_PAD, SSD_STATE), f32),
                        pltpu.VMEM((seq + 2 * CONV_PAD, SSD_STATE), f32),
                        pltpu.VMEM((nc, SSD_GW, t), f32),
                        pltpu.VMEM((ltot, SSD_STATE), bf),
                        pltpu.VMEM((ltot, SSD_STATE), bf),
                        pltpu.VMEM((nc, DT_PAD, t), f32),
                        pltpu.VMEM((nc, 5 * 2 * hpg, t), f32),
                        pltpu.VMEM((ltot, LANES), f32),
                        pltpu.VMEM((nc, 2 * hpg, LANES), f32),
                        pltpu.VMEM((nc, SSD_GW, t), f32),
                        pltpu.VMEM((2, nc, SSD_GW, SSD_STATE), f32),
                        pltpu.VMEM((2, nc, SSD_GW, SSD_STATE), bf),
                        pltpu.VMEM((2, SSD_GW, SSD_STATE), f32)],
        compiler_params=pltpu.CompilerParams(dimension_semantics=("arbitrary", "arbitrary"),
                                             vmem_limit_bytes=V7X_VMEM_LIMIT_BYTES),
        name="ssd_mixer",
    )(p_lat, p_lat, p_lat, p_ctx, p_ctx, p_ctx, dt_lat, dt_ctx, per_row(dt_bias), per_row(a_log), dsk,
      cw, cw, cw, cb, cb, cb, cos, sin)
    return y_ctx, y_lat


INPROJ_TM = 1024
INPROJ_TN = 2560
ADA_TN = 1024
BRANCH_TM = 512
CTX_OUT_TILE = 1024
ADA_ROWS = 16


def _layer(xc, xl, c, c_ctx, w_ada_all, layer, b_ada, norm_w, w_main_all, w_dt_all, rpb, four_wb, conv_w, conv_b, dt_bias,
           a_log, d_skip, ssd_norm_w, wb_nab, wb_fourb, wb_ssdb, w_outb, update_ctx, final_norm_w):
    bsz, seq, d = xl.shape
    ctx_len = xc.shape[1]
    bf = jnp.bfloat16
    assert bsz < ADA_ROWS

    cc = jnp.concatenate([c, c_ctx[None], jnp.zeros((ADA_ROWS - bsz - 1, d), c.dtype)], axis=0)
    mod = matmul(jax.nn.silu(cc), w_ada_all, layer, tm=ADA_ROWS, tn=ADA_TN) + b_ada
    sh, sc, gt = jnp.split(mod, 3, axis=-1)
    sh_l, sc_l, g_l = sh[:bsz], sc[:bsz], gt[:bsz]
    sh_c, sc_c, g_c = sh[bsz:bsz + 1], sc[bsz:bsz + 1], gt[bsz]

    xl2d = xl.reshape(bsz * seq, d)
    xc2d = xc.reshape(bsz * ctx_len, d)
    pl2d, dt_l = in_projection(xl2d, norm_w, sc_l[:, None], sh_l[:, None], w_main_all, layer, w_dt_all, layer,
                               rows_per_mod=seq, tm=INPROJ_TM, tn=INPROJ_TN, out_dtype=bf)
    if update_ctx:
        ctx_cols, w_ctx, ctx_layer, ctx_tn = FULL_COLS, w_main_all, layer, INPROJ_TN
    else:
        ctx_cols, ctx_layer = KV_XBC_COLS, 0
        w_ctx = jnp.concatenate([w_main_all[layer, FULL_COLS.k:FULL_COLS.v + NA_WIDTH],
                                 w_main_all[layer, FULL_COLS.xbc:FULL_COLS.xbc + SSD_CONV_CH]], axis=0)[None]
        ctx_tn = w_ctx.shape[1] // 2
    pc2d, dt_c = in_projection(xc2d, norm_w, sc_c[:, None], sh_c[:, None], w_ctx, ctx_layer, w_dt_all, layer,
                               rows_per_mod=bsz * ctx_len, tm=INPROJ_TM, tn=ctx_tn, out_dtype=bf)
    p_l = pl2d.reshape(bsz, seq, MAIN_WIDTH)
    p_c = pc2d.reshape(bsz, ctx_len, w_ctx.shape[1])
    dt_l = dt_l.reshape(bsz, seq, DT_PAD)
    dt_c = dt_c.reshape(bsz, ctx_len, DT_PAD)

    a_l, a_c = na_attention(p_l, p_c, na_bias_table(rpb), with_ctx=update_ctx, out_dtype=bf, ctx_cols=ctx_cols)
    gfo_l, nt2_l = fourier_branch(p_l, four_wb, layer, tm=BRANCH_TM)
    s_c, s_l = ssd_mixer(p_l, p_c, dt_l, dt_c, conv_w, conv_b, dt_bias, a_log, d_skip, ctx_cols=ctx_cols)
    m_l = merge_branches(pl2d, a_l.reshape(bsz * seq, NA_WIDTH), gfo_l.reshape(bsz * seq, FOUR_WIDTH), nt2_l,
                         s_l.reshape(bsz * seq, SSD_WIDTH), ssd_norm_w, wb_nab, wb_fourb, wb_ssdb, layer, tm=BRANCH_TM)
    xl_new = out_projection(m_l, w_outb, layer, xl2d, g_l[:, None], rows_per_mod=seq, tm=BRANCH_TM, tn=d,
                            final_norm_w=final_norm_w)
    xl_new = xl_new.reshape(bsz, seq, d)
    if update_ctx:
        gfo_c, nt2_c = fourier_branch(p_c, four_wb, layer, tm=ctx_len // 2)
        assert nt2_c == 1
        m_c = merge_branches(pc2d, a_c.reshape(bsz * ctx_len, NA_WIDTH), gfo_c.reshape(bsz * ctx_len, FOUR_WIDTH), None,
                             s_c.reshape(bsz * ctx_len, SSD_WIDTH), ssd_norm_w, wb_nab, wb_fourb, wb_ssdb, layer, tm=BRANCH_TM)
        xc = out_projection(m_c, w_outb, layer, xc2d, g_c[None, None], rows_per_mod=bsz * ctx_len, tm=CTX_OUT_TILE, tn=CTX_OUT_TILE)
        xc = xc.reshape(bsz, ctx_len, d)
    return xc, xl_new


def kernel(x, c, ctx, c_ctx, w_ada, b_ada, norm_w, w_in, na_rpb, four_w, ssd_conv_w, ssd_conv_b, ssd_dt_bias,
           ssd_a_log, ssd_d, ssd_norm_w, wb_na, wb_four, wb_ssd, w_out, final_norm_w):
    xc, xl = ctx, x
    w_main_all, w_dt_all = repack_w_in(w_in)
    four_wb, wb_nab, wb_fourb, wb_ssdb, w_outb = (t.astype(jnp.bfloat16) for t in (four_w, wb_na, wb_four, wb_ssd, w_out))
    for l in range(DEPTH):
        xc, xl = _layer(xc, xl, c, c_ctx, w_ada, l, b_ada[l], norm_w[l], w_main_all, w_dt_all, na_rpb[l], four_wb,
                        ssd_conv_w[l], ssd_conv_b[l], ssd_dt_bias[l], ssd_a_log[l], ssd_d[l], ssd_norm_w[l],
                        wb_nab, wb_fourb, wb_ssdb, w_outb, update_ctx=(l < DEPTH - 1),
                        final_norm_w=final_norm_w if l == DEPTH - 1 else None)
    return xl
```

```python
import math
from functools import partial
from typing import NamedTuple

import jax
import jax.numpy as jnp
import numpy as np
from jax import lax
from jax.experimental import pallas as pl
from jax.experimental.pallas import tpu as pltpu

D_MODEL = 2048
DEPTH = 2
GRID_W = 64
EPS = 1e-6
NA_HEADS = 8
NA_HEAD_DIM = 128
NA_WIDTH = NA_HEADS * NA_HEAD_DIM
NA_WIN_R = 8
NA_WIN_C = 16
LOG2E = math.log2(math.e)
FOUR_GROUPS = 4
FOUR_GROUP_DIM = 256
FOUR_WIDTH = FOUR_GROUPS * FOUR_GROUP_DIM
SSD_HEADS = 16
SSD_HEAD_DIM = 64
SSD_WIDTH = SSD_HEADS * SSD_HEAD_DIM
SSD_GROUPS = 4
SSD_HPG = SSD_HEADS // SSD_GROUPS
SSD_STATE = 128
SSD_CONV = 7
SSD_CHUNK = 128
SSD_CONV_CH = SSD_WIDTH + 2 * SSD_GROUPS * SSD_STATE
ROPE_BASE = 10000.0
DT_WIDTH = 2 * SSD_HEADS
MAIN_SPLITS = (NA_WIDTH, NA_WIDTH, NA_WIDTH, NA_WIDTH, FOUR_WIDTH, FOUR_WIDTH, SSD_CONV_CH, SSD_WIDTH,
               D_MODEL, D_MODEL, D_MODEL)
MAIN_WIDTH = sum(MAIN_SPLITS)
DT_OFFSET = sum(MAIN_SPLITS[:8])
DT_PAD = 128


class ProjCols(NamedTuple):
    k: int
    v: int
    xbc: int


FULL_COLS = ProjCols(k=NA_WIDTH, v=2 * NA_WIDTH, xbc=sum(MAIN_SPLITS[:6]))
KV_XBC_COLS = ProjCols(k=0, v=NA_WIDTH, xbc=2 * NA_WIDTH)

LANES = 128
V7X_VMEM_LIMIT_BYTES = 56 * 1024 * 1024


def _mm_kernel(a_ref, b_ref, o_ref):
    a = a_ref[...].astype(jnp.bfloat16)
    b = b_ref[...].astype(jnp.bfloat16)
    o_ref[...] = jnp.dot(a, b, preferred_element_type=jnp.float32).astype(o_ref.dtype)


def matmul(a, b, layer, *, tm, tn, out_dtype=jnp.float32):
    m, k = a.shape
    _, _, n = b.shape
    assert m % tm == 0 and n % tn == 0, (a.shape, b.shape, tm, tn)
    return pl.pallas_call(
        _mm_kernel,
        grid=(m // tm, n // tn),
        in_specs=[pl.BlockSpec((tm, k), lambda i, j: (i, 0)),
                  pl.BlockSpec((None, k, tn), lambda i, j: (layer, 0, j))],
        out_specs=pl.BlockSpec((tm, tn), lambda i, j: (i, j)),
        out_shape=jax.ShapeDtypeStruct((m, n), out_dtype),
        compiler_params=pltpu.CompilerParams(dimension_semantics=("arbitrary", "arbitrary"),
                                             vmem_limit_bytes=V7X_VMEM_LIMIT_BYTES),
        name="matmul",
    )(a, b)


REPACK_TN = 1024


def _repack_kernel(a_ref, nxt_ref, o_ref, odt_ref):
    j = pl.program_id(1)
    first_below = DT_OFFSET // REPACK_TN

    @pl.when(j < first_below)
    def _():
        o_ref[...] = a_ref[...].astype(o_ref.dtype)

    @pl.when(j >= first_below)
    def _():
        o_ref[...] = jnp.concatenate([a_ref[DT_WIDTH:, :], nxt_ref[...]], axis=0).astype(o_ref.dtype)

    @pl.when(j == first_below)
    def _():
        odt_ref[...] = a_ref[:DT_PAD, :].astype(odt_ref.dtype)


def repack_w_in(w_in):
    depth, d, _ = w_in.shape
    tn = REPACK_TN
    w_t = jnp.swapaxes(w_in, 1, 2)
    return pl.pallas_call(
        _repack_kernel,
        grid=(depth, MAIN_WIDTH // tn),
        in_specs=[pl.BlockSpec((None, tn, d), lambda l, j: (l, j, 0)),
                  pl.BlockSpec((None, DT_WIDTH, d), lambda l, j: (l, (j + 1) * (tn // DT_WIDTH), 0))],
        out_specs=[pl.BlockSpec((None, tn, d), lambda l, j: (l, j, 0)),
                   pl.BlockSpec((None, DT_PAD, d), lambda l, j: (l, 0, 0))],
        out_shape=[jax.ShapeDtypeStruct((depth, MAIN_WIDTH, d), jnp.bfloat16),
                   jax.ShapeDtypeStruct((depth, DT_PAD, d), jnp.bfloat16)],
        compiler_params=pltpu.CompilerParams(dimension_semantics=("arbitrary", "arbitrary"),
                                             vmem_limit_bytes=V7X_VMEM_LIMIT_BYTES),
        name="repack_w_in",
    )(w_t, w_t)


def _inproj_kernel(x_ref, nw_ref, sc_ref, sh_ref, w_ref, wdt_ref, o_ref, dt_ref, h_ref):
    @pl.when(pl.program_id(1) == 0)
    def _():
        x = x_ref[...]
        y = x * lax.rsqrt(jnp.mean(x * x, axis=-1, keepdims=True) + EPS) * nw_ref[...]
        h = (y * (1.0 + sc_ref[0]) + sh_ref[0]).astype(jnp.bfloat16)
        h_ref[...] = h
        dt_ref[...] = lax.dot_general(h, wdt_ref[...], _NT, preferred_element_type=jnp.float32)

    o_ref[...] = lax.dot_general(h_ref[...], w_ref[...], _NT, preferred_element_type=jnp.float32).astype(o_ref.dtype)


def in_projection(x2d, norm_w, scale, shift, w_main_t, layer, w_dt_t, dt_layer, *, rows_per_mod, tm, tn, out_dtype):
    m, d = x2d.shape
    n = w_main_t.shape[1]
    assert m % tm == 0 and n % tn == 0 and rows_per_mod % tm == 0
    per = rows_per_mod // tm
    return pl.pallas_call(
        _inproj_kernel,
        grid=(m // tm, n // tn),
        in_specs=[pl.BlockSpec((tm, d), lambda i, j: (i, 0)),
                  pl.BlockSpec((1, d), lambda i, j: (0, 0)),
                  pl.BlockSpec((1, 1, d), lambda i, j: (i // per, 0, 0)),
                  pl.BlockSpec((1, 1, d), lambda i, j: (i // per, 0, 0)),
                  pl.BlockSpec((None, tn, d), lambda i, j: (layer, j, 0)),
                  pl.BlockSpec((None, DT_PAD, d), lambda i, j: (dt_layer, 0, 0))],
        out_specs=[pl.BlockSpec((tm, tn), lambda i, j: (i, j)),
                   pl.BlockSpec((tm, DT_PAD), lambda i, j: (i, 0))],
        out_shape=[jax.ShapeDtypeStruct((m, n), out_dtype),
                   jax.ShapeDtypeStruct((m, DT_PAD), jnp.float32)],
        scratch_shapes=[pltpu.VMEM((tm, d), jnp.bfloat16)],
        compiler_params=pltpu.CompilerParams(dimension_semantics=("arbitrary", "arbitrary"),
                                             vmem_limit_bytes=V7X_VMEM_LIMIT_BYTES),
        name="in_projection",
    )(x2d, norm_w.reshape(1, d), scale, shift, w_main_t, w_dt_t)


GRID_ROWS = 32
NA_WIN_TOKENS = NA_WIN_R * GRID_W
_NT = (((1,), (1,)), ((), ()))
NA_ROW_UNROLL = 16


def _na_kernel(*refs, with_ctx):
    if with_ctx:
        q_ref, k_ref, v_ref, kc_ref, vc_ref, bias_ref, qc_ref, o_ref, oc_ref = refs
    else:
        q_ref, k_ref, v_ref, kc_ref, vc_ref, bias_ref, o_ref = refs
    bf = jnp.bfloat16
    scale = NA_HEAD_DIM ** -0.5
    scale2 = scale * LOG2E
    kc = kc_ref[0].astype(bf)
    vc = vc_ref[0].astype(bf)
    nq = NA_ROW_UNROLL * GRID_W

    def lane_tiles(x, op):
        acc = x[:, 0:LANES]
        for l0 in range(LANES, x.shape[1], LANES):
            acc = op(acc, x[:, l0:l0 + LANES])
        return acc

    def rows(rb, carry):
        qb0 = pl.multiple_of(rb * nq, nq)
        q_all = q_ref[0, pl.ds(qb0, nq), :].astype(bf)
        s_c_all = lax.dot_general(q_all, kc, _NT, preferred_element_type=jnp.float32) * scale2
        k0s, scores, probs = [], [], []
        for i in range(NA_ROW_UNROLL):
            r = rb * NA_ROW_UNROLL + i
            start = jnp.clip(r - NA_WIN_R // 2, 0, GRID_ROWS - NA_WIN_R)
            k0 = pl.multiple_of(start * GRID_W, GRID_W)
            q = q_all[i * GRID_W:(i + 1) * GRID_W]
            kw = k_ref[0, pl.ds(k0, NA_WIN_TOKENS), :].astype(bf)
            s_w = lax.dot_general(q, kw, _NT, preferred_element_type=jnp.float32) * scale2 + bias_ref[0, r - start]
            k0s.append(k0)
            scores.append((s_w, s_c_all[i * GRID_W:(i + 1) * GRID_W]))
        for s_w, s_c in scores:
            m = jnp.max(jnp.maximum(lane_tiles(s_w, jnp.maximum), lane_tiles(s_c, jnp.maximum)),
                        axis=-1, keepdims=True)
            e_w = jnp.exp2(s_w - m)
            e_c = jnp.exp2(s_c - m)
            denom = jnp.sum(lane_tiles(e_w, jnp.add) + lane_tiles(e_c, jnp.add), axis=-1, keepdims=True)
            probs.append((e_w.astype(bf), e_c.astype(bf), denom))
        o_c_all = jnp.dot(jnp.concatenate([p[1] for p in probs], axis=0), vc, preferred_element_type=jnp.float32)
        for i, (k0, (e_w, _, denom)) in enumerate(zip(k0s, probs)):
            vw = v_ref[0, pl.ds(k0, NA_WIN_TOKENS), :].astype(bf)
            o = jnp.dot(e_w, vw, preferred_element_type=jnp.float32) + o_c_all[i * GRID_W:(i + 1) * GRID_W]
            o_ref[0, pl.ds(qb0 + i * GRID_W, GRID_W), :] = (o / denom).astype(o_ref.dtype)
        return carry

    lax.fori_loop(0, GRID_ROWS // NA_ROW_UNROLL, rows, 0)

    if with_ctx:
        s = lax.dot_general(qc_ref[0].astype(bf), kc, _NT, preferred_element_type=jnp.float32) * scale
        e = jnp.exp(s - jnp.max(s, axis=-1, keepdims=True))
        o = jnp.dot(e.astype(bf), vc, preferred_element_type=jnp.float32)
        oc_ref[0] = (o / jnp.sum(e, axis=-1, keepdims=True)).astype(oc_ref.dtype)


def na_bias_table(rpb):
    qcol = np.arange(GRID_W)[:, None]
    kcol = np.arange(GRID_W)[None, :]
    qstart = np.clip(qcol - NA_WIN_C // 2, 0, GRID_W - NA_WIN_C)
    mask = (kcol >= qstart) & (kcol < qstart + NA_WIN_C)
    dc = np.clip(kcol - qcol + NA_WIN_C - 1, 0, 2 * NA_WIN_C - 2)
    pick_col = (np.arange(2 * NA_WIN_C - 1)[:, None, None] == dc[None]).astype(np.float32)
    rows = jnp.stack([rpb.astype(jnp.float32)[:, NA_WIN_R - 1 - dl:2 * NA_WIN_R - 1 - dl] for dl in range(NA_WIN_R)],
                     axis=1)
    t = jnp.einsum("hdwc,cqk->hdqwk", rows, pick_col, precision=lax.Precision.HIGHEST)
    t = jnp.where(mask[None, None, :, None, :], t * LOG2E, -1e30)
    return t.reshape(rpb.shape[0], NA_WIN_R, GRID_W, NA_WIN_TOKENS)


def na_attention(p_lat, p_ctx, bias, *, with_ctx, out_dtype, ctx_cols=FULL_COLS):
    bsz, seq, _ = p_lat.shape
    ctx_len = p_ctx.shape[1]
    assert seq == GRID_ROWS * GRID_W
    d, h = NA_HEAD_DIM, NA_HEADS
    in_specs = [pl.BlockSpec((1, seq, d), lambda b, i: (b, 0, i)),
                pl.BlockSpec((1, seq, d), lambda b, i: (b, 0, h + i)),
                pl.BlockSpec((1, seq, d), lambda b, i: (b, 0, 2 * h + i)),
                pl.BlockSpec((1, ctx_len, d), lambda b, i: (b, 0, ctx_cols.k // d + i)),
                pl.BlockSpec((1, ctx_len, d), lambda b, i: (b, 0, ctx_cols.v // d + i)),
                pl.BlockSpec((1, NA_WIN_R, GRID_W, NA_WIN_TOKENS), lambda b, i: (i, 0, 0, 0))]
    args = [p_lat, p_lat, p_lat, p_ctx, p_ctx, bias]
    out_specs = [pl.BlockSpec((1, seq, d), lambda b, i: (b, 0, i))]
    out_shape = [jax.ShapeDtypeStruct((bsz, seq, h * d), out_dtype)]
    if with_ctx:
        in_specs.append(pl.BlockSpec((1, ctx_len, d), lambda b, i: (b, 0, i)))
        args.append(p_ctx)
        out_specs.append(pl.BlockSpec((1, ctx_len, d), lambda b, i: (b, 0, i)))
        out_shape.append(jax.ShapeDtypeStruct((bsz, ctx_len, h * d), out_dtype))
    outs = pl.pallas_call(
        partial(_na_kernel, with_ctx=with_ctx),
        grid=(bsz, h),
        in_specs=in_specs, out_specs=out_specs, out_shape=out_shape,
        compiler_params=pltpu.CompilerParams(dimension_semantics=("arbitrary", "arbitrary"),
                                             vmem_limit_bytes=V7X_VMEM_LIMIT_BYTES),
        name="na_attention",
    )(*args)
    return (outs[0], outs[1]) if with_ctx else (outs[0], None)


def _dft_cos_sin(n):
    jk = np.outer(np.arange(n), np.arange(n)) % n
    ang = 2.0 * np.pi * jk / n
    return np.cos(ang) / np.sqrt(n), np.sin(ang) / np.sqrt(n)


FOUR_EXT = 8


def _fourier_kernel(u_ref, cc_ref, sc_ref, a_ref, flip_ref, fw_ref, zlo_ref, zhi_ref, o_ref, *, tm):
    bf = jnp.bfloat16
    gd, ext = FOUR_GROUP_DIM, tm + FOUR_EXT
    pos = jnp.dot(a_ref[...], u_ref[0].astype(bf), preferred_element_type=jnp.float32).astype(bf)
    even, odd = [], []
    for g in range(FOUR_GROUPS):
        cols = slice(g * gd, (g + 1) * gd)
        even.append(jnp.dot(pos[:ext, cols], cc_ref[...], preferred_element_type=jnp.float32))
        odd.append(jnp.dot(pos[ext:, cols], sc_ref[...], preferred_element_type=jnp.float32))
    even = jnp.concatenate(even, axis=1)
    odd = jnp.concatenate(odd, axis=1)
    y_lo = (even[:tm] - odd[:tm]).astype(bf)
    y_hi = jnp.dot(flip_ref[...], (even + odd).astype(bf), preferred_element_type=jnp.float32).astype(bf)
    for half, (y, z_ref) in enumerate(((y_lo, zlo_ref), (y_hi, zhi_ref))):
        f = jnp.dot(y, fw_ref[...], preferred_element_type=jnp.float32)
        zf = z_ref[0].astype(jnp.float32)
        o_ref[0, half] = (f * (zf * jax.nn.sigmoid(zf))).astype(o_ref.dtype)


def fourier_branch(p, four_w_bf, layer, *, tm):
    bsz, seq, _ = p.shape
    half = seq // 2
    assert half % tm == 0
    nt2, ext = half // tm, tm + FOUR_EXT
    cc, sc = _dft_cos_sin(FOUR_GROUP_DIM)
    cl, sl = _dft_cos_sin(seq)
    a = np.stack([np.concatenate([cl[i * tm:i * tm + ext], sl[i * tm:i * tm + ext]], axis=0) for i in range(nt2)])
    flip = np.zeros((tm, ext))
    flip[np.arange(tm), tm - np.arange(tm)] = 1.0
    fw, gd = FOUR_WIDTH, FOUR_GROUP_DIM
    bf = jnp.bfloat16
    out = pl.pallas_call(
        partial(_fourier_kernel, tm=tm),
        grid=(bsz, nt2),
        in_specs=[pl.BlockSpec((1, seq, fw), lambda b, i: (b, 0, 4)),
                  pl.BlockSpec((gd, gd), lambda b, i: (0, 0)),
                  pl.BlockSpec((gd, gd), lambda b, i: (0, 0)),
                  pl.BlockSpec((None, 2 * ext, seq), lambda b, i: (i, 0, 0)),
                  pl.BlockSpec((tm, ext), lambda b, i: (0, 0)),
                  pl.BlockSpec((None, fw, fw), lambda b, i: (layer, 0, 0)),
                  pl.BlockSpec((1, tm, fw), lambda b, i: (b, i, 5)),
                  pl.BlockSpec((1, tm, fw), lambda b, i: (b, 2 * nt2 - 1 - i, 5))],
        out_specs=pl.BlockSpec((1, 2, tm, fw), lambda b, i: (b, 0, i, 0)),
        out_shape=jax.ShapeDtypeStruct((bsz, 2, half, fw), bf),
        compiler_params=pltpu.CompilerParams(dimension_semantics=("arbitrary", "arbitrary"),
                                             vmem_limit_bytes=V7X_VMEM_LIMIT_BYTES),
        name="fourier_branch",
    )(p, jnp.asarray(cc, bf), jnp.asarray(sc, bf), jnp.asarray(a, bf), jnp.asarray(flip, bf), four_w_bf, p, p)
    return out.reshape(bsz, seq, fw), nt2


def fourier_row_block(t, nt2):
    return jnp.where(t < nt2, t, 3 * nt2 - 1 - t)


MERGE_TN = 1024


def _merge_kernel(a_ref, zna_ref, gfo_ref, s_ref, zs_ref, nw_ref, wa_ref, wf_ref, ws_ref, *rest):
    n_half = D_MODEL // MERGE_TN
    gate_refs, (o_ref, ap_ref, sp_ref) = rest[:3 * n_half], rest[3 * n_half:]
    bf = jnp.bfloat16
    zna = zna_ref[...].astype(jnp.float32)
    ap_ref[...] = (a_ref[...].astype(jnp.float32) * (zna * jax.nn.sigmoid(zna))).astype(bf)
    zs = zs_ref[...].astype(jnp.float32)
    t = s_ref[...].astype(jnp.float32) * (zs * jax.nn.sigmoid(zs))
    t = t * lax.rsqrt(jnp.mean(t * t, axis=-1, keepdims=True) + EPS) * nw_ref[...]
    sp_ref[...] = t.astype(bf)
    for h in range(n_half):
        cols = slice(h * MERGE_TN, (h + 1) * MERGE_TN)
        ga_ref, gf_ref, gs_ref = gate_refs[h], gate_refs[n_half + h], gate_refs[2 * n_half + h]
        o_a = jnp.dot(ap_ref[...], wa_ref[:, cols], preferred_element_type=jnp.float32)
        o_f = jnp.dot(gfo_ref[...], wf_ref[:, cols], preferred_element_type=jnp.float32)
        o_s = jnp.dot(sp_ref[...], ws_ref[:, cols], preferred_element_type=jnp.float32)
        m = (jax.nn.sigmoid(ga_ref[...].astype(jnp.float32)) * o_a
             + jax.nn.sigmoid(gf_ref[...].astype(jnp.float32)) * o_f
             + jax.nn.sigmoid(gs_ref[...].astype(jnp.float32)) * o_s)
        o_ref[:, cols] = m.astype(o_ref.dtype)


def merge_branches(p2d, a2d, gfo2d, gfo_nt2, s2d, ssd_norm_w, wa, wf, ws, layer, *, tm):
    m = p2d.shape[0]
    w, tn = NA_WIDTH, MERGE_TN
    n_half = D_MODEL // tn
    assert m % tm == 0
    row = lambda i: (i, 0)
    if gfo_nt2 is None:
        gfo_row = row
    else:
        assert p2d.shape[0] % (2 * gfo_nt2 * tm) == 0
        per_seq = 2 * gfo_nt2
        gfo_row = lambda i: ((i // per_seq) * per_seq + fourier_row_block(i % per_seq, gfo_nt2), 0)
    resident = lambda shape, idx: pl.BlockSpec(shape, lambda i: idx, pipeline_mode=pl.Buffered(1))
    w_spec = resident((None, w, D_MODEL), (layer, 0, 0))
    gate0 = DT_OFFSET // tn
    gate_specs = [pl.BlockSpec((tm, tn), partial(lambda i, blk: (i, blk), blk=gate0 + g * n_half + h))
                  for g in range(3) for h in range(n_half)]
    return pl.pallas_call(
        _merge_kernel,
        grid=(m // tm,),
        in_specs=[pl.BlockSpec((tm, w), row),
                  pl.BlockSpec((tm, w), lambda i: (i, 3)),
                  pl.BlockSpec((tm, w), gfo_row),
                  pl.BlockSpec((tm, w), row),
                  pl.BlockSpec((tm, w), lambda i: (i, 8)),
                  resident((1, w), (0, 0)), w_spec, w_spec, w_spec] + gate_specs,
        out_specs=pl.BlockSpec((tm, D_MODEL), row),
        out_shape=jax.ShapeDtypeStruct((m, D_MODEL), jnp.bfloat16),
        scratch_shapes=[pltpu.VMEM((tm, w), jnp.bfloat16), pltpu.VMEM((tm, w), jnp.bfloat16)],
        compiler_params=pltpu.CompilerParams(dimension_semantics=("arbitrary",),
                                             vmem_limit_bytes=V7X_VMEM_LIMIT_BYTES),
        name="merge_branches",
    )(a2d, p2d, gfo2d, s2d, p2d, ssd_norm_w.reshape(1, w), wa, wf, ws, *([p2d] * (3 * n_half)))


def _outproj_kernel(m_ref, w_ref, x_ref, g_ref, *rest, final_norm):
    y = jnp.dot(m_ref[...], w_ref[...], preferred_element_type=jnp.float32)
    r = x_ref[...] + g_ref[0] * y
    if final_norm:
        nw_ref, o_ref = rest
        o_ref[...] = r * lax.rsqrt(jnp.mean(r * r, axis=-1, keepdims=True) + EPS) * nw_ref[...]
    else:
        rest[0][...] = r


def out_projection(m2d, w_out_bf, layer, x2d, gate, *, rows_per_mod, tm, tn, final_norm_w=None):
    m, d = x2d.shape
    assert m % tm == 0 and d % tn == 0 and rows_per_mod % tm == 0
    per = rows_per_mod // tm
    in_specs = [pl.BlockSpec((tm, d), lambda i, j: (i, 0)),
                pl.BlockSpec((None, d, tn), lambda i, j: (layer, 0, j)),
                pl.BlockSpec((tm, tn), lambda i, j: (i, j)),
                pl.BlockSpec((1, 1, tn), lambda i, j: (i // per, 0, j))]
    args = [m2d, w_out_bf, x2d, gate]
    if final_norm_w is not None:
        assert tn == d
        in_specs.append(pl.BlockSpec((1, d), lambda i, j: (0, 0)))
        args.append(final_norm_w.astype(jnp.float32).reshape(1, d))
    return pl.pallas_call(
        partial(_outproj_kernel, final_norm=final_norm_w is not None),
        grid=(m // tm, d // tn),
        in_specs=in_specs,
        out_specs=pl.BlockSpec((tm, tn), lambda i, j: (i, j)),
        out_shape=jax.ShapeDtypeStruct((m, d), jnp.float32),
        compiler_params=pltpu.CompilerParams(dimension_semantics=("arbitrary", "arbitrary"),
                                             vmem_limit_bytes=V7X_VMEM_LIMIT_BYTES),
        name="out_projection",
    )(*args)


SSD_GW = SSD_HPG * SSD_HEAD_DIM
CONV_PAD = 8
SSD_CHUNK_UNROLL = 2
SSD_CONV_UNROLL = 4
SSD_C_UNROLL = 4
SSD_A_UNROLL = 6


def _split3_bf16(a):
    rnd = lambda v: v.astype(jnp.bfloat16).astype(jnp.float32)
    a1 = rnd(a)
    a2 = rnd(a - a1)
    a3 = rnd((a - a1) - a2)
    return a1, a2, a3


def _head_rows(rows, r0, n):
    return jnp.concatenate([jnp.broadcast_to(rows[r0 + k:r0 + k + 1, :], (SSD_HEAD_DIM, n))
                            for k in range(SSD_HPG)], axis=0)


def _ssd_kernel(xl_ref, bl_ref, cl_ref, xc_ref, bc_ref, cc_ref, dtl_ref, dtc_ref, bias_ref, alog_ref, dsk_ref,
                wx_ref, wb_ref, wc_ref, bx_ref, bb_ref, bcb_ref, cos_ref, sin_ref,
                yl_ref, yc_ref,
                padx, padb, padc, xt_s, b_s, c_s, dtt_s, row_s, dec_s, yt_s, p_s, sp_s, st_s, *, seq, ctx_len):
    f32, bf = jnp.float32, jnp.bfloat16
    t = SSD_CHUNK
    nc_c, nc_l = ctx_len // t, seq // t
    nc = nc_c + nc_l

    lane = lax.broadcasted_iota(jnp.int32, (t, LANES), 1)
    first_half = (lane % 64) < 32

    def conv_tile(pad, w_ref, b_ref, t0, l0):
        acc = jnp.broadcast_to(b_ref[:, l0:l0 + LANES], (t, LANES))
        for j in range(SSD_CONV):
            off = CONV_PAD - SSD_CONV // 2 + j
            acc = acc + pad[pl.ds(t0 + off, t), :] * w_ref[j:j + 1, l0:l0 + LANES]
        return acc * jax.nn.sigmoid(acc)

    def rope(u, t0):
        swapped = jnp.where(first_half, pltpu.roll(u, 96, axis=1), pltpu.roll(u, 32, axis=1))
        return u * cos_ref[pl.ds(t0, t), :] + swapped * sin_ref[pl.ds(t0, t), :]

    def conv_seq(x_ref, bm_ref, cm_ref, n, base, use_rope):
        slabs = ([(padx.at[h], x_ref, h * LANES) for h in range(SSD_GW // LANES)]
                 + [(padb, bm_ref, 0), (padc, cm_ref, 0)])
        for pad, src, l0 in slabs:
            pad[0:CONV_PAD, :] = jnp.zeros((CONV_PAD, LANES), f32)
            pad[CONV_PAD + n:2 * CONV_PAD + n, :] = jnp.zeros((CONV_PAD, LANES), f32)
            pad[CONV_PAD:CONV_PAD + n, :] = src[0, :, l0:l0 + LANES].astype(f32)

        unroll = min(SSD_CONV_UNROLL, n // t)

        def chunks(i, carry):
            for j in range(unroll):
                ci = i * unroll + j
                t0 = pl.multiple_of(ci * t, t)
                r0 = pl.multiple_of(base + ci * t, t)
                for h in range(SSD_GW // LANES):
                    l0 = h * LANES
                    xt_s[base // t + ci, l0:l0 + LANES, :] = conv_tile(padx.at[h], wx_ref, bx_ref, t0, l0).T
                ub = conv_tile(padb, wb_ref, bb_ref, t0, 0)
                uc = conv_tile(padc, wc_ref, bcb_ref, t0, 0)
                if use_rope:
                    ub, uc = rope(ub, t0), rope(uc, t0)
                b_s[pl.ds(r0, t), :] = ub.astype(bf)
                c_s[pl.ds(r0, t), :] = uc.astype(bf)
            return carry

        lax.fori_loop(0, n // t // unroll, chunks, 0)

    conv_seq(xc_ref, bc_ref, cc_ref, ctx_len, 0, False)
    conv_seq(xl_ref, bl_ref, cl_ref, seq, ctx_len, True)

    i0 = lax.broadcasted_iota(jnp.int32, (t, t), 0)
    i1 = lax.broadcasted_iota(jnp.int32, (t, t), 1)
    tri_f = (i0 <= i1).astype(bf)
    tri_r = (i0 >= i1).astype(bf)
    nr = 2 * SSD_HPG
    fwd_row = (lax.broadcasted_iota(jnp.int32, (nc * nr, 1), 0) % nr) < SSD_HPG
    g4 = pl.program_id(1) * SSD_HPG
    raw = []
    for c in range(nc):
        src, c0 = (dtc_ref, c) if c < nc_c else (dtl_ref, c - nc_c)
        dtt_s[c] = src[0, c0 * t:(c0 + 1) * t, :].T
        raw += [dtt_s[c, pl.ds(g4, SSD_HPG), :], dtt_s[c, pl.ds(SSD_HEADS + g4, SSD_HPG), :]]
    x = jnp.concatenate(raw, axis=0) + jnp.tile(bias_ref[0], (nc, 1))
    dt = jnp.maximum(x, 0.0) + jnp.log1p(jnp.exp(-jnp.abs(x)))
    pieces = _split3_bf16(dt * jnp.tile(-jnp.exp(alog_ref[0]), (nc, 1)))
    pf = [jnp.dot(p.astype(bf), tri_f, preferred_element_type=f32) for p in pieces]
    pr = [jnp.dot(p.astype(bf), tri_r, preferred_element_type=f32) for p in pieces]
    cs = jnp.where(fwd_row, (pf[0] + pf[1]) + pf[2], (pr[0] + pr[1]) + pr[2])
    tot = jnp.where(fwd_row, cs[:, t - 1:t], cs[:, 0:1])
    e_in = jnp.exp(cs)
    e_out = dt * jnp.exp(tot - cs)
    u = cs - jnp.log(dt)
    decb = jnp.broadcast_to(jnp.exp(tot), (nc * nr, LANES))
    for c in range(nc):
        sl = slice(c * nr, (c + 1) * nr)
        row_s[c] = jnp.concatenate([cs[sl], dt[sl], e_in[sl], e_out[sl], u[sl]], axis=0)
        dec_s[c] = decb[sl]

    lower = i1 <= i0
    lower_t = i0 <= i1
    diag = i1 == i0

    def chunks_a(i, carry):
        ids = [i * SSD_A_UNROLL + j for j in range(SSD_A_UNROLL)]
        r0s = [pl.multiple_of(c * t, t) for c in ids]
        cms = [c_s[pl.ds(r0, t), :] for r0 in r0s]
        bms = [b_s[pl.ds(r0, t), :] for r0 in r0s]
        gs = [lax.dot_general(cm, bm, _NT, preferred_element_type=f32) for cm, bm in zip(cms, bms)]
        for c, r0, bm, g in zip(ids, r0s, bms, gs):
            rows = row_s[c]
            xt = xt_s[c]
            xt_bf = xt.astype(bf)
            parts = []
            for k in range(SSD_HPG):
                kr = SSD_HPG + k
                cs_l = jnp.where(lower_t, rows[k:k + 1, :], rows[kr:kr + 1, :]).T
                w = jnp.exp(cs_l - jnp.where(lower, rows[32 + k:33 + k, :], rows[32 + kr:33 + kr, :]))
                mk = (g * (w + jnp.where(diag, rows[8 + kr:9 + kr, :], 0.0))).astype(bf)
                xk = xt_bf[k * SSD_HEAD_DIM:(k + 1) * SSD_HEAD_DIM, :]
                parts.append(lax.dot_general(xk, mk, _NT, preferred_element_type=f32))
            yt_s[c] = dsk_ref[0] * xt + jnp.concatenate(parts, axis=0)
            for d in range(2):
                wt = (xt * _head_rows(rows, 24 + SSD_HPG * d, t)).astype(bf)
                p_s[d, c] = jnp.dot(wt, bm, preferred_element_type=f32)
        return carry

    lax.fori_loop(0, nc // SSD_A_UNROLL, chunks_a, 0)

    st_s[...] = jnp.zeros(st_s.shape, f32)

    def state_step(c, d):
        st = st_s[d]
        sp_s[d, c] = st.astype(bf)
        st_s[d] = st * _head_rows(dec_s[c], SSD_HPG * d, SSD_STATE) + p_s[d, c]

    def ctx_step(i, carry):
        state_step(i, 0)
        state_step(nc_c - 1 - i, 1)
        return carry

    def lat_step(i, carry):
        state_step(nc_c + i, 0)
        state_step(nc - 1 - i, 1)
        return carry

    lax.fori_loop(0, nc_c, ctx_step, 0)
    lax.fori_loop(0, nc_l, lat_step, 0)

    def chunks_c(first, out_ref, o_first, count):
        ids = [first + u for u in range(count)]
        cms = [c_s[pl.ds(pl.multiple_of(c * t, t), t), :] for c in ids]
        offs = [[lax.dot_general(sp_s[d, c], cm, _NT, preferred_element_type=f32) for d in range(2)]
                for c, cm in zip(ids, cms)]
        for u, (c, off) in enumerate(zip(ids, offs)):
            rows = row_s[c]
            yt = yt_s[c]
            for d in range(2):
                yt = yt + off[d] * _head_rows(rows, 16 + SSD_HPG * d, t)
            for l0 in range(0, SSD_GW, LANES):
                out_ref[0, pl.ds(pl.multiple_of((o_first + u) * t, t), t), l0:l0 + LANES] = yt[l0:l0 + LANES, :].T

    def ctx_out(i, carry):
        chunks_c(i * SSD_CHUNK_UNROLL, yc_ref, i * SSD_CHUNK_UNROLL, SSD_CHUNK_UNROLL)
        return carry

    def lat_out(i, carry):
        chunks_c(nc_c + i * SSD_C_UNROLL, yl_ref, i * SSD_C_UNROLL, SSD_C_UNROLL)
        return carry

    lax.fori_loop(0, nc_c // SSD_CHUNK_UNROLL, ctx_out, 0)
    lax.fori_loop(0, nc_l // SSD_C_UNROLL, lat_out, 0)


def _rope_tables(seq):
    quarter = SSD_STATE // 4
    inv = ROPE_BASE ** (-np.arange(quarter, dtype=np.float64) / quarter)
    pos = np.arange(seq)
    ang_r = (pos // GRID_W)[:, None] * inv
    ang_c = (pos % GRID_W)[:, None] * inv
    cos = np.concatenate([np.cos(ang_r)] * 2 + [np.cos(ang_c)] * 2, axis=1)
    sin = np.concatenate([-np.sin(ang_r), np.sin(ang_r), -np.sin(ang_c), np.sin(ang_c)], axis=1)
    return jnp.asarray(cos, jnp.float32), jnp.asarray(sin, jnp.float32)


def ssd_mixer(p_lat, p_ctx, dt_lat, dt_ctx, conv_w, conv_b, dt_bias, a_log, d_skip, ctx_cols=FULL_COLS):
    bsz, seq, _ = p_lat.shape
    ctx_len = p_ctx.shape[1]
    t, g, hpg = SSD_CHUNK, SSD_GROUPS, SSD_HPG
    assert seq % t == 0 and ctx_len % t == 0
    nc = (seq + ctx_len) // t
    per_row = lambda v: jnp.broadcast_to(
        v.astype(jnp.float32).reshape(2, g, hpg).transpose(1, 0, 2).reshape(g, 2 * hpg, 1), (g, 2 * hpg, LANES))
    dsk = jnp.broadcast_to(jnp.repeat(d_skip.astype(jnp.float32).reshape(g, hpg), SSD_HEAD_DIM, axis=1)[:, :, None],
                           (g, SSD_GW, LANES))
    cos, sin = _rope_tables(seq)
    x0 = sum(MAIN_SPLITS[:6]) // SSD_GW
    b0 = (sum(MAIN_SPLITS[:6]) + SSD_WIDTH) // SSD_STATE
    c0 = b0 + g
    cw = conv_w.astype(jnp.float32)
    cb = conv_b.astype(jnp.float32).reshape(1, SSD_CONV_CH)
    wb0 = SSD_WIDTH // SSD_STATE
    ltot = seq + ctx_len
    f32, bf = jnp.float32, jnp.bfloat16
    y_lat, y_ctx = pl.pallas_call(
        partial(_ssd_kernel, seq=seq, ctx_len=ctx_len),
        grid=(bsz, g),
        in_specs=[pl.BlockSpec((1, seq, SSD_GW), lambda b, i: (b, 0, x0 + i)),
                  pl.BlockSpec((1, seq, SSD_STATE), lambda b, i: (b, 0, b0 + i)),
                  pl.BlockSpec((1, seq, SSD_STATE), lambda b, i: (b, 0, c0 + i)),
                  pl.BlockSpec((1, ctx_len, SSD_GW), lambda b, i: (b, 0, ctx_cols.xbc // SSD_GW + i)),
                  pl.BlockSpec((1, ctx_len, SSD_STATE), lambda b, i: (b, 0, (ctx_cols.xbc + SSD_WIDTH) // SSD_STATE + i)),
                  pl.BlockSpec((1, ctx_len, SSD_STATE), lambda b, i: (b, 0, (ctx_cols.xbc + SSD_WIDTH) // SSD_STATE + g + i)),
                  pl.BlockSpec((1, seq, DT_PAD), lambda b, i: (b, 0, 0)),
                  pl.BlockSpec((1, ctx_len, DT_PAD), lambda b, i: (b, 0, 0)),
                  pl.BlockSpec((1, 2 * hpg, LANES), lambda b, i: (i, 0, 0)),
                  pl.BlockSpec((1, 2 * hpg, LANES), lambda b, i: (i, 0, 0)),
                  pl.BlockSpec((1, SSD_GW, LANES), lambda b, i: (i, 0, 0)),
                  pl.BlockSpec((SSD_CONV, SSD_GW), lambda b, i: (0, i)),
                  pl.BlockSpec((SSD_CONV, SSD_STATE), lambda b, i: (0, wb0 + i)),
                  pl.BlockSpec((SSD_CONV, SSD_STATE), lambda b, i: (0, wb0 + g + i)),
                  pl.BlockSpec((1, SSD_GW), lambda b, i: (0, i)),
                  pl.BlockSpec((1, SSD_STATE), lambda b, i: (0, wb0 + i)),
                  pl.BlockSpec((1, SSD_STATE), lambda b, i: (0, wb0 + g + i)),
                  pl.BlockSpec((seq, SSD_STATE), lambda b, i: (0, 0)),
                  pl.BlockSpec((seq, SSD_STATE), lambda b, i: (0, 0))],
        out_specs=[pl.BlockSpec((1, seq, SSD_GW), lambda b, i: (b, 0, i)),
                   pl.BlockSpec((1, ctx_len, SSD_GW), lambda b, i: (b, 0, i))],
        out_shape=[jax.ShapeDtypeStruct((bsz, seq, SSD_WIDTH), f32),
                   jax.ShapeDtypeStruct((bsz, ctx_len, SSD_WIDTH), f32)],
        scratch_shapes=[pltpu.VMEM((SSD_GW // LANES, seq + 2 * CONV_PAD, LANES), f32),
                        pltpu.VMEM((seq + 2 * CONV_PAD, SSD_STATE), f32),
                        pltpu.VMEM((seq + 2 * CONV_PAD, SSD_STATE), f32),
                        pltpu.VMEM((nc, SSD_GW, t), f32),
                        pltpu.VMEM((ltot, SSD_STATE), bf),
                        pltpu.VMEM((ltot, SSD_STATE), bf),
                        pltpu.VMEM((nc, DT_PAD, t), f32),
                        pltpu.VMEM((nc, 5 * 2 * hpg, t), f32),
                        pltpu.VMEM((nc, 2 * hpg, LANES), f32),
                        pltpu.VMEM((nc, SSD_GW, t), f32),
                        pltpu.VMEM((2, nc, SSD_GW, SSD_STATE), f32),
                        pltpu.VMEM((2, nc, SSD_GW, SSD_STATE), bf),
                        pltpu.VMEM((2, SSD_GW, SSD_STATE), f32)],
        compiler_params=pltpu.CompilerParams(dimension_semantics=("arbitrary", "arbitrary"),
                                             vmem_limit_bytes=V7X_VMEM_LIMIT_BYTES),
        name="ssd_mixer",
    )(p_lat, p_lat, p_lat, p_ctx, p_ctx, p_ctx, dt_lat, dt_ctx, per_row(dt_bias), per_row(a_log), dsk,
      cw, cw, cw, cb, cb, cb, cos, sin)
    return y_ctx, y_lat


INPROJ_TM = 1024
INPROJ_TN = 2560
ADA_TN = 1024
BRANCH_TM = 512
CTX_OUT_TILE = 1024
ADA_ROWS = 16


def _layer(xc, xl, c, c_ctx, w_ada_all, layer, b_ada, norm_w, w_main_all, w_dt_all, rpb, four_wb, conv_w, conv_b, dt_bias,
           a_log, d_skip, ssd_norm_w, wb_nab, wb_fourb, wb_ssdb, w_outb, update_ctx, final_norm_w):
    bsz, seq, d = xl.shape
    ctx_len = xc.shape[1]
    bf = jnp.bfloat16
    assert bsz < ADA_ROWS

    cc = jnp.concatenate([c, c_ctx[None], jnp.zeros((ADA_ROWS - bsz - 1, d), c.dtype)], axis=0)
    mod = matmul(jax.nn.silu(cc), w_ada_all, layer, tm=ADA_ROWS, tn=ADA_TN) + b_ada
    sh, sc, gt = jnp.split(mod, 3, axis=-1)
    sh_l, sc_l, g_l = sh[:bsz], sc[:bsz], gt[:bsz]
    sh_c, sc_c, g_c = sh[bsz:bsz + 1], sc[bsz:bsz + 1], gt[bsz]

    xl2d = xl.reshape(bsz * seq, d)
    xc2d = xc.reshape(bsz * ctx_len, d)
    pl2d, dt_l = in_projection(xl2d, norm_w, sc_l[:, None], sh_l[:, None], w_main_all, layer, w_dt_all, layer,
                               rows_per_mod=seq, tm=INPROJ_TM, tn=INPROJ_TN, out_dtype=bf)
    if update_ctx:
        ctx_cols, w_ctx, ctx_layer, ctx_tn = FULL_COLS, w_main_all, layer, INPROJ_TN
    else:
        ctx_cols, ctx_layer = KV_XBC_COLS, 0
        w_ctx = jnp.concatenate([w_main_all[layer, FULL_COLS.k:FULL_COLS.v + NA_WIDTH],
                                 w_main_all[layer, FULL_COLS.xbc:FULL_COLS.xbc + SSD_CONV_CH]], axis=0)[None]
        ctx_tn = w_ctx.shape[1] // 2
    pc2d, dt_c = in_projection(xc2d, norm_w, sc_c[:, None], sh_c[:, None], w_ctx, ctx_layer, w_dt_all, layer,
                               rows_per_mod=bsz * ctx_len, tm=INPROJ_TM, tn=ctx_tn, out_dtype=bf)
    p_l = pl2d.reshape(bsz, seq, MAIN_WIDTH)
    p_c = pc2d.reshape(bsz, ctx_len, w_ctx.shape[1])
    dt_l = dt_l.reshape(bsz, seq, DT_PAD)
    dt_c = dt_c.reshape(bsz, ctx_len, DT_PAD)

    a_l, a_c = na_attention(p_l, p_c, na_bias_table(rpb), with_ctx=update_ctx, out_dtype=bf, ctx_cols=ctx_cols)
    gfo_l, nt2_l = fourier_branch(p_l, four_wb, layer, tm=BRANCH_TM)
    s_c, s_l = ssd_mixer(p_l, p_c, dt_l, dt_c, conv_w, conv_b, dt_bias, a_log, d_skip, ctx_cols=ctx_cols)
    m_l = merge_branches(pl2d, a_l.reshape(bsz * seq, NA_WIDTH), gfo_l.reshape(bsz * seq, FOUR_WIDTH), nt2_l,
                         s_l.reshape(bsz * seq, SSD_WIDTH), ssd_norm_w, wb_nab, wb_fourb, wb_ssdb, layer, tm=BRANCH_TM)
    xl_new = out_projection(m_l, w_outb, layer, xl2d, g_l[:, None], rows_per_mod=seq, tm=BRANCH_TM, tn=d,
                            final_norm_w=final_norm_w)
    xl_new = xl_new.reshape(bsz, seq, d)
    if update_ctx:
        gfo_c, nt2_c = fourier_branch(p_c, four_wb, layer, tm=ctx_len // 2)
        assert nt2_c == 1
        m_c = merge_branches(pc2d, a_c.reshape(bsz * ctx_len, NA_WIDTH), gfo_c.reshape(bsz * ctx_len, FOUR_WIDTH), None,
                             s_c.reshape(bsz * ctx_len, SSD_WIDTH), ssd_norm_w, wb_nab, wb_fourb, wb_ssdb, layer, tm=BRANCH_TM)
        xc = out_projection(m_c, w_outb, layer, xc2d, g_c[None, None], rows_per_mod=bsz * ctx_len, tm=CTX_OUT_TILE, tn=CTX_OUT_TILE)
        xc = xc.reshape(bsz, ctx_len, d)
    return xc, xl_new


def kernel(x, c, ctx, c_ctx, w_ada, b_ada, norm_w, w_in, na_rpb, four_w, ssd_conv_w, ssd_conv_b, ssd_dt_bias,
           ssd_a_log, ssd_d, ssd_norm_w, wb_na, wb_four, wb_ssd, w_out, final_norm_w):
    xc, xl = ctx, x
    w_main_all, w_dt_all = repack_w_in(w_in)
    four_wb, wb_nab, wb_fourb, wb_ssdb, w_outb = (t.astype(jnp.bfloat16) for t in (four_w, wb_na, wb_four, wb_ssd, w_out))
    for l in range(DEPTH):
        xc, xl = _layer(xc, xl, c, c_ctx, w_ada, l, b_ada[l], norm_w[l], w_main_all, w_dt_all, na_rpb[l], four_wb,
                        ssd_conv_w[l], ssd_conv_b[l], ssd_dt_bias[l], ssd_a_log[l], ssd_d[l], ssd_norm_w[l],
                        wb_nab, wb_fourb, wb_ssdb, w_outb, update_ctx=(l < DEPTH - 1),
                        final_norm_w=final_norm_w if l == DEPTH - 1 else None)
    return xl
```

```python
import math
from functools import partial
from typing import NamedTuple

import jax
import jax.numpy as jnp
import numpy as np
from jax import lax
from jax.experimental import pallas as pl
from jax.experimental.pallas import tpu as pltpu

D_MODEL = 2048
DEPTH = 2
GRID_W = 64
EPS = 1e-6
NA_HEADS = 8
NA_HEAD_DIM = 128
NA_WIDTH = NA_HEADS * NA_HEAD_DIM
NA_WIN_R = 8
NA_WIN_C = 16
LOG2E = math.log2(math.e)
FOUR_GROUPS = 4
FOUR_GROUP_DIM = 256
FOUR_WIDTH = FOUR_GROUPS * FOUR_GROUP_DIM
SSD_HEADS = 16
SSD_HEAD_DIM = 64
SSD_WIDTH = SSD_HEADS * SSD_HEAD_DIM
SSD_GROUPS = 4
SSD_HPG = SSD_HEADS // SSD_GROUPS
SSD_STATE = 128
SSD_CONV = 7
SSD_CHUNK = 128
SSD_CONV_CH = SSD_WIDTH + 2 * SSD_GROUPS * SSD_STATE
ROPE_BASE = 10000.0
DT_WIDTH = 2 * SSD_HEADS
MAIN_SPLITS = (NA_WIDTH, NA_WIDTH, NA_WIDTH, NA_WIDTH, FOUR_WIDTH, FOUR_WIDTH, SSD_CONV_CH, SSD_WIDTH,
               D_MODEL, D_MODEL, D_MODEL)
MAIN_WIDTH = sum(MAIN_SPLITS)
DT_OFFSET = sum(MAIN_SPLITS[:8])
DT_PAD = 128


class ProjCols(NamedTuple):
    k: int
    v: int
    xbc: int


FULL_COLS = ProjCols(k=NA_WIDTH, v=2 * NA_WIDTH, xbc=sum(MAIN_SPLITS[:6]))
KV_XBC_COLS = ProjCols(k=0, v=NA_WIDTH, xbc=2 * NA_WIDTH)

LANES = 128
V7X_VMEM_LIMIT_BYTES = 56 * 1024 * 1024


def _mm_kernel(a_ref, b_ref, o_ref):
    a = a_ref[...].astype(jnp.bfloat16)
    b = b_ref[...].astype(jnp.bfloat16)
    o_ref[...] = jnp.dot(a, b, preferred_element_type=jnp.float32).astype(o_ref.dtype)


def matmul(a, b, layer, *, tm, tn, out_dtype=jnp.float32):
    m, k = a.shape
    _, _, n = b.shape
    assert m % tm == 0 and n % tn == 0, (a.shape, b.shape, tm, tn)
    return pl.pallas_call(
        _mm_kernel,
        grid=(m // tm, n // tn),
        in_specs=[pl.BlockSpec((tm, k), lambda i, j: (i, 0)),
                  pl.BlockSpec((None, k, tn), lambda i, j: (layer, 0, j))],
        out_specs=pl.BlockSpec((tm, tn), lambda i, j: (i, j)),
        out_shape=jax.ShapeDtypeStruct((m, n), out_dtype),
        compiler_params=pltpu.CompilerParams(dimension_semantics=("arbitrary", "arbitrary"),
                                             vmem_limit_bytes=V7X_VMEM_LIMIT_BYTES),
        name="matmul",
    )(a, b)


REPACK_TN = 1024


def _repack_kernel(a_ref, nxt_ref, o_ref, odt_ref):
    j = pl.program_id(1)
    first_below = DT_OFFSET // REPACK_TN

    @pl.when(j < first_below)
    def _():
        o_ref[...] = a_ref[...].astype(o_ref.dtype)

    @pl.when(j >= first_below)
    def _():
        o_ref[...] = jnp.concatenate([a_ref[DT_WIDTH:, :], nxt_ref[...]], axis=0).astype(o_ref.dtype)

    @pl.when(j == first_below)
    def _():
        odt_ref[...] = a_ref[:DT_PAD, :].astype(odt_ref.dtype)


def repack_w_in(w_in):
    depth, d, _ = w_in.shape
    tn = REPACK_TN
    w_t = jnp.swapaxes(w_in, 1, 2)
    return pl.pallas_call(
        _repack_kernel,
        grid=(depth, MAIN_WIDTH // tn),
        in_specs=[pl.BlockSpec((None, tn, d), lambda l, j: (l, j, 0)),
                  pl.BlockSpec((None, DT_WIDTH, d), lambda l, j: (l, (j + 1) * (tn // DT_WIDTH), 0))],
        out_specs=[pl.BlockSpec((None, tn, d), lambda l, j: (l, j, 0)),
                   pl.BlockSpec((None, DT_PAD, d), lambda l, j: (l, 0, 0))],
        out_shape=[jax.ShapeDtypeStruct((depth, MAIN_WIDTH, d), jnp.bfloat16),
                   jax.ShapeDtypeStruct((depth, DT_PAD, d), jnp.bfloat16)],
        compiler_params=pltpu.CompilerParams(dimension_semantics=("arbitrary", "arbitrary"),
                                             vmem_limit_bytes=V7X_VMEM_LIMIT_BYTES),
        name="repack_w_in",
    )(w_t, w_t)


def _inproj_kernel(x_ref, nw_ref, sc_ref, sh_ref, w_ref, wdt_ref, o_ref, dt_ref, h_ref):
    @pl.when(pl.program_id(1) == 0)
    def _():
        x = x_ref[...]
        y = x * lax.rsqrt(jnp.mean(x * x, axis=-1, keepdims=True) + EPS) * nw_ref[...]
        h = (y * (1.0 + sc_ref[0]) + sh_ref[0]).astype(jnp.bfloat16)
        h_ref[...] = h
        dt_ref[...] = lax.dot_general(h, wdt_ref[...], _NT, preferred_element_type=jnp.float32)

    o_ref[...] = lax.dot_general(h_ref[...], w_ref[...], _NT, preferred_element_type=jnp.float32).astype(o_ref.dtype)


def in_projection(x2d, norm_w, scale, shift, w_main_t, layer, w_dt_t, dt_layer, *, rows_per_mod, tm, tn, out_dtype):
    m, d = x2d.shape
    n = w_main_t.shape[1]
    assert m % tm == 0 and n % tn == 0 and rows_per_mod % tm == 0
    per = rows_per_mod // tm
    return pl.pallas_call(
        _inproj_kernel,
        grid=(m // tm, n // tn),
        in_specs=[pl.BlockSpec((tm, d), lambda i, j: (i, 0)),
                  pl.BlockSpec((1, d), lambda i, j: (0, 0)),
                  pl.BlockSpec((1, 1, d), lambda i, j: (i // per, 0, 0)),
                  pl.BlockSpec((1, 1, d), lambda i, j: (i // per, 0, 0)),
                  pl.BlockSpec((None, tn, d), lambda i, j: (layer, j, 0)),
                  pl.BlockSpec((None, DT_PAD, d), lambda i, j: (dt_layer, 0, 0))],
        out_specs=[pl.BlockSpec((tm, tn), lambda i, j: (i, j)),
                   pl.BlockSpec((tm, DT_PAD), lambda i, j: (i, 0))],
        out_shape=[jax.ShapeDtypeStruct((m, n), out_dtype),
                   jax.ShapeDtypeStruct((m, DT_PAD), jnp.float32)],
        scratch_shapes=[pltpu.VMEM((tm, d), jnp.bfloat16)],
        compiler_params=pltpu.CompilerParams(dimension_semantics=("arbitrary", "arbitrary"),
                                             vmem_limit_bytes=V7X_VMEM_LIMIT_BYTES),
        name="in_projection",
    )(x2d, norm_w.reshape(1, d), scale, shift, w_main_t, w_dt_t)


GRID_ROWS = 32
NA_WIN_TOKENS = NA_WIN_R * GRID_W
_NT = (((1,), (1,)), ((), ()))
NA_HEADS_PER_STEP = 2
NA_ROW_UNROLL = 16


def _na_kernel(*refs, with_ctx):
    if with_ctx:
        q_ref, k_ref, v_ref, kc_ref, vc_ref, bias_ref, qc_ref, o_ref, oc_ref = refs
    else:
        q_ref, k_ref, v_ref, kc_ref, vc_ref, bias_ref, o_ref = refs
    bf = jnp.bfloat16
    scale = NA_HEAD_DIM ** -0.5
    scale2 = scale * LOG2E
    nq = NA_ROW_UNROLL * GRID_W

    def lane_tiles(x, op):
        acc = x[:, 0:LANES]
        for l0 in range(LANES, x.shape[1], LANES):
            acc = op(acc, x[:, l0:l0 + LANES])
        return acc

    for hh in range(NA_HEADS_PER_STEP):
        lanes = slice(hh * NA_HEAD_DIM, (hh + 1) * NA_HEAD_DIM)
        kc = kc_ref[0, :, lanes].astype(bf)
        vc = vc_ref[0, :, lanes].astype(bf)

        def rows(rb, carry, hh=hh, lanes=lanes, kc=kc, vc=vc):
            qb0 = pl.multiple_of(rb * nq, nq)
            q_all = q_ref[0, pl.ds(qb0, nq), lanes].astype(bf)
            s_c_all = lax.dot_general(q_all, kc, _NT, preferred_element_type=jnp.float32) * scale2
            k0s, scores, probs = [], [], []
            for i in range(NA_ROW_UNROLL):
                r = rb * NA_ROW_UNROLL + i
                start = jnp.clip(r - NA_WIN_R // 2, 0, GRID_ROWS - NA_WIN_R)
                k0 = pl.multiple_of(start * GRID_W, GRID_W)
                q = q_all[i * GRID_W:(i + 1) * GRID_W]
                kw = k_ref[0, pl.ds(k0, NA_WIN_TOKENS), lanes].astype(bf)
                s_w = (lax.dot_general(q, kw, _NT, preferred_element_type=jnp.float32) * scale2
                       + bias_ref[hh, r - start])
                k0s.append(k0)
                scores.append((s_w, s_c_all[i * GRID_W:(i + 1) * GRID_W]))
            for s_w, s_c in scores:
                m = jnp.max(jnp.maximum(lane_tiles(s_w, jnp.maximum), lane_tiles(s_c, jnp.maximum)),
                            axis=-1, keepdims=True)
                e_w = jnp.exp2(s_w - m)
                e_c = jnp.exp2(s_c - m)
                denom = jnp.sum(lane_tiles(e_w, jnp.add) + lane_tiles(e_c, jnp.add), axis=-1, keepdims=True)
                probs.append((e_w.astype(bf), e_c.astype(bf), denom))
            o_c_all = jnp.dot(jnp.concatenate([p[1] for p in probs], axis=0), vc, preferred_element_type=jnp.float32)
            for i, (k0, (e_w, _, denom)) in enumerate(zip(k0s, probs)):
                vw = v_ref[0, pl.ds(k0, NA_WIN_TOKENS), lanes].astype(bf)
                o = jnp.dot(e_w, vw, preferred_element_type=jnp.float32) + o_c_all[i * GRID_W:(i + 1) * GRID_W]
                o_ref[0, pl.ds(qb0 + i * GRID_W, GRID_W), lanes] = (o / denom).astype(o_ref.dtype)
            return carry

        lax.fori_loop(0, GRID_ROWS // NA_ROW_UNROLL, rows, 0)

        if with_ctx:
            s = lax.dot_general(qc_ref[0, :, lanes].astype(bf), kc, _NT, preferred_element_type=jnp.float32) * scale
            e = jnp.exp(s - jnp.max(s, axis=-1, keepdims=True))
            o = jnp.dot(e.astype(bf), vc, preferred_element_type=jnp.float32)
            oc_ref[0, :, lanes] = (o / jnp.sum(e, axis=-1, keepdims=True)).astype(oc_ref.dtype)


def na_bias_table(rpb):
    qcol = np.arange(GRID_W)[:, None]
    kcol = np.arange(GRID_W)[None, :]
    qstart = np.clip(qcol - NA_WIN_C // 2, 0, GRID_W - NA_WIN_C)
    mask = (kcol >= qstart) & (kcol < qstart + NA_WIN_C)
    dc = np.clip(kcol - qcol + NA_WIN_C - 1, 0, 2 * NA_WIN_C - 2)
    pick_col = (np.arange(2 * NA_WIN_C - 1)[:, None, None] == dc[None]).astype(np.float32)
    rows = jnp.stack([rpb.astype(jnp.float32)[:, NA_WIN_R - 1 - dl:2 * NA_WIN_R - 1 - dl] for dl in range(NA_WIN_R)],
                     axis=1)
    t = jnp.einsum("hdwc,cqk->hdqwk", rows, pick_col, precision=lax.Precision.HIGHEST)
    t = jnp.where(mask[None, None, :, None, :], t * LOG2E, -1e30)
    return t.reshape(rpb.shape[0], NA_WIN_R, GRID_W, NA_WIN_TOKENS)


def na_attention(p_lat, p_ctx, bias, *, with_ctx, out_dtype, ctx_cols=FULL_COLS):
    bsz, seq, _ = p_lat.shape
    ctx_len = p_ctx.shape[1]
    assert seq == GRID_ROWS * GRID_W
    hs = NA_HEADS_PER_STEP
    d, nblk = hs * NA_HEAD_DIM, NA_HEADS // hs
    in_specs = [pl.BlockSpec((1, seq, d), lambda b, i: (b, 0, i)),
                pl.BlockSpec((1, seq, d), lambda b, i: (b, 0, nblk + i)),
                pl.BlockSpec((1, seq, d), lambda b, i: (b, 0, 2 * nblk + i)),
                pl.BlockSpec((1, ctx_len, d), lambda b, i: (b, 0, ctx_cols.k // d + i)),
                pl.BlockSpec((1, ctx_len, d), lambda b, i: (b, 0, ctx_cols.v // d + i)),
                pl.BlockSpec((hs, NA_WIN_R, GRID_W, NA_WIN_TOKENS), lambda b, i: (i, 0, 0, 0))]
    args = [p_lat, p_lat, p_lat, p_ctx, p_ctx, bias]
    out_specs = [pl.BlockSpec((1, seq, d), lambda b, i: (b, 0, i))]
    out_shape = [jax.ShapeDtypeStruct((bsz, seq, NA_WIDTH), out_dtype)]
    if with_ctx:
        in_specs.append(pl.BlockSpec((1, ctx_len, d), lambda b, i: (b, 0, i)))
        args.append(p_ctx)
        out_specs.append(pl.BlockSpec((1, ctx_len, d), lambda b, i: (b, 0, i)))
        out_shape.append(jax.ShapeDtypeStruct((bsz, ctx_len, NA_WIDTH), out_dtype))
    outs = pl.pallas_call(
        partial(_na_kernel, with_ctx=with_ctx),
        grid=(bsz, nblk),
        in_specs=in_specs, out_specs=out_specs, out_shape=out_shape,
        compiler_params=pltpu.CompilerParams(dimension_semantics=("arbitrary", "arbitrary"),
                                             vmem_limit_bytes=V7X_VMEM_LIMIT_BYTES),
        name="na_attention",
    )(*args)
    return (outs[0], outs[1]) if with_ctx else (outs[0], None)


def _dft_cos_sin(n):
    jk = np.outer(np.arange(n), np.arange(n)) % n
    ang = 2.0 * np.pi * jk / n
    return np.cos(ang) / np.sqrt(n), np.sin(ang) / np.sqrt(n)


FOUR_EXT = 8


def _fourier_kernel(u_ref, cc_ref, sc_ref, a_ref, flip_ref, fw_ref, zlo_ref, zhi_ref, o_ref, *, tm):
    bf = jnp.bfloat16
    gd, ext = FOUR_GROUP_DIM, tm + FOUR_EXT
    pos = jnp.dot(a_ref[...], u_ref[0].astype(bf), preferred_element_type=jnp.float32).astype(bf)
    even, odd = [], []
    for g in range(FOUR_GROUPS):
        cols = slice(g * gd, (g + 1) * gd)
        even.append(jnp.dot(pos[:ext, cols], cc_ref[...], preferred_element_type=jnp.float32))
        odd.append(jnp.dot(pos[ext:, cols], sc_ref[...], preferred_element_type=jnp.float32))
    even = jnp.concatenate(even, axis=1)
    odd = jnp.concatenate(odd, axis=1)
    y_lo = (even[:tm] - odd[:tm]).astype(bf)
    y_hi = jnp.dot(flip_ref[...], (even + odd).astype(bf), preferred_element_type=jnp.float32).astype(bf)
    for half, (y, z_ref) in enumerate(((y_lo, zlo_ref), (y_hi, zhi_ref))):
        f = jnp.dot(y, fw_ref[...], preferred_element_type=jnp.float32)
        zf = z_ref[0].astype(jnp.float32)
        o_ref[0, half] = (f * (zf * jax.nn.sigmoid(zf))).astype(o_ref.dtype)


def fourier_branch(p, four_w_bf, layer, *, tm):
    bsz, seq, _ = p.shape
    half = seq // 2
    assert half % tm == 0
    nt2, ext = half // tm, tm + FOUR_EXT
    cc, sc = _dft_cos_sin(FOUR_GROUP_DIM)
    cl, sl = _dft_cos_sin(seq)
    a = np.stack([np.concatenate([cl[i * tm:i * tm + ext], sl[i * tm:i * tm + ext]], axis=0) for i in range(nt2)])
    flip = np.zeros((tm, ext))
    flip[np.arange(tm), tm - np.arange(tm)] = 1.0
    fw, gd = FOUR_WIDTH, FOUR_GROUP_DIM
    bf = jnp.bfloat16
    out = pl.pallas_call(
        partial(_fourier_kernel, tm=tm),
        grid=(bsz, nt2),
        in_specs=[pl.BlockSpec((1, seq, fw), lambda b, i: (b, 0, 4)),
                  pl.BlockSpec((gd, gd), lambda b, i: (0, 0)),
                  pl.BlockSpec((gd, gd), lambda b, i: (0, 0)),
                  pl.BlockSpec((None, 2 * ext, seq), lambda b, i: (i, 0, 0)),
                  pl.BlockSpec((tm, ext), lambda b, i: (0, 0)),
                  pl.BlockSpec((None, fw, fw), lambda b, i: (layer, 0, 0)),
                  pl.BlockSpec((1, tm, fw), lambda b, i: (b, i, 5)),
                  pl.BlockSpec((1, tm, fw), lambda b, i: (b, 2 * nt2 - 1 - i, 5))],
        out_specs=pl.BlockSpec((1, 2, tm, fw), lambda b, i: (b, 0, i, 0)),
        out_shape=jax.ShapeDtypeStruct((bsz, 2, half, fw), bf),
        compiler_params=pltpu.CompilerParams(dimension_semantics=("arbitrary", "arbitrary"),
                                             vmem_limit_bytes=V7X_VMEM_LIMIT_BYTES),
        name="fourier_branch",
    )(p, jnp.asarray(cc, bf), jnp.asarray(sc, bf), jnp.asarray(a, bf), jnp.asarray(flip, bf), four_w_bf, p, p)
    return out.reshape(bsz, seq, fw), nt2


def fourier_row_block(t, nt2):
    return jnp.where(t < nt2, t, 3 * nt2 - 1 - t)


MERGE_TN = 1024


def _merge_kernel(a_ref, zna_ref, gfo_ref, s_ref, zs_ref, nw_ref, wa_ref, wf_ref, ws_ref, *rest):
    n_half = D_MODEL // MERGE_TN
    gate_refs, (o_ref, ap_ref, sp_ref) = rest[:3 * n_half], rest[3 * n_half:]
    bf = jnp.bfloat16
    zna = zna_ref[...].astype(jnp.float32)
    ap_ref[...] = (a_ref[...].astype(jnp.float32) * (zna * jax.nn.sigmoid(zna))).astype(bf)
    zs = zs_ref[...].astype(jnp.float32)
    t = s_ref[...].astype(jnp.float32) * (zs * jax.nn.sigmoid(zs))
    t = t * lax.rsqrt(jnp.mean(t * t, axis=-1, keepdims=True) + EPS) * nw_ref[...]
    sp_ref[...] = t.astype(bf)
    for h in range(n_half):
        cols = slice(h * MERGE_TN, (h + 1) * MERGE_TN)
        ga_ref, gf_ref, gs_ref = gate_refs[h], gate_refs[n_half + h], gate_refs[2 * n_half + h]
        o_a = jnp.dot(ap_ref[...], wa_ref[:, cols], preferred_element_type=jnp.float32)
        o_f = jnp.dot(gfo_ref[...], wf_ref[:, cols], preferred_element_type=jnp.float32)
        o_s = jnp.dot(sp_ref[...], ws_ref[:, cols], preferred_element_type=jnp.float32)
        m = (jax.nn.sigmoid(ga_ref[...].astype(jnp.float32)) * o_a
             + jax.nn.sigmoid(gf_ref[...].astype(jnp.float32)) * o_f
             + jax.nn.sigmoid(gs_ref[...].astype(jnp.float32)) * o_s)
        o_ref[:, cols] = m.astype(o_ref.dtype)


def merge_branches(p2d, a2d, gfo2d, gfo_nt2, s2d, ssd_norm_w, wa, wf, ws, layer, *, tm):
    m = p2d.shape[0]
    w, tn = NA_WIDTH, MERGE_TN
    n_half = D_MODEL // tn
    assert m % tm == 0
    row = lambda i: (i, 0)
    if gfo_nt2 is None:
        gfo_row = row
    else:
        assert p2d.shape[0] % (2 * gfo_nt2 * tm) == 0
        per_seq = 2 * gfo_nt2
        gfo_row = lambda i: ((i // per_seq) * per_seq + fourier_row_block(i % per_seq, gfo_nt2), 0)
    resident = lambda shape, idx: pl.BlockSpec(shape, lambda i: idx, pipeline_mode=pl.Buffered(1))
    w_spec = resident((None, w, D_MODEL), (layer, 0, 0))
    gate0 = DT_OFFSET // tn
    gate_specs = [pl.BlockSpec((tm, tn), partial(lambda i, blk: (i, blk), blk=gate0 + g * n_half + h))
                  for g in range(3) for h in range(n_half)]
    return pl.pallas_call(
        _merge_kernel,
        grid=(m // tm,),
        in_specs=[pl.BlockSpec((tm, w), row),
                  pl.BlockSpec((tm, w), lambda i: (i, 3)),
                  pl.BlockSpec((tm, w), gfo_row),
                  pl.BlockSpec((tm, w), row),
                  pl.BlockSpec((tm, w), lambda i: (i, 8)),
                  resident((1, w), (0, 0)), w_spec, w_spec, w_spec] + gate_specs,
        out_specs=pl.BlockSpec((tm, D_MODEL), row),
        out_shape=jax.ShapeDtypeStruct((m, D_MODEL), jnp.bfloat16),
        scratch_shapes=[pltpu.VMEM((tm, w), jnp.bfloat16), pltpu.VMEM((tm, w), jnp.bfloat16)],
        compiler_params=pltpu.CompilerParams(dimension_semantics=("arbitrary",),
                                             vmem_limit_bytes=V7X_VMEM_LIMIT_BYTES),
        name="merge_branches",
    )(a2d, p2d, gfo2d, s2d, p2d, ssd_norm_w.reshape(1, w), wa, wf, ws, *([p2d] * (3 * n_half)))


def _outproj_kernel(m_ref, w_ref, x_ref, g_ref, *rest, final_norm):
    y = jnp.dot(m_ref[...], w_ref[...], preferred_element_type=jnp.float32)
    r = x_ref[...] + g_ref[0] * y
    if final_norm:
        nw_ref, o_ref = rest
        o_ref[...] = r * lax.rsqrt(jnp.mean(r * r, axis=-1, keepdims=True) + EPS) * nw_ref[...]
    else:
        rest[0][...] = r


def out_projection(m2d, w_out_bf, layer, x2d, gate, *, rows_per_mod, tm, tn, final_norm_w=None):
    m, d = x2d.shape
    assert m % tm == 0 and d % tn == 0 and rows_per_mod % tm == 0
    per = rows_per_mod // tm
    in_specs = [pl.BlockSpec((tm, d), lambda i, j: (i, 0)),
                pl.BlockSpec((None, d, tn), lambda i, j: (layer, 0, j)),
                pl.BlockSpec((tm, tn), lambda i, j: (i, j)),
                pl.BlockSpec((1, 1, tn), lambda i, j: (i // per, 0, j))]
    args = [m2d, w_out_bf, x2d, gate]
    if final_norm_w is not None:
        assert tn == d
        in_specs.append(pl.BlockSpec((1, d), lambda i, j: (0, 0)))
        args.append(final_norm_w.astype(jnp.float32).reshape(1, d))
    return pl.pallas_call(
        partial(_outproj_kernel, final_norm=final_norm_w is not None),
        grid=(m // tm, d // tn),
        in_specs=in_specs,
        out_specs=pl.BlockSpec((tm, tn), lambda i, j: (i, j)),
        out_shape=jax.ShapeDtypeStruct((m, d), jnp.float32),
        compiler_params=pltpu.CompilerParams(dimension_semantics=("arbitrary", "arbitrary"),
                                             vmem_limit_bytes=V7X_VMEM_LIMIT_BYTES),
        name="out_projection",
    )(*args)


SSD_GW = SSD_HPG * SSD_HEAD_DIM
CONV_PAD = 8
SSD_CHUNK_UNROLL = 2
SSD_CONV_UNROLL = 4
SSD_C_UNROLL = 8
SSD_A_UNROLL = 6


def _split3_bf16(a):
    rnd = lambda v: v.astype(jnp.bfloat16).astype(jnp.float32)
    a1 = rnd(a)
    a2 = rnd(a - a1)
    a3 = rnd((a - a1) - a2)
    return a1, a2, a3


def _head_rows(rows, r0, n):
    return jnp.concatenate([jnp.broadcast_to(rows[r0 + k:r0 + k + 1, :], (SSD_HEAD_DIM, n))
                            for k in range(SSD_HPG)], axis=0)


def _ssd_kernel(xl_ref, bl_ref, cl_ref, xc_ref, bc_ref, cc_ref, dtl_ref, dtc_ref, bias_ref, alog_ref, dsk_ref,
                wx_ref, wb_ref, wc_ref, bx_ref, bb_ref, bcb_ref, cos_ref, sin_ref,
                yl_ref, yc_ref,
                padx, padb, padc, xt_s, b_s, c_s, dtt_s, row_s, dec_s, yt_s, p_s, sp_s, st_s, *, seq, ctx_len):
    f32, bf = jnp.float32, jnp.bfloat16
    t = SSD_CHUNK
    nc_c, nc_l = ctx_len // t, seq // t
    nc = nc_c + nc_l

    lane = lax.broadcasted_iota(jnp.int32, (t, LANES), 1)
    first_half = (lane % 64) < 32

    def conv_tile(pad, w_ref, b_ref, t0, l0):
        acc = jnp.broadcast_to(b_ref[:, l0:l0 + LANES], (t, LANES))
        for j in range(SSD_CONV):
            off = CONV_PAD - SSD_CONV // 2 + j
            acc = acc + pad[pl.ds(t0 + off, t), :] * w_ref[j:j + 1, l0:l0 + LANES]
        return acc * jax.nn.sigmoid(acc)

    def rope(u, t0):
        swapped = jnp.where(first_half, pltpu.roll(u, 96, axis=1), pltpu.roll(u, 32, axis=1))
        return u * cos_ref[pl.ds(t0, t), :] + swapped * sin_ref[pl.ds(t0, t), :]

    def conv_seq(x_ref, bm_ref, cm_ref, n, base, use_rope):
        slabs = ([(padx.at[h], x_ref, h * LANES) for h in range(SSD_GW // LANES)]
                 + [(padb, bm_ref, 0), (padc, cm_ref, 0)])
        for pad, src, l0 in slabs:
            pad[0:CONV_PAD, :] = jnp.zeros((CONV_PAD, LANES), f32)
            pad[CONV_PAD + n:2 * CONV_PAD + n, :] = jnp.zeros((CONV_PAD, LANES), f32)
            pad[CONV_PAD:CONV_PAD + n, :] = src[0, :, l0:l0 + LANES].astype(f32)

        unroll = min(SSD_CONV_UNROLL, n // t)

        def chunks(i, carry):
            for j in range(unroll):
                ci = i * unroll + j
                t0 = pl.multiple_of(ci * t, t)
                r0 = pl.multiple_of(base + ci * t, t)
                for h in range(SSD_GW // LANES):
                    l0 = h * LANES
                    xt_s[base // t + ci, l0:l0 + LANES, :] = conv_tile(padx.at[h], wx_ref, bx_ref, t0, l0).T
                ub = conv_tile(padb, wb_ref, bb_ref, t0, 0)
                uc = conv_tile(padc, wc_ref, bcb_ref, t0, 0)
                if use_rope:
                    ub, uc = rope(ub, t0), rope(uc, t0)
                b_s[pl.ds(r0, t), :] = ub.astype(bf)
                c_s[pl.ds(r0, t), :] = uc.astype(bf)
            return carry

        lax.fori_loop(0, n // t // unroll, chunks, 0)

    conv_seq(xc_ref, bc_ref, cc_ref, ctx_len, 0, False)
    conv_seq(xl_ref, bl_ref, cl_ref, seq, ctx_len, True)

    i0 = lax.broadcasted_iota(jnp.int32, (t, t), 0)
    i1 = lax.broadcasted_iota(jnp.int32, (t, t), 1)
    tri_f = (i0 <= i1).astype(bf)
    tri_r = (i0 >= i1).astype(bf)
    nr = 2 * SSD_HPG
    fwd_row = (lax.broadcasted_iota(jnp.int32, (nc * nr, 1), 0) % nr) < SSD_HPG
    g4 = pl.program_id(1) * SSD_HPG
    raw = []
    for c in range(nc):
        src, c0 = (dtc_ref, c) if c < nc_c else (dtl_ref, c - nc_c)
        dtt_s[c] = src[0, c0 * t:(c0 + 1) * t, :].T
        raw += [dtt_s[c, pl.ds(g4, SSD_HPG), :], dtt_s[c, pl.ds(SSD_HEADS + g4, SSD_HPG), :]]
    x = jnp.concatenate(raw, axis=0) + jnp.tile(bias_ref[0], (nc, 1))
    dt = jnp.maximum(x, 0.0) + jnp.log1p(jnp.exp(-jnp.abs(x)))
    pieces = _split3_bf16(dt * jnp.tile(-jnp.exp(alog_ref[0]), (nc, 1)))
    pf = [jnp.dot(p.astype(bf), tri_f, preferred_element_type=f32) for p in pieces]
    pr = [jnp.dot(p.astype(bf), tri_r, preferred_element_type=f32) for p in pieces]
    cs = jnp.where(fwd_row, (pf[0] + pf[1]) + pf[2], (pr[0] + pr[1]) + pr[2])
    tot = jnp.where(fwd_row, cs[:, t - 1:t], cs[:, 0:1])
    e_in = jnp.exp(cs)
    e_out = dt * jnp.exp(tot - cs)
    u = cs - jnp.log(dt)
    decb = jnp.broadcast_to(jnp.exp(tot), (nc * nr, LANES))
    for c in range(nc):
        sl = slice(c * nr, (c + 1) * nr)
        row_s[c] = jnp.concatenate([cs[sl], dt[sl], e_in[sl], e_out[sl], u[sl]], axis=0)
        dec_s[c] = decb[sl]

    lower = i1 <= i0
    lower_t = i0 <= i1
    diag = i1 == i0

    def chunks_a(i, carry):
        ids = [i * SSD_A_UNROLL + j for j in range(SSD_A_UNROLL)]
        r0s = [pl.multiple_of(c * t, t) for c in ids]
        cms = [c_s[pl.ds(r0, t), :] for r0 in r0s]
        bms = [b_s[pl.ds(r0, t), :] for r0 in r0s]
        gs = [lax.dot_general(cm, bm, _NT, preferred_element_type=f32) for cm, bm in zip(cms, bms)]
        for c, r0, bm, g in zip(ids, r0s, bms, gs):
            rows = row_s[c]
            xt = xt_s[c]
            xt_bf = xt.astype(bf)
            parts = []
            for k in range(SSD_HPG):
                kr = SSD_HPG + k
                cs_l = jnp.where(lower_t, rows[k:k + 1, :], rows[kr:kr + 1, :]).T
                w = jnp.exp(cs_l - jnp.where(lower, rows[32 + k:33 + k, :], rows[32 + kr:33 + kr, :]))
                mk = (g * (w + jnp.where(diag, rows[8 + kr:9 + kr, :], 0.0))).astype(bf)
                xk = xt_bf[k * SSD_HEAD_DIM:(k + 1) * SSD_HEAD_DIM, :]
                parts.append(lax.dot_general(xk, mk, _NT, preferred_element_type=f32))
            yt_s[c] = dsk_ref[0] * xt + jnp.concatenate(parts, axis=0)
            for d in range(2):
                wt = (xt * _head_rows(rows, 24 + SSD_HPG * d, t)).astype(bf)
                p_s[d, c] = jnp.dot(wt, bm, preferred_element_type=f32)
        return carry

    lax.fori_loop(0, nc // SSD_A_UNROLL, chunks_a, 0)

    st_s[...] = jnp.zeros(st_s.shape, f32)

    def state_step(c, d):
        st = st_s[d]
        sp_s[d, c] = st.astype(bf)
        st_s[d] = st * _head_rows(dec_s[c], SSD_HPG * d, SSD_STATE) + p_s[d, c]

    def ctx_step(i, carry):
        state_step(i, 0)
        state_step(nc_c - 1 - i, 1)
        return carry

    def lat_step(i, carry):
        state_step(nc_c + i, 0)
        state_step(nc - 1 - i, 1)
        return carry

    lax.fori_loop(0, nc_c, ctx_step, 0)
    lax.fori_loop(0, nc_l, lat_step, 0)

    def chunks_c(first, out_ref, o_first, count):
        ids = [first + u for u in range(count)]
        cms = [c_s[pl.ds(pl.multiple_of(c * t, t), t), :] for c in ids]
        offs = [[lax.dot_general(sp_s[d, c], cm, _NT, preferred_element_type=f32) for d in range(2)]
                for c, cm in zip(ids, cms)]
        for u, (c, off) in enumerate(zip(ids, offs)):
            rows = row_s[c]
            yt = yt_s[c]
            for d in range(2):
                yt = yt + off[d] * _head_rows(rows, 16 + SSD_HPG * d, t)
            for l0 in range(0, SSD_GW, LANES):
                out_ref[0, pl.ds(pl.multiple_of((o_first + u) * t, t), t), l0:l0 + LANES] = yt[l0:l0 + LANES, :].T

    def ctx_out(i, carry):
        chunks_c(i * SSD_CHUNK_UNROLL, yc_ref, i * SSD_CHUNK_UNROLL, SSD_CHUNK_UNROLL)
        return carry

    def lat_out(i, carry):
        chunks_c(nc_c + i * SSD_C_UNROLL, yl_ref, i * SSD_C_UNROLL, SSD_C_UNROLL)
        return carry

    lax.fori_loop(0, nc_c // SSD_CHUNK_UNROLL, ctx_out, 0)
    lax.fori_loop(0, nc_l // SSD_C_UNROLL, lat_out, 0)


def _rope_tables(seq):
    quarter = SSD_STATE // 4
    inv = ROPE_BASE ** (-np.arange(quarter, dtype=np.float64) / quarter)
    pos = np.arange(seq)
    ang_r = (pos // GRID_W)[:, None] * inv
    ang_c = (pos % GRID_W)[:, None] * inv
    cos = np.concatenate([np.cos(ang_r)] * 2 + [np.cos(ang_c)] * 2, axis=1)
    sin = np.concatenate([-np.sin(ang_r), np.sin(ang_r), -np.sin(ang_c), np.sin(ang_c)], axis=1)
    return jnp.asarray(cos, jnp.float32), jnp.asarray(sin, jnp.float32)


def ssd_mixer(p_lat, p_ctx, dt_lat, dt_ctx, conv_w, conv_b, dt_bias, a_log, d_skip, ctx_cols=FULL_COLS):
    bsz, seq, _ = p_lat.shape
    ctx_len = p_ctx.shape[1]
    t, g, hpg = SSD_CHUNK, SSD_GROUPS, SSD_HPG
    assert seq % t == 0 and ctx_len % t == 0
    nc = (seq + ctx_len) // t
    per_row = lambda v: jnp.broadcast_to(
        v.astype(jnp.float32).reshape(2, g, hpg).transpose(1, 0, 2).reshape(g, 2 * hpg, 1), (g, 2 * hpg, LANES))
    dsk = jnp.broadcast_to(jnp.repeat(d_skip.astype(jnp.float32).reshape(g, hpg), SSD_HEAD_DIM, axis=1)[:, :, None],
                           (g, SSD_GW, LANES))
    cos, sin = _rope_tables(seq)
    x0 = sum(MAIN_SPLITS[:6]) // SSD_GW
    b0 = (sum(MAIN_SPLITS[:6]) + SSD_WIDTH) // SSD_STATE
    c0 = b0 + g
    cw = conv_w.astype(jnp.float32)
    cb = conv_b.astype(jnp.float32).reshape(1, SSD_CONV_CH)
    wb0 = SSD_WIDTH // SSD_STATE
    ltot = seq + ctx_len
    f32, bf = jnp.float32, jnp.bfloat16
    y_lat, y_ctx = pl.pallas_call(
        partial(_ssd_kernel, seq=seq, ctx_len=ctx_len),
        grid=(bsz, g),
        in_specs=[pl.BlockSpec((1, seq, SSD_GW), lambda b, i: (b, 0, x0 + i)),
                  pl.BlockSpec((1, seq, SSD_STATE), lambda b, i: (b, 0, b0 + i)),
                  pl.BlockSpec((1, seq, SSD_STATE), lambda b, i: (b, 0, c0 + i)),
                  pl.BlockSpec((1, ctx_len, SSD_GW), lambda b, i: (b, 0, ctx_cols.xbc // SSD_GW + i)),
                  pl.BlockSpec((1, ctx_len, SSD_STATE), lambda b, i: (b, 0, (ctx_cols.xbc + SSD_WIDTH) // SSD_STATE + i)),
                  pl.BlockSpec((1, ctx_len, SSD_STATE), lambda b, i: (b, 0, (ctx_cols.xbc + SSD_WIDTH) // SSD_STATE + g + i)),
                  pl.BlockSpec((1, seq, DT_PAD), lambda b, i: (b, 0, 0)),
                  pl.BlockSpec((1, ctx_len, DT_PAD), lambda b, i: (b, 0, 0)),
                  pl.BlockSpec((1, 2 * hpg, LANES), lambda b, i: (i, 0, 0)),
                  pl.BlockSpec((1, 2 * hpg, LANES), lambda b, i: (i, 0, 0)),
                  pl.BlockSpec((1, SSD_GW, LANES), lambda b, i: (i, 0, 0)),
                  pl.BlockSpec((SSD_CONV, SSD_GW), lambda b, i: (0, i)),
                  pl.BlockSpec((SSD_CONV, SSD_STATE), lambda b, i: (0, wb0 + i)),
                  pl.BlockSpec((SSD_CONV, SSD_STATE), lambda b, i: (0, wb0 + g + i)),
                  pl.BlockSpec((1, SSD_GW), lambda b, i: (0, i)),
                  pl.BlockSpec((1, SSD_STATE), lambda b, i: (0, wb0 + i)),
                  pl.BlockSpec((1, SSD_STATE), lambda b, i: (0, wb0 + g + i)),
                  pl.BlockSpec((seq, SSD_STATE), lambda b, i: (0, 0)),
                  pl.BlockSpec((seq, SSD_STATE), lambda b, i: (0, 0))],
        out_specs=[pl.BlockSpec((1, seq, SSD_GW), lambda b, i: (b, 0, i)),
                   pl.BlockSpec((1, ctx_len, SSD_GW), lambda b, i: (b, 0, i))],
        out_shape=[jax.ShapeDtypeStruct((bsz, seq, SSD_WIDTH), f32),
                   jax.ShapeDtypeStruct((bsz, ctx_len, SSD_WIDTH), f32)],
        scratch_shapes=[pltpu.VMEM((SSD_GW // LANES, seq + 2 * CONV_PAD, LANES), f32),
                        pltpu.VMEM((seq + 2 * CONV_PAD, SSD_STATE), f32),
                        pltpu.VMEM((seq + 2 * CONV_PAD, SSD_STATE), f32),
                        pltpu.VMEM((nc, SSD_GW, t), f32),
                        pltpu.VMEM((ltot, SSD_STATE), bf),
                        pltpu.VMEM((ltot, SSD_STATE), bf),
                        pltpu.VMEM((nc, DT_PAD, t), f32),
                        pltpu.VMEM((nc, 5 * 2 * hpg, t), f32),
                        pltpu.VMEM((nc, 2 * hpg, LANES), f32),
                        pltpu.VMEM((nc, SSD_GW, t), f32),
                        pltpu.VMEM((2, nc, SSD_GW, SSD_STATE), f32),
                        pltpu.VMEM((2, nc, SSD_GW, SSD_STATE), bf),
                        pltpu.VMEM((2, SSD_GW, SSD_STATE), f32)],
        compiler_params=pltpu.CompilerParams(dimension_semantics=("arbitrary", "arbitrary"),
                                             vmem_limit_bytes=V7X_VMEM_LIMIT_BYTES),
        name="ssd_mixer",
    )(p_lat, p_lat, p_lat, p_ctx, p_ctx, p_ctx, dt_lat, dt_ctx, per_row(dt_bias), per_row(a_log), dsk,
      cw, cw, cw, cb, cb, cb, cos, sin)
    return y_ctx, y_lat


INPROJ_TM = 1024
INPROJ_TN = 2560
ADA_TN = 1024
BRANCH_TM = 512
CTX_OUT_TILE = 1024
ADA_ROWS = 16


def _layer(xc, xl, c, c_ctx, w_ada_all, layer, b_ada, norm_w, w_main_all, w_dt_all, rpb, four_wb, conv_w, conv_b, dt_bias,
           a_log, d_skip, ssd_norm_w, wb_nab, wb_fourb, wb_ssdb, w_outb, update_ctx, final_norm_w):
    bsz, seq, d = xl.shape
    ctx_len = xc.shape[1]
    bf = jnp.bfloat16
    assert bsz < ADA_ROWS

    cc = jnp.concatenate([c, c_ctx[None], jnp.zeros((ADA_ROWS - bsz - 1, d), c.dtype)], axis=0)
    mod = matmul(jax.nn.silu(cc), w_ada_all, layer, tm=ADA_ROWS, tn=ADA_TN) + b_ada
    sh, sc, gt = jnp.split(mod, 3, axis=-1)
    sh_l, sc_l, g_l = sh[:bsz], sc[:bsz], gt[:bsz]
    sh_c, sc_c, g_c = sh[bsz:bsz + 1], sc[bsz:bsz + 1], gt[bsz]

    xl2d = xl.reshape(bsz * seq, d)
    xc2d = xc.reshape(bsz * ctx_len, d)
    pl2d, dt_l = in_projection(xl2d, norm_w, sc_l[:, None], sh_l[:, None], w_main_all, layer, w_dt_all, layer,
                               rows_per_mod=seq, tm=INPROJ_TM, tn=INPROJ_TN, out_dtype=bf)
    if update_ctx:
        ctx_cols, w_ctx, ctx_layer, ctx_tn = FULL_COLS, w_main_all, layer, INPROJ_TN
    else:
        ctx_cols, ctx_layer = KV_XBC_COLS, 0
        w_ctx = jnp.concatenate([w_main_all[layer, FULL_COLS.k:FULL_COLS.v + NA_WIDTH],
                                 w_main_all[layer, FULL_COLS.xbc:FULL_COLS.xbc + SSD_CONV_CH]], axis=0)[None]
        ctx_tn = w_ctx.shape[1] // 2
    pc2d, dt_c = in_projection(xc2d, norm_w, sc_c[:, None], sh_c[:, None], w_ctx, ctx_layer, w_dt_all, layer,
                               rows_per_mod=bsz * ctx_len, tm=INPROJ_TM, tn=ctx_tn, out_dtype=bf)
    p_l = pl2d.reshape(bsz, seq, MAIN_WIDTH)
    p_c = pc2d.reshape(bsz, ctx_len, w_ctx.shape[1])
    dt_l = dt_l.reshape(bsz, seq, DT_PAD)
    dt_c = dt_c.reshape(bsz, ctx_len, DT_PAD)

    a_l, a_c = na_attention(p_l, p_c, na_bias_table(rpb), with_ctx=update_ctx, out_dtype=bf, ctx_cols=ctx_cols)
    gfo_l, nt2_l = fourier_branch(p_l, four_wb, layer, tm=BRANCH_TM)
    s_c, s_l = ssd_mixer(p_l, p_c, dt_l, dt_c, conv_w, conv_b, dt_bias, a_log, d_skip, ctx_cols=ctx_cols)
    m_l = merge_branches(pl2d, a_l.reshape(bsz * seq, NA_WIDTH), gfo_l.reshape(bsz * seq, FOUR_WIDTH), nt2_l,
                         s_l.reshape(bsz * seq, SSD_WIDTH), ssd_norm_w, wb_nab, wb_fourb, wb_ssdb, layer, tm=BRANCH_TM)
    xl_new = out_projection(m_l, w_outb, layer, xl2d, g_l[:, None], rows_per_mod=seq, tm=BRANCH_TM, tn=d,
                            final_norm_w=final_norm_w)
    xl_new = xl_new.reshape(bsz, seq, d)
    if update_ctx:
        gfo_c, nt2_c = fourier_branch(p_c, four_wb, layer, tm=ctx_len // 2)
        assert nt2_c == 1
        m_c = merge_branches(pc2d, a_c.reshape(bsz * ctx_len, NA_WIDTH), gfo_c.reshape(bsz * ctx_len, FOUR_WIDTH), None,
                             s_c.reshape(bsz * ctx_len, SSD_WIDTH), ssd_norm_w, wb_nab, wb_fourb, wb_ssdb, layer, tm=BRANCH_TM)
        xc = out_projection(m_c, w_outb, layer, xc2d, g_c[None, None], rows_per_mod=bsz * ctx_len, tm=CTX_OUT_TILE, tn=CTX_OUT_TILE)
        xc = xc.reshape(bsz, ctx_len, d)
    return xc, xl_new


def kernel(x, c, ctx, c_ctx, w_ada, b_ada, norm_w, w_in, na_rpb, four_w, ssd_conv_w, ssd_conv_b, ssd_dt_bias,
           ssd_a_log, ssd_d, ssd_norm_w, wb_na, wb_four, wb_ssd, w_out, final_norm_w):
    xc, xl = ctx, x
    w_main_all, w_dt_all = repack_w_in(w_in)
    four_wb, wb_nab, wb_fourb, wb_ssdb, w_outb = (t.astype(jnp.bfloat16) for t in (four_w, wb_na, wb_four, wb_ssd, w_out))
    for l in range(DEPTH):
        xc, xl = _layer(xc, xl, c, c_ctx, w_ada, l, b_ada[l], norm_w[l], w_main_all, w_dt_all, na_rpb[l], four_wb,
                        ssd_conv_w[l], ssd_conv_b[l], ssd_dt_bias[l], ssd_a_log[l], ssd_d[l], ssd_norm_w[l],
                        wb_nab, wb_fourb, wb_ssdb, w_outb, update_ctx=(l < DEPTH - 1),
                        final_norm_w=final_norm_w if l == DEPTH - 1 else None)
    return xl
```

```python
import math
from functools import partial
from typing import NamedTuple

import jax
import jax.numpy as jnp
import numpy as np
from jax import lax
from jax.experimental import pallas as pl
from jax.experimental.pallas import tpu as pltpu

D_MODEL = 2048
DEPTH = 2
GRID_W = 64
EPS = 1e-6
NA_HEADS = 8
NA_HEAD_DIM = 128
NA_WIDTH = NA_HEADS * NA_HEAD_DIM
NA_WIN_R = 8
NA_WIN_C = 16
LOG2E = math.log2(math.e)
FOUR_GROUPS = 4
FOUR_GROUP_DIM = 256
FOUR_WIDTH = FOUR_GROUPS * FOUR_GROUP_DIM
SSD_HEADS = 16
SSD_HEAD_DIM = 64
SSD_WIDTH = SSD_HEADS * SSD_HEAD_DIM
SSD_GROUPS = 4
SSD_HPG = SSD_HEADS // SSD_GROUPS
SSD_STATE = 128
SSD_CONV = 7
SSD_CHUNK = 128
SSD_CONV_CH = SSD_WIDTH + 2 * SSD_GROUPS * SSD_STATE
ROPE_BASE = 10000.0
DT_WIDTH = 2 * SSD_HEADS
MAIN_SPLITS = (NA_WIDTH, NA_WIDTH, NA_WIDTH, NA_WIDTH, FOUR_WIDTH, FOUR_WIDTH, SSD_CONV_CH, SSD_WIDTH,
               D_MODEL, D_MODEL, D_MODEL)
MAIN_WIDTH = sum(MAIN_SPLITS)
DT_OFFSET = sum(MAIN_SPLITS[:8])
DT_PAD = 128


class ProjCols(NamedTuple):
    k: int
    v: int
    xbc: int


FULL_COLS = ProjCols(k=NA_WIDTH, v=2 * NA_WIDTH, xbc=sum(MAIN_SPLITS[:6]))
KV_XBC_COLS = ProjCols(k=0, v=NA_WIDTH, xbc=2 * NA_WIDTH)

LANES = 128
V7X_VMEM_LIMIT_BYTES = 56 * 1024 * 1024


def _mm_kernel(a_ref, b_ref, o_ref):
    a = a_ref[...].astype(jnp.bfloat16)
    b = b_ref[...].astype(jnp.bfloat16)
    o_ref[...] = jnp.dot(a, b, preferred_element_type=jnp.float32).astype(o_ref.dtype)


def matmul(a, b, layer, *, tm, tn, out_dtype=jnp.float32):
    m, k = a.shape
    _, _, n = b.shape
    assert m % tm == 0 and n % tn == 0, (a.shape, b.shape, tm, tn)
    return pl.pallas_call(
        _mm_kernel,
        grid=(m // tm, n // tn),
        in_specs=[pl.BlockSpec((tm, k), lambda i, j: (i, 0)),
                  pl.BlockSpec((None, k, tn), lambda i, j: (layer, 0, j))],
        out_specs=pl.BlockSpec((tm, tn), lambda i, j: (i, j)),
        out_shape=jax.ShapeDtypeStruct((m, n), out_dtype),
        compiler_params=pltpu.CompilerParams(dimension_semantics=("arbitrary", "arbitrary"),
                                             vmem_limit_bytes=V7X_VMEM_LIMIT_BYTES),
        name="matmul",
    )(a, b)


REPACK_TN = 1024


def _repack_kernel(a_ref, nxt_ref, o_ref, odt_ref):
    j = pl.program_id(1)
    first_below = DT_OFFSET // REPACK_TN

    @pl.when(j < first_below)
    def _():
        o_ref[...] = a_ref[...].astype(o_ref.dtype)

    @pl.when(j >= first_below)
    def _():
        o_ref[...] = jnp.concatenate([a_ref[DT_WIDTH:, :], nxt_ref[...]], axis=0).astype(o_ref.dtype)

    @pl.when(j == first_below)
    def _():
        odt_ref[...] = a_ref[:DT_PAD, :].astype(odt_ref.dtype)


def repack_w_in(w_in):
    depth, d, _ = w_in.shape
    tn = REPACK_TN
    w_t = jnp.swapaxes(w_in, 1, 2)
    return pl.pallas_call(
        _repack_kernel,
        grid=(depth, MAIN_WIDTH // tn),
        in_specs=[pl.BlockSpec((None, tn, d), lambda l, j: (l, j, 0)),
                  pl.BlockSpec((None, DT_WIDTH, d), lambda l, j: (l, (j + 1) * (tn // DT_WIDTH), 0))],
        out_specs=[pl.BlockSpec((None, tn, d), lambda l, j: (l, j, 0)),
                   pl.BlockSpec((None, DT_PAD, d), lambda l, j: (l, 0, 0))],
        out_shape=[jax.ShapeDtypeStruct((depth, MAIN_WIDTH, d), jnp.bfloat16),
                   jax.ShapeDtypeStruct((depth, DT_PAD, d), jnp.bfloat16)],
        compiler_params=pltpu.CompilerParams(dimension_semantics=("arbitrary", "arbitrary"),
                                             vmem_limit_bytes=V7X_VMEM_LIMIT_BYTES),
        name="repack_w_in",
    )(w_t, w_t)


def _inproj_kernel(x_ref, nw_ref, sc_ref, sh_ref, w_ref, wdt_ref, o_ref, dt_ref, h_ref):
    @pl.when(pl.program_id(1) == 0)
    def _():
        x = x_ref[...]
        y = x * lax.rsqrt(jnp.mean(x * x, axis=-1, keepdims=True) + EPS) * nw_ref[...]
        h = (y * (1.0 + sc_ref[0]) + sh_ref[0]).astype(jnp.bfloat16)
        h_ref[...] = h
        dt_ref[...] = lax.dot_general(h, wdt_ref[...], _NT, preferred_element_type=jnp.float32)

    o_ref[...] = lax.dot_general(h_ref[...], w_ref[...], _NT, preferred_element_type=jnp.float32).astype(o_ref.dtype)


def in_projection(x2d, norm_w, scale, shift, w_main_t, layer, w_dt_t, dt_layer, *, rows_per_mod, tm, tn, out_dtype):
    m, d = x2d.shape
    n = w_main_t.shape[1]
    assert m % tm == 0 and n % tn == 0 and rows_per_mod % tm == 0
    per = rows_per_mod // tm
    return pl.pallas_call(
        _inproj_kernel,
        grid=(m // tm, n // tn),
        in_specs=[pl.BlockSpec((tm, d), lambda i, j: (i, 0)),
                  pl.BlockSpec((1, d), lambda i, j: (0, 0)),
                  pl.BlockSpec((1, 1, d), lambda i, j: (i // per, 0, 0)),
                  pl.BlockSpec((1, 1, d), lambda i, j: (i // per, 0, 0)),
                  pl.BlockSpec((None, tn, d), lambda i, j: (layer, j, 0)),
                  pl.BlockSpec((None, DT_PAD, d), lambda i, j: (dt_layer, 0, 0))],
        out_specs=[pl.BlockSpec((tm, tn), lambda i, j: (i, j)),
                   pl.BlockSpec((tm, DT_PAD), lambda i, j: (i, 0))],
        out_shape=[jax.ShapeDtypeStruct((m, n), out_dtype),
                   jax.ShapeDtypeStruct((m, DT_PAD), jnp.float32)],
        scratch_shapes=[pltpu.VMEM((tm, d), jnp.bfloat16)],
        compiler_params=pltpu.CompilerParams(dimension_semantics=("arbitrary", "arbitrary"),
                                             vmem_limit_bytes=V7X_VMEM_LIMIT_BYTES),
        name="in_projection",
    )(x2d, norm_w.reshape(1, d), scale, shift, w_main_t, w_dt_t)


GRID_ROWS = 32
NA_WIN_TOKENS = NA_WIN_R * GRID_W
_NT = (((1,), (1,)), ((), ()))
NA_HEADS_PER_STEP = 2
NA_ROW_UNROLL = 16


def _na_kernel(*refs, with_ctx):
    if with_ctx:
        q_ref, k_ref, v_ref, kc_ref, vc_ref, bias_ref, qc_ref, o_ref, oc_ref, bias_s = refs
    else:
        q_ref, k_ref, v_ref, kc_ref, vc_ref, bias_ref, o_ref, bias_s = refs
    bf = jnp.bfloat16

    @pl.when(pl.program_id(1) == 0)
    def _():
        for hh in range(NA_HEADS_PER_STEP):
            for dl in range(NA_WIN_R):
                for pr in range(NA_WIN_R // 2):
                    pair = bias_ref[hh, dl, 2 * pr] + pltpu.roll(bias_ref[hh, dl, 2 * pr + 1], GRID_W, axis=1)
                    bias_s[hh, dl, :, pr * LANES:(pr + 1) * LANES] = pair

    scale = NA_HEAD_DIM ** -0.5
    scale2 = scale * LOG2E
    nq = NA_ROW_UNROLL * GRID_W

    def lane_tiles(x, op):
        acc = x[:, 0:LANES]
        for l0 in range(LANES, x.shape[1], LANES):
            acc = op(acc, x[:, l0:l0 + LANES])
        return acc

    for hh in range(NA_HEADS_PER_STEP):
        lanes = slice(hh * NA_HEAD_DIM, (hh + 1) * NA_HEAD_DIM)
        kc = kc_ref[0, :, lanes].astype(bf)
        vc = vc_ref[0, :, lanes].astype(bf)

        def rows(rb, carry, hh=hh, lanes=lanes, kc=kc, vc=vc):
            qb0 = pl.multiple_of(rb * nq, nq)
            q_all = q_ref[0, pl.ds(qb0, nq), lanes].astype(bf)
            s_c_all = lax.dot_general(q_all, kc, _NT, preferred_element_type=jnp.float32) * scale2
            k0s, scores, probs = [], [], []
            for i in range(NA_ROW_UNROLL):
                r = rb * NA_ROW_UNROLL + i
                start = jnp.clip(r - NA_WIN_R // 2, 0, GRID_ROWS - NA_WIN_R)
                k0 = pl.multiple_of(start * GRID_W, GRID_W)
                q = q_all[i * GRID_W:(i + 1) * GRID_W]
                kw = k_ref[0, pl.ds(k0, NA_WIN_TOKENS), lanes].astype(bf)
                s_w = (lax.dot_general(q, kw, _NT, preferred_element_type=jnp.float32) * scale2
                       + bias_s[hh, r - start])
                k0s.append(k0)
                scores.append((s_w, s_c_all[i * GRID_W:(i + 1) * GRID_W]))
            for s_w, s_c in scores:
                m = jnp.max(jnp.maximum(lane_tiles(s_w, jnp.maximum), lane_tiles(s_c, jnp.maximum)),
                            axis=-1, keepdims=True)
                e_w = jnp.exp2(s_w - m)
                e_c = jnp.exp2(s_c - m)
                denom = jnp.sum(lane_tiles(e_w, jnp.add) + lane_tiles(e_c, jnp.add), axis=-1, keepdims=True)
                probs.append((e_w.astype(bf), e_c.astype(bf), denom))
            o_c_all = jnp.dot(jnp.concatenate([p[1] for p in probs], axis=0), vc, preferred_element_type=jnp.float32)
            for i, (k0, (e_w, _, denom)) in enumerate(zip(k0s, probs)):
                vw = v_ref[0, pl.ds(k0, NA_WIN_TOKENS), lanes].astype(bf)
                o = jnp.dot(e_w, vw, preferred_element_type=jnp.float32) + o_c_all[i * GRID_W:(i + 1) * GRID_W]
                o_ref[0, pl.ds(qb0 + i * GRID_W, GRID_W), lanes] = (o / denom).astype(o_ref.dtype)
            return carry

        lax.fori_loop(0, GRID_ROWS // NA_ROW_UNROLL, rows, 0)

        if with_ctx:
            s = lax.dot_general(qc_ref[0, :, lanes].astype(bf), kc, _NT, preferred_element_type=jnp.float32) * scale
            e = jnp.exp(s - jnp.max(s, axis=-1, keepdims=True))
            o = jnp.dot(e.astype(bf), vc, preferred_element_type=jnp.float32)
            oc_ref[0, :, lanes] = (o / jnp.sum(e, axis=-1, keepdims=True)).astype(oc_ref.dtype)


def na_bias_table(rpb):
    qcol = np.arange(GRID_W)[:, None]
    kcol = np.arange(GRID_W)[None, :]
    qstart = np.clip(qcol - NA_WIN_C // 2, 0, GRID_W - NA_WIN_C)
    mask = (kcol >= qstart) & (kcol < qstart + NA_WIN_C)
    dc = np.clip(kcol - qcol + NA_WIN_C - 1, 0, 2 * NA_WIN_C - 2)
    pick_col = (np.arange(2 * NA_WIN_C - 1)[:, None, None] == dc[None]).astype(np.float32)
    rows = jnp.stack([rpb.astype(jnp.float32)[:, NA_WIN_R - 1 - dl:2 * NA_WIN_R - 1 - dl] for dl in range(NA_WIN_R)],
                     axis=1)
    t = jnp.einsum("hdwc,cqk->hdwqk", rows, pick_col, precision=lax.Precision.HIGHEST)
    t = jnp.where(mask[None, None, None], t * LOG2E, -1e30)
    return jnp.pad(t, ((0, 0),) * 4 + ((0, LANES - GRID_W),))


def na_attention(p_lat, p_ctx, bias, *, with_ctx, out_dtype, ctx_cols=FULL_COLS):
    bsz, seq, _ = p_lat.shape
    ctx_len = p_ctx.shape[1]
    assert seq == GRID_ROWS * GRID_W
    hs = NA_HEADS_PER_STEP
    d, nblk = hs * NA_HEAD_DIM, NA_HEADS // hs
    in_specs = [pl.BlockSpec((1, seq, d), lambda i, b: (b, 0, i)),
                pl.BlockSpec((1, seq, d), lambda i, b: (b, 0, nblk + i)),
                pl.BlockSpec((1, seq, d), lambda i, b: (b, 0, 2 * nblk + i)),
                pl.BlockSpec((1, ctx_len, d), lambda i, b: (b, 0, ctx_cols.k // d + i)),
                pl.BlockSpec((1, ctx_len, d), lambda i, b: (b, 0, ctx_cols.v // d + i)),
                pl.BlockSpec((hs, NA_WIN_R, NA_WIN_R, GRID_W, LANES), lambda i, b: (i, 0, 0, 0, 0))]
    args = [p_lat, p_lat, p_lat, p_ctx, p_ctx, bias]
    out_specs = [pl.BlockSpec((1, seq, d), lambda i, b: (b, 0, i))]
    out_shape = [jax.ShapeDtypeStruct((bsz, seq, NA_WIDTH), out_dtype)]
    if with_ctx:
        in_specs.append(pl.BlockSpec((1, ctx_len, d), lambda i, b: (b, 0, i)))
        args.append(p_ctx)
        out_specs.append(pl.BlockSpec((1, ctx_len, d), lambda i, b: (b, 0, i)))
        out_shape.append(jax.ShapeDtypeStruct((bsz, ctx_len, NA_WIDTH), out_dtype))
    outs = pl.pallas_call(
        partial(_na_kernel, with_ctx=with_ctx),
        grid=(nblk, bsz),
        in_specs=in_specs, out_specs=out_specs, out_shape=out_shape,
        scratch_shapes=[pltpu.VMEM((hs, NA_WIN_R, GRID_W, NA_WIN_TOKENS), jnp.float32)],
        compiler_params=pltpu.CompilerParams(dimension_semantics=("arbitrary", "arbitrary"),
                                             vmem_limit_bytes=V7X_VMEM_LIMIT_BYTES),
        name="na_attention",
    )(*args)
    return (outs[0], outs[1]) if with_ctx else (outs[0], None)


def _dft_cos_sin(n):
    jk = np.outer(np.arange(n), np.arange(n)) % n
    ang = 2.0 * np.pi * jk / n
    return np.cos(ang) / np.sqrt(n), np.sin(ang) / np.sqrt(n)


FOUR_EXT = 8


def _fourier_kernel(u_ref, cc_ref, sc_ref, a_ref, flip_ref, fw_ref, zlo_ref, zhi_ref, o_ref, *, tm):
    bf = jnp.bfloat16
    gd, ext = FOUR_GROUP_DIM, tm + FOUR_EXT
    pos = jnp.dot(a_ref[...], u_ref[0].astype(bf), preferred_element_type=jnp.float32).astype(bf)
    even, odd = [], []
    for g in range(FOUR_GROUPS):
        cols = slice(g * gd, (g + 1) * gd)
        even.append(jnp.dot(pos[:ext, cols], cc_ref[...], preferred_element_type=jnp.float32))
        odd.append(jnp.dot(pos[ext:, cols], sc_ref[...], preferred_element_type=jnp.float32))
    even = jnp.concatenate(even, axis=1)
    odd = jnp.concatenate(odd, axis=1)
    y_lo = (even[:tm] - odd[:tm]).astype(bf)
    y_hi = jnp.dot(flip_ref[...], (even + odd).astype(bf), preferred_element_type=jnp.float32).astype(bf)
    for half, (y, z_ref) in enumerate(((y_lo, zlo_ref), (y_hi, zhi_ref))):
        f = jnp.dot(y, fw_ref[...], preferred_element_type=jnp.float32)
        zf = z_ref[0].astype(jnp.float32)
        o_ref[0, half] = (f * (zf * jax.nn.sigmoid(zf))).astype(o_ref.dtype)


def fourier_branch(p, four_w_bf, layer, *, tm):
    bsz, seq, _ = p.shape
    half = seq // 2
    assert half % tm == 0
    nt2, ext = half // tm, tm + FOUR_EXT
    cc, sc = _dft_cos_sin(FOUR_GROUP_DIM)
    cl, sl = _dft_cos_sin(seq)
    a = np.stack([np.concatenate([cl[i * tm:i * tm + ext], sl[i * tm:i * tm + ext]], axis=0) for i in range(nt2)])
    flip = np.zeros((tm, ext))
    flip[np.arange(tm), tm - np.arange(tm)] = 1.0
    fw, gd = FOUR_WIDTH, FOUR_GROUP_DIM
    bf = jnp.bfloat16
    out = pl.pallas_call(
        partial(_fourier_kernel, tm=tm),
        grid=(bsz, nt2),
        in_specs=[pl.BlockSpec((1, seq, fw), lambda b, i: (b, 0, 4)),
                  pl.BlockSpec((gd, gd), lambda b, i: (0, 0)),
                  pl.BlockSpec((gd, gd), lambda b, i: (0, 0)),
                  pl.BlockSpec((None, 2 * ext, seq), lambda b, i: (i, 0, 0)),
                  pl.BlockSpec((tm, ext), lambda b, i: (0, 0)),
                  pl.BlockSpec((None, fw, fw), lambda b, i: (layer, 0, 0)),
                  pl.BlockSpec((1, tm, fw), lambda b, i: (b, i, 5)),
                  pl.BlockSpec((1, tm, fw), lambda b, i: (b, 2 * nt2 - 1 - i, 5))],
        out_specs=pl.BlockSpec((1, 2, tm, fw), lambda b, i: (b, 0, i, 0)),
        out_shape=jax.ShapeDtypeStruct((bsz, 2, half, fw), bf),
        compiler_params=pltpu.CompilerParams(dimension_semantics=("arbitrary", "arbitrary"),
                                             vmem_limit_bytes=V7X_VMEM_LIMIT_BYTES),
        name="fourier_branch",
    )(p, jnp.asarray(cc, bf), jnp.asarray(sc, bf), jnp.asarray(a, bf), jnp.asarray(flip, bf), four_w_bf, p, p)
    return out.reshape(bsz, seq, fw), nt2


def fourier_row_block(t, nt2):
    return jnp.where(t < nt2, t, 3 * nt2 - 1 - t)


MERGE_TN = 1024


def _merge_kernel(a_ref, zna_ref, gfo_ref, s_ref, zs_ref, nw_ref, wa_ref, wf_ref, ws_ref, *rest):
    n_half = D_MODEL // MERGE_TN
    gate_refs, (o_ref, ap_ref, sp_ref) = rest[:3 * n_half], rest[3 * n_half:]
    bf = jnp.bfloat16
    zna = zna_ref[...].astype(jnp.float32)
    ap_ref[...] = (a_ref[...].astype(jnp.float32) * (zna * jax.nn.sigmoid(zna))).astype(bf)
    zs = zs_ref[...].astype(jnp.float32)
    t = s_ref[...].astype(jnp.float32) * (zs * jax.nn.sigmoid(zs))
    t = t * lax.rsqrt(jnp.mean(t * t, axis=-1, keepdims=True) + EPS) * nw_ref[...]
    sp_ref[...] = t.astype(bf)
    for h in range(n_half):
        cols = slice(h * MERGE_TN, (h + 1) * MERGE_TN)
        ga_ref, gf_ref, gs_ref = gate_refs[h], gate_refs[n_half + h], gate_refs[2 * n_half + h]
        o_a = jnp.dot(ap_ref[...], wa_ref[:, cols], preferred_element_type=jnp.float32)
        o_f = jnp.dot(gfo_ref[...], wf_ref[:, cols], preferred_element_type=jnp.float32)
        o_s = jnp.dot(sp_ref[...], ws_ref[:, cols], preferred_element_type=jnp.float32)
        m = (jax.nn.sigmoid(ga_ref[...].astype(jnp.float32)) * o_a
             + jax.nn.sigmoid(gf_ref[...].astype(jnp.float32)) * o_f
             + jax.nn.sigmoid(gs_ref[...].astype(jnp.float32)) * o_s)
        o_ref[:, cols] = m.astype(o_ref.dtype)


def merge_branches(p2d, a2d, gfo2d, gfo_nt2, s2d, ssd_norm_w, wa, wf, ws, layer, *, tm):
    m = p2d.shape[0]
    w, tn = NA_WIDTH, MERGE_TN
    n_half = D_MODEL // tn
    assert m % tm == 0
    row = lambda i: (i, 0)
    if gfo_nt2 is None:
        gfo_row = row
    else:
        assert p2d.shape[0] % (2 * gfo_nt2 * tm) == 0
        per_seq = 2 * gfo_nt2
        gfo_row = lambda i: ((i // per_seq) * per_seq + fourier_row_block(i % per_seq, gfo_nt2), 0)
    resident = lambda shape, idx: pl.BlockSpec(shape, lambda i: idx, pipeline_mode=pl.Buffered(1))
    w_spec = resident((None, w, D_MODEL), (layer, 0, 0))
    gate0 = DT_OFFSET // tn
    gate_specs = [pl.BlockSpec((tm, tn), partial(lambda i, blk: (i, blk), blk=gate0 + g * n_half + h))
                  for g in range(3) for h in range(n_half)]
    return pl.pallas_call(
        _merge_kernel,
        grid=(m // tm,),
        in_specs=[pl.BlockSpec((tm, w), row),
                  pl.BlockSpec((tm, w), lambda i: (i, 3)),
                  pl.BlockSpec((tm, w), gfo_row),
                  pl.BlockSpec((tm, w), row),
                  pl.BlockSpec((tm, w), lambda i: (i, 8)),
                  resident((1, w), (0, 0)), w_spec, w_spec, w_spec] + gate_specs,
        out_specs=pl.BlockSpec((tm, D_MODEL), row),
        out_shape=jax.ShapeDtypeStruct((m, D_MODEL), jnp.bfloat16),
        scratch_shapes=[pltpu.VMEM((tm, w), jnp.bfloat16), pltpu.VMEM((tm, w), jnp.bfloat16)],
        compiler_params=pltpu.CompilerParams(dimension_semantics=("arbitrary",),
                                             vmem_limit_bytes=V7X_VMEM_LIMIT_BYTES),
        name="merge_branches",
    )(a2d, p2d, gfo2d, s2d, p2d, ssd_norm_w.reshape(1, w), wa, wf, ws, *([p2d] * (3 * n_half)))


def _outproj_kernel(m_ref, w_ref, x_ref, g_ref, *rest, final_norm):
    y = jnp.dot(m_ref[...], w_ref[...], preferred_element_type=jnp.float32)
    r = x_ref[...] + g_ref[0] * y
    if final_norm:
        nw_ref, o_ref = rest
        o_ref[...] = r * lax.rsqrt(jnp.mean(r * r, axis=-1, keepdims=True) + EPS) * nw_ref[...]
    else:
        rest[0][...] = r


def out_projection(m2d, w_out_bf, layer, x2d, gate, *, rows_per_mod, tm, tn, final_norm_w=None):
    m, d = x2d.shape
    assert m % tm == 0 and d % tn == 0 and rows_per_mod % tm == 0
    per = rows_per_mod // tm
    in_specs = [pl.BlockSpec((tm, d), lambda i, j: (i, 0)),
                pl.BlockSpec((None, d, tn), lambda i, j: (layer, 0, j)),
                pl.BlockSpec((tm, tn), lambda i, j: (i, j)),
                pl.BlockSpec((1, 1, tn), lambda i, j: (i // per, 0, j))]
    args = [m2d, w_out_bf, x2d, gate]
    if final_norm_w is not None:
        assert tn == d
        in_specs.append(pl.BlockSpec((1, d), lambda i, j: (0, 0)))
        args.append(final_norm_w.astype(jnp.float32).reshape(1, d))
    return pl.pallas_call(
        partial(_outproj_kernel, final_norm=final_norm_w is not None),
        grid=(m // tm, d // tn),
        in_specs=in_specs,
        out_specs=pl.BlockSpec((tm, tn), lambda i, j: (i, j)),
        out_shape=jax.ShapeDtypeStruct((m, d), jnp.float32),
        compiler_params=pltpu.CompilerParams(dimension_semantics=("arbitrary", "arbitrary"),
                                             vmem_limit_bytes=V7X_VMEM_LIMIT_BYTES),
        name="out_projection",
    )(*args)


SSD_GW = SSD_HPG * SSD_HEAD_DIM
CONV_PAD = 8
SSD_CHUNK_UNROLL = 2
SSD_CONV_UNROLL = 4
SSD_C_UNROLL = 8
SSD_A_UNROLL = 6


def _split3_bf16(a):
    rnd = lambda v: v.astype(jnp.bfloat16).astype(jnp.float32)
    a1 = rnd(a)
    a2 = rnd(a - a1)
    a3 = rnd((a - a1) - a2)
    return a1, a2, a3


def _head_rows(rows, r0, n):
    return jnp.concatenate([jnp.broadcast_to(rows[r0 + k:r0 + k + 1, :], (SSD_HEAD_DIM, n))
                            for k in range(SSD_HPG)], axis=0)


def _ssd_kernel(xl_ref, bl_ref, cl_ref, xc_ref, bc_ref, cc_ref, dtl_ref, dtc_ref, bias_ref, alog_ref, dsk_ref,
                wx_ref, wb_ref, wc_ref, bx_ref, bb_ref, bcb_ref, cos_ref, sin_ref,
                yl_ref, yc_ref,
                padx, padb, padc, xt_s, b_s, c_s, dtt_s, row_s, dec_s, yt_s, p_s, sp_s, st_s, *, seq, ctx_len):
    f32, bf = jnp.float32, jnp.bfloat16
    t = SSD_CHUNK
    nc_c, nc_l = ctx_len // t, seq // t
    nc = nc_c + nc_l

    lane = lax.broadcasted_iota(jnp.int32, (t, LANES), 1)
    first_half = (lane % 64) < 32

    def conv_tile(pad, w_ref, b_ref, t0, l0):
        acc = jnp.broadcast_to(b_ref[:, l0:l0 + LANES], (t, LANES))
        for j in range(SSD_CONV):
            off = CONV_PAD - SSD_CONV // 2 + j
            acc = acc + pad[pl.ds(t0 + off, t), :] * w_ref[j:j + 1, l0:l0 + LANES]
        return acc * jax.nn.sigmoid(acc)

    def rope(u, t0):
        swapped = jnp.where(first_half, pltpu.roll(u, 96, axis=1), pltpu.roll(u, 32, axis=1))
        return u * cos_ref[pl.ds(t0, t), :] + swapped * sin_ref[pl.ds(t0, t), :]

    def conv_seq(x_ref, bm_ref, cm_ref, n, base, use_rope):
        slabs = ([(padx.at[h], x_ref, h * LANES) for h in range(SSD_GW // LANES)]
                 + [(padb, bm_ref, 0), (padc, cm_ref, 0)])
        for pad, src, l0 in slabs:
            pad[0:CONV_PAD, :] = jnp.zeros((CONV_PAD, LANES), f32)
            pad[CONV_PAD + n:2 * CONV_PAD + n, :] = jnp.zeros((CONV_PAD, LANES), f32)
            pad[CONV_PAD:CONV_PAD + n, :] = src[0, :, l0:l0 + LANES].astype(f32)

        unroll = min(SSD_CONV_UNROLL, n // t)

        def chunks(i, carry):
            for j in range(unroll):
                ci = i * unroll + j
                t0 = pl.multiple_of(ci * t, t)
                r0 = pl.multiple_of(base + ci * t, t)
                for h in range(SSD_GW // LANES):
                    l0 = h * LANES
                    xt_s[base // t + ci, l0:l0 + LANES, :] = conv_tile(padx.at[h], wx_ref, bx_ref, t0, l0).T
                ub = conv_tile(padb, wb_ref, bb_ref, t0, 0)
                uc = conv_tile(padc, wc_ref, bcb_ref, t0, 0)
                if use_rope:
                    ub, uc = rope(ub, t0), rope(uc, t0)
                b_s[pl.ds(r0, t), :] = ub.astype(bf)
                c_s[pl.ds(r0, t), :] = uc.astype(bf)
            return carry

        lax.fori_loop(0, n // t // unroll, chunks, 0)

    conv_seq(xc_ref, bc_ref, cc_ref, ctx_len, 0, False)
    conv_seq(xl_ref, bl_ref, cl_ref, seq, ctx_len, True)

    i0 = lax.broadcasted_iota(jnp.int32, (t, t), 0)
    i1 = lax.broadcasted_iota(jnp.int32, (t, t), 1)
    tri_f = (i0 <= i1).astype(bf)
    tri_r = (i0 >= i1).astype(bf)
    nr = 2 * SSD_HPG
    fwd_row = (lax.broadcasted_iota(jnp.int32, (nc * nr, 1), 0) % nr) < SSD_HPG
    g4 = pl.program_id(1) * SSD_HPG
    raw = []
    for c in range(nc):
        src, c0 = (dtc_ref, c) if c < nc_c else (dtl_ref, c - nc_c)
        dtt_s[c] = src[0, c0 * t:(c0 + 1) * t, :].T
        raw += [dtt_s[c, pl.ds(g4, SSD_HPG), :], dtt_s[c, pl.ds(SSD_HEADS + g4, SSD_HPG), :]]
    x = jnp.concatenate(raw, axis=0) + jnp.tile(bias_ref[0], (nc, 1))
    dt = jnp.maximum(x, 0.0) + jnp.log1p(jnp.exp(-jnp.abs(x)))
    pieces = _split3_bf16(dt * jnp.tile(-jnp.exp(alog_ref[0]), (nc, 1)))
    pf = [jnp.dot(p.astype(bf), tri_f, preferred_element_type=f32) for p in pieces]
    pr = [jnp.dot(p.astype(bf), tri_r, preferred_element_type=f32) for p in pieces]
    cs = jnp.where(fwd_row, (pf[0] + pf[1]) + pf[2], (pr[0] + pr[1]) + pr[2])
    tot = jnp.where(fwd_row, cs[:, t - 1:t], cs[:, 0:1])
    e_in = jnp.exp(cs)
    e_out = dt * jnp.exp(tot - cs)
    u = cs - jnp.log(dt)
    decb = jnp.broadcast_to(jnp.exp(tot), (nc * nr, LANES))
    for c in range(nc):
        sl = slice(c * nr, (c + 1) * nr)
        row_s[c] = jnp.concatenate([cs[sl], dt[sl], e_in[sl], e_out[sl], u[sl]], axis=0)
        dec_s[c] = decb[sl]

    lower = i1 <= i0
    lower_t = i0 <= i1
    diag = i1 == i0

    def chunks_a(i, carry):
        ids = [i * SSD_A_UNROLL + j for j in range(SSD_A_UNROLL)]
        r0s = [pl.multiple_of(c * t, t) for c in ids]
        cms = [c_s[pl.ds(r0, t), :] for r0 in r0s]
        bms = [b_s[pl.ds(r0, t), :] for r0 in r0s]
        gs = [lax.dot_general(cm, bm, _NT, preferred_element_type=f32) for cm, bm in zip(cms, bms)]
        for c, r0, bm, g in zip(ids, r0s, bms, gs):
            rows = row_s[c]
            xt = xt_s[c]
            xt_bf = xt.astype(bf)
            parts = []
            for k in range(SSD_HPG):
                kr = SSD_HPG + k
                cs_l = jnp.where(lower_t, rows[k:k + 1, :], rows[kr:kr + 1, :]).T
                w = jnp.exp(cs_l - jnp.where(lower, rows[32 + k:33 + k, :], rows[32 + kr:33 + kr, :]))
                mk = (g * (w + jnp.where(diag, rows[8 + kr:9 + kr, :], 0.0))).astype(bf)
                xk = xt_bf[k * SSD_HEAD_DIM:(k + 1) * SSD_HEAD_DIM, :]
                parts.append(lax.dot_general(xk, mk, _NT, preferred_element_type=f32))
            yt_s[c] = dsk_ref[0] * xt + jnp.concatenate(parts, axis=0)
            for d in range(2):
                wt = (xt * _head_rows(rows, 24 + SSD_HPG * d, t)).astype(bf)
                p_s[d, c] = jnp.dot(wt, bm, preferred_element_type=f32)
        return carry

    lax.fori_loop(0, nc // SSD_A_UNROLL, chunks_a, 0)

    st_s[...] = jnp.zeros(st_s.shape, f32)

    def state_step(c, d):
        st = st_s[d]
        sp_s[d, c] = st.astype(bf)
        st_s[d] = st * _head_rows(dec_s[c], SSD_HPG * d, SSD_STATE) + p_s[d, c]

    def ctx_step(i, carry):
        state_step(i, 0)
        state_step(nc_c - 1 - i, 1)
        return carry

    def lat_step(i, carry):
        state_step(nc_c + i, 0)
        state_step(nc - 1 - i, 1)
        return carry

    lax.fori_loop(0, nc_c, ctx_step, 0)
    lax.fori_loop(0, nc_l, lat_step, 0)

    def chunks_c(first, out_ref, o_first, count):
        ids = [first + u for u in range(count)]
        cms = [c_s[pl.ds(pl.multiple_of(c * t, t), t), :] for c in ids]
        offs = [[lax.dot_general(sp_s[d, c], cm, _NT, preferred_element_type=f32) for d in range(2)]
                for c, cm in zip(ids, cms)]
        for u, (c, off) in enumerate(zip(ids, offs)):
            rows = row_s[c]
            yt = yt_s[c]
            for d in range(2):
                yt = yt + off[d] * _head_rows(rows, 16 + SSD_HPG * d, t)
            for l0 in range(0, SSD_GW, LANES):
                out_ref[0, pl.ds(pl.multiple_of((o_first + u) * t, t), t), l0:l0 + LANES] = yt[l0:l0 + LANES, :].T

    def ctx_out(i, carry):
        chunks_c(i * SSD_CHUNK_UNROLL, yc_ref, i * SSD_CHUNK_UNROLL, SSD_CHUNK_UNROLL)
        return carry

    def lat_out(i, carry):
        chunks_c(nc_c + i * SSD_C_UNROLL, yl_ref, i * SSD_C_UNROLL, SSD_C_UNROLL)
        return carry

    lax.fori_loop(0, nc_c // SSD_CHUNK_UNROLL, ctx_out, 0)
    lax.fori_loop(0, nc_l // SSD_C_UNROLL, lat_out, 0)


def _rope_tables(seq):
    quarter = SSD_STATE // 4
    inv = ROPE_BASE ** (-np.arange(quarter, dtype=np.float64) / quarter)
    pos = np.arange(seq)
    ang_r = (pos // GRID_W)[:, None] * inv
    ang_c = (pos % GRID_W)[:, None] * inv
    cos = np.concatenate([np.cos(ang_r)] * 2 + [np.cos(ang_c)] * 2, axis=1)
    sin = np.concatenate([-np.sin(ang_r), np.sin(ang_r), -np.sin(ang_c), np.sin(ang_c)], axis=1)
    return jnp.asarray(cos, jnp.float32), jnp.asarray(sin, jnp.float32)


def ssd_mixer(p_lat, p_ctx, dt_lat, dt_ctx, conv_w, conv_b, dt_bias, a_log, d_skip, ctx_cols=FULL_COLS):
    bsz, seq, _ = p_lat.shape
    ctx_len = p_ctx.shape[1]
    t, g, hpg = SSD_CHUNK, SSD_GROUPS, SSD_HPG
    assert seq % t == 0 and ctx_len % t == 0
    nc = (seq + ctx_len) // t
    per_row = lambda v: jnp.broadcast_to(
        v.astype(jnp.float32).reshape(2, g, hpg).transpose(1, 0, 2).reshape(g, 2 * hpg, 1), (g, 2 * hpg, LANES))
    dsk = jnp.broadcast_to(jnp.repeat(d_skip.astype(jnp.float32).reshape(g, hpg), SSD_HEAD_DIM, axis=1)[:, :, None],
                           (g, SSD_GW, LANES))
    cos, sin = _rope_tables(seq)
    x0 = sum(MAIN_SPLITS[:6]) // SSD_GW
    b0 = (sum(MAIN_SPLITS[:6]) + SSD_WIDTH) // SSD_STATE
    c0 = b0 + g
    cw = conv_w.astype(jnp.float32)
    cb = conv_b.astype(jnp.float32).reshape(1, SSD_CONV_CH)
    wb0 = SSD_WIDTH // SSD_STATE
    ltot = seq + ctx_len
    f32, bf = jnp.float32, jnp.bfloat16
    y_lat, y_ctx = pl.pallas_call(
        partial(_ssd_kernel, seq=seq, ctx_len=ctx_len),
        grid=(bsz, g),
        in_specs=[pl.BlockSpec((1, seq, SSD_GW), lambda b, i: (b, 0, x0 + i)),
                  pl.BlockSpec((1, seq, SSD_STATE), lambda b, i: (b, 0, b0 + i)),
                  pl.BlockSpec((1, seq, SSD_STATE), lambda b, i: (b, 0, c0 + i)),
                  pl.BlockSpec((1, ctx_len, SSD_GW), lambda b, i: (b, 0, ctx_cols.xbc // SSD_GW + i)),
                  pl.BlockSpec((1, ctx_len, SSD_STATE), lambda b, i: (b, 0, (ctx_cols.xbc + SSD_WIDTH) // SSD_STATE + i)),
                  pl.BlockSpec((1, ctx_len, SSD_STATE), lambda b, i: (b, 0, (ctx_cols.xbc + SSD_WIDTH) // SSD_STATE + g + i)),
                  pl.BlockSpec((1, seq, DT_PAD), lambda b, i: (b, 0, 0)),
                  pl.BlockSpec((1, ctx_len, DT_PAD), lambda b, i: (b, 0, 0)),
                  pl.BlockSpec((1, 2 * hpg, LANES), lambda b, i: (i, 0, 0)),
                  pl.BlockSpec((1, 2 * hpg, LANES), lambda b, i: (i, 0, 0)),
                  pl.BlockSpec((1, SSD_GW, LANES), lambda b, i: (i, 0, 0)),
                  pl.BlockSpec((SSD_CONV, SSD_GW), lambda b, i: (0, i)),
                  pl.BlockSpec((SSD_CONV, SSD_STATE), lambda b, i: (0, wb0 + i)),
                  pl.BlockSpec((SSD_CONV, SSD_STATE), lambda b, i: (0, wb0 + g + i)),
                  pl.BlockSpec((1, SSD_GW), lambda b, i: (0, i)),
                  pl.BlockSpec((1, SSD_STATE), lambda b, i: (0, wb0 + i)),
                  pl.BlockSpec((1, SSD_STATE), lambda b, i: (0, wb0 + g + i)),
                  pl.BlockSpec((seq, SSD_STATE), lambda b, i: (0, 0)),
                  pl.BlockSpec((seq, SSD_STATE), lambda b, i: (0, 0))],
        out_specs=[pl.BlockSpec((1, seq, SSD_GW), lambda b, i: (b, 0, i)),
                   pl.BlockSpec((1, ctx_len, SSD_GW), lambda b, i: (b, 0, i))],
        out_shape=[jax.ShapeDtypeStruct((bsz, seq, SSD_WIDTH), f32),
                   jax.ShapeDtypeStruct((bsz, ctx_len, SSD_WIDTH), f32)],
        scratch_shapes=[pltpu.VMEM((SSD_GW // LANES, seq + 2 * CONV_PAD, LANES), f32),
                        pltpu.VMEM((seq + 2 * CONV_PAD, SSD_STATE), f32),
                        pltpu.VMEM((seq + 2 * CONV_PAD, SSD_STATE), f32),
                        pltpu.VMEM((nc, SSD_GW, t), f32),
                        pltpu.VMEM((ltot, SSD_STATE), bf),
                        pltpu.VMEM((ltot, SSD_STATE), bf),
                        pltpu.VMEM((nc, DT_PAD, t), f32),
                        pltpu.VMEM((nc, 5 * 2 * hpg, t), f32),
                        pltpu.VMEM((nc, 2 * hpg, LANES), f32),
                        pltpu.VMEM((nc, SSD_GW, t), f32),
                        pltpu.VMEM((2, nc, SSD_GW, SSD_STATE), f32),
                        pltpu.VMEM((2, nc, SSD_GW, SSD_STATE), bf),
                        pltpu.VMEM((2, SSD_GW, SSD_STATE), f32)],
        compiler_params=pltpu.CompilerParams(dimension_semantics=("arbitrary", "arbitrary"),
                                             vmem_limit_bytes=V7X_VMEM_LIMIT_BYTES),
        name="ssd_mixer",
    )(p_lat, p_lat, p_lat, p_ctx, p_ctx, p_ctx, dt_lat, dt_ctx, per_row(dt_bias), per_row(a_log), dsk,
      cw, cw, cw, cb, cb, cb, cos, sin)
    return y_ctx, y_lat


INPROJ_TM = 1024
INPROJ_TN = 2560
ADA_TN = 1024
BRANCH_TM = 512
CTX_OUT_TILE = 1024
ADA_ROWS = 16


def _layer(xc, xl, c, c_ctx, w_ada_all, layer, b_ada, norm_w, w_main_all, w_dt_all, rpb, four_wb, conv_w, conv_b, dt_bias,
           a_log, d_skip, ssd_norm_w, wb_nab, wb_fourb, wb_ssdb, w_outb, update_ctx, final_norm_w):
    bsz, seq, d = xl.shape
    ctx_len = xc.shape[1]
    bf = jnp.bfloat16
    assert bsz < ADA_ROWS

    cc = jnp.concatenate([c, c_ctx[None], jnp.zeros((ADA_ROWS - bsz - 1, d), c.dtype)], axis=0)
    mod = matmul(jax.nn.silu(cc), w_ada_all, layer, tm=ADA_ROWS, tn=ADA_TN) + b_ada
    sh, sc, gt = jnp.split(mod, 3, axis=-1)
    sh_l, sc_l, g_l = sh[:bsz], sc[:bsz], gt[:bsz]
    sh_c, sc_c, g_c = sh[bsz:bsz + 1], sc[bsz:bsz + 1], gt[bsz]

    xl2d = xl.reshape(bsz * seq, d)
    xc2d = xc.reshape(bsz * ctx_len, d)
    pl2d, dt_l = in_projection(xl2d, norm_w, sc_l[:, None], sh_l[:, None], w_main_all, layer, w_dt_all, layer,
                               rows_per_mod=seq, tm=INPROJ_TM, tn=INPROJ_TN, out_dtype=bf)
    if update_ctx:
        ctx_cols, w_ctx, ctx_layer, ctx_tn = FULL_COLS, w_main_all, layer, INPROJ_TN
    else:
        ctx_cols, ctx_layer = KV_XBC_COLS, 0
        w_ctx = jnp.concatenate([w_main_all[layer, FULL_COLS.k:FULL_COLS.v + NA_WIDTH],
                                 w_main_all[layer, FULL_COLS.xbc:FULL_COLS.xbc + SSD_CONV_CH]], axis=0)[None]
        ctx_tn = w_ctx.shape[1] // 2
    pc2d, dt_c = in_projection(xc2d, norm_w, sc_c[:, None], sh_c[:, None], w_ctx, ctx_layer, w_dt_all, layer,
                               rows_per_mod=bsz * ctx_len, tm=INPROJ_TM, tn=ctx_tn, out_dtype=bf)
    p_l = pl2d.reshape(bsz, seq, MAIN_WIDTH)
    p_c = pc2d.reshape(bsz, ctx_len, w_ctx.shape[1])
    dt_l = dt_l.reshape(bsz, seq, DT_PAD)
    dt_c = dt_c.reshape(bsz, ctx_len, DT_PAD)

    a_l, a_c = na_attention(p_l, p_c, na_bias_table(rpb), with_ctx=update_ctx, out_dtype=bf, ctx_cols=ctx_cols)
    gfo_l, nt2_l = fourier_branch(p_l, four_wb, layer, tm=BRANCH_TM)
    s_c, s_l = ssd_mixer(p_l, p_c, dt_l, dt_c, conv_w, conv_b, dt_bias, a_log, d_skip, ctx_cols=ctx_cols)
    m_l = merge_branches(pl2d, a_l.reshape(bsz * seq, NA_WIDTH), gfo_l.reshape(bsz * seq, FOUR_WIDTH), nt2_l,
                         s_l.reshape(bsz * seq, SSD_WIDTH), ssd_norm_w, wb_nab, wb_fourb, wb_ssdb, layer, tm=BRANCH_TM)
    xl_new = out_projection(m_l, w_outb, layer, xl2d, g_l[:, None], rows_per_mod=seq, tm=BRANCH_TM, tn=d,
                            final_norm_w=final_norm_w)
    xl_new = xl_new.reshape(bsz, seq, d)
    if update_ctx:
        gfo_c, nt2_c = fourier_branch(p_c, four_wb, layer, tm=ctx_len // 2)
        assert nt2_c == 1
        m_c = merge_branches(pc2d, a_c.reshape(bsz * ctx_len, NA_WIDTH), gfo_c.reshape(bsz * ctx_len, FOUR_WIDTH), None,
                             s_c.reshape(bsz * ctx_len, SSD_WIDTH), ssd_norm_w, wb_nab, wb_fourb, wb_ssdb, layer, tm=BRANCH_TM)
        xc = out_projection(m_c, w_outb, layer, xc2d, g_c[None, None], rows_per_mod=bsz * ctx_len, tm=CTX_OUT_TILE, tn=CTX_OUT_TILE)
        xc = xc.reshape(bsz, ctx_len, d)
    return xc, xl_new


def kernel(x, c, ctx, c_ctx, w_ada, b_ada, norm_w, w_in, na_rpb, four_w, ssd_conv_w, ssd_conv_b, ssd_dt_bias,
           ssd_a_log, ssd_d, ssd_norm_w, wb_na, wb_four, wb_ssd, w_out, final_norm_w):
    xc, xl = ctx, x
    w_main_all, w_dt_all = repack_w_in(w_in)
    four_wb, wb_nab, wb_fourb, wb_ssdb, w_outb = (t.astype(jnp.bfloat16) for t in (four_w, wb_na, wb_four, wb_ssd, w_out))
    for l in range(DEPTH):
        xc, xl = _layer(xc, xl, c, c_ctx, w_ada, l, b_ada[l], norm_w[l], w_main_all, w_dt_all, na_rpb[l], four_wb,
                        ssd_conv_w[l], ssd_conv_b[l], ssd_dt_bias[l], ssd_a_log[l], ssd_d[l], ssd_norm_w[l],
                        wb_nab, wb_fourb, wb_ssdb, w_outb, update_ctx=(l < DEPTH - 1),
                        final_norm_w=final_norm_w if l == DEPTH - 1 else None)
    return xl
```

```python
import math
from functools import partial
from typing import NamedTuple

import jax
import jax.numpy as jnp
import numpy as np
from jax import lax
from jax.experimental import pallas as pl
from jax.experimental.pallas import tpu as pltpu

D_MODEL = 2048
DEPTH = 2
GRID_W = 64
EPS = 1e-6
NA_HEADS = 8
NA_HEAD_DIM = 128
NA_WIDTH = NA_HEADS * NA_HEAD_DIM
NA_WIN_R = 8
NA_WIN_C = 16
LOG2E = math.log2(math.e)
FOUR_GROUPS = 4
FOUR_GROUP_DIM = 256
FOUR_WIDTH = FOUR_GROUPS * FOUR_GROUP_DIM
SSD_HEADS = 16
SSD_HEAD_DIM = 64
SSD_WIDTH = SSD_HEADS * SSD_HEAD_DIM
SSD_GROUPS = 4
SSD_HPG = SSD_HEADS // SSD_GROUPS
SSD_STATE = 128
SSD_CONV = 7
SSD_CHUNK = 128
SSD_CONV_CH = SSD_WIDTH + 2 * SSD_GROUPS * SSD_STATE
ROPE_BASE = 10000.0
DT_WIDTH = 2 * SSD_HEADS
MAIN_SPLITS = (NA_WIDTH, NA_WIDTH, NA_WIDTH, NA_WIDTH, FOUR_WIDTH, FOUR_WIDTH, SSD_CONV_CH, SSD_WIDTH,
               D_MODEL, D_MODEL, D_MODEL)
MAIN_WIDTH = sum(MAIN_SPLITS)
DT_OFFSET = sum(MAIN_SPLITS[:8])
DT_PAD = 128


class ProjCols(NamedTuple):
    k: int
    v: int
    xbc: int


FULL_COLS = ProjCols(k=NA_WIDTH, v=2 * NA_WIDTH, xbc=sum(MAIN_SPLITS[:6]))
KV_XBC_COLS = ProjCols(k=0, v=NA_WIDTH, xbc=2 * NA_WIDTH)

LANES = 128
V7X_VMEM_LIMIT_BYTES = 56 * 1024 * 1024


def _mm_kernel(a_ref, b_ref, o_ref):
    a = a_ref[...].astype(jnp.bfloat16)
    b = b_ref[...].astype(jnp.bfloat16)
    o_ref[...] = jnp.dot(a, b, preferred_element_type=jnp.float32).astype(o_ref.dtype)


def matmul(a, b, layer, *, tm, tn, out_dtype=jnp.float32):
    m, k = a.shape
    _, _, n = b.shape
    assert m % tm == 0 and n % tn == 0, (a.shape, b.shape, tm, tn)
    return pl.pallas_call(
        _mm_kernel,
        grid=(m // tm, n // tn),
        in_specs=[pl.BlockSpec((tm, k), lambda i, j: (i, 0)),
                  pl.BlockSpec((None, k, tn), lambda i, j: (layer, 0, j))],
        out_specs=pl.BlockSpec((tm, tn), lambda i, j: (i, j)),
        out_shape=jax.ShapeDtypeStruct((m, n), out_dtype),
        compiler_params=pltpu.CompilerParams(dimension_semantics=("arbitrary", "arbitrary"),
                                             vmem_limit_bytes=V7X_VMEM_LIMIT_BYTES),
        name="matmul",
    )(a, b)


REPACK_TN = 1024


def _repack_kernel(a_ref, nxt_ref, o_ref, odt_ref):
    j = pl.program_id(1)
    first_below = DT_OFFSET // REPACK_TN

    @pl.when(j < first_below)
    def _():
        o_ref[...] = a_ref[...].astype(o_ref.dtype)

    @pl.when(j >= first_below)
    def _():
        o_ref[...] = jnp.concatenate([a_ref[DT_WIDTH:, :], nxt_ref[...]], axis=0).astype(o_ref.dtype)

    @pl.when(j == first_below)
    def _():
        odt_ref[...] = a_ref[:DT_PAD, :].astype(odt_ref.dtype)


def repack_w_in(w_in):
    depth, d, _ = w_in.shape
    tn = REPACK_TN
    w_t = jnp.swapaxes(w_in, 1, 2)
    return pl.pallas_call(
        _repack_kernel,
        grid=(depth, MAIN_WIDTH // tn),
        in_specs=[pl.BlockSpec((None, tn, d), lambda l, j: (l, j, 0)),
                  pl.BlockSpec((None, DT_WIDTH, d), lambda l, j: (l, (j + 1) * (tn // DT_WIDTH), 0))],
        out_specs=[pl.BlockSpec((None, tn, d), lambda l, j: (l, j, 0)),
                   pl.BlockSpec((None, DT_PAD, d), lambda l, j: (l, 0, 0))],
        out_shape=[jax.ShapeDtypeStruct((depth, MAIN_WIDTH, d), jnp.bfloat16),
                   jax.ShapeDtypeStruct((depth, DT_PAD, d), jnp.bfloat16)],
        compiler_params=pltpu.CompilerParams(dimension_semantics=("arbitrary", "arbitrary"),
                                             vmem_limit_bytes=V7X_VMEM_LIMIT_BYTES),
        name="repack_w_in",
    )(w_t, w_t)


def _inproj_kernel(x_ref, nw_ref, sc_ref, sh_ref, w_ref, wdt_ref, o_ref, dt_ref, h_ref):
    @pl.when(pl.program_id(1) == 0)
    def _():
        x = x_ref[...]
        y = x * lax.rsqrt(jnp.mean(x * x, axis=-1, keepdims=True) + EPS) * nw_ref[...]
        h = (y * (1.0 + sc_ref[0]) + sh_ref[0]).astype(jnp.bfloat16)
        h_ref[...] = h
        dt_ref[...] = lax.dot_general(h, wdt_ref[...], _NT, preferred_element_type=jnp.float32)

    o_ref[...] = lax.dot_general(h_ref[...], w_ref[...], _NT, preferred_element_type=jnp.float32).astype(o_ref.dtype)


def in_projection(x2d, norm_w, scale, shift, w_main_t, layer, w_dt_t, dt_layer, *, rows_per_mod, tm, tn, out_dtype):
    m, d = x2d.shape
    n = w_main_t.shape[1]
    assert m % tm == 0 and n % tn == 0 and rows_per_mod % tm == 0
    per = rows_per_mod // tm
    return pl.pallas_call(
        _inproj_kernel,
        grid=(m // tm, n // tn),
        in_specs=[pl.BlockSpec((tm, d), lambda i, j: (i, 0)),
                  pl.BlockSpec((1, d), lambda i, j: (0, 0)),
                  pl.BlockSpec((1, 1, d), lambda i, j: (i // per, 0, 0)),
                  pl.BlockSpec((1, 1, d), lambda i, j: (i // per, 0, 0)),
                  pl.BlockSpec((None, tn, d), lambda i, j: (layer, j, 0)),
                  pl.BlockSpec((None, DT_PAD, d), lambda i, j: (dt_layer, 0, 0))],
        out_specs=[pl.BlockSpec((tm, tn), lambda i, j: (i, j)),
                   pl.BlockSpec((tm, DT_PAD), lambda i, j: (i, 0))],
        out_shape=[jax.ShapeDtypeStruct((m, n), out_dtype),
                   jax.ShapeDtypeStruct((m, DT_PAD), jnp.float32)],
        scratch_shapes=[pltpu.VMEM((tm, d), jnp.bfloat16)],
        compiler_params=pltpu.CompilerParams(dimension_semantics=("arbitrary", "arbitrary"),
                                             vmem_limit_bytes=V7X_VMEM_LIMIT_BYTES),
        name="in_projection",
    )(x2d, norm_w.reshape(1, d), scale, shift, w_main_t, w_dt_t)


GRID_ROWS = 32
NA_WIN_TOKENS = NA_WIN_R * GRID_W
_NT = (((1,), (1,)), ((), ()))
NA_HEADS_PER_STEP = 2
NA_ROW_UNROLL = 16


def _na_kernel(*refs, with_ctx):
    if with_ctx:
        q_ref, k_ref, v_ref, kc_ref, vc_ref, rpb_ref, qc_ref, o_ref, oc_ref, bias_s = refs
    else:
        q_ref, k_ref, v_ref, kc_ref, vc_ref, rpb_ref, o_ref, bias_s = refs
    bf = jnp.bfloat16

    @pl.when(pl.program_id(1) == 0)
    def _():
        qi = lax.broadcasted_iota(jnp.int32, (GRID_W, LANES), 0)
        li = lax.broadcasted_iota(jnp.int32, (GRID_W, LANES), 1)
        qstart = jnp.clip(qi - NA_WIN_C // 2, 0, GRID_W - NA_WIN_C)
        in_window = (li >= qstart) & (li < qstart + NA_WIN_C)
        for hh in range(NA_HEADS_PER_STEP):
            lo, hi = [], []
            for dr in range(2 * NA_WIN_R - 1):
                row = jnp.broadcast_to(rpb_ref[hh, dr:dr + 1, :], (GRID_W, LANES))
                blk = pltpu.roll(row, LANES - (NA_WIN_C - 1), axis=1, stride=1, stride_axis=0)
                blk = jnp.where(in_window, blk * LOG2E, -1e30)
                lo.append(blk)
                hi.append(pltpu.roll(blk, GRID_W, axis=1))
            for dl in range(NA_WIN_R):
                for pr in range(NA_WIN_R // 2):
                    d0 = 2 * pr - dl + NA_WIN_R - 1
                    bias_s[hh, dl, :, pr * LANES:(pr + 1) * LANES] = jnp.where(li < GRID_W, lo[d0], hi[d0 + 1])

    scale = NA_HEAD_DIM ** -0.5
    scale2 = scale * LOG2E
    nq = NA_ROW_UNROLL * GRID_W

    def lane_tiles(x, op):
        acc = x[:, 0:LANES]
        for l0 in range(LANES, x.shape[1], LANES):
            acc = op(acc, x[:, l0:l0 + LANES])
        return acc

    for hh in range(NA_HEADS_PER_STEP):
        lanes = slice(hh * NA_HEAD_DIM, (hh + 1) * NA_HEAD_DIM)
        kc = kc_ref[0, :, lanes].astype(bf)
        vc = vc_ref[0, :, lanes].astype(bf)

        def rows(rb, carry, hh=hh, lanes=lanes, kc=kc, vc=vc):
            qb0 = pl.multiple_of(rb * nq, nq)
            q_all = q_ref[0, pl.ds(qb0, nq), lanes].astype(bf)
            s_c_all = lax.dot_general(q_all, kc, _NT, preferred_element_type=jnp.float32) * scale2
            k0s, scores, probs = [], [], []
            for i in range(NA_ROW_UNROLL):
                r = rb * NA_ROW_UNROLL + i
                start = jnp.clip(r - NA_WIN_R // 2, 0, GRID_ROWS - NA_WIN_R)
                k0 = pl.multiple_of(start * GRID_W, GRID_W)
                q = q_all[i * GRID_W:(i + 1) * GRID_W]
                kw = k_ref[0, pl.ds(k0, NA_WIN_TOKENS), lanes].astype(bf)
                s_w = (lax.dot_general(q, kw, _NT, preferred_element_type=jnp.float32) * scale2
                       + bias_s[hh, r - start])
                k0s.append(k0)
                scores.append((s_w, s_c_all[i * GRID_W:(i + 1) * GRID_W]))
            for s_w, s_c in scores:
                m = jnp.max(jnp.maximum(lane_tiles(s_w, jnp.maximum), lane_tiles(s_c, jnp.maximum)),
                            axis=-1, keepdims=True)
                e_w = jnp.exp2(s_w - m)
                e_c = jnp.exp2(s_c - m)
                denom = jnp.sum(lane_tiles(e_w, jnp.add) + lane_tiles(e_c, jnp.add), axis=-1, keepdims=True)
                probs.append((e_w.astype(bf), e_c.astype(bf), denom))
            o_c_all = jnp.dot(jnp.concatenate([p[1] for p in probs], axis=0), vc, preferred_element_type=jnp.float32)
            for i, (k0, (e_w, _, denom)) in enumerate(zip(k0s, probs)):
                vw = v_ref[0, pl.ds(k0, NA_WIN_TOKENS), lanes].astype(bf)
                o = jnp.dot(e_w, vw, preferred_element_type=jnp.float32) + o_c_all[i * GRID_W:(i + 1) * GRID_W]
                o_ref[0, pl.ds(qb0 + i * GRID_W, GRID_W), lanes] = (o / denom).astype(o_ref.dtype)
            return carry

        lax.fori_loop(0, GRID_ROWS // NA_ROW_UNROLL, rows, 0)

        if with_ctx:
            s = lax.dot_general(qc_ref[0, :, lanes].astype(bf), kc, _NT, preferred_element_type=jnp.float32) * scale
            e = jnp.exp(s - jnp.max(s, axis=-1, keepdims=True))
            o = jnp.dot(e.astype(bf), vc, preferred_element_type=jnp.float32)
            oc_ref[0, :, lanes] = (o / jnp.sum(e, axis=-1, keepdims=True)).astype(oc_ref.dtype)


def na_rpb_rows(rpb):
    h, nr, nc = rpb.shape
    return jnp.pad(rpb.astype(jnp.float32), ((0, 0), (0, 2 * NA_WIN_R - nr), (0, LANES - nc)))


def na_attention(p_lat, p_ctx, rpb_rows, *, with_ctx, out_dtype, ctx_cols=FULL_COLS):
    bsz, seq, _ = p_lat.shape
    ctx_len = p_ctx.shape[1]
    assert seq == GRID_ROWS * GRID_W
    hs = NA_HEADS_PER_STEP
    d, nblk = hs * NA_HEAD_DIM, NA_HEADS // hs
    in_specs = [pl.BlockSpec((1, seq, d), lambda i, b: (b, 0, i)),
                pl.BlockSpec((1, seq, d), lambda i, b: (b, 0, nblk + i)),
                pl.BlockSpec((1, seq, d), lambda i, b: (b, 0, 2 * nblk + i)),
                pl.BlockSpec((1, ctx_len, d), lambda i, b: (b, 0, ctx_cols.k // d + i)),
                pl.BlockSpec((1, ctx_len, d), lambda i, b: (b, 0, ctx_cols.v // d + i)),
                pl.BlockSpec((hs, 2 * NA_WIN_R, LANES), lambda i, b: (i, 0, 0))]
    args = [p_lat, p_lat, p_lat, p_ctx, p_ctx, rpb_rows]
    out_specs = [pl.BlockSpec((1, seq, d), lambda i, b: (b, 0, i))]
    out_shape = [jax.ShapeDtypeStruct((bsz, seq, NA_WIDTH), out_dtype)]
    if with_ctx:
        in_specs.append(pl.BlockSpec((1, ctx_len, d), lambda i, b: (b, 0, i)))
        args.append(p_ctx)
        out_specs.append(pl.BlockSpec((1, ctx_len, d), lambda i, b: (b, 0, i)))
        out_shape.append(jax.ShapeDtypeStruct((bsz, ctx_len, NA_WIDTH), out_dtype))
    outs = pl.pallas_call(
        partial(_na_kernel, with_ctx=with_ctx),
        grid=(nblk, bsz),
        in_specs=in_specs, out_specs=out_specs, out_shape=out_shape,
        scratch_shapes=[pltpu.VMEM((hs, NA_WIN_R, GRID_W, NA_WIN_TOKENS), jnp.float32)],
        compiler_params=pltpu.CompilerParams(dimension_semantics=("arbitrary", "arbitrary"),
                                             vmem_limit_bytes=V7X_VMEM_LIMIT_BYTES),
        name="na_attention",
    )(*args)
    return (outs[0], outs[1]) if with_ctx else (outs[0], None)


def _dft_cos_sin(n):
    jk = np.outer(np.arange(n), np.arange(n)) % n
    ang = 2.0 * np.pi * jk / n
    return np.cos(ang) / np.sqrt(n), np.sin(ang) / np.sqrt(n)


FOUR_EXT = 8


def _fourier_kernel(u_ref, cc_ref, sc_ref, a_ref, flip_ref, fw_ref, zlo_ref, zhi_ref, o_ref, *, tm):
    bf = jnp.bfloat16
    gd, ext = FOUR_GROUP_DIM, tm + FOUR_EXT
    pos = jnp.dot(a_ref[...], u_ref[0].astype(bf), preferred_element_type=jnp.float32).astype(bf)
    even, odd = [], []
    for g in range(FOUR_GROUPS):
        cols = slice(g * gd, (g + 1) * gd)
        even.append(jnp.dot(pos[:ext, cols], cc_ref[...], preferred_element_type=jnp.float32))
        odd.append(jnp.dot(pos[ext:, cols], sc_ref[...], preferred_element_type=jnp.float32))
    even = jnp.concatenate(even, axis=1)
    odd = jnp.concatenate(odd, axis=1)
    y_lo = (even[:tm] - odd[:tm]).astype(bf)
    y_hi = jnp.dot(flip_ref[...], (even + odd).astype(bf), preferred_element_type=jnp.float32).astype(bf)
    for half, (y, z_ref) in enumerate(((y_lo, zlo_ref), (y_hi, zhi_ref))):
        f = jnp.dot(y, fw_ref[...], preferred_element_type=jnp.float32)
        zf = z_ref[0].astype(jnp.float32)
        o_ref[0, half] = (f * (zf * jax.nn.sigmoid(zf))).astype(o_ref.dtype)


def fourier_branch(p, four_w_bf, layer, *, tm):
    bsz, seq, _ = p.shape
    half = seq // 2
    assert half % tm == 0
    nt2, ext = half // tm, tm + FOUR_EXT
    cc, sc = _dft_cos_sin(FOUR_GROUP_DIM)
    cl, sl = _dft_cos_sin(seq)
    a = np.stack([np.concatenate([cl[i * tm:i * tm + ext], sl[i * tm:i * tm + ext]], axis=0) for i in range(nt2)])
    flip = np.zeros((tm, ext))
    flip[np.arange(tm), tm - np.arange(tm)] = 1.0
    fw, gd = FOUR_WIDTH, FOUR_GROUP_DIM
    bf = jnp.bfloat16
    out = pl.pallas_call(
        partial(_fourier_kernel, tm=tm),
        grid=(bsz, nt2),
        in_specs=[pl.BlockSpec((1, seq, fw), lambda b, i: (b, 0, 4)),
                  pl.BlockSpec((gd, gd), lambda b, i: (0, 0)),
                  pl.BlockSpec((gd, gd), lambda b, i: (0, 0)),
                  pl.BlockSpec((None, 2 * ext, seq), lambda b, i: (i, 0, 0)),
                  pl.BlockSpec((tm, ext), lambda b, i: (0, 0)),
                  pl.BlockSpec((None, fw, fw), lambda b, i: (layer, 0, 0)),
                  pl.BlockSpec((1, tm, fw), lambda b, i: (b, i, 5)),
                  pl.BlockSpec((1, tm, fw), lambda b, i: (b, 2 * nt2 - 1 - i, 5))],
        out_specs=pl.BlockSpec((1, 2, tm, fw), lambda b, i: (b, 0, i, 0)),
        out_shape=jax.ShapeDtypeStruct((bsz, 2, half, fw), bf),
        compiler_params=pltpu.CompilerParams(dimension_semantics=("arbitrary", "arbitrary"),
                                             vmem_limit_bytes=V7X_VMEM_LIMIT_BYTES),
        name="fourier_branch",
    )(p, jnp.asarray(cc, bf), jnp.asarray(sc, bf), jnp.asarray(a, bf), jnp.asarray(flip, bf), four_w_bf, p, p)
    return out.reshape(bsz, seq, fw), nt2


def fourier_row_block(t, nt2):
    return jnp.where(t < nt2, t, 3 * nt2 - 1 - t)


MERGE_TN = 1024


def _merge_kernel(a_ref, zna_ref, gfo_ref, s_ref, zs_ref, nw_ref, wa_ref, wf_ref, ws_ref, *rest):
    n_half = D_MODEL // MERGE_TN
    gate_refs, (o_ref, ap_ref, sp_ref) = rest[:3 * n_half], rest[3 * n_half:]
    bf = jnp.bfloat16
    zna = zna_ref[...].astype(jnp.float32)
    ap_ref[...] = (a_ref[...].astype(jnp.float32) * (zna * jax.nn.sigmoid(zna))).astype(bf)
    zs = zs_ref[...].astype(jnp.float32)
    t = s_ref[...].astype(jnp.float32) * (zs * jax.nn.sigmoid(zs))
    t = t * lax.rsqrt(jnp.mean(t * t, axis=-1, keepdims=True) + EPS) * nw_ref[...]
    sp_ref[...] = t.astype(bf)
    for h in range(n_half):
        cols = slice(h * MERGE_TN, (h + 1) * MERGE_TN)
        ga_ref, gf_ref, gs_ref = gate_refs[h], gate_refs[n_half + h], gate_refs[2 * n_half + h]
        o_a = jnp.dot(ap_ref[...], wa_ref[:, cols], preferred_element_type=jnp.float32)
        o_f = jnp.dot(gfo_ref[...], wf_ref[:, cols], preferred_element_type=jnp.float32)
        o_s = jnp.dot(sp_ref[...], ws_ref[:, cols], preferred_element_type=jnp.float32)
        m = (jax.nn.sigmoid(ga_ref[...].astype(jnp.float32)) * o_a
             + jax.nn.sigmoid(gf_ref[...].astype(jnp.float32)) * o_f
             + jax.nn.sigmoid(gs_ref[...].astype(jnp.float32)) * o_s)
        o_ref[:, cols] = m.astype(o_ref.dtype)


def merge_branches(p2d, a2d, gfo2d, gfo_nt2, s2d, ssd_norm_w, wa, wf, ws, layer, *, tm):
    m = p2d.shape[0]
    w, tn = NA_WIDTH, MERGE_TN
    n_half = D_MODEL // tn
    assert m % tm == 0
    row = lambda i: (i, 0)
    if gfo_nt2 is None:
        gfo_row = row
    else:
        assert p2d.shape[0] % (2 * gfo_nt2 * tm) == 0
        per_seq = 2 * gfo_nt2
        gfo_row = lambda i: ((i // per_seq) * per_seq + fourier_row_block(i % per_seq, gfo_nt2), 0)
    resident = lambda shape, idx: pl.BlockSpec(shape, lambda i: idx, pipeline_mode=pl.Buffered(1))
    w_spec = resident((None, w, D_MODEL), (layer, 0, 0))
    gate0 = DT_OFFSET // tn
    gate_specs = [pl.BlockSpec((tm, tn), partial(lambda i, blk: (i, blk), blk=gate0 + g * n_half + h))
                  for g in range(3) for h in range(n_half)]
    return pl.pallas_call(
        _merge_kernel,
        grid=(m // tm,),
        in_specs=[pl.BlockSpec((tm, w), row),
                  pl.BlockSpec((tm, w), lambda i: (i, 3)),
                  pl.BlockSpec((tm, w), gfo_row),
                  pl.BlockSpec((tm, w), row),
                  pl.BlockSpec((tm, w), lambda i: (i, 8)),
                  resident((1, w), (0, 0)), w_spec, w_spec, w_spec] + gate_specs,
        out_specs=pl.BlockSpec((tm, D_MODEL), row),
        out_shape=jax.ShapeDtypeStruct((m, D_MODEL), jnp.bfloat16),
        scratch_shapes=[pltpu.VMEM((tm, w), jnp.bfloat16), pltpu.VMEM((tm, w), jnp.bfloat16)],
        compiler_params=pltpu.CompilerParams(dimension_semantics=("arbitrary",),
                                             vmem_limit_bytes=V7X_VMEM_LIMIT_BYTES),
        name="merge_branches",
    )(a2d, p2d, gfo2d, s2d, p2d, ssd_norm_w.reshape(1, w), wa, wf, ws, *([p2d] * (3 * n_half)))


def _outproj_kernel(m_ref, w_ref, x_ref, g_ref, *rest, final_norm):
    y = jnp.dot(m_ref[...], w_ref[...], preferred_element_type=jnp.float32)
    r = x_ref[...] + g_ref[0] * y
    if final_norm:
        nw_ref, o_ref = rest
        o_ref[...] = r * lax.rsqrt(jnp.mean(r * r, axis=-1, keepdims=True) + EPS) * nw_ref[...]
    else:
        rest[0][...] = r


def out_projection(m2d, w_out_bf, layer, x2d, gate, *, rows_per_mod, tm, tn, final_norm_w=None):
    m, d = x2d.shape
    assert m % tm == 0 and d % tn == 0 and rows_per_mod % tm == 0
    per = rows_per_mod // tm
    in_specs = [pl.BlockSpec((tm, d), lambda i, j: (i, 0)),
                pl.BlockSpec((None, d, tn), lambda i, j: (layer, 0, j)),
                pl.BlockSpec((tm, tn), lambda i, j: (i, j)),
                pl.BlockSpec((1, 1, tn), lambda i, j: (i // per, 0, j))]
    args = [m2d, w_out_bf, x2d, gate]
    if final_norm_w is not None:
        assert tn == d
        in_specs.append(pl.BlockSpec((1, d), lambda i, j: (0, 0)))
        args.append(final_norm_w.astype(jnp.float32).reshape(1, d))
    return pl.pallas_call(
        partial(_outproj_kernel, final_norm=final_norm_w is not None),
        grid=(m // tm, d // tn),
        in_specs=in_specs,
        out_specs=pl.BlockSpec((tm, tn), lambda i, j: (i, j)),
        out_shape=jax.ShapeDtypeStruct((m, d), jnp.float32),
        compiler_params=pltpu.CompilerParams(dimension_semantics=("arbitrary", "arbitrary"),
                                             vmem_limit_bytes=V7X_VMEM_LIMIT_BYTES),
        name="out_projection",
    )(*args)


SSD_GW = SSD_HPG * SSD_HEAD_DIM
CONV_PAD = 8
SSD_CHUNK_UNROLL = 2
SSD_CONV_UNROLL = 4
SSD_C_UNROLL = 8
SSD_A_UNROLL = 6


def _split3_bf16(a):
    rnd = lambda v: v.astype(jnp.bfloat16).astype(jnp.float32)
    a1 = rnd(a)
    a2 = rnd(a - a1)
    a3 = rnd((a - a1) - a2)
    return a1, a2, a3


def _head_rows(rows, r0, n):
    return jnp.concatenate([jnp.broadcast_to(rows[r0 + k:r0 + k + 1, :], (SSD_HEAD_DIM, n))
                            for k in range(SSD_HPG)], axis=0)


def _ssd_kernel(xl_ref, bl_ref, cl_ref, xc_ref, bc_ref, cc_ref, dtl_ref, dtc_ref, bias_ref, alog_ref, dsk_ref,
                wx_ref, wb_ref, wc_ref, bx_ref, bb_ref, bcb_ref, cos_ref, sin_ref,
                yl_ref, yc_ref,
                padx, padb, padc, xt_s, b_s, c_s, dtt_s, row_s, dec_s, yt_s, p_s, sp_s, st_s, *, seq, ctx_len):
    f32, bf = jnp.float32, jnp.bfloat16
    t = SSD_CHUNK
    nc_c, nc_l = ctx_len // t, seq // t
    nc = nc_c + nc_l

    lane = lax.broadcasted_iota(jnp.int32, (t, LANES), 1)
    first_half = (lane % 64) < 32

    def conv_tile(pad, w_ref, b_ref, t0, l0):
        acc = jnp.broadcast_to(b_ref[:, l0:l0 + LANES], (t, LANES))
        for j in range(SSD_CONV):
            off = CONV_PAD - SSD_CONV // 2 + j
            acc = acc + pad[pl.ds(t0 + off, t), :] * w_ref[j:j + 1, l0:l0 + LANES]
        return acc * jax.nn.sigmoid(acc)

    def rope(u, t0):
        swapped = jnp.where(first_half, pltpu.roll(u, 96, axis=1), pltpu.roll(u, 32, axis=1))
        return u * cos_ref[pl.ds(t0, t), :] + swapped * sin_ref[pl.ds(t0, t), :]

    def conv_seq(x_ref, bm_ref, cm_ref, n, base, use_rope):
        slabs = ([(padx.at[h], x_ref, h * LANES) for h in range(SSD_GW // LANES)]
                 + [(padb, bm_ref, 0), (padc, cm_ref, 0)])
        for pad, src, l0 in slabs:
            pad[0:CONV_PAD, :] = jnp.zeros((CONV_PAD, LANES), f32)
            pad[CONV_PAD + n:2 * CONV_PAD + n, :] = jnp.zeros((CONV_PAD, LANES), f32)
            pad[CONV_PAD:CONV_PAD + n, :] = src[0, :, l0:l0 + LANES].astype(f32)

        unroll = min(SSD_CONV_UNROLL, n // t)

        def chunks(i, carry):
            for j in range(unroll):
                ci = i * unroll + j
                t0 = pl.multiple_of(ci * t, t)
                r0 = pl.multiple_of(base + ci * t, t)
                for h in range(SSD_GW // LANES):
                    l0 = h * LANES
                    xt_s[base // t + ci, l0:l0 + LANES, :] = conv_tile(padx.at[h], wx_ref, bx_ref, t0, l0).T
                ub = conv_tile(padb, wb_ref, bb_ref, t0, 0)
                uc = conv_tile(padc, wc_ref, bcb_ref, t0, 0)
                if use_rope:
                    ub, uc = rope(ub, t0), rope(uc, t0)
                b_s[pl.ds(r0, t), :] = ub.astype(bf)
                c_s[pl.ds(r0, t), :] = uc.astype(bf)
            return carry

        lax.fori_loop(0, n // t // unroll, chunks, 0)

    conv_seq(xc_ref, bc_ref, cc_ref, ctx_len, 0, False)
    conv_seq(xl_ref, bl_ref, cl_ref, seq, ctx_len, True)

    i0 = lax.broadcasted_iota(jnp.int32, (t, t), 0)
    i1 = lax.broadcasted_iota(jnp.int32, (t, t), 1)
    tri_f = (i0 <= i1).astype(bf)
    tri_r = (i0 >= i1).astype(bf)
    nr = 2 * SSD_HPG
    fwd_row = (lax.broadcasted_iota(jnp.int32, (nc * nr, 1), 0) % nr) < SSD_HPG
    g4 = pl.program_id(1) * SSD_HPG
    raw = []
    for c in range(nc):
        src, c0 = (dtc_ref, c) if c < nc_c else (dtl_ref, c - nc_c)
        dtt_s[c] = src[0, c0 * t:(c0 + 1) * t, :].T
        raw += [dtt_s[c, pl.ds(g4, SSD_HPG), :], dtt_s[c, pl.ds(SSD_HEADS + g4, SSD_HPG), :]]
    x = jnp.concatenate(raw, axis=0) + jnp.tile(bias_ref[0], (nc, 1))
    dt = jnp.maximum(x, 0.0) + jnp.log1p(jnp.exp(-jnp.abs(x)))
    pieces = _split3_bf16(dt * jnp.tile(-jnp.exp(alog_ref[0]), (nc, 1)))
    pf = [jnp.dot(p.astype(bf), tri_f, preferred_element_type=f32) for p in pieces]
    pr = [jnp.dot(p.astype(bf), tri_r, preferred_element_type=f32) for p in pieces]
    cs = jnp.where(fwd_row, (pf[0] + pf[1]) + pf[2], (pr[0] + pr[1]) + pr[2])
    tot = jnp.where(fwd_row, cs[:, t - 1:t], cs[:, 0:1])
    e_in = jnp.exp(cs)
    e_out = dt * jnp.exp(tot - cs)
    u = cs - jnp.log(dt)
    decb = jnp.broadcast_to(jnp.exp(tot), (nc * nr, LANES))
    for c in range(nc):
        sl = slice(c * nr, (c + 1) * nr)
        row_s[c] = jnp.concatenate([cs[sl], dt[sl], e_in[sl], e_out[sl], u[sl]], axis=0)
        dec_s[c] = decb[sl]

    lower = i1 <= i0
    lower_t = i0 <= i1
    diag = i1 == i0

    def chunks_a(i, carry):
        ids = [i * SSD_A_UNROLL + j for j in range(SSD_A_UNROLL)]
        r0s = [pl.multiple_of(c * t, t) for c in ids]
        cms = [c_s[pl.ds(r0, t), :] for r0 in r0s]
        bms = [b_s[pl.ds(r0, t), :] for r0 in r0s]
        gs = [lax.dot_general(cm, bm, _NT, preferred_element_type=f32) for cm, bm in zip(cms, bms)]
        for c, r0, bm, g in zip(ids, r0s, bms, gs):
            rows = row_s[c]
            xt = xt_s[c]
            xt_bf = xt.astype(bf)
            parts = []
            for k in range(SSD_HPG):
                kr = SSD_HPG + k
                cs_l = jnp.where(lower_t, rows[k:k + 1, :], rows[kr:kr + 1, :]).T
                w = jnp.exp(cs_l - jnp.where(lower, rows[32 + k:33 + k, :], rows[32 + kr:33 + kr, :]))
                mk = (g * (w + jnp.where(diag, rows[8 + kr:9 + kr, :], 0.0))).astype(bf)
                xk = xt_bf[k * SSD_HEAD_DIM:(k + 1) * SSD_HEAD_DIM, :]
                parts.append(lax.dot_general(xk, mk, _NT, preferred_element_type=f32))
            yt_s[c] = dsk_ref[0] * xt + jnp.concatenate(parts, axis=0)
            for d in range(2):
                wt = (xt * _head_rows(rows, 24 + SSD_HPG * d, t)).astype(bf)
                p_s[d, c] = jnp.dot(wt, bm, preferred_element_type=f32)
        return carry

    lax.fori_loop(0, nc // SSD_A_UNROLL, chunks_a, 0)

    st_s[...] = jnp.zeros(st_s.shape, f32)

    def state_step(c, d):
        st = st_s[d]
        sp_s[d, c] = st.astype(bf)
        st_s[d] = st * _head_rows(dec_s[c], SSD_HPG * d, SSD_STATE) + p_s[d, c]

    def ctx_step(i, carry):
        state_step(i, 0)
        state_step(nc_c - 1 - i, 1)
        return carry

    def lat_step(i, carry):
        state_step(nc_c + i, 0)
        state_step(nc - 1 - i, 1)
        return carry

    lax.fori_loop(0, nc_c, ctx_step, 0)
    lax.fori_loop(0, nc_l, lat_step, 0)

    def chunks_c(first, out_ref, o_first, count):
        ids = [first + u for u in range(count)]
        cms = [c_s[pl.ds(pl.multiple_of(c * t, t), t), :] for c in ids]
        offs = [[lax.dot_general(sp_s[d, c], cm, _NT, preferred_element_type=f32) for d in range(2)]
                for c, cm in zip(ids, cms)]
        for u, (c, off) in enumerate(zip(ids, offs)):
            rows = row_s[c]
            yt = yt_s[c]
            for d in range(2):
                yt = yt + off[d] * _head_rows(rows, 16 + SSD_HPG * d, t)
            for l0 in range(0, SSD_GW, LANES):
                out_ref[0, pl.ds(pl.multiple_of((o_first + u) * t, t), t), l0:l0 + LANES] = yt[l0:l0 + LANES, :].T

    def ctx_out(i, carry):
        chunks_c(i * SSD_CHUNK_UNROLL, yc_ref, i * SSD_CHUNK_UNROLL, SSD_CHUNK_UNROLL)
        return carry

    def lat_out(i, carry):
        chunks_c(nc_c + i * SSD_C_UNROLL, yl_ref, i * SSD_C_UNROLL, SSD_C_UNROLL)
        return carry

    lax.fori_loop(0, nc_c // SSD_CHUNK_UNROLL, ctx_out, 0)
    lax.fori_loop(0, nc_l // SSD_C_UNROLL, lat_out, 0)


def _rope_tables(seq):
    quarter = SSD_STATE // 4
    inv = ROPE_BASE ** (-np.arange(quarter, dtype=np.float64) / quarter)
    pos = np.arange(seq)
    ang_r = (pos // GRID_W)[:, None] * inv
    ang_c = (pos % GRID_W)[:, None] * inv
    cos = np.concatenate([np.cos(ang_r)] * 2 + [np.cos(ang_c)] * 2, axis=1)
    sin = np.concatenate([-np.sin(ang_r), np.sin(ang_r), -np.sin(ang_c), np.sin(ang_c)], axis=1)
    return jnp.asarray(cos, jnp.float32), jnp.asarray(sin, jnp.float32)


def ssd_mixer(p_lat, p_ctx, dt_lat, dt_ctx, conv_w, conv_b, dt_bias, a_log, d_skip, ctx_cols=FULL_COLS):
    bsz, seq, _ = p_lat.shape
    ctx_len = p_ctx.shape[1]
    t, g, hpg = SSD_CHUNK, SSD_GROUPS, SSD_HPG
    assert seq % t == 0 and ctx_len % t == 0
    nc = (seq + ctx_len) // t
    per_row = lambda v: jnp.broadcast_to(
        v.astype(jnp.float32).reshape(2, g, hpg).transpose(1, 0, 2).reshape(g, 2 * hpg, 1), (g, 2 * hpg, LANES))
    dsk = jnp.broadcast_to(jnp.repeat(d_skip.astype(jnp.float32).reshape(g, hpg), SSD_HEAD_DIM, axis=1)[:, :, None],
                           (g, SSD_GW, LANES))
    cos, sin = _rope_tables(seq)
    x0 = sum(MAIN_SPLITS[:6]) // SSD_GW
    b0 = (sum(MAIN_SPLITS[:6]) + SSD_WIDTH) // SSD_STATE
    c0 = b0 + g
    cw = conv_w.astype(jnp.float32)
    cb = conv_b.astype(jnp.float32).reshape(1, SSD_CONV_CH)
    wb0 = SSD_WIDTH // SSD_STATE
    ltot = seq + ctx_len
    f32, bf = jnp.float32, jnp.bfloat16
    y_lat, y_ctx = pl.pallas_call(
        partial(_ssd_kernel, seq=seq, ctx_len=ctx_len),
        grid=(bsz, g),
        in_specs=[pl.BlockSpec((1, seq, SSD_GW), lambda b, i: (b, 0, x0 + i)),
                  pl.BlockSpec((1, seq, SSD_STATE), lambda b, i: (b, 0, b0 + i)),
                  pl.BlockSpec((1, seq, SSD_STATE), lambda b, i: (b, 0, c0 + i)),
                  pl.BlockSpec((1, ctx_len, SSD_GW), lambda b, i: (b, 0, ctx_cols.xbc // SSD_GW + i)),
                  pl.BlockSpec((1, ctx_len, SSD_STATE), lambda b, i: (b, 0, (ctx_cols.xbc + SSD_WIDTH) // SSD_STATE + i)),
                  pl.BlockSpec((1, ctx_len, SSD_STATE), lambda b, i: (b, 0, (ctx_cols.xbc + SSD_WIDTH) // SSD_STATE + g + i)),
                  pl.BlockSpec((1, seq, DT_PAD), lambda b, i: (b, 0, 0)),
                  pl.BlockSpec((1, ctx_len, DT_PAD), lambda b, i: (b, 0, 0)),
                  pl.BlockSpec((1, 2 * hpg, LANES), lambda b, i: (i, 0, 0)),
                  pl.BlockSpec((1, 2 * hpg, LANES), lambda b, i: (i, 0, 0)),
                  pl.BlockSpec((1, SSD_GW, LANES), lambda b, i: (i, 0, 0)),
                  pl.BlockSpec((SSD_CONV, SSD_GW), lambda b, i: (0, i)),
                  pl.BlockSpec((SSD_CONV, SSD_STATE), lambda b, i: (0, wb0 + i)),
                  pl.BlockSpec((SSD_CONV, SSD_STATE), lambda b, i: (0, wb0 + g + i)),
                  pl.BlockSpec((1, SSD_GW), lambda b, i: (0, i)),
                  pl.BlockSpec((1, SSD_STATE), lambda b, i: (0, wb0 + i)),
                  pl.BlockSpec((1, SSD_STATE), lambda b, i: (0, wb0 + g + i)),
                  pl.BlockSpec((seq, SSD_STATE), lambda b, i: (0, 0)),
                  pl.BlockSpec((seq, SSD_STATE), lambda b, i: (0, 0))],
        out_specs=[pl.BlockSpec((1, seq, SSD_GW), lambda b, i: (b, 0, i)),
                   pl.BlockSpec((1, ctx_len, SSD_GW), lambda b, i: (b, 0, i))],
        out_shape=[jax.ShapeDtypeStruct((bsz, seq, SSD_WIDTH), f32),
                   jax.ShapeDtypeStruct((bsz, ctx_len, SSD_WIDTH), f32)],
        scratch_shapes=[pltpu.VMEM((SSD_GW // LANES, seq + 2 * CONV_PAD, LANES), f32),
                        pltpu.VMEM((seq + 2 * CONV_PAD, SSD_STATE), f32),
                        pltpu.VMEM((seq + 2 * CONV_PAD, SSD_STATE), f32),
                        pltpu.VMEM((nc, SSD_GW, t), f32),
                        pltpu.VMEM((ltot, SSD_STATE), bf),
                        pltpu.VMEM((ltot, SSD_STATE), bf),
                        pltpu.VMEM((nc, DT_PAD, t), f32),
                        pltpu.VMEM((nc, 5 * 2 * hpg, t), f32),
                        pltpu.VMEM((nc, 2 * hpg, LANES), f32),
                        pltpu.VMEM((nc, SSD_GW, t), f32),
                        pltpu.VMEM((2, nc, SSD_GW, SSD_STATE), f32),
                        pltpu.VMEM((2, nc, SSD_GW, SSD_STATE), bf),
                        pltpu.VMEM((2, SSD_GW, SSD_STATE), f32)],
        compiler_params=pltpu.CompilerParams(dimension_semantics=("arbitrary", "arbitrary"),
                                             vmem_limit_bytes=V7X_VMEM_LIMIT_BYTES),
        name="ssd_mixer",
    )(p_lat, p_lat, p_lat, p_ctx, p_ctx, p_ctx, dt_lat, dt_ctx, per_row(dt_bias), per_row(a_log), dsk,
      cw, cw, cw, cb, cb, cb, cos, sin)
    return y_ctx, y_lat


INPROJ_TM = 1024
INPROJ_TN = 2560
ADA_TN = 1024
BRANCH_TM = 512
CTX_OUT_TILE = 1024
ADA_ROWS = 16


def _layer(xc, xl, c, c_ctx, w_ada_all, layer, b_ada, norm_w, w_main_all, w_dt_all, rpb, four_wb, conv_w, conv_b, dt_bias,
           a_log, d_skip, ssd_norm_w, wb_nab, wb_fourb, wb_ssdb, w_outb, update_ctx, final_norm_w):
    bsz, seq, d = xl.shape
    ctx_len = xc.shape[1]
    bf = jnp.bfloat16
    assert bsz < ADA_ROWS

    cc = jnp.concatenate([c, c_ctx[None], jnp.zeros((ADA_ROWS - bsz - 1, d), c.dtype)], axis=0)
    mod = matmul(jax.nn.silu(cc), w_ada_all, layer, tm=ADA_ROWS, tn=ADA_TN) + b_ada
    sh, sc, gt = jnp.split(mod, 3, axis=-1)
    sh_l, sc_l, g_l = sh[:bsz], sc[:bsz], gt[:bsz]
    sh_c, sc_c, g_c = sh[bsz:bsz + 1], sc[bsz:bsz + 1], gt[bsz]

    xl2d = xl.reshape(bsz * seq, d)
    xc2d = xc.reshape(bsz * ctx_len, d)
    pl2d, dt_l = in_projection(xl2d, norm_w, sc_l[:, None], sh_l[:, None], w_main_all, layer, w_dt_all, layer,
                               rows_per_mod=seq, tm=INPROJ_TM, tn=INPROJ_TN, out_dtype=bf)
    if update_ctx:
        ctx_cols, w_ctx, ctx_layer, ctx_tn = FULL_COLS, w_main_all, layer, INPROJ_TN
    else:
        ctx_cols, ctx_layer = KV_XBC_COLS, 0
        w_ctx = jnp.concatenate([w_main_all[layer, FULL_COLS.k:FULL_COLS.v + NA_WIDTH],
                                 w_main_all[layer, FULL_COLS.xbc:FULL_COLS.xbc + SSD_CONV_CH]], axis=0)[None]
        ctx_tn = w_ctx.shape[1] // 2
    pc2d, dt_c = in_projection(xc2d, norm_w, sc_c[:, None], sh_c[:, None], w_ctx, ctx_layer, w_dt_all, layer,
                               rows_per_mod=bsz * ctx_len, tm=INPROJ_TM, tn=ctx_tn, out_dtype=bf)
    p_l = pl2d.reshape(bsz, seq, MAIN_WIDTH)
    p_c = pc2d.reshape(bsz, ctx_len, w_ctx.shape[1])
    dt_l = dt_l.reshape(bsz, seq, DT_PAD)
    dt_c = dt_c.reshape(bsz, ctx_len, DT_PAD)

    a_l, a_c = na_attention(p_l, p_c, na_rpb_rows(rpb), with_ctx=update_ctx, out_dtype=bf, ctx_cols=ctx_cols)
    gfo_l, nt2_l = fourier_branch(p_l, four_wb, layer, tm=BRANCH_TM)
    s_c, s_l = ssd_mixer(p_l, p_c, dt_l, dt_c, conv_w, conv_b, dt_bias, a_log, d_skip, ctx_cols=ctx_cols)
    m_l = merge_branches(pl2d, a_l.reshape(bsz * seq, NA_WIDTH), gfo_l.reshape(bsz * seq, FOUR_WIDTH), nt2_l,
                         s_l.reshape(bsz * seq, SSD_WIDTH), ssd_norm_w, wb_nab, wb_fourb, wb_ssdb, layer, tm=BRANCH_TM)
    xl_new = out_projection(m_l, w_outb, layer, xl2d, g_l[:, None], rows_per_mod=seq, tm=BRANCH_TM, tn=d,
                            final_norm_w=final_norm_w)
    xl_new = xl_new.reshape(bsz, seq, d)
    if update_ctx:
        gfo_c, nt2_c = fourier_branch(p_c, four_wb, layer, tm=ctx_len // 2)
        assert nt2_c == 1
        m_c = merge_branches(pc2d, a_c.reshape(bsz * ctx_len, NA_WIDTH), gfo_c.reshape(bsz * ctx_len, FOUR_WIDTH), None,
                             s_c.reshape(bsz * ctx_len, SSD_WIDTH), ssd_norm_w, wb_nab, wb_fourb, wb_ssdb, layer, tm=BRANCH_TM)
        xc = out_projection(m_c, w_outb, layer, xc2d, g_c[None, None], rows_per_mod=bsz * ctx_len, tm=CTX_OUT_TILE, tn=CTX_OUT_TILE)
        xc = xc.reshape(bsz, ctx_len, d)
    return xc, xl_new


def kernel(x, c, ctx, c_ctx, w_ada, b_ada, norm_w, w_in, na_rpb, four_w, ssd_conv_w, ssd_conv_b, ssd_dt_bias,
           ssd_a_log, ssd_d, ssd_norm_w, wb_na, wb_four, wb_ssd, w_out, final_norm_w):
    xc, xl = ctx, x
    w_main_all, w_dt_all = repack_w_in(w_in)
    four_wb, wb_nab, wb_fourb, wb_ssdb, w_outb = (t.astype(jnp.bfloat16) for t in (four_w, wb_na, wb_four, wb_ssd, w_out))
    for l in range(DEPTH):
        xc, xl = _layer(xc, xl, c, c_ctx, w_ada, l, b_ada[l], norm_w[l], w_main_all, w_dt_all, na_rpb[l], four_wb,
                        ssd_conv_w[l], ssd_conv_b[l], ssd_dt_bias[l], ssd_a_log[l], ssd_d[l], ssd_norm_w[l],
                        wb_nab, wb_fourb, wb_ssdb, w_outb, update_ctx=(l < DEPTH - 1),
                        final_norm_w=final_norm_w if l == DEPTH - 1 else None)
    return xl
```

```python
import math
from functools import partial
from typing import NamedTuple

import jax
import jax.numpy as jnp
import numpy as np
from jax import lax
from jax.experimental import pallas as pl
from jax.experimental.pallas import tpu as pltpu

D_MODEL = 2048
DEPTH = 2
GRID_W = 64
EPS = 1e-6
NA_HEADS = 8
NA_HEAD_DIM = 128
NA_WIDTH = NA_HEADS * NA_HEAD_DIM
NA_WIN_R = 8
NA_WIN_C = 16
LOG2E = math.log2(math.e)
FOUR_GROUPS = 4
FOUR_GROUP_DIM = 256
FOUR_WIDTH = FOUR_GROUPS * FOUR_GROUP_DIM
SSD_HEADS = 16
SSD_HEAD_DIM = 64
SSD_WIDTH = SSD_HEADS * SSD_HEAD_DIM
SSD_GROUPS = 4
SSD_HPG = SSD_HEADS // SSD_GROUPS
SSD_STATE = 128
SSD_CONV = 7
SSD_CHUNK = 128
SSD_CONV_CH = SSD_WIDTH + 2 * SSD_GROUPS * SSD_STATE
ROPE_BASE = 10000.0
DT_WIDTH = 2 * SSD_HEADS
MAIN_SPLITS = (NA_WIDTH, NA_WIDTH, NA_WIDTH, NA_WIDTH, FOUR_WIDTH, FOUR_WIDTH, SSD_CONV_CH, SSD_WIDTH,
               D_MODEL, D_MODEL, D_MODEL)
MAIN_WIDTH = sum(MAIN_SPLITS)
DT_OFFSET = sum(MAIN_SPLITS[:8])
DT_PAD = 128


class ProjCols(NamedTuple):
    k: int
    v: int
    xbc: int


FULL_COLS = ProjCols(k=NA_WIDTH, v=2 * NA_WIDTH, xbc=sum(MAIN_SPLITS[:6]))
KV_XBC_COLS = ProjCols(k=0, v=NA_WIDTH, xbc=2 * NA_WIDTH)

LANES = 128
V7X_VMEM_LIMIT_BYTES = 56 * 1024 * 1024


def _mm_kernel(a_ref, b_ref, o_ref):
    a = a_ref[...].astype(jnp.bfloat16)
    b = b_ref[...].astype(jnp.bfloat16)
    o_ref[...] = jnp.dot(a, b, preferred_element_type=jnp.float32).astype(o_ref.dtype)


def matmul(a, b, layer, *, tm, tn, out_dtype=jnp.float32):
    m, k = a.shape
    _, _, n = b.shape
    assert m % tm == 0 and n % tn == 0, (a.shape, b.shape, tm, tn)
    return pl.pallas_call(
        _mm_kernel,
        grid=(m // tm, n // tn),
        in_specs=[pl.BlockSpec((tm, k), lambda i, j: (i, 0)),
                  pl.BlockSpec((None, k, tn), lambda i, j: (layer, 0, j))],
        out_specs=pl.BlockSpec((tm, tn), lambda i, j: (i, j)),
        out_shape=jax.ShapeDtypeStruct((m, n), out_dtype),
        compiler_params=pltpu.CompilerParams(dimension_semantics=("arbitrary", "arbitrary"),
                                             vmem_limit_bytes=V7X_VMEM_LIMIT_BYTES),
        name="matmul",
    )(a, b)


REPACK_TN = 1024


def _repack_kernel(a_ref, nxt_ref, o_ref, odt_ref):
    j = pl.program_id(1)
    first_below = DT_OFFSET // REPACK_TN

    @pl.when(j < first_below)
    def _():
        o_ref[...] = a_ref[...].astype(o_ref.dtype)

    @pl.when(j >= first_below)
    def _():
        o_ref[...] = jnp.concatenate([a_ref[DT_WIDTH:, :], nxt_ref[...]], axis=0).astype(o_ref.dtype)

    @pl.when(j == first_below)
    def _():
        odt_ref[...] = a_ref[:DT_PAD, :].astype(odt_ref.dtype)


def repack_w_in(w_in):
    depth, d, _ = w_in.shape
    tn = REPACK_TN
    w_t = jnp.swapaxes(w_in, 1, 2)
    return pl.pallas_call(
        _repack_kernel,
        grid=(depth, MAIN_WIDTH // tn),
        in_specs=[pl.BlockSpec((None, tn, d), lambda l, j: (l, j, 0)),
                  pl.BlockSpec((None, DT_WIDTH, d), lambda l, j: (l, (j + 1) * (tn // DT_WIDTH), 0))],
        out_specs=[pl.BlockSpec((None, tn, d), lambda l, j: (l, j, 0)),
                   pl.BlockSpec((None, DT_PAD, d), lambda l, j: (l, 0, 0))],
        out_shape=[jax.ShapeDtypeStruct((depth, MAIN_WIDTH, d), jnp.bfloat16),
                   jax.ShapeDtypeStruct((depth, DT_PAD, d), jnp.bfloat16)],
        compiler_params=pltpu.CompilerParams(dimension_semantics=("arbitrary", "arbitrary"),
                                             vmem_limit_bytes=V7X_VMEM_LIMIT_BYTES),
        name="repack_w_in",
    )(w_t, w_t)


def _inproj_kernel(x_ref, nw_ref, sc_ref, sh_ref, w_ref, wdt_ref, o_ref, dt_ref, h_ref):
    @pl.when(pl.program_id(1) == 0)
    def _():
        x = x_ref[...]
        y = x * lax.rsqrt(jnp.mean(x * x, axis=-1, keepdims=True) + EPS) * nw_ref[...]
        h = (y * (1.0 + sc_ref[0]) + sh_ref[0]).astype(jnp.bfloat16)
        h_ref[...] = h
        dt_ref[...] = lax.dot_general(h, wdt_ref[...], _NT, preferred_element_type=jnp.float32)

    o_ref[...] = lax.dot_general(h_ref[...], w_ref[...], _NT, preferred_element_type=jnp.float32).astype(o_ref.dtype)


def in_projection(x2d, norm_w, scale, shift, w_main_t, layer, w_dt_t, dt_layer, *, rows_per_mod, tm, tn, out_dtype):
    m, d = x2d.shape
    n = w_main_t.shape[1]
    assert m % tm == 0 and n % tn == 0 and rows_per_mod % tm == 0
    per = rows_per_mod // tm
    return pl.pallas_call(
        _inproj_kernel,
        grid=(m // tm, n // tn),
        in_specs=[pl.BlockSpec((tm, d), lambda i, j: (i, 0)),
                  pl.BlockSpec((1, d), lambda i, j: (0, 0)),
                  pl.BlockSpec((1, 1, d), lambda i, j: (i // per, 0, 0)),
                  pl.BlockSpec((1, 1, d), lambda i, j: (i // per, 0, 0)),
                  pl.BlockSpec((None, tn, d), lambda i, j: (layer, j, 0)),
                  pl.BlockSpec((None, DT_PAD, d), lambda i, j: (dt_layer, 0, 0))],
        out_specs=[pl.BlockSpec((tm, tn), lambda i, j: (i, j)),
                   pl.BlockSpec((tm, DT_PAD), lambda i, j: (i, 0))],
        out_shape=[jax.ShapeDtypeStruct((m, n), out_dtype),
                   jax.ShapeDtypeStruct((m, DT_PAD), jnp.float32)],
        scratch_shapes=[pltpu.VMEM((tm, d), jnp.bfloat16)],
        compiler_params=pltpu.CompilerParams(dimension_semantics=("arbitrary", "arbitrary"),
                                             vmem_limit_bytes=V7X_VMEM_LIMIT_BYTES),
        name="in_projection",
    )(x2d, norm_w.reshape(1, d), scale, shift, w_main_t, w_dt_t)


GRID_ROWS = 32
NA_WIN_TOKENS = NA_WIN_R * GRID_W
_NT = (((1,), (1,)), ((), ()))
NA_HEADS_PER_STEP = 2
NA_ROW_UNROLL = 32


def _na_kernel(*refs, with_ctx):
    if with_ctx:
        q_ref, k_ref, v_ref, kc_ref, vc_ref, rpb_ref, qc_ref, o_ref, oc_ref, bias_s = refs
    else:
        q_ref, k_ref, v_ref, kc_ref, vc_ref, rpb_ref, o_ref, bias_s = refs
    bf = jnp.bfloat16

    @pl.when(pl.program_id(1) == 0)
    def _():
        qi = lax.broadcasted_iota(jnp.int32, (GRID_W, LANES), 0)
        li = lax.broadcasted_iota(jnp.int32, (GRID_W, LANES), 1)
        qstart = jnp.clip(qi - NA_WIN_C // 2, 0, GRID_W - NA_WIN_C)
        in_window = (li >= qstart) & (li < qstart + NA_WIN_C)
        for hh in range(NA_HEADS_PER_STEP):
            lo, hi = [], []
            for dr in range(2 * NA_WIN_R - 1):
                row = jnp.broadcast_to(rpb_ref[hh, dr:dr + 1, :], (GRID_W, LANES))
                blk = pltpu.roll(row, LANES - (NA_WIN_C - 1), axis=1, stride=1, stride_axis=0)
                blk = jnp.where(in_window, blk * LOG2E, -1e30)
                lo.append(blk)
                hi.append(pltpu.roll(blk, GRID_W, axis=1))
            for dl in range(NA_WIN_R):
                for pr in range(NA_WIN_R // 2):
                    d0 = 2 * pr - dl + NA_WIN_R - 1
                    bias_s[hh, dl, :, pr * LANES:(pr + 1) * LANES] = jnp.where(li < GRID_W, lo[d0], hi[d0 + 1])

    scale = NA_HEAD_DIM ** -0.5
    scale2 = scale * LOG2E
    nq = NA_ROW_UNROLL * GRID_W

    def lane_tiles(x, op):
        acc = x[:, 0:LANES]
        for l0 in range(LANES, x.shape[1], LANES):
            acc = op(acc, x[:, l0:l0 + LANES])
        return acc

    for hh in range(NA_HEADS_PER_STEP):
        lanes = slice(hh * NA_HEAD_DIM, (hh + 1) * NA_HEAD_DIM)
        kc = kc_ref[0, :, lanes].astype(bf)
        vc = vc_ref[0, :, lanes].astype(bf)

        def rows(rb, carry, hh=hh, lanes=lanes, kc=kc, vc=vc):
            qb0 = pl.multiple_of(rb * nq, nq)
            q_all = q_ref[0, pl.ds(qb0, nq), lanes].astype(bf)
            s_c_all = lax.dot_general(q_all, kc, _NT, preferred_element_type=jnp.float32) * scale2
            k0s, scores, probs = [], [], []
            for i in range(NA_ROW_UNROLL):
                r = rb * NA_ROW_UNROLL + i
                start = jnp.clip(r - NA_WIN_R // 2, 0, GRID_ROWS - NA_WIN_R)
                k0 = pl.multiple_of(start * GRID_W, GRID_W)
                q = q_all[i * GRID_W:(i + 1) * GRID_W]
                kw = k_ref[0, pl.ds(k0, NA_WIN_TOKENS), lanes].astype(bf)
                s_w = (lax.dot_general(q, kw, _NT, preferred_element_type=jnp.float32) * scale2
                       + bias_s[hh, r - start])
                k0s.append(k0)
                scores.append((s_w, s_c_all[i * GRID_W:(i + 1) * GRID_W]))
            for s_w, s_c in scores:
                m = jnp.max(jnp.maximum(lane_tiles(s_w, jnp.maximum), lane_tiles(s_c, jnp.maximum)),
                            axis=-1, keepdims=True)
                e_w = jnp.exp2(s_w - m)
                e_c = jnp.exp2(s_c - m)
                denom = jnp.sum(lane_tiles(e_w, jnp.add) + lane_tiles(e_c, jnp.add), axis=-1, keepdims=True)
                probs.append((e_w.astype(bf), e_c.astype(bf), denom))
            o_c_all = jnp.dot(jnp.concatenate([p[1] for p in probs], axis=0), vc, preferred_element_type=jnp.float32)
            for i, (k0, (e_w, _, denom)) in enumerate(zip(k0s, probs)):
                vw = v_ref[0, pl.ds(k0, NA_WIN_TOKENS), lanes].astype(bf)
                o = jnp.dot(e_w, vw, preferred_element_type=jnp.float32) + o_c_all[i * GRID_W:(i + 1) * GRID_W]
                o_ref[0, pl.ds(qb0 + i * GRID_W, GRID_W), lanes] = (o / denom).astype(o_ref.dtype)
            return carry

        lax.fori_loop(0, GRID_ROWS // NA_ROW_UNROLL, rows, 0)

        if with_ctx:
            s = lax.dot_general(qc_ref[0, :, lanes].astype(bf), kc, _NT, preferred_element_type=jnp.float32) * scale
            e = jnp.exp(s - jnp.max(s, axis=-1, keepdims=True))
            o = jnp.dot(e.astype(bf), vc, preferred_element_type=jnp.float32)
            oc_ref[0, :, lanes] = (o / jnp.sum(e, axis=-1, keepdims=True)).astype(oc_ref.dtype)


def na_rpb_rows(rpb):
    h, nr, nc = rpb.shape
    return jnp.pad(rpb.astype(jnp.float32), ((0, 0), (0, 2 * NA_WIN_R - nr), (0, LANES - nc)))


def na_attention(p_lat, p_ctx, rpb_rows, *, with_ctx, out_dtype, ctx_cols=FULL_COLS):
    bsz, seq, _ = p_lat.shape
    ctx_len = p_ctx.shape[1]
    assert seq == GRID_ROWS * GRID_W
    hs = NA_HEADS_PER_STEP
    d, nblk = hs * NA_HEAD_DIM, NA_HEADS // hs
    in_specs = [pl.BlockSpec((1, seq, d), lambda i, b: (b, 0, i)),
                pl.BlockSpec((1, seq, d), lambda i, b: (b, 0, nblk + i)),
                pl.BlockSpec((1, seq, d), lambda i, b: (b, 0, 2 * nblk + i)),
                pl.BlockSpec((1, ctx_len, d), lambda i, b: (b, 0, ctx_cols.k // d + i)),
                pl.BlockSpec((1, ctx_len, d), lambda i, b: (b, 0, ctx_cols.v // d + i)),
                pl.BlockSpec((hs, 2 * NA_WIN_R, LANES), lambda i, b: (i, 0, 0))]
    args = [p_lat, p_lat, p_lat, p_ctx, p_ctx, rpb_rows]
    out_specs = [pl.BlockSpec((1, seq, d), lambda i, b: (b, 0, i))]
    out_shape = [jax.ShapeDtypeStruct((bsz, seq, NA_WIDTH), out_dtype)]
    if with_ctx:
        in_specs.append(pl.BlockSpec((1, ctx_len, d), lambda i, b: (b, 0, i)))
        args.append(p_ctx)
        out_specs.append(pl.BlockSpec((1, ctx_len, d), lambda i, b: (b, 0, i)))
        out_shape.append(jax.ShapeDtypeStruct((bsz, ctx_len, NA_WIDTH), out_dtype))
    outs = pl.pallas_call(
        partial(_na_kernel, with_ctx=with_ctx),
        grid=(nblk, bsz),
        in_specs=in_specs, out_specs=out_specs, out_shape=out_shape,
        scratch_shapes=[pltpu.VMEM((hs, NA_WIN_R, GRID_W, NA_WIN_TOKENS), jnp.float32)],
        compiler_params=pltpu.CompilerParams(dimension_semantics=("arbitrary", "arbitrary"),
                                             vmem_limit_bytes=V7X_VMEM_LIMIT_BYTES),
        name="na_attention",
    )(*args)
    return (outs[0], outs[1]) if with_ctx else (outs[0], None)


def _dft_cos_sin(n):
    jk = np.outer(np.arange(n), np.arange(n)) % n
    ang = 2.0 * np.pi * jk / n
    return np.cos(ang) / np.sqrt(n), np.sin(ang) / np.sqrt(n)


FOUR_EXT = 8


def _fourier_kernel(u_ref, cc_ref, sc_ref, a_ref, flip_ref, fw_ref, zlo_ref, zhi_ref, o_ref, *, tm):
    bf = jnp.bfloat16
    gd, ext = FOUR_GROUP_DIM, tm + FOUR_EXT
    pos = jnp.dot(a_ref[...], u_ref[0].astype(bf), preferred_element_type=jnp.float32).astype(bf)
    even, odd = [], []
    for g in range(FOUR_GROUPS):
        cols = slice(g * gd, (g + 1) * gd)
        even.append(jnp.dot(pos[:ext, cols], cc_ref[...], preferred_element_type=jnp.float32))
        odd.append(jnp.dot(pos[ext:, cols], sc_ref[...], preferred_element_type=jnp.float32))
    even = jnp.concatenate(even, axis=1)
    odd = jnp.concatenate(odd, axis=1)
    y_lo = (even[:tm] - odd[:tm]).astype(bf)
    y_hi = jnp.dot(flip_ref[...], (even + odd).astype(bf), preferred_element_type=jnp.float32).astype(bf)
    for half, (y, z_ref) in enumerate(((y_lo, zlo_ref), (y_hi, zhi_ref))):
        f = jnp.dot(y, fw_ref[...], preferred_element_type=jnp.float32)
        zf = z_ref[0].astype(jnp.float32)
        o_ref[0, half] = (f * (zf * jax.nn.sigmoid(zf))).astype(o_ref.dtype)


def fourier_branch(p, four_w_bf, layer, *, tm):
    bsz, seq, _ = p.shape
    half = seq // 2
    assert half % tm == 0
    nt2, ext = half // tm, tm + FOUR_EXT
    cc, sc = _dft_cos_sin(FOUR_GROUP_DIM)
    cl, sl = _dft_cos_sin(seq)
    a = np.stack([np.concatenate([cl[i * tm:i * tm + ext], sl[i * tm:i * tm + ext]], axis=0) for i in range(nt2)])
    flip = np.zeros((tm, ext))
    flip[np.arange(tm), tm - np.arange(tm)] = 1.0
    fw, gd = FOUR_WIDTH, FOUR_GROUP_DIM
    bf = jnp.bfloat16
    out = pl.pallas_call(
        partial(_fourier_kernel, tm=tm),
        grid=(bsz, nt2),
        in_specs=[pl.BlockSpec((1, seq, fw), lambda b, i: (b, 0, 4)),
                  pl.BlockSpec((gd, gd), lambda b, i: (0, 0)),
                  pl.BlockSpec((gd, gd), lambda b, i: (0, 0)),
                  pl.BlockSpec((None, 2 * ext, seq), lambda b, i: (i, 0, 0)),
                  pl.BlockSpec((tm, ext), lambda b, i: (0, 0)),
                  pl.BlockSpec((None, fw, fw), lambda b, i: (layer, 0, 0)),
                  pl.BlockSpec((1, tm, fw), lambda b, i: (b, i, 5)),
                  pl.BlockSpec((1, tm, fw), lambda b, i: (b, 2 * nt2 - 1 - i, 5))],
        out_specs=pl.BlockSpec((1, 2, tm, fw), lambda b, i: (b, 0, i, 0)),
        out_shape=jax.ShapeDtypeStruct((bsz, 2, half, fw), bf),
        compiler_params=pltpu.CompilerParams(dimension_semantics=("arbitrary", "arbitrary"),
                                             vmem_limit_bytes=V7X_VMEM_LIMIT_BYTES),
        name="fourier_branch",
    )(p, jnp.asarray(cc, bf), jnp.asarray(sc, bf), jnp.asarray(a, bf), jnp.asarray(flip, bf), four_w_bf, p, p)
    return out.reshape(bsz, seq, fw), nt2


def fourier_row_block(t, nt2):
    return jnp.where(t < nt2, t, 3 * nt2 - 1 - t)


MERGE_TN = 1024


def _merge_kernel(a_ref, zna_ref, gfo_ref, s_ref, zs_ref, nw_ref, wa_ref, wf_ref, ws_ref, *rest):
    n_half = D_MODEL // MERGE_TN
    gate_refs, (o_ref, ap_ref, sp_ref) = rest[:3 * n_half], rest[3 * n_half:]
    bf = jnp.bfloat16
    zna = zna_ref[...].astype(jnp.float32)
    ap_ref[...] = (a_ref[...].astype(jnp.float32) * (zna * jax.nn.sigmoid(zna))).astype(bf)
    zs = zs_ref[...].astype(jnp.float32)
    t = s_ref[...].astype(jnp.float32) * (zs * jax.nn.sigmoid(zs))
    t = t * lax.rsqrt(jnp.mean(t * t, axis=-1, keepdims=True) + EPS) * nw_ref[...]
    sp_ref[...] = t.astype(bf)
    for h in range(n_half):
        cols = slice(h * MERGE_TN, (h + 1) * MERGE_TN)
        ga_ref, gf_ref, gs_ref = gate_refs[h], gate_refs[n_half + h], gate_refs[2 * n_half + h]
        o_a = jnp.dot(ap_ref[...], wa_ref[:, cols], preferred_element_type=jnp.float32)
        o_f = jnp.dot(gfo_ref[...], wf_ref[:, cols], preferred_element_type=jnp.float32)
        o_s = jnp.dot(sp_ref[...], ws_ref[:, cols], preferred_element_type=jnp.float32)
        m = (jax.nn.sigmoid(ga_ref[...].astype(jnp.float32)) * o_a
             + jax.nn.sigmoid(gf_ref[...].astype(jnp.float32)) * o_f
             + jax.nn.sigmoid(gs_ref[...].astype(jnp.float32)) * o_s)
        o_ref[:, cols] = m.astype(o_ref.dtype)


def merge_branches(p2d, a2d, gfo2d, gfo_nt2, s2d, ssd_norm_w, wa, wf, ws, layer, *, tm):
    m = p2d.shape[0]
    w, tn = NA_WIDTH, MERGE_TN
    n_half = D_MODEL // tn
    assert m % tm == 0
    row = lambda i: (i, 0)
    if gfo_nt2 is None:
        gfo_row = row
    else:
        assert p2d.shape[0] % (2 * gfo_nt2 * tm) == 0
        per_seq = 2 * gfo_nt2
        gfo_row = lambda i: ((i // per_seq) * per_seq + fourier_row_block(i % per_seq, gfo_nt2), 0)
    resident = lambda shape, idx: pl.BlockSpec(shape, lambda i: idx, pipeline_mode=pl.Buffered(1))
    w_spec = resident((None, w, D_MODEL), (layer, 0, 0))
    gate0 = DT_OFFSET // tn
    gate_specs = [pl.BlockSpec((tm, tn), partial(lambda i, blk: (i, blk), blk=gate0 + g * n_half + h))
                  for g in range(3) for h in range(n_half)]
    return pl.pallas_call(
        _merge_kernel,
        grid=(m // tm,),
        in_specs=[pl.BlockSpec((tm, w), row),
                  pl.BlockSpec((tm, w), lambda i: (i, 3)),
                  pl.BlockSpec((tm, w), gfo_row),
                  pl.BlockSpec((tm, w), row),
                  pl.BlockSpec((tm, w), lambda i: (i, 8)),
                  resident((1, w), (0, 0)), w_spec, w_spec, w_spec] + gate_specs,
        out_specs=pl.BlockSpec((tm, D_MODEL), row),
        out_shape=jax.ShapeDtypeStruct((m, D_MODEL), jnp.bfloat16),
        scratch_shapes=[pltpu.VMEM((tm, w), jnp.bfloat16), pltpu.VMEM((tm, w), jnp.bfloat16)],
        compiler_params=pltpu.CompilerParams(dimension_semantics=("arbitrary",),
                                             vmem_limit_bytes=V7X_VMEM_LIMIT_BYTES),
        name="merge_branches",
    )(a2d, p2d, gfo2d, s2d, p2d, ssd_norm_w.reshape(1, w), wa, wf, ws, *([p2d] * (3 * n_half)))


def _outproj_kernel(m_ref, w_ref, x_ref, g_ref, *rest, final_norm):
    y = jnp.dot(m_ref[...], w_ref[...], preferred_element_type=jnp.float32)
    r = x_ref[...] + g_ref[0] * y
    if final_norm:
        nw_ref, o_ref = rest
        o_ref[...] = r * lax.rsqrt(jnp.mean(r * r, axis=-1, keepdims=True) + EPS) * nw_ref[...]
    else:
        rest[0][...] = r


def out_projection(m2d, w_out_bf, layer, x2d, gate, *, rows_per_mod, tm, tn, final_norm_w=None):
    m, d = x2d.shape
    assert m % tm == 0 and d % tn == 0 and rows_per_mod % tm == 0
    per = rows_per_mod // tm
    in_specs = [pl.BlockSpec((tm, d), lambda i, j: (i, 0)),
                pl.BlockSpec((None, d, tn), lambda i, j: (layer, 0, j)),
                pl.BlockSpec((tm, tn), lambda i, j: (i, j)),
                pl.BlockSpec((1, 1, tn), lambda i, j: (i // per, 0, j))]
    args = [m2d, w_out_bf, x2d, gate]
    if final_norm_w is not None:
        assert tn == d
        in_specs.append(pl.BlockSpec((1, d), lambda i, j: (0, 0)))
        args.append(final_norm_w.astype(jnp.float32).reshape(1, d))
    return pl.pallas_call(
        partial(_outproj_kernel, final_norm=final_norm_w is not None),
        grid=(m // tm, d // tn),
        in_specs=in_specs,
        out_specs=pl.BlockSpec((tm, tn), lambda i, j: (i, j)),
        out_shape=jax.ShapeDtypeStruct((m, d), jnp.float32),
        compiler_params=pltpu.CompilerParams(dimension_semantics=("arbitrary", "arbitrary"),
                                             vmem_limit_bytes=V7X_VMEM_LIMIT_BYTES),
        name="out_projection",
    )(*args)


SSD_GW = SSD_HPG * SSD_HEAD_DIM
CONV_PAD = 8
SSD_CHUNK_UNROLL = 2
SSD_CONV_UNROLL = 4
SSD_C_UNROLL = 8
SSD_A_UNROLL = 9


def _split3_bf16(a):
    rnd = lambda v: v.astype(jnp.bfloat16).astype(jnp.float32)
    a1 = rnd(a)
    a2 = rnd(a - a1)
    a3 = rnd((a - a1) - a2)
    return a1, a2, a3


def _head_rows(rows, r0, n):
    return jnp.concatenate([jnp.broadcast_to(rows[r0 + k:r0 + k + 1, :], (SSD_HEAD_DIM, n))
                            for k in range(SSD_HPG)], axis=0)


def _ssd_kernel(xl_ref, bl_ref, cl_ref, xc_ref, bc_ref, cc_ref, dtl_ref, dtc_ref, bias_ref, alog_ref, dsk_ref,
                wx_ref, wb_ref, wc_ref, bx_ref, bb_ref, bcb_ref, cos_ref, sin_ref,
                yl_ref, yc_ref,
                padx, padb, padc, xt_s, b_s, c_s, dtt_s, row_s, dec_s, yt_s, p_s, sp_s, st_s, *, seq, ctx_len):
    f32, bf = jnp.float32, jnp.bfloat16
    t = SSD_CHUNK
    nc_c, nc_l = ctx_len // t, seq // t
    nc = nc_c + nc_l

    lane = lax.broadcasted_iota(jnp.int32, (t, LANES), 1)
    first_half = (lane % 64) < 32

    def conv_tile(pad, w_ref, b_ref, t0, l0):
        acc = jnp.broadcast_to(b_ref[:, l0:l0 + LANES], (t, LANES))
        for j in range(SSD_CONV):
            off = CONV_PAD - SSD_CONV // 2 + j
            acc = acc + pad[pl.ds(t0 + off, t), :] * w_ref[j:j + 1, l0:l0 + LANES]
        return acc * jax.nn.sigmoid(acc)

    def rope(u, t0):
        swapped = jnp.where(first_half, pltpu.roll(u, 96, axis=1), pltpu.roll(u, 32, axis=1))
        return u * cos_ref[pl.ds(t0, t), :] + swapped * sin_ref[pl.ds(t0, t), :]

    def conv_seq(x_ref, bm_ref, cm_ref, n, base, use_rope):
        slabs = ([(padx.at[h], x_ref, h * LANES) for h in range(SSD_GW // LANES)]
                 + [(padb, bm_ref, 0), (padc, cm_ref, 0)])
        for pad, src, l0 in slabs:
            pad[0:CONV_PAD, :] = jnp.zeros((CONV_PAD, LANES), f32)
            pad[CONV_PAD + n:2 * CONV_PAD + n, :] = jnp.zeros((CONV_PAD, LANES), f32)
            pad[CONV_PAD:CONV_PAD + n, :] = src[0, :, l0:l0 + LANES].astype(f32)

        unroll = min(SSD_CONV_UNROLL, n // t)

        def chunks(i, carry):
            for j in range(unroll):
                ci = i * unroll + j
                t0 = pl.multiple_of(ci * t, t)
                r0 = pl.multiple_of(base + ci * t, t)
                for h in range(SSD_GW // LANES):
                    l0 = h * LANES
                    xt_s[base // t + ci, l0:l0 + LANES, :] = conv_tile(padx.at[h], wx_ref, bx_ref, t0, l0).T
                ub = conv_tile(padb, wb_ref, bb_ref, t0, 0)
                uc = conv_tile(padc, wc_ref, bcb_ref, t0, 0)
                if use_rope:
                    ub, uc = rope(ub, t0), rope(uc, t0)
                b_s[pl.ds(r0, t), :] = ub.astype(bf)
                c_s[pl.ds(r0, t), :] = uc.astype(bf)
            return carry

        lax.fori_loop(0, n // t // unroll, chunks, 0)

    conv_seq(xc_ref, bc_ref, cc_ref, ctx_len, 0, False)
    conv_seq(xl_ref, bl_ref, cl_ref, seq, ctx_len, True)

    i0 = lax.broadcasted_iota(jnp.int32, (t, t), 0)
    i1 = lax.broadcasted_iota(jnp.int32, (t, t), 1)
    tri_f = (i0 <= i1).astype(bf)
    tri_r = (i0 >= i1).astype(bf)
    nr = 2 * SSD_HPG
    fwd_row = (lax.broadcasted_iota(jnp.int32, (nc * nr, 1), 0) % nr) < SSD_HPG
    g4 = pl.program_id(1) * SSD_HPG
    raw = []
    for c in range(nc):
        src, c0 = (dtc_ref, c) if c < nc_c else (dtl_ref, c - nc_c)
        dtt_s[c] = src[0, c0 * t:(c0 + 1) * t, :].T
        raw += [dtt_s[c, pl.ds(g4, SSD_HPG), :], dtt_s[c, pl.ds(SSD_HEADS + g4, SSD_HPG), :]]
    x = jnp.concatenate(raw, axis=0) + jnp.tile(bias_ref[0], (nc, 1))
    dt = jnp.maximum(x, 0.0) + jnp.log1p(jnp.exp(-jnp.abs(x)))
    pieces = _split3_bf16(dt * jnp.tile(-jnp.exp(alog_ref[0]), (nc, 1)))
    pf = [jnp.dot(p.astype(bf), tri_f, preferred_element_type=f32) for p in pieces]
    pr = [jnp.dot(p.astype(bf), tri_r, preferred_element_type=f32) for p in pieces]
    cs = jnp.where(fwd_row, (pf[0] + pf[1]) + pf[2], (pr[0] + pr[1]) + pr[2])
    tot = jnp.where(fwd_row, cs[:, t - 1:t], cs[:, 0:1])
    e_in = jnp.exp(cs)
    e_out = dt * jnp.exp(tot - cs)
    u = cs - jnp.log(dt)
    decb = jnp.broadcast_to(jnp.exp(tot), (nc * nr, LANES))
    for c in range(nc):
        sl = slice(c * nr, (c + 1) * nr)
        row_s[c] = jnp.concatenate([cs[sl], dt[sl], e_in[sl], e_out[sl], u[sl]], axis=0)
        dec_s[c] = decb[sl]

    lower = i1 <= i0
    lower_t = i0 <= i1
    diag = i1 == i0

    def chunks_a(i, carry):
        ids = [i * SSD_A_UNROLL + j for j in range(SSD_A_UNROLL)]
        r0s = [pl.multiple_of(c * t, t) for c in ids]
        cms = [c_s[pl.ds(r0, t), :] for r0 in r0s]
        bms = [b_s[pl.ds(r0, t), :] for r0 in r0s]
        gs = [lax.dot_general(cm, bm, _NT, preferred_element_type=f32) for cm, bm in zip(cms, bms)]
        for c, r0, bm, g in zip(ids, r0s, bms, gs):
            rows = row_s[c]
            xt = xt_s[c]
            xt_bf = xt.astype(bf)
            parts = []
            for k in range(SSD_HPG):
                kr = SSD_HPG + k
                cs_l = jnp.where(lower_t, rows[k:k + 1, :], rows[kr:kr + 1, :]).T
                w = jnp.exp(cs_l - jnp.where(lower, rows[32 + k:33 + k, :], rows[32 + kr:33 + kr, :]))
                mk = (g * (w + jnp.where(diag, rows[8 + kr:9 + kr, :], 0.0))).astype(bf)
                xk = xt_bf[k * SSD_HEAD_DIM:(k + 1) * SSD_HEAD_DIM, :]
                parts.append(lax.dot_general(xk, mk, _NT, preferred_element_type=f32))
            yt_s[c] = dsk_ref[0] * xt + jnp.concatenate(parts, axis=0)
            for d in range(2):
                wt = (xt * _head_rows(rows, 24 + SSD_HPG * d, t)).astype(bf)
                p_s[d, c] = jnp.dot(wt, bm, preferred_element_type=f32)
        return carry

    lax.fori_loop(0, nc // SSD_A_UNROLL, chunks_a, 0)

    st_s[...] = jnp.zeros(st_s.shape, f32)

    def state_step(c, d):
        st = st_s[d]
        sp_s[d, c] = st.astype(bf)
        st_s[d] = st * _head_rows(dec_s[c], SSD_HPG * d, SSD_STATE) + p_s[d, c]

    def ctx_step(i, carry):
        state_step(i, 0)
        state_step(nc_c - 1 - i, 1)
        return carry

    def lat_step(i, carry):
        state_step(nc_c + i, 0)
        state_step(nc - 1 - i, 1)
        return carry

    lax.fori_loop(0, nc_c, ctx_step, 0)
    lax.fori_loop(0, nc_l, lat_step, 0)

    def chunks_c(first, out_ref, o_first, count):
        ids = [first + u for u in range(count)]
        cms = [c_s[pl.ds(pl.multiple_of(c * t, t), t), :] for c in ids]
        offs = [[lax.dot_general(sp_s[d, c], cm, _NT, preferred_element_type=f32) for d in range(2)]
                for c, cm in zip(ids, cms)]
        for u, (c, off) in enumerate(zip(ids, offs)):
            rows = row_s[c]
            yt = yt_s[c]
            for d in range(2):
                yt = yt + off[d] * _head_rows(rows, 16 + SSD_HPG * d, t)
            for l0 in range(0, SSD_GW, LANES):
                out_ref[0, pl.ds(pl.multiple_of((o_first + u) * t, t), t), l0:l0 + LANES] = yt[l0:l0 + LANES, :].T

    def ctx_out(i, carry):
        chunks_c(i * SSD_CHUNK_UNROLL, yc_ref, i * SSD_CHUNK_UNROLL, SSD_CHUNK_UNROLL)
        return carry

    def lat_out(i, carry):
        chunks_c(nc_c + i * SSD_C_UNROLL, yl_ref, i * SSD_C_UNROLL, SSD_C_UNROLL)
        return carry

    lax.fori_loop(0, nc_c // SSD_CHUNK_UNROLL, ctx_out, 0)
    lax.fori_loop(0, nc_l // SSD_C_UNROLL, lat_out, 0)


def _rope_tables(seq):
    quarter = SSD_STATE // 4
    inv = ROPE_BASE ** (-np.arange(quarter, dtype=np.float64) / quarter)
    pos = np.arange(seq)
    ang_r = (pos // GRID_W)[:, None] * inv
    ang_c = (pos % GRID_W)[:, None] * inv
    cos = np.concatenate([np.cos(ang_r)] * 2 + [np.cos(ang_c)] * 2, axis=1)
    sin = np.concatenate([-np.sin(ang_r), np.sin(ang_r), -np.sin(ang_c), np.sin(ang_c)], axis=1)
    return jnp.asarray(cos, jnp.float32), jnp.asarray(sin, jnp.float32)


def ssd_mixer(p_lat, p_ctx, dt_lat, dt_ctx, conv_w, conv_b, dt_bias, a_log, d_skip, ctx_cols=FULL_COLS):
    bsz, seq, _ = p_lat.shape
    ctx_len = p_ctx.shape[1]
    t, g, hpg = SSD_CHUNK, SSD_GROUPS, SSD_HPG
    assert seq % t == 0 and ctx_len % t == 0
    nc = (seq + ctx_len) // t
    per_row = lambda v: jnp.broadcast_to(
        v.astype(jnp.float32).reshape(2, g, hpg).transpose(1, 0, 2).reshape(g, 2 * hpg, 1), (g, 2 * hpg, LANES))
    dsk = jnp.broadcast_to(jnp.repeat(d_skip.astype(jnp.float32).reshape(g, hpg), SSD_HEAD_DIM, axis=1)[:, :, None],
                           (g, SSD_GW, LANES))
    cos, sin = _rope_tables(seq)
    x0 = sum(MAIN_SPLITS[:6]) // SSD_GW
    b0 = (sum(MAIN_SPLITS[:6]) + SSD_WIDTH) // SSD_STATE
    c0 = b0 + g
    cw = conv_w.astype(jnp.float32)
    cb = conv_b.astype(jnp.float32).reshape(1, SSD_CONV_CH)
    wb0 = SSD_WIDTH // SSD_STATE
    ltot = seq + ctx_len
    f32, bf = jnp.float32, jnp.bfloat16
    y_lat, y_ctx = pl.pallas_call(
        partial(_ssd_kernel, seq=seq, ctx_len=ctx_len),
        grid=(bsz, g),
        in_specs=[pl.BlockSpec((1, seq, SSD_GW), lambda b, i: (b, 0, x0 + i)),
                  pl.BlockSpec((1, seq, SSD_STATE), lambda b, i: (b, 0, b0 + i)),
                  pl.BlockSpec((1, seq, SSD_STATE), lambda b, i: (b, 0, c0 + i)),
                  pl.BlockSpec((1, ctx_len, SSD_GW), lambda b, i: (b, 0, ctx_cols.xbc // SSD_GW + i)),
                  pl.BlockSpec((1, ctx_len, SSD_STATE), lambda b, i: (b, 0, (ctx_cols.xbc + SSD_WIDTH) // SSD_STATE + i)),
                  pl.BlockSpec((1, ctx_len, SSD_STATE), lambda b, i: (b, 0, (ctx_cols.xbc + SSD_WIDTH) // SSD_STATE + g + i)),
                  pl.BlockSpec((1, seq, DT_PAD), lambda b, i: (b, 0, 0)),
                  pl.BlockSpec((1, ctx_len, DT_PAD), lambda b, i: (b, 0, 0)),
                  pl.BlockSpec((1, 2 * hpg, LANES), lambda b, i: (i, 0, 0)),
                  pl.BlockSpec((1, 2 * hpg, LANES), lambda b, i: (i, 0, 0)),
                  pl.BlockSpec((1, SSD_GW, LANES), lambda b, i: (i, 0, 0)),
                  pl.BlockSpec((SSD_CONV, SSD_GW), lambda b, i: (0, i)),
                  pl.BlockSpec((SSD_CONV, SSD_STATE), lambda b, i: (0, wb0 + i)),
                  pl.BlockSpec((SSD_CONV, SSD_STATE), lambda b, i: (0, wb0 + g + i)),
                  pl.BlockSpec((1, SSD_GW), lambda b, i: (0, i)),
                  pl.BlockSpec((1, SSD_STATE), lambda b, i: (0, wb0 + i)),
                  pl.BlockSpec((1, SSD_STATE), lambda b, i: (0, wb0 + g + i)),
                  pl.BlockSpec((seq, SSD_STATE), lambda b, i: (0, 0)),
                  pl.BlockSpec((seq, SSD_STATE), lambda b, i: (0, 0))],
        out_specs=[pl.BlockSpec((1, seq, SSD_GW), lambda b, i: (b, 0, i)),
                   pl.BlockSpec((1, ctx_len, SSD_GW), lambda b, i: (b, 0, i))],
        out_shape=[jax.ShapeDtypeStruct((bsz, seq, SSD_WIDTH), f32),
                   jax.ShapeDtypeStruct((bsz, ctx_len, SSD_WIDTH), f32)],
        scratch_shapes=[pltpu.VMEM((SSD_GW // LANES, seq + 2 * CONV_PAD, LANES), f32),
                        pltpu.VMEM((seq + 2 * CONV_PAD, SSD_STATE), f32),
                        pltpu.VMEM((seq + 2 * CONV_PAD, SSD_STATE), f32),
                        pltpu.VMEM((nc, SSD_GW, t), f32),
                        pltpu.VMEM((ltot, SSD_STATE), bf),
                        pltpu.VMEM((ltot, SSD_STATE), bf),
                        pltpu.VMEM((nc, DT_PAD, t), f32),
                        pltpu.VMEM((nc, 5 * 2 * hpg, t), f32),
                        pltpu.VMEM((nc, 2 * hpg, LANES), f32),
                        pltpu.VMEM((nc, SSD_GW, t), f32),
                        pltpu.VMEM((2, nc, SSD_GW, SSD_STATE), f32),
                        pltpu.VMEM((2, nc, SSD_GW, SSD_STATE), bf),
                        pltpu.VMEM((2, SSD_GW, SSD_STATE), f32)],
        compiler_params=pltpu.CompilerParams(dimension_semantics=("arbitrary", "arbitrary"),
                                             vmem_limit_bytes=V7X_VMEM_LIMIT_BYTES),
        name="ssd_mixer",
    )(p_lat, p_lat, p_lat, p_ctx, p_ctx, p_ctx, dt_lat, dt_ctx, per_row(dt_bias), per_row(a_log), dsk,
      cw, cw, cw, cb, cb, cb, cos, sin)
    return y_ctx, y_lat


INPROJ_TM = 1024
INPROJ_TN = 2560
ADA_TN = 1024
BRANCH_TM = 512
CTX_OUT_TILE = 1024
ADA_ROWS = 16


def _layer(xc, xl, c, c_ctx, w_ada_all, layer, b_ada, norm_w, w_main_all, w_dt_all, rpb, four_wb, conv_w, conv_b, dt_bias,
           a_log, d_skip, ssd_norm_w, wb_nab, wb_fourb, wb_ssdb, w_outb, update_ctx, final_norm_w):
    bsz, seq, d = xl.shape
    ctx_len = xc.shape[1]
    bf = jnp.bfloat16
    assert bsz < ADA_ROWS

    cc = jnp.concatenate([c, c_ctx[None], jnp.zeros((ADA_ROWS - bsz - 1, d), c.dtype)], axis=0)
    mod = matmul(jax.nn.silu(cc), w_ada_all, layer, tm=ADA_ROWS, tn=ADA_TN) + b_ada
    sh, sc, gt = jnp.split(mod, 3, axis=-1)
    sh_l, sc_l, g_l = sh[:bsz], sc[:bsz], gt[:bsz]
    sh_c, sc_c, g_c = sh[bsz:bsz + 1], sc[bsz:bsz + 1], gt[bsz]

    xl2d = xl.reshape(bsz * seq, d)
    xc2d = xc.reshape(bsz * ctx_len, d)
    pl2d, dt_l = in_projection(xl2d, norm_w, sc_l[:, None], sh_l[:, None], w_main_all, layer, w_dt_all, layer,
                               rows_per_mod=seq, tm=INPROJ_TM, tn=INPROJ_TN, out_dtype=bf)
    if update_ctx:
        ctx_cols, w_ctx, ctx_layer, ctx_tn = FULL_COLS, w_main_all, layer, INPROJ_TN
    else:
        ctx_cols, ctx_layer = KV_XBC_COLS, 0
        w_ctx = jnp.concatenate([w_main_all[layer, FULL_COLS.k:FULL_COLS.v + NA_WIDTH],
                                 w_main_all[layer, FULL_COLS.xbc:FULL_COLS.xbc + SSD_CONV_CH]], axis=0)[None]
        ctx_tn = w_ctx.shape[1] // 2
    pc2d, dt_c = in_projection(xc2d, norm_w, sc_c[:, None], sh_c[:, None], w_ctx, ctx_layer, w_dt_all, layer,
                               rows_per_mod=bsz * ctx_len, tm=INPROJ_TM, tn=ctx_tn, out_dtype=bf)
    p_l = pl2d.reshape(bsz, seq, MAIN_WIDTH)
    p_c = pc2d.reshape(bsz, ctx_len, w_ctx.shape[1])
    dt_l = dt_l.reshape(bsz, seq, DT_PAD)
    dt_c = dt_c.reshape(bsz, ctx_len, DT_PAD)

    a_l, a_c = na_attention(p_l, p_c, na_rpb_rows(rpb), with_ctx=update_ctx, out_dtype=bf, ctx_cols=ctx_cols)
    gfo_l, nt2_l = fourier_branch(p_l, four_wb, layer, tm=BRANCH_TM)
    s_c, s_l = ssd_mixer(p_l, p_c, dt_l, dt_c, conv_w, conv_b, dt_bias, a_log, d_skip, ctx_cols=ctx_cols)
    m_l = merge_branches(pl2d, a_l.reshape(bsz * seq, NA_WIDTH), gfo_l.reshape(bsz * seq, FOUR_WIDTH), nt2_l,
                         s_l.reshape(bsz * seq, SSD_WIDTH), ssd_norm_w, wb_nab, wb_fourb, wb_ssdb, layer, tm=BRANCH_TM)
    xl_new = out_projection(m_l, w_outb, layer, xl2d, g_l[:, None], rows_per_mod=seq, tm=BRANCH_TM, tn=d,
                            final_norm_w=final_norm_w)
    xl_new = xl_new.reshape(bsz, seq, d)
    if update_ctx:
        gfo_c, nt2_c = fourier_branch(p_c, four_wb, layer, tm=ctx_len // 2)
        assert nt2_c == 1
        m_c = merge_branches(pc2d, a_c.reshape(bsz * ctx_len, NA_WIDTH), gfo_c.reshape(bsz * ctx_len, FOUR_WIDTH), None,
                             s_c.reshape(bsz * ctx_len, SSD_WIDTH), ssd_norm_w, wb_nab, wb_fourb, wb_ssdb, layer, tm=BRANCH_TM)
        xc = out_projection(m_c, w_outb, layer, xc2d, g_c[None, None], rows_per_mod=bsz * ctx_len, tm=CTX_OUT_TILE, tn=CTX_OUT_TILE)
        xc = xc.reshape(bsz, ctx_len, d)
    return xc, xl_new


def kernel(x, c, ctx, c_ctx, w_ada, b_ada, norm_w, w_in, na_rpb, four_w, ssd_conv_w, ssd_conv_b, ssd_dt_bias,
           ssd_a_log, ssd_d, ssd_norm_w, wb_na, wb_four, wb_ssd, w_out, final_norm_w):
    xc, xl = ctx, x
    w_main_all, w_dt_all = repack_w_in(w_in)
    four_wb, wb_nab, wb_fourb, wb_ssdb, w_outb = (t.astype(jnp.bfloat16) for t in (four_w, wb_na, wb_four, wb_ssd, w_out))
    for l in range(DEPTH):
        xc, xl = _layer(xc, xl, c, c_ctx, w_ada, l, b_ada[l], norm_w[l], w_main_all, w_dt_all, na_rpb[l], four_wb,
                        ssd_conv_w[l], ssd_conv_b[l], ssd_dt_bias[l], ssd_a_log[l], ssd_d[l], ssd_norm_w[l],
                        wb_nab, wb_fourb, wb_ssdb, w_outb, update_ctx=(l < DEPTH - 1),
                        final_norm_w=final_norm_w if l == DEPTH - 1 else None)
    return xl
```

```python
import math
from functools import partial
from typing import NamedTuple

import jax
import jax.numpy as jnp
import numpy as np
from jax import lax
from jax.experimental import pallas as pl
from jax.experimental.pallas import tpu as pltpu

D_MODEL = 2048
DEPTH = 2
GRID_W = 64
EPS = 1e-6
NA_HEADS = 8
NA_HEAD_DIM = 128
NA_WIDTH = NA_HEADS * NA_HEAD_DIM
NA_WIN_R = 8
NA_WIN_C = 16
LOG2E = math.log2(math.e)
FOUR_GROUPS = 4
FOUR_GROUP_DIM = 256
FOUR_WIDTH = FOUR_GROUPS * FOUR_GROUP_DIM
SSD_HEADS = 16
SSD_HEAD_DIM = 64
SSD_WIDTH = SSD_HEADS * SSD_HEAD_DIM
SSD_GROUPS = 4
SSD_HPG = SSD_HEADS // SSD_GROUPS
SSD_STATE = 128
SSD_CONV = 7
SSD_CHUNK = 128
SSD_CONV_CH = SSD_WIDTH + 2 * SSD_GROUPS * SSD_STATE
ROPE_BASE = 10000.0
DT_WIDTH = 2 * SSD_HEADS
MAIN_SPLITS = (NA_WIDTH, NA_WIDTH, NA_WIDTH, NA_WIDTH, FOUR_WIDTH, FOUR_WIDTH, SSD_CONV_CH, SSD_WIDTH,
               D_MODEL, D_MODEL, D_MODEL)
MAIN_WIDTH = sum(MAIN_SPLITS)
DT_OFFSET = sum(MAIN_SPLITS[:8])
DT_PAD = 128


class ProjCols(NamedTuple):
    k: int
    v: int
    xbc: int


FULL_COLS = ProjCols(k=NA_WIDTH, v=2 * NA_WIDTH, xbc=sum(MAIN_SPLITS[:6]))
KV_XBC_COLS = ProjCols(k=0, v=NA_WIDTH, xbc=2 * NA_WIDTH)

LANES = 128
V7X_VMEM_LIMIT_BYTES = 56 * 1024 * 1024


def _mm_kernel(a_ref, b_ref, o_ref):
    a = a_ref[...].astype(jnp.bfloat16)
    b = b_ref[...].astype(jnp.bfloat16)
    o_ref[...] = jnp.dot(a, b, preferred_element_type=jnp.float32).astype(o_ref.dtype)


def matmul(a, b, layer, *, tm, tn, out_dtype=jnp.float32):
    m, k = a.shape
    _, _, n = b.shape
    assert m % tm == 0 and n % tn == 0, (a.shape, b.shape, tm, tn)
    return pl.pallas_call(
        _mm_kernel,
        grid=(m // tm, n // tn),
        in_specs=[pl.BlockSpec((tm, k), lambda i, j: (i, 0)),
                  pl.BlockSpec((None, k, tn), lambda i, j: (layer, 0, j))],
        out_specs=pl.BlockSpec((tm, tn), lambda i, j: (i, j)),
        out_shape=jax.ShapeDtypeStruct((m, n), out_dtype),
        compiler_params=pltpu.CompilerParams(dimension_semantics=("arbitrary", "arbitrary"),
                                             vmem_limit_bytes=V7X_VMEM_LIMIT_BYTES),
        name="matmul",
    )(a, b)


REPACK_TN = 1024


def _repack_kernel(a_ref, nxt_ref, o_ref, odt_ref):
    j = pl.program_id(1)
    first_below = DT_OFFSET // REPACK_TN

    @pl.when(j < first_below)
    def _():
        o_ref[...] = a_ref[...].astype(o_ref.dtype)

    @pl.when(j >= first_below)
    def _():
        o_ref[...] = jnp.concatenate([a_ref[DT_WIDTH:, :], nxt_ref[...]], axis=0).astype(o_ref.dtype)

    @pl.when(j == first_below)
    def _():
        odt_ref[...] = a_ref[:DT_PAD, :].astype(odt_ref.dtype)


def repack_w_in(w_in):
    depth, d, _ = w_in.shape
    tn = REPACK_TN
    w_t = jnp.swapaxes(w_in, 1, 2)
    return pl.pallas_call(
        _repack_kernel,
        grid=(depth, MAIN_WIDTH // tn),
        in_specs=[pl.BlockSpec((None, tn, d), lambda l, j: (l, j, 0)),
                  pl.BlockSpec((None, DT_WIDTH, d), lambda l, j: (l, (j + 1) * (tn // DT_WIDTH), 0))],
        out_specs=[pl.BlockSpec((None, tn, d), lambda l, j: (l, j, 0)),
                   pl.BlockSpec((None, DT_PAD, d), lambda l, j: (l, 0, 0))],
        out_shape=[jax.ShapeDtypeStruct((depth, MAIN_WIDTH, d), jnp.bfloat16),
                   jax.ShapeDtypeStruct((depth, DT_PAD, d), jnp.bfloat16)],
        compiler_params=pltpu.CompilerParams(dimension_semantics=("arbitrary", "arbitrary"),
                                             vmem_limit_bytes=V7X_VMEM_LIMIT_BYTES),
        name="repack_w_in",
    )(w_t, w_t)


def _inproj_kernel(x_ref, nw_ref, sc_ref, sh_ref, w_ref, wdt_ref, o_ref, dt_ref, h_ref):
    @pl.when(pl.program_id(1) == 0)
    def _():
        x = x_ref[...]
        y = x * lax.rsqrt(jnp.mean(x * x, axis=-1, keepdims=True) + EPS) * nw_ref[...]
        h = (y * (1.0 + sc_ref[0]) + sh_ref[0]).astype(jnp.bfloat16)
        h_ref[...] = h
        dt_ref[...] = lax.dot_general(h, wdt_ref[...], _NT, preferred_element_type=jnp.float32)

    o_ref[...] = lax.dot_general(h_ref[...], w_ref[...], _NT, preferred_element_type=jnp.float32).astype(o_ref.dtype)


def in_projection(x2d, norm_w, scale, shift, w_main_t, layer, w_dt_t, dt_layer, *, rows_per_mod, tm, tn, out_dtype):
    m, d = x2d.shape
    n = w_main_t.shape[1]
    assert m % tm == 0 and n % tn == 0 and rows_per_mod % tm == 0
    per = rows_per_mod // tm
    return pl.pallas_call(
        _inproj_kernel,
        grid=(m // tm, n // tn),
        in_specs=[pl.BlockSpec((tm, d), lambda i, j: (i, 0)),
                  pl.BlockSpec((1, d), lambda i, j: (0, 0)),
                  pl.BlockSpec((1, 1, d), lambda i, j: (i // per, 0, 0)),
                  pl.BlockSpec((1, 1, d), lambda i, j: (i // per, 0, 0)),
                  pl.BlockSpec((None, tn, d), lambda i, j: (layer, j, 0)),
                  pl.BlockSpec((None, DT_PAD, d), lambda i, j: (dt_layer, 0, 0))],
        out_specs=[pl.BlockSpec((tm, tn), lambda i, j: (i, j)),
                   pl.BlockSpec((tm, DT_PAD), lambda i, j: (i, 0))],
        out_shape=[jax.ShapeDtypeStruct((m, n), out_dtype),
                   jax.ShapeDtypeStruct((m, DT_PAD), jnp.float32)],
        scratch_shapes=[pltpu.VMEM((tm, d), jnp.bfloat16)],
        compiler_params=pltpu.CompilerParams(dimension_semantics=("arbitrary", "arbitrary"),
                                             vmem_limit_bytes=V7X_VMEM_LIMIT_BYTES),
        name="in_projection",
    )(x2d, norm_w.reshape(1, d), scale, shift, w_main_t, w_dt_t)


GRID_ROWS = 32
NA_WIN_TOKENS = NA_WIN_R * GRID_W
_NT = (((1,), (1,)), ((), ()))
NA_HEADS_PER_STEP = 2
NA_ROW_UNROLL = 32


def _na_kernel(*refs, with_ctx):
    if with_ctx:
        q_ref, k_ref, v_ref, kc_ref, vc_ref, rpb_ref, qc_ref, o_ref, oc_ref, bias_s = refs
    else:
        q_ref, k_ref, v_ref, kc_ref, vc_ref, rpb_ref, o_ref, bias_s = refs
    bf = jnp.bfloat16

    @pl.when(pl.program_id(1) == 0)
    def _():
        qi = lax.broadcasted_iota(jnp.int32, (GRID_W, LANES), 0)
        li = lax.broadcasted_iota(jnp.int32, (GRID_W, LANES), 1)
        qstart = jnp.clip(qi - NA_WIN_C // 2, 0, GRID_W - NA_WIN_C)
        in_window = (li >= qstart) & (li < qstart + NA_WIN_C)
        for hh in range(NA_HEADS_PER_STEP):
            lo, hi = [], []
            for dr in range(2 * NA_WIN_R - 1):
                row = jnp.broadcast_to(rpb_ref[hh, dr:dr + 1, :], (GRID_W, LANES))
                blk = pltpu.roll(row, LANES - (NA_WIN_C - 1), axis=1, stride=1, stride_axis=0)
                blk = jnp.where(in_window, blk * LOG2E, -1e30)
                lo.append(blk)
                hi.append(pltpu.roll(blk, GRID_W, axis=1))
            for dl in range(NA_WIN_R):
                for pr in range(NA_WIN_R // 2):
                    d0 = 2 * pr - dl + NA_WIN_R - 1
                    bias_s[hh, dl, :, pr * LANES:(pr + 1) * LANES] = jnp.where(li < GRID_W, lo[d0], hi[d0 + 1])

    scale = NA_HEAD_DIM ** -0.5
    scale2 = scale * LOG2E
    nq = NA_ROW_UNROLL * GRID_W

    def lane_tiles(x, op):
        acc = x[:, 0:LANES]
        for l0 in range(LANES, x.shape[1], LANES):
            acc = op(acc, x[:, l0:l0 + LANES])
        return acc

    for hh in range(NA_HEADS_PER_STEP):
        lanes = slice(hh * NA_HEAD_DIM, (hh + 1) * NA_HEAD_DIM)
        kc = kc_ref[0, :, lanes].astype(bf)
        vc = vc_ref[0, :, lanes].astype(bf)

        def rows(rb, carry, hh=hh, lanes=lanes, kc=kc, vc=vc):
            qb0 = pl.multiple_of(rb * nq, nq)
            q_all = q_ref[0, pl.ds(qb0, nq), lanes].astype(bf)
            s_c_all = lax.dot_general(q_all, kc, _NT, preferred_element_type=jnp.float32) * scale2
            k0s, scores, probs = [], [], []
            for i in range(NA_ROW_UNROLL):
                r = rb * NA_ROW_UNROLL + i
                start = jnp.clip(r - NA_WIN_R // 2, 0, GRID_ROWS - NA_WIN_R)
                k0 = pl.multiple_of(start * GRID_W, GRID_W)
                q = q_all[i * GRID_W:(i + 1) * GRID_W]
                kw = k_ref[0, pl.ds(k0, NA_WIN_TOKENS), lanes].astype(bf)
                s_w = (lax.dot_general(q, kw, _NT, preferred_element_type=jnp.float32) * scale2
                       + bias_s[hh, r - start])
                k0s.append(k0)
                scores.append((s_w, s_c_all[i * GRID_W:(i + 1) * GRID_W]))
            for s_w, s_c in scores:
                m = jnp.max(jnp.maximum(lane_tiles(s_w, jnp.maximum), lane_tiles(s_c, jnp.maximum)),
                            axis=-1, keepdims=True)
                e_w = jnp.exp2(s_w - m)
                e_c = jnp.exp2(s_c - m)
                denom = jnp.sum(lane_tiles(e_w, jnp.add) + lane_tiles(e_c, jnp.add), axis=-1, keepdims=True)
                probs.append((e_w.astype(bf), e_c.astype(bf), denom))
            o_c_all = jnp.dot(jnp.concatenate([p[1] for p in probs], axis=0), vc, preferred_element_type=jnp.float32)
            for i, (k0, (e_w, _, denom)) in enumerate(zip(k0s, probs)):
                vw = v_ref[0, pl.ds(k0, NA_WIN_TOKENS), lanes].astype(bf)
                o = jnp.dot(e_w, vw, preferred_element_type=jnp.float32) + o_c_all[i * GRID_W:(i + 1) * GRID_W]
                o_ref[0, pl.ds(qb0 + i * GRID_W, GRID_W), lanes] = (o / denom).astype(o_ref.dtype)
            return carry

        lax.fori_loop(0, GRID_ROWS // NA_ROW_UNROLL, rows, 0)

        if with_ctx:
            s = lax.dot_general(qc_ref[0, :, lanes].astype(bf), kc, _NT, preferred_element_type=jnp.float32) * scale
            e = jnp.exp(s - jnp.max(s, axis=-1, keepdims=True))
            o = jnp.dot(e.astype(bf), vc, preferred_element_type=jnp.float32)
            oc_ref[0, :, lanes] = (o / jnp.sum(e, axis=-1, keepdims=True)).astype(oc_ref.dtype)


def na_rpb_rows(rpb):
    h, nr, nc = rpb.shape
    return jnp.pad(rpb.astype(jnp.float32), ((0, 0), (0, 2 * NA_WIN_R - nr), (0, LANES - nc)))


def na_attention(p_lat, p_ctx, rpb_rows, *, with_ctx, out_dtype, ctx_cols=FULL_COLS):
    bsz, seq, _ = p_lat.shape
    ctx_len = p_ctx.shape[1]
    assert seq == GRID_ROWS * GRID_W
    hs = NA_HEADS_PER_STEP
    d, nblk = hs * NA_HEAD_DIM, NA_HEADS // hs
    in_specs = [pl.BlockSpec((1, seq, d), lambda i, b: (b, 0, i)),
                pl.BlockSpec((1, seq, d), lambda i, b: (b, 0, nblk + i)),
                pl.BlockSpec((1, seq, d), lambda i, b: (b, 0, 2 * nblk + i)),
                pl.BlockSpec((1, ctx_len, d), lambda i, b: (b, 0, ctx_cols.k // d + i)),
                pl.BlockSpec((1, ctx_len, d), lambda i, b: (b, 0, ctx_cols.v // d + i)),
                pl.BlockSpec((hs, 2 * NA_WIN_R, LANES), lambda i, b: (i, 0, 0))]
    args = [p_lat, p_lat, p_lat, p_ctx, p_ctx, rpb_rows]
    out_specs = [pl.BlockSpec((1, seq, d), lambda i, b: (b, 0, i))]
    out_shape = [jax.ShapeDtypeStruct((bsz, seq, NA_WIDTH), out_dtype)]
    if with_ctx:
        in_specs.append(pl.BlockSpec((1, ctx_len, d), lambda i, b: (b, 0, i)))
        args.append(p_ctx)
        out_specs.append(pl.BlockSpec((1, ctx_len, d), lambda i, b: (b, 0, i)))
        out_shape.append(jax.ShapeDtypeStruct((bsz, ctx_len, NA_WIDTH), out_dtype))
    outs = pl.pallas_call(
        partial(_na_kernel, with_ctx=with_ctx),
        grid=(nblk, bsz),
        in_specs=in_specs, out_specs=out_specs, out_shape=out_shape,
        scratch_shapes=[pltpu.VMEM((hs, NA_WIN_R, GRID_W, NA_WIN_TOKENS), jnp.float32)],
        compiler_params=pltpu.CompilerParams(dimension_semantics=("arbitrary", "arbitrary"),
                                             vmem_limit_bytes=V7X_VMEM_LIMIT_BYTES),
        name="na_attention",
    )(*args)
    return (outs[0], outs[1]) if with_ctx else (outs[0], None)


def _dft_cos_sin(n):
    jk = np.outer(np.arange(n), np.arange(n)) % n
    ang = 2.0 * np.pi * jk / n
    return np.cos(ang) / np.sqrt(n), np.sin(ang) / np.sqrt(n)


FOUR_EXT = 8


def _fourier_kernel(u_ref, cc_ref, sc_ref, a_ref, flip_ref, fw_ref, zlo_ref, zhi_ref, o_ref, *, tm):
    bf = jnp.bfloat16
    gd, ext = FOUR_GROUP_DIM, tm + FOUR_EXT
    pos = jnp.dot(a_ref[...], u_ref[0].astype(bf), preferred_element_type=jnp.float32).astype(bf)
    even, odd = [], []
    for g in range(FOUR_GROUPS):
        cols = slice(g * gd, (g + 1) * gd)
        even.append(jnp.dot(pos[:ext, cols], cc_ref[...], preferred_element_type=jnp.float32))
        odd.append(jnp.dot(pos[ext:, cols], sc_ref[...], preferred_element_type=jnp.float32))
    even = jnp.concatenate(even, axis=1)
    odd = jnp.concatenate(odd, axis=1)
    y_lo = (even[:tm] - odd[:tm]).astype(bf)
    y_hi = jnp.dot(flip_ref[...], (even + odd).astype(bf), preferred_element_type=jnp.float32).astype(bf)
    for half, (y, z_ref) in enumerate(((y_lo, zlo_ref), (y_hi, zhi_ref))):
        f = jnp.dot(y, fw_ref[...], preferred_element_type=jnp.float32)
        zf = z_ref[0].astype(jnp.float32)
        o_ref[0, half] = (f * (zf * jax.nn.sigmoid(zf))).astype(o_ref.dtype)


def fourier_branch(p, four_w_bf, layer, *, tm):
    bsz, seq, _ = p.shape
    half = seq // 2
    assert half % tm == 0
    nt2, ext = half // tm, tm + FOUR_EXT
    cc, sc = _dft_cos_sin(FOUR_GROUP_DIM)
    cl, sl = _dft_cos_sin(seq)
    a = np.stack([np.concatenate([cl[i * tm:i * tm + ext], sl[i * tm:i * tm + ext]], axis=0) for i in range(nt2)])
    flip = np.zeros((tm, ext))
    flip[np.arange(tm), tm - np.arange(tm)] = 1.0
    fw, gd = FOUR_WIDTH, FOUR_GROUP_DIM
    bf = jnp.bfloat16
    out = pl.pallas_call(
        partial(_fourier_kernel, tm=tm),
        grid=(bsz, nt2),
        in_specs=[pl.BlockSpec((1, seq, fw), lambda b, i: (b, 0, 4)),
                  pl.BlockSpec((gd, gd), lambda b, i: (0, 0)),
                  pl.BlockSpec((gd, gd), lambda b, i: (0, 0)),
                  pl.BlockSpec((None, 2 * ext, seq), lambda b, i: (i, 0, 0)),
                  pl.BlockSpec((tm, ext), lambda b, i: (0, 0)),
                  pl.BlockSpec((None, fw, fw), lambda b, i: (layer, 0, 0)),
                  pl.BlockSpec((1, tm, fw), lambda b, i: (b, i, 5)),
                  pl.BlockSpec((1, tm, fw), lambda b, i: (b, 2 * nt2 - 1 - i, 5))],
        out_specs=pl.BlockSpec((1, 2, tm, fw), lambda b, i: (b, 0, i, 0)),
        out_shape=jax.ShapeDtypeStruct((bsz, 2, half, fw), bf),
        compiler_params=pltpu.CompilerParams(dimension_semantics=("arbitrary", "arbitrary"),
                                             vmem_limit_bytes=V7X_VMEM_LIMIT_BYTES),
        name="fourier_branch",
    )(p, jnp.asarray(cc, bf), jnp.asarray(sc, bf), jnp.asarray(a, bf), jnp.asarray(flip, bf), four_w_bf, p, p)
    return out.reshape(bsz, seq, fw), nt2


def fourier_row_block(t, nt2):
    return jnp.where(t < nt2, t, 3 * nt2 - 1 - t)


MERGE_TN = 1024


def _merge_kernel(a_ref, zna_ref, gfo_ref, s_ref, zs_ref, nw_ref, wa_ref, wf_ref, ws_ref, *rest):
    n_half = D_MODEL // MERGE_TN
    gate_refs, (o_ref, ap_ref, sp_ref) = rest[:3 * n_half], rest[3 * n_half:]
    bf = jnp.bfloat16
    zna = zna_ref[...].astype(jnp.float32)
    ap_ref[...] = (a_ref[...].astype(jnp.float32) * (zna * jax.nn.sigmoid(zna))).astype(bf)
    zs = zs_ref[...].astype(jnp.float32)
    t = s_ref[...].astype(jnp.float32) * (zs * jax.nn.sigmoid(zs))
    t = t * lax.rsqrt(jnp.mean(t * t, axis=-1, keepdims=True) + EPS) * nw_ref[...]
    sp_ref[...] = t.astype(bf)
    for h in range(n_half):
        cols = slice(h * MERGE_TN, (h + 1) * MERGE_TN)
        ga_ref, gf_ref, gs_ref = gate_refs[h], gate_refs[n_half + h], gate_refs[2 * n_half + h]
        o_a = jnp.dot(ap_ref[...], wa_ref[:, cols], preferred_element_type=jnp.float32)
        o_f = jnp.dot(gfo_ref[...], wf_ref[:, cols], preferred_element_type=jnp.float32)
        o_s = jnp.dot(sp_ref[...], ws_ref[:, cols], preferred_element_type=jnp.float32)
        m = (jax.nn.sigmoid(ga_ref[...].astype(jnp.float32)) * o_a
             + jax.nn.sigmoid(gf_ref[...].astype(jnp.float32)) * o_f
             + jax.nn.sigmoid(gs_ref[...].astype(jnp.float32)) * o_s)
        o_ref[:, cols] = m.astype(o_ref.dtype)


def merge_branches(p2d, a2d, gfo2d, gfo_nt2, s2d, ssd_norm_w, wa, wf, ws, layer, *, tm):
    m = p2d.shape[0]
    w, tn = NA_WIDTH, MERGE_TN
    n_half = D_MODEL // tn
    assert m % tm == 0
    row = lambda i: (i, 0)
    if gfo_nt2 is None:
        gfo_row = row
    else:
        assert p2d.shape[0] % (2 * gfo_nt2 * tm) == 0
        per_seq = 2 * gfo_nt2
        gfo_row = lambda i: ((i // per_seq) * per_seq + fourier_row_block(i % per_seq, gfo_nt2), 0)
    resident = lambda shape, idx: pl.BlockSpec(shape, lambda i: idx, pipeline_mode=pl.Buffered(1))
    w_spec = resident((None, w, D_MODEL), (layer, 0, 0))
    gate0 = DT_OFFSET // tn
    gate_specs = [pl.BlockSpec((tm, tn), partial(lambda i, blk: (i, blk), blk=gate0 + g * n_half + h))
                  for g in range(3) for h in range(n_half)]
    return pl.pallas_call(
        _merge_kernel,
        grid=(m // tm,),
        in_specs=[pl.BlockSpec((tm, w), row),
                  pl.BlockSpec((tm, w), lambda i: (i, 3)),
                  pl.BlockSpec((tm, w), gfo_row),
                  pl.BlockSpec((tm, w), row),
                  pl.BlockSpec((tm, w), lambda i: (i, 8)),
                  resident((1, w), (0, 0)), w_spec, w_spec, w_spec] + gate_specs,
        out_specs=pl.BlockSpec((tm, D_MODEL), row),
        out_shape=jax.ShapeDtypeStruct((m, D_MODEL), jnp.bfloat16),
        scratch_shapes=[pltpu.VMEM((tm, w), jnp.bfloat16), pltpu.VMEM((tm, w), jnp.bfloat16)],
        compiler_params=pltpu.CompilerParams(dimension_semantics=("arbitrary",),
                                             vmem_limit_bytes=V7X_VMEM_LIMIT_BYTES),
        name="merge_branches",
    )(a2d, p2d, gfo2d, s2d, p2d, ssd_norm_w.reshape(1, w), wa, wf, ws, *([p2d] * (3 * n_half)))


def _outproj_kernel(m_ref, w_ref, x_ref, g_ref, *rest, final_norm):
    y = jnp.dot(m_ref[...], w_ref[...], preferred_element_type=jnp.float32)
    r = x_ref[...] + g_ref[0] * y
    if final_norm:
        nw_ref, o_ref = rest
        o_ref[...] = r * lax.rsqrt(jnp.mean(r * r, axis=-1, keepdims=True) + EPS) * nw_ref[...]
    else:
        rest[0][...] = r


def out_projection(m2d, w_out_bf, layer, x2d, gate, *, rows_per_mod, tm, tn, final_norm_w=None):
    m, d = x2d.shape
    assert m % tm == 0 and d % tn == 0 and rows_per_mod % tm == 0
    per = rows_per_mod // tm
    in_specs = [pl.BlockSpec((tm, d), lambda i, j: (i, 0)),
                pl.BlockSpec((None, d, tn), lambda i, j: (layer, 0, j)),
                pl.BlockSpec((tm, tn), lambda i, j: (i, j)),
                pl.BlockSpec((1, 1, tn), lambda i, j: (i // per, 0, j))]
    args = [m2d, w_out_bf, x2d, gate]
    if final_norm_w is not None:
        assert tn == d
        in_specs.append(pl.BlockSpec((1, d), lambda i, j: (0, 0)))
        args.append(final_norm_w.astype(jnp.float32).reshape(1, d))
    return pl.pallas_call(
        partial(_outproj_kernel, final_norm=final_norm_w is not None),
        grid=(m // tm, d // tn),
        in_specs=in_specs,
        out_specs=pl.BlockSpec((tm, tn), lambda i, j: (i, j)),
        out_shape=jax.ShapeDtypeStruct((m, d), jnp.float32),
        compiler_params=pltpu.CompilerParams(dimension_semantics=("arbitrary", "arbitrary"),
                                             vmem_limit_bytes=V7X_VMEM_LIMIT_BYTES),
        name="out_projection",
    )(*args)


SSD_GW = SSD_HPG * SSD_HEAD_DIM
CONV_PAD = 8
SSD_CHUNK_UNROLL = 2
SSD_CONV_UNROLL = 8
SSD_C_UNROLL = 8
SSD_A_UNROLL = 18


def _split3_bf16(a):
    rnd = lambda v: v.astype(jnp.bfloat16).astype(jnp.float32)
    a1 = rnd(a)
    a2 = rnd(a - a1)
    a3 = rnd((a - a1) - a2)
    return a1, a2, a3


def _head_rows(rows, r0, n):
    return jnp.concatenate([jnp.broadcast_to(rows[r0 + k:r0 + k + 1, :], (SSD_HEAD_DIM, n))
                            for k in range(SSD_HPG)], axis=0)


def _ssd_kernel(xl_ref, bl_ref, cl_ref, xc_ref, bc_ref, cc_ref, dtl_ref, dtc_ref, bias_ref, alog_ref, dsk_ref,
                wx_ref, wb_ref, wc_ref, bx_ref, bb_ref, bcb_ref, cos_ref, sin_ref,
                yl_ref, yc_ref,
                padx, padb, padc, xt_s, b_s, c_s, dtt_s, row_s, dec_s, yt_s, p_s, sp_s, st_s, *, seq, ctx_len):
    f32, bf = jnp.float32, jnp.bfloat16
    t = SSD_CHUNK
    nc_c, nc_l = ctx_len // t, seq // t
    nc = nc_c + nc_l

    lane = lax.broadcasted_iota(jnp.int32, (t, LANES), 1)
    first_half = (lane % 64) < 32

    def conv_tile(pad, w_ref, b_ref, t0, l0):
        acc = jnp.broadcast_to(b_ref[:, l0:l0 + LANES], (t, LANES))
        for j in range(SSD_CONV):
            off = CONV_PAD - SSD_CONV // 2 + j
            acc = acc + pad[pl.ds(t0 + off, t), :] * w_ref[j:j + 1, l0:l0 + LANES]
        return acc * jax.nn.sigmoid(acc)

    def rope(u, t0):
        swapped = jnp.where(first_half, pltpu.roll(u, 96, axis=1), pltpu.roll(u, 32, axis=1))
        return u * cos_ref[pl.ds(t0, t), :] + swapped * sin_ref[pl.ds(t0, t), :]

    def conv_seq(x_ref, bm_ref, cm_ref, n, base, use_rope):
        slabs = ([(padx.at[h], x_ref, h * LANES) for h in range(SSD_GW // LANES)]
                 + [(padb, bm_ref, 0), (padc, cm_ref, 0)])
        for pad, src, l0 in slabs:
            pad[0:CONV_PAD, :] = jnp.zeros((CONV_PAD, LANES), f32)
            pad[CONV_PAD + n:2 * CONV_PAD + n, :] = jnp.zeros((CONV_PAD, LANES), f32)
            pad[CONV_PAD:CONV_PAD + n, :] = src[0, :, l0:l0 + LANES].astype(f32)

        unroll = min(SSD_CONV_UNROLL, n // t)

        def chunks(i, carry):
            for j in range(unroll):
                ci = i * unroll + j
                t0 = pl.multiple_of(ci * t, t)
                r0 = pl.multiple_of(base + ci * t, t)
                for h in range(SSD_GW // LANES):
                    l0 = h * LANES
                    xt_s[base // t + ci, l0:l0 + LANES, :] = conv_tile(padx.at[h], wx_ref, bx_ref, t0, l0).T
                ub = conv_tile(padb, wb_ref, bb_ref, t0, 0)
                uc = conv_tile(padc, wc_ref, bcb_ref, t0, 0)
                if use_rope:
                    ub, uc = rope(ub, t0), rope(uc, t0)
                b_s[pl.ds(r0, t), :] = ub.astype(bf)
                c_s[pl.ds(r0, t), :] = uc.astype(bf)
            return carry

        lax.fori_loop(0, n // t // unroll, chunks, 0)

    conv_seq(xc_ref, bc_ref, cc_ref, ctx_len, 0, False)
    conv_seq(xl_ref, bl_ref, cl_ref, seq, ctx_len, True)

    i0 = lax.broadcasted_iota(jnp.int32, (t, t), 0)
    i1 = lax.broadcasted_iota(jnp.int32, (t, t), 1)
    tri_f = (i0 <= i1).astype(bf)
    tri_r = (i0 >= i1).astype(bf)
    nr = 2 * SSD_HPG
    fwd_row = (lax.broadcasted_iota(jnp.int32, (nc * nr, 1), 0) % nr) < SSD_HPG
    g4 = pl.program_id(1) * SSD_HPG
    raw = []
    for c in range(nc):
        src, c0 = (dtc_ref, c) if c < nc_c else (dtl_ref, c - nc_c)
        dtt_s[c] = src[0, c0 * t:(c0 + 1) * t, :].T
        raw += [dtt_s[c, pl.ds(g4, SSD_HPG), :], dtt_s[c, pl.ds(SSD_HEADS + g4, SSD_HPG), :]]
    x = jnp.concatenate(raw, axis=0) + jnp.tile(bias_ref[0], (nc, 1))
    dt = jnp.maximum(x, 0.0) + jnp.log1p(jnp.exp(-jnp.abs(x)))
    pieces = _split3_bf16(dt * jnp.tile(-jnp.exp(alog_ref[0]), (nc, 1)))
    pf = [jnp.dot(p.astype(bf), tri_f, preferred_element_type=f32) for p in pieces]
    pr = [jnp.dot(p.astype(bf), tri_r, preferred_element_type=f32) for p in pieces]
    cs = jnp.where(fwd_row, (pf[0] + pf[1]) + pf[2], (pr[0] + pr[1]) + pr[2])
    tot = jnp.where(fwd_row, cs[:, t - 1:t], cs[:, 0:1])
    e_in = jnp.exp(cs)
    e_out = dt * jnp.exp(tot - cs)
    u = cs - jnp.log(dt)
    decb = jnp.broadcast_to(jnp.exp(tot), (nc * nr, LANES))
    for c in range(nc):
        sl = slice(c * nr, (c + 1) * nr)
        row_s[c] = jnp.concatenate([cs[sl], dt[sl], e_in[sl], e_out[sl], u[sl]], axis=0)
        dec_s[c] = decb[sl]

    lower = i1 <= i0
    lower_t = i0 <= i1
    diag = i1 == i0

    def chunks_a(i, carry):
        ids = [i * SSD_A_UNROLL + j for j in range(SSD_A_UNROLL)]
        r0s = [pl.multiple_of(c * t, t) for c in ids]
        cms = [c_s[pl.ds(r0, t), :] for r0 in r0s]
        bms = [b_s[pl.ds(r0, t), :] for r0 in r0s]
        gs = [lax.dot_general(cm, bm, _NT, preferred_element_type=f32) for cm, bm in zip(cms, bms)]
        for c, r0, bm, g in zip(ids, r0s, bms, gs):
            rows = row_s[c]
            xt = xt_s[c]
            xt_bf = xt.astype(bf)
            parts = []
            for k in range(SSD_HPG):
                kr = SSD_HPG + k
                cs_l = jnp.where(lower_t, rows[k:k + 1, :], rows[kr:kr + 1, :]).T
                w = jnp.exp(cs_l - jnp.where(lower, rows[32 + k:33 + k, :], rows[32 + kr:33 + kr, :]))
                mk = (g * (w + jnp.where(diag, rows[8 + kr:9 + kr, :], 0.0))).astype(bf)
                xk = xt_bf[k * SSD_HEAD_DIM:(k + 1) * SSD_HEAD_DIM, :]
                parts.append(lax.dot_general(xk, mk, _NT, preferred_element_type=f32))
            yt_s[c] = dsk_ref[0] * xt + jnp.concatenate(parts, axis=0)
            for d in range(2):
                wt = (xt * _head_rows(rows, 24 + SSD_HPG * d, t)).astype(bf)
                p_s[d, c] = jnp.dot(wt, bm, preferred_element_type=f32)
        return carry

    lax.fori_loop(0, nc // SSD_A_UNROLL, chunks_a, 0)

    st_s[...] = jnp.zeros(st_s.shape, f32)

    def state_step(c, d):
        st = st_s[d]
        sp_s[d, c] = st.astype(bf)
        st_s[d] = st * _head_rows(dec_s[c], SSD_HPG * d, SSD_STATE) + p_s[d, c]

    def ctx_step(i, carry):
        state_step(i, 0)
        state_step(nc_c - 1 - i, 1)
        return carry

    def lat_step(i, carry):
        state_step(nc_c + i, 0)
        state_step(nc - 1 - i, 1)
        return carry

    lax.fori_loop(0, nc_c, ctx_step, 0)
    lax.fori_loop(0, nc_l, lat_step, 0)

    def chunks_c(first, out_ref, o_first, count):
        ids = [first + u for u in range(count)]
        cms = [c_s[pl.ds(pl.multiple_of(c * t, t), t), :] for c in ids]
        offs = [[lax.dot_general(sp_s[d, c], cm, _NT, preferred_element_type=f32) for d in range(2)]
                for c, cm in zip(ids, cms)]
        for u, (c, off) in enumerate(zip(ids, offs)):
            rows = row_s[c]
            yt = yt_s[c]
            for d in range(2):
                yt = yt + off[d] * _head_rows(rows, 16 + SSD_HPG * d, t)
            for l0 in range(0, SSD_GW, LANES):
                out_ref[0, pl.ds(pl.multiple_of((o_first + u) * t, t), t), l0:l0 + LANES] = yt[l0:l0 + LANES, :].T

    def ctx_out(i, carry):
        chunks_c(i * SSD_CHUNK_UNROLL, yc_ref, i * SSD_CHUNK_UNROLL, SSD_CHUNK_UNROLL)
        return carry

    def lat_out(i, carry):
        chunks_c(nc_c + i * SSD_C_UNROLL, yl_ref, i * SSD_C_UNROLL, SSD_C_UNROLL)
        return carry

    lax.fori_loop(0, nc_c // SSD_CHUNK_UNROLL, ctx_out, 0)
    lax.fori_loop(0, nc_l // SSD_C_UNROLL, lat_out, 0)


def _rope_tables(seq):
    quarter = SSD_STATE // 4
    inv = ROPE_BASE ** (-np.arange(quarter, dtype=np.float64) / quarter)
    pos = np.arange(seq)
    ang_r = (pos // GRID_W)[:, None] * inv
    ang_c = (pos % GRID_W)[:, None] * inv
    cos = np.concatenate([np.cos(ang_r)] * 2 + [np.cos(ang_c)] * 2, axis=1)
    sin = np.concatenate([-np.sin(ang_r), np.sin(ang_r), -np.sin(ang_c), np.sin(ang_c)], axis=1)
    return jnp.asarray(cos, jnp.float32), jnp.asarray(sin, jnp.float32)


def ssd_mixer(p_lat, p_ctx, dt_lat, dt_ctx, conv_w, conv_b, dt_bias, a_log, d_skip, ctx_cols=FULL_COLS):
    bsz, seq, _ = p_lat.shape
    ctx_len = p_ctx.shape[1]
    t, g, hpg = SSD_CHUNK, SSD_GROUPS, SSD_HPG
    assert seq % t == 0 and ctx_len % t == 0
    nc = (seq + ctx_len) // t
    per_row = lambda v: jnp.broadcast_to(
        v.astype(jnp.float32).reshape(2, g, hpg).transpose(1, 0, 2).reshape(g, 2 * hpg, 1), (g, 2 * hpg, LANES))
    dsk = jnp.broadcast_to(jnp.repeat(d_skip.astype(jnp.float32).reshape(g, hpg), SSD_HEAD_DIM, axis=1)[:, :, None],
                           (g, SSD_GW, LANES))
    cos, sin = _rope_tables(seq)
    x0 = sum(MAIN_SPLITS[:6]) // SSD_GW
    b0 = (sum(MAIN_SPLITS[:6]) + SSD_WIDTH) // SSD_STATE
    c0 = b0 + g
    cw = conv_w.astype(jnp.float32)
    cb = conv_b.astype(jnp.float32).reshape(1, SSD_CONV_CH)
    wb0 = SSD_WIDTH // SSD_STATE
    ltot = seq + ctx_len
    f32, bf = jnp.float32, jnp.bfloat16
    y_lat, y_ctx = pl.pallas_call(
        partial(_ssd_kernel, seq=seq, ctx_len=ctx_len),
        grid=(bsz, g),
        in_specs=[pl.BlockSpec((1, seq, SSD_GW), lambda b, i: (b, 0, x0 + i)),
                  pl.BlockSpec((1, seq, SSD_STATE), lambda b, i: (b, 0, b0 + i)),
                  pl.BlockSpec((1, seq, SSD_STATE), lambda b, i: (b, 0, c0 + i)),
                  pl.BlockSpec((1, ctx_len, SSD_GW), lambda b, i: (b, 0, ctx_cols.xbc // SSD_GW + i)),
                  pl.BlockSpec((1, ctx_len, SSD_STATE), lambda b, i: (b, 0, (ctx_cols.xbc + SSD_WIDTH) // SSD_STATE + i)),
                  pl.BlockSpec((1, ctx_len, SSD_STATE), lambda b, i: (b, 0, (ctx_cols.xbc + SSD_WIDTH) // SSD_STATE + g + i)),
                  pl.BlockSpec((1, seq, DT_PAD), lambda b, i: (b, 0, 0)),
                  pl.BlockSpec((1, ctx_len, DT_PAD), lambda b, i: (b, 0, 0)),
                  pl.BlockSpec((1, 2 * hpg, LANES), lambda b, i: (i, 0, 0)),
                  pl.BlockSpec((1, 2 * hpg, LANES), lambda b, i: (i, 0, 0)),
                  pl.BlockSpec((1, SSD_GW, LANES), lambda b, i: (i, 0, 0)),
                  pl.BlockSpec((SSD_CONV, SSD_GW), lambda b, i: (0, i)),
                  pl.BlockSpec((SSD_CONV, SSD_STATE), lambda b, i: (0, wb0 + i)),
                  pl.BlockSpec((SSD_CONV, SSD_STATE), lambda b, i: (0, wb0 + g + i)),
                  pl.BlockSpec((1, SSD_GW), lambda b, i: (0, i)),
                  pl.BlockSpec((1, SSD_STATE), lambda b, i: (0, wb0 + i)),
                  pl.BlockSpec((1, SSD_STATE), lambda b, i: (0, wb0 + g + i)),
                  pl.BlockSpec((seq, SSD_STATE), lambda b, i: (0, 0)),
                  pl.BlockSpec((seq, SSD_STATE), lambda b, i: (0, 0))],
        out_specs=[pl.BlockSpec((1, seq, SSD_GW), lambda b, i: (b, 0, i)),
                   pl.BlockSpec((1, ctx_len, SSD_GW), lambda b, i: (b, 0, i))],
        out_shape=[jax.ShapeDtypeStruct((bsz, seq, SSD_WIDTH), f32),
                   jax.ShapeDtypeStruct((bsz, ctx_len, SSD_WIDTH), f32)],
        scratch_shapes=[pltpu.VMEM((SSD_GW // LANES, seq + 2 * CONV_PAD, LANES), f32),
                        pltpu.VMEM((seq + 2 * CONV_PAD, SSD_STATE), f32),
                        pltpu.VMEM((seq + 2 * CONV_PAD, SSD_STATE), f32),
                        pltpu.VMEM((nc, SSD_GW, t), f32),
                        pltpu.VMEM((ltot, SSD_STATE), bf),
                        pltpu.VMEM((ltot, SSD_STATE), bf),
                        pltpu.VMEM((nc, DT_PAD, t), f32),
                        pltpu.VMEM((nc, 5 * 2 * hpg, t), f32),
                        pltpu.VMEM((nc, 2 * hpg, LANES), f32),
                        pltpu.VMEM((nc, SSD_GW, t), f32),
                        pltpu.VMEM((2, nc, SSD_GW, SSD_STATE), f32),
                        pltpu.VMEM((2, nc, SSD_GW, SSD_STATE), bf),
                        pltpu.VMEM((2, SSD_GW, SSD_STATE), f32)],
        compiler_params=pltpu.CompilerParams(dimension_semantics=("arbitrary", "arbitrary"),
                                             vmem_limit_bytes=V7X_VMEM_LIMIT_BYTES),
        name="ssd_mixer",
    )(p_lat, p_lat, p_lat, p_ctx, p_ctx, p_ctx, dt_lat, dt_ctx, per_row(dt_bias), per_row(a_log), dsk,
      cw, cw, cw, cb, cb, cb, cos, sin)
    return y_ctx, y_lat


INPROJ_TM = 1024
INPROJ_TN = 2560
ADA_TN = 1024
BRANCH_TM = 512
CTX_OUT_TILE = 1024
ADA_ROWS = 16


def _layer(xc, xl, c, c_ctx, w_ada_all, layer, b_ada, norm_w, w_main_all, w_dt_all, rpb, four_wb, conv_w, conv_b, dt_bias,
           a_log, d_skip, ssd_norm_w, wb_nab, wb_fourb, wb_ssdb, w_outb, update_ctx, final_norm_w):
    bsz, seq, d = xl.shape
    ctx_len = xc.shape[1]
    bf = jnp.bfloat16
    assert bsz < ADA_ROWS

    cc = jnp.concatenate([c, c_ctx[None], jnp.zeros((ADA_ROWS - bsz - 1, d), c.dtype)], axis=0)
    mod = matmul(jax.nn.silu(cc), w_ada_all, layer, tm=ADA_ROWS, tn=ADA_TN) + b_ada
    sh, sc, gt = jnp.split(mod, 3, axis=-1)
    sh_l, sc_l, g_l = sh[:bsz], sc[:bsz], gt[:bsz]
    sh_c, sc_c, g_c = sh[bsz:bsz + 1], sc[bsz:bsz + 1], gt[bsz]

    xl2d = xl.reshape(bsz * seq, d)
    xc2d = xc.reshape(bsz * ctx_len, d)
    pl2d, dt_l = in_projection(xl2d, norm_w, sc_l[:, None], sh_l[:, None], w_main_all, layer, w_dt_all, layer,
                               rows_per_mod=seq, tm=INPROJ_TM, tn=INPROJ_TN, out_dtype=bf)
    if update_ctx:
        ctx_cols, w_ctx, ctx_layer, ctx_tn = FULL_COLS, w_main_all, layer, INPROJ_TN
    else:
        ctx_cols, ctx_layer = KV_XBC_COLS, 0
        w_ctx = jnp.concatenate([w_main_all[layer, FULL_COLS.k:FULL_COLS.v + NA_WIDTH],
                                 w_main_all[layer, FULL_COLS.xbc:FULL_COLS.xbc + SSD_CONV_CH]], axis=0)[None]
        ctx_tn = w_ctx.shape[1] // 2
    pc2d, dt_c = in_projection(xc2d, norm_w, sc_c[:, None], sh_c[:, None], w_ctx, ctx_layer, w_dt_all, layer,
                               rows_per_mod=bsz * ctx_len, tm=INPROJ_TM, tn=ctx_tn, out_dtype=bf)
    p_l = pl2d.reshape(bsz, seq, MAIN_WIDTH)
    p_c = pc2d.reshape(bsz, ctx_len, w_ctx.shape[1])
    dt_l = dt_l.reshape(bsz, seq, DT_PAD)
    dt_c = dt_c.reshape(bsz, ctx_len, DT_PAD)

    a_l, a_c = na_attention(p_l, p_c, na_rpb_rows(rpb), with_ctx=update_ctx, out_dtype=bf, ctx_cols=ctx_cols)
    gfo_l, nt2_l = fourier_branch(p_l, four_wb, layer, tm=BRANCH_TM)
    s_c, s_l = ssd_mixer(p_l, p_c, dt_l, dt_c, conv_w, conv_b, dt_bias, a_log, d_skip, ctx_cols=ctx_cols)
    m_l = merge_branches(pl2d, a_l.reshape(bsz * seq, NA_WIDTH), gfo_l.reshape(bsz * seq, FOUR_WIDTH), nt2_l,
                         s_l.reshape(bsz * seq, SSD_WIDTH), ssd_norm_w, wb_nab, wb_fourb, wb_ssdb, layer, tm=BRANCH_TM)
    xl_new = out_projection(m_l, w_outb, layer, xl2d, g_l[:, None], rows_per_mod=seq, tm=BRANCH_TM, tn=d,
                            final_norm_w=final_norm_w)
    xl_new = xl_new.reshape(bsz, seq, d)
    if update_ctx:
        gfo_c, nt2_c = fourier_branch(p_c, four_wb, layer, tm=ctx_len // 2)
        assert nt2_c == 1
        m_c = merge_branches(pc2d, a_c.reshape(bsz * ctx_len, NA_WIDTH), gfo_c.reshape(bsz * ctx_len, FOUR_WIDTH), None,
                             s_c.reshape(bsz * ctx_len, SSD_WIDTH), ssd_norm_w, wb_nab, wb_fourb, wb_ssdb, layer, tm=BRANCH_TM)
        xc = out_projection(m_c, w_outb, layer, xc2d, g_c[None, None], rows_per_mod=bsz * ctx_len, tm=CTX_OUT_TILE, tn=CTX_OUT_TILE)
        xc = xc.reshape(bsz, ctx_len, d)
    return xc, xl_new


def kernel(x, c, ctx, c_ctx, w_ada, b_ada, norm_w, w_in, na_rpb, four_w, ssd_conv_w, ssd_conv_b, ssd_dt_bias,
           ssd_a_log, ssd_d, ssd_norm_w, wb_na, wb_four, wb_ssd, w_out, final_norm_w):
    xc, xl = ctx, x
    w_main_all, w_dt_all = repack_w_in(w_in)
    four_wb, wb_nab, wb_fourb, wb_ssdb, w_outb = (t.astype(jnp.bfloat16) for t in (four_w, wb_na, wb_four, wb_ssd, w_out))
    for l in range(DEPTH):
        xc, xl = _layer(xc, xl, c, c_ctx, w_ada, l, b_ada[l], norm_w[l], w_main_all, w_dt_all, na_rpb[l], four_wb,
                        ssd_conv_w[l], ssd_conv_b[l], ssd_dt_bias[l], ssd_a_log[l], ssd_d[l], ssd_norm_w[l],
                        wb_nab, wb_fourb, wb_ssdb, w_outb, update_ctx=(l < DEPTH - 1),
                        final_norm_w=final_norm_w if l == DEPTH - 1 else None)
    return xl
```

```python
import math
from functools import partial
from typing import NamedTuple

import jax
import jax.numpy as jnp
import numpy as np
from jax import lax
from jax.experimental import pallas as pl
from jax.experimental.pallas import tpu as pltpu

D_MODEL = 2048
DEPTH = 2
GRID_W = 64
EPS = 1e-6
NA_HEADS = 8
NA_HEAD_DIM = 128
NA_WIDTH = NA_HEADS * NA_HEAD_DIM
NA_WIN_R = 8
NA_WIN_C = 16
LOG2E = math.log2(math.e)
FOUR_GROUPS = 4
FOUR_GROUP_DIM = 256
FOUR_WIDTH = FOUR_GROUPS * FOUR_GROUP_DIM
SSD_HEADS = 16
SSD_HEAD_DIM = 64
SSD_WIDTH = SSD_HEADS * SSD_HEAD_DIM
SSD_GROUPS = 4
SSD_HPG = SSD_HEADS // SSD_GROUPS
SSD_STATE = 128
SSD_CONV = 7
SSD_CHUNK = 128
SSD_CONV_CH = SSD_WIDTH + 2 * SSD_GROUPS * SSD_STATE
ROPE_BASE = 10000.0
DT_WIDTH = 2 * SSD_HEADS
MAIN_SPLITS = (NA_WIDTH, NA_WIDTH, NA_WIDTH, NA_WIDTH, FOUR_WIDTH, FOUR_WIDTH, SSD_CONV_CH, SSD_WIDTH,
               D_MODEL, D_MODEL, D_MODEL)
MAIN_WIDTH = sum(MAIN_SPLITS)
DT_OFFSET = sum(MAIN_SPLITS[:8])
DT_PAD = 128


class ProjCols(NamedTuple):
    k: int
    v: int
    xbc: int


FULL_COLS = ProjCols(k=NA_WIDTH, v=2 * NA_WIDTH, xbc=sum(MAIN_SPLITS[:6]))
KV_XBC_COLS = ProjCols(k=0, v=NA_WIDTH, xbc=2 * NA_WIDTH)

LANES = 128
V7X_VMEM_LIMIT_BYTES = 56 * 1024 * 1024


def _mm_kernel(a_ref, b_ref, o_ref):
    a = a_ref[...].astype(jnp.bfloat16)
    b = b_ref[...].astype(jnp.bfloat16)
    o_ref[...] = jnp.dot(a, b, preferred_element_type=jnp.float32).astype(o_ref.dtype)


def matmul(a, b, layer, *, tm, tn, out_dtype=jnp.float32):
    m, k = a.shape
    _, _, n = b.shape
    assert m % tm == 0 and n % tn == 0, (a.shape, b.shape, tm, tn)
    return pl.pallas_call(
        _mm_kernel,
        grid=(m // tm, n // tn),
        in_specs=[pl.BlockSpec((tm, k), lambda i, j: (i, 0)),
                  pl.BlockSpec((None, k, tn), lambda i, j: (layer, 0, j))],
        out_specs=pl.BlockSpec((tm, tn), lambda i, j: (i, j)),
        out_shape=jax.ShapeDtypeStruct((m, n), out_dtype),
        compiler_params=pltpu.CompilerParams(dimension_semantics=("arbitrary", "arbitrary"),
                                             vmem_limit_bytes=V7X_VMEM_LIMIT_BYTES),
        name="matmul",
    )(a, b)


REPACK_TN = 1024


def _repack_kernel(a_ref, nxt_ref, o_ref, odt_ref):
    j = pl.program_id(1)
    first_below = DT_OFFSET // REPACK_TN

    @pl.when(j < first_below)
    def _():
        o_ref[...] = a_ref[...].astype(o_ref.dtype)

    @pl.when(j >= first_below)
    def _():
        o_ref[...] = jnp.concatenate([a_ref[DT_WIDTH:, :], nxt_ref[...]], axis=0).astype(o_ref.dtype)

    @pl.when(j == first_below)
    def _():
        odt_ref[...] = a_ref[:DT_PAD, :].astype(odt_ref.dtype)


def repack_w_in(w_in):
    depth, d, _ = w_in.shape
    tn = REPACK_TN
    w_t = jnp.swapaxes(w_in, 1, 2)
    return pl.pallas_call(
        _repack_kernel,
        grid=(depth, MAIN_WIDTH // tn),
        in_specs=[pl.BlockSpec((None, tn, d), lambda l, j: (l, j, 0)),
                  pl.BlockSpec((None, DT_WIDTH, d), lambda l, j: (l, (j + 1) * (tn // DT_WIDTH), 0))],
        out_specs=[pl.BlockSpec((None, tn, d), lambda l, j: (l, j, 0)),
                   pl.BlockSpec((None, DT_PAD, d), lambda l, j: (l, 0, 0))],
        out_shape=[jax.ShapeDtypeStruct((depth, MAIN_WIDTH, d), jnp.bfloat16),
                   jax.ShapeDtypeStruct((depth, DT_PAD, d), jnp.bfloat16)],
        compiler_params=pltpu.CompilerParams(dimension_semantics=("arbitrary", "arbitrary"),
                                             vmem_limit_bytes=V7X_VMEM_LIMIT_BYTES),
        name="repack_w_in",
    )(w_t, w_t)


def _inproj_kernel(x_ref, nw_ref, sc_ref, sh_ref, w_ref, wdt_ref, o_ref, dt_ref, h_ref):
    @pl.when(pl.program_id(1) == 0)
    def _():
        x = x_ref[...]
        y = x * lax.rsqrt(jnp.mean(x * x, axis=-1, keepdims=True) + EPS) * nw_ref[...]
        h = (y * (1.0 + sc_ref[0]) + sh_ref[0]).astype(jnp.bfloat16)
        h_ref[...] = h
        dt_ref[...] = lax.dot_general(h, wdt_ref[...], _NT, preferred_element_type=jnp.float32)

    o_ref[...] = lax.dot_general(h_ref[...], w_ref[...], _NT, preferred_element_type=jnp.float32).astype(o_ref.dtype)


def in_projection(x2d, norm_w, scale, shift, w_main_t, layer, w_dt_t, dt_layer, *, rows_per_mod, tm, tn, out_dtype):
    m, d = x2d.shape
    n = w_main_t.shape[1]
    assert m % tm == 0 and n % tn == 0 and rows_per_mod % tm == 0
    per = rows_per_mod // tm
    return pl.pallas_call(
        _inproj_kernel,
        grid=(m // tm, n // tn),
        in_specs=[pl.BlockSpec((tm, d), lambda i, j: (i, 0)),
                  pl.BlockSpec((1, d), lambda i, j: (0, 0)),
                  pl.BlockSpec((1, 1, d), lambda i, j: (i // per, 0, 0)),
                  pl.BlockSpec((1, 1, d), lambda i, j: (i // per, 0, 0)),
                  pl.BlockSpec((None, tn, d), lambda i, j: (layer, j, 0)),
                  pl.BlockSpec((None, DT_PAD, d), lambda i, j: (dt_layer, 0, 0))],
        out_specs=[pl.BlockSpec((tm, tn), lambda i, j: (i, j)),
                   pl.BlockSpec((tm, DT_PAD), lambda i, j: (i, 0))],
        out_shape=[jax.ShapeDtypeStruct((m, n), out_dtype),
                   jax.ShapeDtypeStruct((m, DT_PAD), jnp.float32)],
        scratch_shapes=[pltpu.VMEM((tm, d), jnp.bfloat16)],
        compiler_params=pltpu.CompilerParams(dimension_semantics=("arbitrary", "arbitrary"),
                                             vmem_limit_bytes=V7X_VMEM_LIMIT_BYTES),
        name="in_projection",
    )(x2d, norm_w.reshape(1, d), scale, shift, w_main_t, w_dt_t)


GRID_ROWS = 32
NA_WIN_TOKENS = NA_WIN_R * GRID_W
_NT = (((1,), (1,)), ((), ()))
NA_HEADS_PER_STEP = 2
NA_ROW_UNROLL = 32


def _na_kernel(*refs, with_ctx):
    if with_ctx:
        q_ref, k_ref, v_ref, kc_ref, vc_ref, rpb_ref, qc_ref, o_ref, oc_ref, bias_s = refs
    else:
        q_ref, k_ref, v_ref, kc_ref, vc_ref, rpb_ref, o_ref, bias_s = refs
    bf = jnp.bfloat16

    @pl.when(pl.program_id(1) == 0)
    def _():
        qi = lax.broadcasted_iota(jnp.int32, (GRID_W, LANES), 0)
        li = lax.broadcasted_iota(jnp.int32, (GRID_W, LANES), 1)
        qstart = jnp.clip(qi - NA_WIN_C // 2, 0, GRID_W - NA_WIN_C)
        in_window = (li >= qstart) & (li < qstart + NA_WIN_C)
        for hh in range(NA_HEADS_PER_STEP):
            lo, hi = [], []
            for dr in range(2 * NA_WIN_R - 1):
                row = jnp.broadcast_to(rpb_ref[hh, dr:dr + 1, :], (GRID_W, LANES))
                blk = pltpu.roll(row, LANES - (NA_WIN_C - 1), axis=1, stride=1, stride_axis=0)
                blk = jnp.where(in_window, blk * LOG2E, -1e30)
                lo.append(blk)
                hi.append(pltpu.roll(blk, GRID_W, axis=1))
            for dl in range(NA_WIN_R):
                for pr in range(NA_WIN_R // 2):
                    d0 = 2 * pr - dl + NA_WIN_R - 1
                    bias_s[hh, dl, :, pr * LANES:(pr + 1) * LANES] = jnp.where(li < GRID_W, lo[d0], hi[d0 + 1])

    scale = NA_HEAD_DIM ** -0.5
    scale2 = scale * LOG2E
    nq = NA_ROW_UNROLL * GRID_W

    def lane_tiles(x, op):
        acc = x[:, 0:LANES]
        for l0 in range(LANES, x.shape[1], LANES):
            acc = op(acc, x[:, l0:l0 + LANES])
        return acc

    for hh in range(NA_HEADS_PER_STEP):
        lanes = slice(hh * NA_HEAD_DIM, (hh + 1) * NA_HEAD_DIM)
        kc = kc_ref[0, :, lanes].astype(bf)
        vc = vc_ref[0, :, lanes].astype(bf)

        def rows(rb, carry, hh=hh, lanes=lanes, kc=kc, vc=vc):
            qb0 = pl.multiple_of(rb * nq, nq)
            q_all = q_ref[0, pl.ds(qb0, nq), lanes].astype(bf)
            s_c_all = lax.dot_general(q_all, kc, _NT, preferred_element_type=jnp.float32) * scale2
            k0s, scores, probs = [], [], []
            for i in range(NA_ROW_UNROLL):
                r = rb * NA_ROW_UNROLL + i
                start = jnp.clip(r - NA_WIN_R // 2, 0, GRID_ROWS - NA_WIN_R)
                k0 = pl.multiple_of(start * GRID_W, GRID_W)
                q = q_all[i * GRID_W:(i + 1) * GRID_W]
                kw = k_ref[0, pl.ds(k0, NA_WIN_TOKENS), lanes].astype(bf)
                s_w = (lax.dot_general(q, kw, _NT, preferred_element_type=jnp.float32) * scale2
                       + bias_s[hh, r - start])
                k0s.append(k0)
                scores.append((s_w, s_c_all[i * GRID_W:(i + 1) * GRID_W]))
            for s_w, s_c in scores:
                m = jnp.max(jnp.maximum(lane_tiles(s_w, jnp.maximum), lane_tiles(s_c, jnp.maximum)),
                            axis=-1, keepdims=True)
                e_w = jnp.exp2(s_w - m)
                e_c = jnp.exp2(s_c - m)
                denom = jnp.sum(lane_tiles(e_w, jnp.add) + lane_tiles(e_c, jnp.add), axis=-1, keepdims=True)
                probs.append((e_w.astype(bf), e_c.astype(bf), denom))
            o_c_all = jnp.dot(jnp.concatenate([p[1] for p in probs], axis=0), vc, preferred_element_type=jnp.float32)
            for i, (k0, (e_w, _, denom)) in enumerate(zip(k0s, probs)):
                vw = v_ref[0, pl.ds(k0, NA_WIN_TOKENS), lanes].astype(bf)
                o = jnp.dot(e_w, vw, preferred_element_type=jnp.float32) + o_c_all[i * GRID_W:(i + 1) * GRID_W]
                o_ref[0, pl.ds(qb0 + i * GRID_W, GRID_W), lanes] = (o / denom).astype(o_ref.dtype)
            return carry

        lax.fori_loop(0, GRID_ROWS // NA_ROW_UNROLL, rows, 0)

        if with_ctx:
            s = lax.dot_general(qc_ref[0, :, lanes].astype(bf), kc, _NT, preferred_element_type=jnp.float32) * scale
            e = jnp.exp(s - jnp.max(s, axis=-1, keepdims=True))
            o = jnp.dot(e.astype(bf), vc, preferred_element_type=jnp.float32)
            oc_ref[0, :, lanes] = (o / jnp.sum(e, axis=-1, keepdims=True)).astype(oc_ref.dtype)


def na_rpb_rows(rpb):
    h, nr, nc = rpb.shape
    return jnp.pad(rpb.astype(jnp.float32), ((0, 0), (0, 2 * NA_WIN_R - nr), (0, LANES - nc)))


def na_attention(p_lat, p_ctx, rpb_rows, *, with_ctx, out_dtype, ctx_cols=FULL_COLS):
    bsz, seq, _ = p_lat.shape
    ctx_len = p_ctx.shape[1]
    assert seq == GRID_ROWS * GRID_W
    hs = NA_HEADS_PER_STEP
    d, nblk = hs * NA_HEAD_DIM, NA_HEADS // hs
    in_specs = [pl.BlockSpec((1, seq, d), lambda i, b: (b, 0, i)),
                pl.BlockSpec((1, seq, d), lambda i, b: (b, 0, nblk + i)),
                pl.BlockSpec((1, seq, d), lambda i, b: (b, 0, 2 * nblk + i)),
                pl.BlockSpec((1, ctx_len, d), lambda i, b: (b, 0, ctx_cols.k // d + i)),
                pl.BlockSpec((1, ctx_len, d), lambda i, b: (b, 0, ctx_cols.v // d + i)),
                pl.BlockSpec((hs, 2 * NA_WIN_R, LANES), lambda i, b: (i, 0, 0))]
    args = [p_lat, p_lat, p_lat, p_ctx, p_ctx, rpb_rows]
    out_specs = [pl.BlockSpec((1, seq, d), lambda i, b: (b, 0, i))]
    out_shape = [jax.ShapeDtypeStruct((bsz, seq, NA_WIDTH), out_dtype)]
    if with_ctx:
        in_specs.append(pl.BlockSpec((1, ctx_len, d), lambda i, b: (b, 0, i)))
        args.append(p_ctx)
        out_specs.append(pl.BlockSpec((1, ctx_len, d), lambda i, b: (b, 0, i)))
        out_shape.append(jax.ShapeDtypeStruct((bsz, ctx_len, NA_WIDTH), out_dtype))
    outs = pl.pallas_call(
        partial(_na_kernel, with_ctx=with_ctx),
        grid=(nblk, bsz),
        in_specs=in_specs, out_specs=out_specs, out_shape=out_shape,
        scratch_shapes=[pltpu.VMEM((hs, NA_WIN_R, GRID_W, NA_WIN_TOKENS), jnp.float32)],
        compiler_params=pltpu.CompilerParams(dimension_semantics=("arbitrary", "arbitrary"),
                                             vmem_limit_bytes=V7X_VMEM_LIMIT_BYTES),
        name="na_attention",
    )(*args)
    return (outs[0], outs[1]) if with_ctx else (outs[0], None)


def _dft_cos_sin(n):
    jk = np.outer(np.arange(n), np.arange(n)) % n
    ang = 2.0 * np.pi * jk / n
    return np.cos(ang) / np.sqrt(n), np.sin(ang) / np.sqrt(n)


FOUR_EXT = 8


def _fourier_kernel(u_ref, cc_ref, sc_ref, a_ref, flip_ref, fw_ref, zlo_ref, zhi_ref, o_ref, *, tm):
    bf = jnp.bfloat16
    gd, ext = FOUR_GROUP_DIM, tm + FOUR_EXT
    pos = jnp.dot(a_ref[...], u_ref[0].astype(bf), preferred_element_type=jnp.float32).astype(bf)
    even, odd = [], []
    for g in range(FOUR_GROUPS):
        cols = slice(g * gd, (g + 1) * gd)
        even.append(jnp.dot(pos[:ext, cols], cc_ref[...], preferred_element_type=jnp.float32))
        odd.append(jnp.dot(pos[ext:, cols], sc_ref[...], preferred_element_type=jnp.float32))
    even = jnp.concatenate(even, axis=1)
    odd = jnp.concatenate(odd, axis=1)
    y_lo = (even[:tm] - odd[:tm]).astype(bf)
    y_hi = jnp.dot(flip_ref[...], (even + odd).astype(bf), preferred_element_type=jnp.float32).astype(bf)
    for half, (y, z_ref) in enumerate(((y_lo, zlo_ref), (y_hi, zhi_ref))):
        f = jnp.dot(y, fw_ref[...], preferred_element_type=jnp.float32)
        zf = z_ref[0].astype(jnp.float32)
        o_ref[0, half] = (f * (zf * jax.nn.sigmoid(zf))).astype(o_ref.dtype)


def fourier_branch(p, four_w_bf, layer, *, tm):
    bsz, seq, _ = p.shape
    half = seq // 2
    assert half % tm == 0
    nt2, ext = half // tm, tm + FOUR_EXT
    cc, sc = _dft_cos_sin(FOUR_GROUP_DIM)
    cl, sl = _dft_cos_sin(seq)
    a = np.stack([np.concatenate([cl[i * tm:i * tm + ext], sl[i * tm:i * tm + ext]], axis=0) for i in range(nt2)])
    flip = np.zeros((tm, ext))
    flip[np.arange(tm), tm - np.arange(tm)] = 1.0
    fw, gd = FOUR_WIDTH, FOUR_GROUP_DIM
    bf = jnp.bfloat16
    out = pl.pallas_call(
        partial(_fourier_kernel, tm=tm),
        grid=(bsz, nt2),
        in_specs=[pl.BlockSpec((1, seq, fw), lambda b, i: (b, 0, 4)),
                  pl.BlockSpec((gd, gd), lambda b, i: (0, 0)),
                  pl.BlockSpec((gd, gd), lambda b, i: (0, 0)),
                  pl.BlockSpec((None, 2 * ext, seq), lambda b, i: (i, 0, 0)),
                  pl.BlockSpec((tm, ext), lambda b, i: (0, 0)),
                  pl.BlockSpec((None, fw, fw), lambda b, i: (layer, 0, 0)),
                  pl.BlockSpec((1, tm, fw), lambda b, i: (b, i, 5)),
                  pl.BlockSpec((1, tm, fw), lambda b, i: (b, 2 * nt2 - 1 - i, 5))],
        out_specs=pl.BlockSpec((1, 2, tm, fw), lambda b, i: (b, 0, i, 0)),
        out_shape=jax.ShapeDtypeStruct((bsz, 2, half, fw), bf),
        compiler_params=pltpu.CompilerParams(dimension_semantics=("arbitrary", "arbitrary"),
                                             vmem_limit_bytes=V7X_VMEM_LIMIT_BYTES),
        name="fourier_branch",
    )(p, jnp.asarray(cc, bf), jnp.asarray(sc, bf), jnp.asarray(a, bf), jnp.asarray(flip, bf), four_w_bf, p, p)
    return out.reshape(bsz, seq, fw), nt2


def fourier_row_block(t, nt2):
    return jnp.where(t < nt2, t, 3 * nt2 - 1 - t)


MERGE_TN = 1024


def _merge_kernel(a_ref, zna_ref, gfo_ref, s_ref, zs_ref, nw_ref, wa_ref, wf_ref, ws_ref, *rest, fused, final_norm):
    n_half = D_MODEL // MERGE_TN
    gate_refs, rest = rest[:3 * n_half], rest[3 * n_half:]
    if fused:
        (wo_ref, x_ref, g_ref), rest = rest[:3], rest[3:]
        if final_norm:
            fnw_ref, rest = rest[0], rest[1:]
        o_ref, ap_ref, sp_ref, m_ref = rest
    else:
        o_ref, ap_ref, sp_ref = rest
        m_ref = o_ref
    bf = jnp.bfloat16
    zna = zna_ref[...].astype(jnp.float32)
    ap_ref[...] = (a_ref[...].astype(jnp.float32) * (zna * jax.nn.sigmoid(zna))).astype(bf)
    zs = zs_ref[...].astype(jnp.float32)
    t = s_ref[...].astype(jnp.float32) * (zs * jax.nn.sigmoid(zs))
    t = t * lax.rsqrt(jnp.mean(t * t, axis=-1, keepdims=True) + EPS) * nw_ref[...]
    sp_ref[...] = t.astype(bf)
    for h in range(n_half):
        cols = slice(h * MERGE_TN, (h + 1) * MERGE_TN)
        ga_ref, gf_ref, gs_ref = gate_refs[h], gate_refs[n_half + h], gate_refs[2 * n_half + h]
        o_a = jnp.dot(ap_ref[...], wa_ref[:, cols], preferred_element_type=jnp.float32)
        o_f = jnp.dot(gfo_ref[...], wf_ref[:, cols], preferred_element_type=jnp.float32)
        o_s = jnp.dot(sp_ref[...], ws_ref[:, cols], preferred_element_type=jnp.float32)
        m = (jax.nn.sigmoid(ga_ref[...].astype(jnp.float32)) * o_a
             + jax.nn.sigmoid(gf_ref[...].astype(jnp.float32)) * o_f
             + jax.nn.sigmoid(gs_ref[...].astype(jnp.float32)) * o_s)
        m_ref[:, cols] = m.astype(m_ref.dtype)
    if fused:
        y = jnp.dot(m_ref[...], wo_ref[...], preferred_element_type=jnp.float32)
        r = x_ref[...] + g_ref[0] * y
        if final_norm:
            r = r * lax.rsqrt(jnp.mean(r * r, axis=-1, keepdims=True) + EPS) * fnw_ref[...]
        o_ref[...] = r


class OutProj(NamedTuple):
    w_out_bf: jax.Array
    x2d: jax.Array
    gate: jax.Array
    rows_per_mod: int
    final_norm_w: object


def merge_branches(p2d, a2d, gfo2d, gfo_nt2, s2d, ssd_norm_w, wa, wf, ws, layer, *, tm, gfo_tile=None, out=None):
    m = p2d.shape[0]
    w, tn = NA_WIDTH, MERGE_TN
    n_half = D_MODEL // tn
    assert m % tm == 0
    row = lambda i: (i, 0)
    if gfo_nt2 is None:
        gfo_row = row
    else:
        assert gfo_tile % tm == 0 and p2d.shape[0] % (2 * gfo_nt2 * gfo_tile) == 0
        sub = gfo_tile // tm
        per_seq = 2 * gfo_nt2 * sub
        gfo_row = lambda i: ((i // per_seq) * per_seq
                             + fourier_row_block((i % per_seq) // sub, gfo_nt2) * sub + i % sub, 0)
    resident = lambda shape, idx: pl.BlockSpec(shape, lambda i: idx, pipeline_mode=pl.Buffered(1))
    w_spec = resident((None, w, D_MODEL), (layer, 0, 0))
    gate0 = DT_OFFSET // tn
    gate_specs = [pl.BlockSpec((tm, tn), partial(lambda i, blk: (i, blk), blk=gate0 + g * n_half + h))
                  for g in range(3) for h in range(n_half)]
    out_specs, out_args, out_dtype, final_norm = [], [], jnp.bfloat16, False
    scratch = [pltpu.VMEM((tm, w), jnp.bfloat16), pltpu.VMEM((tm, w), jnp.bfloat16)]
    if out is not None:
        assert out.rows_per_mod % tm == 0
        per = out.rows_per_mod // tm
        out_specs = [resident((None, D_MODEL, D_MODEL), (layer, 0, 0)),
                     pl.BlockSpec((tm, D_MODEL), row),
                     pl.BlockSpec((1, 1, D_MODEL), lambda i: (i // per, 0, 0))]
        out_args = [out.w_out_bf, out.x2d, out.gate]
        final_norm = out.final_norm_w is not None
        if final_norm:
            out_specs.append(resident((1, D_MODEL), (0, 0)))
            out_args.append(out.final_norm_w.astype(jnp.float32).reshape(1, D_MODEL))
        out_dtype = jnp.float32
        scratch.append(pltpu.VMEM((tm, D_MODEL), jnp.bfloat16))
    return pl.pallas_call(
        partial(_merge_kernel, fused=out is not None, final_norm=final_norm),
        grid=(m // tm,),
        in_specs=[pl.BlockSpec((tm, w), row),
                  pl.BlockSpec((tm, w), lambda i: (i, 3)),
                  pl.BlockSpec((tm, w), gfo_row),
                  pl.BlockSpec((tm, w), row),
                  pl.BlockSpec((tm, w), lambda i: (i, 8)),
                  resident((1, w), (0, 0)), w_spec, w_spec, w_spec] + gate_specs + out_specs,
        out_specs=pl.BlockSpec((tm, D_MODEL), row),
        out_shape=jax.ShapeDtypeStruct((m, D_MODEL), out_dtype),
        scratch_shapes=scratch,
        compiler_params=pltpu.CompilerParams(dimension_semantics=("arbitrary",),
                                             vmem_limit_bytes=V7X_VMEM_LIMIT_BYTES),
        name="merge_branches",
    )(a2d, p2d, gfo2d, s2d, p2d, ssd_norm_w.reshape(1, w), wa, wf, ws, *([p2d] * (3 * n_half)), *out_args)


def _outproj_kernel(m_ref, w_ref, x_ref, g_ref, *rest, final_norm):
    y = jnp.dot(m_ref[...], w_ref[...], preferred_element_type=jnp.float32)
    r = x_ref[...] + g_ref[0] * y
    if final_norm:
        nw_ref, o_ref = rest
        o_ref[...] = r * lax.rsqrt(jnp.mean(r * r, axis=-1, keepdims=True) + EPS) * nw_ref[...]
    else:
        rest[0][...] = r


def out_projection(m2d, w_out_bf, layer, x2d, gate, *, rows_per_mod, tm, tn, final_norm_w=None):
    m, d = x2d.shape
    assert m % tm == 0 and d % tn == 0 and rows_per_mod % tm == 0
    per = rows_per_mod // tm
    in_specs = [pl.BlockSpec((tm, d), lambda i, j: (i, 0)),
                pl.BlockSpec((None, d, tn), lambda i, j: (layer, 0, j)),
                pl.BlockSpec((tm, tn), lambda i, j: (i, j)),
                pl.BlockSpec((1, 1, tn), lambda i, j: (i // per, 0, j))]
    args = [m2d, w_out_bf, x2d, gate]
    if final_norm_w is not None:
        assert tn == d
        in_specs.append(pl.BlockSpec((1, d), lambda i, j: (0, 0)))
        args.append(final_norm_w.astype(jnp.float32).reshape(1, d))
    return pl.pallas_call(
        partial(_outproj_kernel, final_norm=final_norm_w is not None),
        grid=(m // tm, d // tn),
        in_specs=in_specs,
        out_specs=pl.BlockSpec((tm, tn), lambda i, j: (i, j)),
        out_shape=jax.ShapeDtypeStruct((m, d), jnp.float32),
        compiler_params=pltpu.CompilerParams(dimension_semantics=("arbitrary", "arbitrary"),
                                             vmem_limit_bytes=V7X_VMEM_LIMIT_BYTES),
        name="out_projection",
    )(*args)


SSD_GW = SSD_HPG * SSD_HEAD_DIM
CONV_PAD = 8
SSD_CHUNK_UNROLL = 2
SSD_CONV_UNROLL = 4
SSD_C_UNROLL = 8
SSD_A_UNROLL = 9


def _split3_bf16(a):
    rnd = lambda v: v.astype(jnp.bfloat16).astype(jnp.float32)
    a1 = rnd(a)
    a2 = rnd(a - a1)
    a3 = rnd((a - a1) - a2)
    return a1, a2, a3


def _head_rows(rows, r0, n):
    return jnp.concatenate([jnp.broadcast_to(rows[r0 + k:r0 + k + 1, :], (SSD_HEAD_DIM, n))
                            for k in range(SSD_HPG)], axis=0)


def _ssd_kernel(xl_ref, bl_ref, cl_ref, xc_ref, bc_ref, cc_ref, dtl_ref, dtc_ref, bias_ref, alog_ref, dsk_ref,
                wx_ref, wb_ref, wc_ref, bx_ref, bb_ref, bcb_ref, cos_ref, sin_ref,
                yl_ref, yc_ref,
                padx, padb, padc, xt_s, b_s, c_s, dtt_s, row_s, dec_s, yt_s, p_s, sp_s, st_s, *, seq, ctx_len):
    f32, bf = jnp.float32, jnp.bfloat16
    t = SSD_CHUNK
    nc_c, nc_l = ctx_len // t, seq // t
    nc = nc_c + nc_l

    lane = lax.broadcasted_iota(jnp.int32, (t, LANES), 1)
    first_half = (lane % 64) < 32

    def conv_tile(pad, w_ref, b_ref, t0, l0):
        acc = jnp.broadcast_to(b_ref[:, l0:l0 + LANES], (t, LANES))
        for j in range(SSD_CONV):
            off = CONV_PAD - SSD_CONV // 2 + j
            acc = acc + pad[pl.ds(t0 + off, t), :] * w_ref[j:j + 1, l0:l0 + LANES]
        return acc * jax.nn.sigmoid(acc)

    def rope(u, t0):
        swapped = jnp.where(first_half, pltpu.roll(u, 96, axis=1), pltpu.roll(u, 32, axis=1))
        return u * cos_ref[pl.ds(t0, t), :] + swapped * sin_ref[pl.ds(t0, t), :]

    def conv_seq(x_ref, bm_ref, cm_ref, n, base, use_rope):
        slabs = ([(padx.at[h], x_ref, h * LANES) for h in range(SSD_GW // LANES)]
                 + [(padb, bm_ref, 0), (padc, cm_ref, 0)])
        for pad, src, l0 in slabs:
            pad[0:CONV_PAD, :] = jnp.zeros((CONV_PAD, LANES), f32)
            pad[CONV_PAD + n:2 * CONV_PAD + n, :] = jnp.zeros((CONV_PAD, LANES), f32)
            pad[CONV_PAD:CONV_PAD + n, :] = src[0, :, l0:l0 + LANES].astype(f32)

        unroll = min(SSD_CONV_UNROLL, n // t)

        def chunks(i, carry):
            for j in range(unroll):
                ci = i * unroll + j
                t0 = pl.multiple_of(ci * t, t)
                r0 = pl.multiple_of(base + ci * t, t)
                for h in range(SSD_GW // LANES):
                    l0 = h * LANES
                    xt_s[base // t + ci, l0:l0 + LANES, :] = conv_tile(padx.at[h], wx_ref, bx_ref, t0, l0).T
                ub = conv_tile(padb, wb_ref, bb_ref, t0, 0)
                uc = conv_tile(padc, wc_ref, bcb_ref, t0, 0)
                if use_rope:
                    ub, uc = rope(ub, t0), rope(uc, t0)
                b_s[pl.ds(r0, t), :] = ub.astype(bf)
                c_s[pl.ds(r0, t), :] = uc.astype(bf)
            return carry

        lax.fori_loop(0, n // t // unroll, chunks, 0)

    conv_seq(xc_ref, bc_ref, cc_ref, ctx_len, 0, False)
    conv_seq(xl_ref, bl_ref, cl_ref, seq, ctx_len, True)

    i0 = lax.broadcasted_iota(jnp.int32, (t, t), 0)
    i1 = lax.broadcasted_iota(jnp.int32, (t, t), 1)
    tri_f = (i0 <= i1).astype(bf)
    tri_r = (i0 >= i1).astype(bf)
    nr = 2 * SSD_HPG
    fwd_row = (lax.broadcasted_iota(jnp.int32, (nc * nr, 1), 0) % nr) < SSD_HPG
    g4 = pl.program_id(1) * SSD_HPG
    raw = []
    for c in range(nc):
        src, c0 = (dtc_ref, c) if c < nc_c else (dtl_ref, c - nc_c)
        dtt_s[c] = src[0, c0 * t:(c0 + 1) * t, :].T
        raw += [dtt_s[c, pl.ds(g4, SSD_HPG), :], dtt_s[c, pl.ds(SSD_HEADS + g4, SSD_HPG), :]]
    x = jnp.concatenate(raw, axis=0) + jnp.tile(bias_ref[0], (nc, 1))
    dt = jnp.maximum(x, 0.0) + jnp.log1p(jnp.exp(-jnp.abs(x)))
    pieces = _split3_bf16(dt * jnp.tile(-jnp.exp(alog_ref[0]), (nc, 1)))
    pf = [jnp.dot(p.astype(bf), tri_f, preferred_element_type=f32) for p in pieces]
    pr = [jnp.dot(p.astype(bf), tri_r, preferred_element_type=f32) for p in pieces]
    cs = jnp.where(fwd_row, (pf[0] + pf[1]) + pf[2], (pr[0] + pr[1]) + pr[2])
    tot = jnp.where(fwd_row, cs[:, t - 1:t], cs[:, 0:1])
    e_in = jnp.exp(cs)
    e_out = dt * jnp.exp(tot - cs)
    u = cs - jnp.log(dt)
    decb = jnp.broadcast_to(jnp.exp(tot), (nc * nr, LANES))
    for c in range(nc):
        sl = slice(c * nr, (c + 1) * nr)
        row_s[c] = jnp.concatenate([cs[sl], dt[sl], e_in[sl], e_out[sl], u[sl]], axis=0)
        dec_s[c] = decb[sl]

    lower = i1 <= i0
    lower_t = i0 <= i1
    diag = i1 == i0

    def chunks_a(i, carry):
        ids = [i * SSD_A_UNROLL + j for j in range(SSD_A_UNROLL)]
        r0s = [pl.multiple_of(c * t, t) for c in ids]
        cms = [c_s[pl.ds(r0, t), :] for r0 in r0s]
        bms = [b_s[pl.ds(r0, t), :] for r0 in r0s]
        gs = [lax.dot_general(cm, bm, _NT, preferred_element_type=f32) for cm, bm in zip(cms, bms)]
        for c, r0, bm, g in zip(ids, r0s, bms, gs):
            rows = row_s[c]
            xt = xt_s[c]
            xt_bf = xt.astype(bf)
            parts = []
            for k in range(SSD_HPG):
                kr = SSD_HPG + k
                cs_l = jnp.where(lower_t, rows[k:k + 1, :], rows[kr:kr + 1, :]).T
                w = jnp.exp(cs_l - jnp.where(lower, rows[32 + k:33 + k, :], rows[32 + kr:33 + kr, :]))
                mk = (g * (w + jnp.where(diag, rows[8 + kr:9 + kr, :], 0.0))).astype(bf)
                xk = xt_bf[k * SSD_HEAD_DIM:(k + 1) * SSD_HEAD_DIM, :]
                parts.append(lax.dot_general(xk, mk, _NT, preferred_element_type=f32))
            yt_s[c] = dsk_ref[0] * xt + jnp.concatenate(parts, axis=0)
            for d in range(2):
                wt = (xt * _head_rows(rows, 24 + SSD_HPG * d, t)).astype(bf)
                p_s[d, c] = jnp.dot(wt, bm, preferred_element_type=f32)
        return carry

    lax.fori_loop(0, nc // SSD_A_UNROLL, chunks_a, 0)

    st_s[...] = jnp.zeros(st_s.shape, f32)

    def state_step(c, d):
        st = st_s[d]
        sp_s[d, c] = st.astype(bf)
        st_s[d] = st * _head_rows(dec_s[c], SSD_HPG * d, SSD_STATE) + p_s[d, c]

    def ctx_step(i, carry):
        state_step(i, 0)
        state_step(nc_c - 1 - i, 1)
        return carry

    def lat_step(i, carry):
        state_step(nc_c + i, 0)
        state_step(nc - 1 - i, 1)
        return carry

    lax.fori_loop(0, nc_c, ctx_step, 0)
    lax.fori_loop(0, nc_l, lat_step, 0)

    def chunks_c(first, out_ref, o_first, count):
        ids = [first + u for u in range(count)]
        cms = [c_s[pl.ds(pl.multiple_of(c * t, t), t), :] for c in ids]
        offs = [[lax.dot_general(sp_s[d, c], cm, _NT, preferred_element_type=f32) for d in range(2)]
                for c, cm in zip(ids, cms)]
        for u, (c, off) in enumerate(zip(ids, offs)):
            rows = row_s[c]
            yt = yt_s[c]
            for d in range(2):
                yt = yt + off[d] * _head_rows(rows, 16 + SSD_HPG * d, t)
            for l0 in range(0, SSD_GW, LANES):
                out_ref[0, pl.ds(pl.multiple_of((o_first + u) * t, t), t), l0:l0 + LANES] = yt[l0:l0 + LANES, :].T

    def ctx_out(i, carry):
        chunks_c(i * SSD_CHUNK_UNROLL, yc_ref, i * SSD_CHUNK_UNROLL, SSD_CHUNK_UNROLL)
        return carry

    def lat_out(i, carry):
        chunks_c(nc_c + i * SSD_C_UNROLL, yl_ref, i * SSD_C_UNROLL, SSD_C_UNROLL)
        return carry

    lax.fori_loop(0, nc_c // SSD_CHUNK_UNROLL, ctx_out, 0)
    lax.fori_loop(0, nc_l // SSD_C_UNROLL, lat_out, 0)


def _rope_tables(seq):
    quarter = SSD_STATE // 4
    inv = ROPE_BASE ** (-np.arange(quarter, dtype=np.float64) / quarter)
    pos = np.arange(seq)
    ang_r = (pos // GRID_W)[:, None] * inv
    ang_c = (pos % GRID_W)[:, None] * inv
    cos = np.concatenate([np.cos(ang_r)] * 2 + [np.cos(ang_c)] * 2, axis=1)
    sin = np.concatenate([-np.sin(ang_r), np.sin(ang_r), -np.sin(ang_c), np.sin(ang_c)], axis=1)
    return jnp.asarray(cos, jnp.float32), jnp.asarray(sin, jnp.float32)


def ssd_mixer(p_lat, p_ctx, dt_lat, dt_ctx, conv_w, conv_b, dt_bias, a_log, d_skip, ctx_cols=FULL_COLS):
    bsz, seq, _ = p_lat.shape
    ctx_len = p_ctx.shape[1]
    t, g, hpg = SSD_CHUNK, SSD_GROUPS, SSD_HPG
    assert seq % t == 0 and ctx_len % t == 0
    nc = (seq + ctx_len) // t
    per_row = lambda v: jnp.broadcast_to(
        v.astype(jnp.float32).reshape(2, g, hpg).transpose(1, 0, 2).reshape(g, 2 * hpg, 1), (g, 2 * hpg, LANES))
    dsk = jnp.broadcast_to(jnp.repeat(d_skip.astype(jnp.float32).reshape(g, hpg), SSD_HEAD_DIM, axis=1)[:, :, None],
                           (g, SSD_GW, LANES))
    cos, sin = _rope_tables(seq)
    x0 = sum(MAIN_SPLITS[:6]) // SSD_GW
    b0 = (sum(MAIN_SPLITS[:6]) + SSD_WIDTH) // SSD_STATE
    c0 = b0 + g
    cw = conv_w.astype(jnp.float32)
    cb = conv_b.astype(jnp.float32).reshape(1, SSD_CONV_CH)
    wb0 = SSD_WIDTH // SSD_STATE
    ltot = seq + ctx_len
    f32, bf = jnp.float32, jnp.bfloat16
    y_lat, y_ctx = pl.pallas_call(
        partial(_ssd_kernel, seq=seq, ctx_len=ctx_len),
        grid=(bsz, g),
        in_specs=[pl.BlockSpec((1, seq, SSD_GW), lambda b, i: (b, 0, x0 + i)),
                  pl.BlockSpec((1, seq, SSD_STATE), lambda b, i: (b, 0, b0 + i)),
                  pl.BlockSpec((1, seq, SSD_STATE), lambda b, i: (b, 0, c0 + i)),
                  pl.BlockSpec((1, ctx_len, SSD_GW), lambda b, i: (b, 0, ctx_cols.xbc // SSD_GW + i)),
                  pl.BlockSpec((1, ctx_len, SSD_STATE), lambda b, i: (b, 0, (ctx_cols.xbc + SSD_WIDTH) // SSD_STATE + i)),
                  pl.BlockSpec((1, ctx_len, SSD_STATE), lambda b, i: (b, 0, (ctx_cols.xbc + SSD_WIDTH) // SSD_STATE + g + i)),
                  pl.BlockSpec((1, seq, DT_PAD), lambda b, i: (b, 0, 0)),
                  pl.BlockSpec((1, ctx_len, DT_PAD), lambda b, i: (b, 0, 0)),
                  pl.BlockSpec((1, 2 * hpg, LANES), lambda b, i: (i, 0, 0)),
                  pl.BlockSpec((1, 2 * hpg, LANES), lambda b, i: (i, 0, 0)),
                  pl.BlockSpec((1, SSD_GW, LANES), lambda b, i: (i, 0, 0)),
                  pl.BlockSpec((SSD_CONV, SSD_GW), lambda b, i: (0, i)),
                  pl.BlockSpec((SSD_CONV, SSD_STATE), lambda b, i: (0, wb0 + i)),
                  pl.BlockSpec((SSD_CONV, SSD_STATE), lambda b, i: (0, wb0 + g + i)),
                  pl.BlockSpec((1, SSD_GW), lambda b, i: (0, i)),
                  pl.BlockSpec((1, SSD_STATE), lambda b, i: (0, wb0 + i)),
                  pl.BlockSpec((1, SSD_STATE), lambda b, i: (0, wb0 + g + i)),
                  pl.BlockSpec((seq, SSD_STATE), lambda b, i: (0, 0)),
                  pl.BlockSpec((seq, SSD_STATE), lambda b, i: (0, 0))],
        out_specs=[pl.BlockSpec((1, seq, SSD_GW), lambda b, i: (b, 0, i)),
                   pl.BlockSpec((1, ctx_len, SSD_GW), lambda b, i: (b, 0, i))],
        out_shape=[jax.ShapeDtypeStruct((bsz, seq, SSD_WIDTH), f32),
                   jax.ShapeDtypeStruct((bsz, ctx_len, SSD_WIDTH), f32)],
        scratch_shapes=[pltpu.VMEM((SSD_GW // LANES, seq + 2 * CONV_PAD, LANES), f32),
                        pltpu.VMEM((seq + 2 * CONV_PAD, SSD_STATE), f32),
                        pltpu.VMEM((seq + 2 * CONV_PAD, SSD_STATE), f32),
                        pltpu.VMEM((nc, SSD_GW, t), f32),
                        pltpu.VMEM((ltot, SSD_STATE), bf),
                        pltpu.VMEM((ltot, SSD_STATE), bf),
                        pltpu.VMEM((nc, DT_PAD, t), f32),
                        pltpu.VMEM((nc, 5 * 2 * hpg, t), f32),
                        pltpu.VMEM((nc, 2 * hpg, LANES), f32),
                        pltpu.VMEM((nc, SSD_GW, t), f32),
                        pltpu.VMEM((2, nc, SSD_GW, SSD_STATE), f32),
                        pltpu.VMEM((2, nc, SSD_GW, SSD_STATE), bf),
                        pltpu.VMEM((2, SSD_GW, SSD_STATE), f32)],
        compiler_params=pltpu.CompilerParams(dimension_semantics=("arbitrary", "arbitrary"),
                                             vmem_limit_bytes=V7X_VMEM_LIMIT_BYTES),
        name="ssd_mixer",
    )(p_lat, p_lat, p_lat, p_ctx, p_ctx, p_ctx, dt_lat, dt_ctx, per_row(dt_bias), per_row(a_log), dsk,
      cw, cw, cw, cb, cb, cb, cos, sin)
    return y_ctx, y_lat


INPROJ_TM = 1024
INPROJ_TN = 2560
ADA_TN = 1024
BRANCH_TM = 512
MERGE_OUT_TM = 256
CTX_OUT_TILE = 1024
ADA_ROWS = 16


def _layer(xc, xl, c, c_ctx, w_ada_all, layer, b_ada, norm_w, w_main_all, w_dt_all, rpb, four_wb, conv_w, conv_b, dt_bias,
           a_log, d_skip, ssd_norm_w, wb_nab, wb_fourb, wb_ssdb, w_outb, update_ctx, final_norm_w):
    bsz, seq, d = xl.shape
    ctx_len = xc.shape[1]
    bf = jnp.bfloat16
    assert bsz < ADA_ROWS

    cc = jnp.concatenate([c, c_ctx[None], jnp.zeros((ADA_ROWS - bsz - 1, d), c.dtype)], axis=0)
    mod = matmul(jax.nn.silu(cc), w_ada_all, layer, tm=ADA_ROWS, tn=ADA_TN) + b_ada
    sh, sc, gt = jnp.split(mod, 3, axis=-1)
    sh_l, sc_l, g_l = sh[:bsz], sc[:bsz], gt[:bsz]
    sh_c, sc_c, g_c = sh[bsz:bsz + 1], sc[bsz:bsz + 1], gt[bsz]

    xl2d = xl.reshape(bsz * seq, d)
    xc2d = xc.reshape(bsz * ctx_len, d)
    pl2d, dt_l = in_projection(xl2d, norm_w, sc_l[:, None], sh_l[:, None], w_main_all, layer, w_dt_all, layer,
                               rows_per_mod=seq, tm=INPROJ_TM, tn=INPROJ_TN, out_dtype=bf)
    if update_ctx:
        ctx_cols, w_ctx, ctx_layer, ctx_tn = FULL_COLS, w_main_all, layer, INPROJ_TN
    else:
        ctx_cols, ctx_layer = KV_XBC_COLS, 0
        w_ctx = jnp.concatenate([w_main_all[layer, FULL_COLS.k:FULL_COLS.v + NA_WIDTH],
                                 w_main_all[layer, FULL_COLS.xbc:FULL_COLS.xbc + SSD_CONV_CH]], axis=0)[None]
        ctx_tn = w_ctx.shape[1] // 2
    pc2d, dt_c = in_projection(xc2d, norm_w, sc_c[:, None], sh_c[:, None], w_ctx, ctx_layer, w_dt_all, layer,
                               rows_per_mod=bsz * ctx_len, tm=INPROJ_TM, tn=ctx_tn, out_dtype=bf)
    p_l = pl2d.reshape(bsz, seq, MAIN_WIDTH)
    p_c = pc2d.reshape(bsz, ctx_len, w_ctx.shape[1])
    dt_l = dt_l.reshape(bsz, seq, DT_PAD)
    dt_c = dt_c.reshape(bsz, ctx_len, DT_PAD)

    a_l, a_c = na_attention(p_l, p_c, na_rpb_rows(rpb), with_ctx=update_ctx, out_dtype=bf, ctx_cols=ctx_cols)
    gfo_l, nt2_l = fourier_branch(p_l, four_wb, layer, tm=BRANCH_TM)
    s_c, s_l = ssd_mixer(p_l, p_c, dt_l, dt_c, conv_w, conv_b, dt_bias, a_log, d_skip, ctx_cols=ctx_cols)
    xl_new = merge_branches(pl2d, a_l.reshape(bsz * seq, NA_WIDTH), gfo_l.reshape(bsz * seq, FOUR_WIDTH), nt2_l,
                            s_l.reshape(bsz * seq, SSD_WIDTH), ssd_norm_w, wb_nab, wb_fourb, wb_ssdb, layer,
                            tm=MERGE_OUT_TM, gfo_tile=BRANCH_TM,
                            out=OutProj(w_outb, xl2d, g_l[:, None], seq, final_norm_w))
    xl_new = xl_new.reshape(bsz, seq, d)
    if update_ctx:
        gfo_c, nt2_c = fourier_branch(p_c, four_wb, layer, tm=ctx_len // 2)
        assert nt2_c == 1
        m_c = merge_branches(pc2d, a_c.reshape(bsz * ctx_len, NA_WIDTH), gfo_c.reshape(bsz * ctx_len, FOUR_WIDTH), None,
                             s_c.reshape(bsz * ctx_len, SSD_WIDTH), ssd_norm_w, wb_nab, wb_fourb, wb_ssdb, layer, tm=BRANCH_TM)
        xc = out_projection(m_c, w_outb, layer, xc2d, g_c[None, None], rows_per_mod=bsz * ctx_len, tm=CTX_OUT_TILE, tn=CTX_OUT_TILE)
        xc = xc.reshape(bsz, ctx_len, d)
    return xc, xl_new


def kernel(x, c, ctx, c_ctx, w_ada, b_ada, norm_w, w_in, na_rpb, four_w, ssd_conv_w, ssd_conv_b, ssd_dt_bias,
           ssd_a_log, ssd_d, ssd_norm_w, wb_na, wb_four, wb_ssd, w_out, final_norm_w):
    xc, xl = ctx, x
    w_main_all, w_dt_all = repack_w_in(w_in)
    four_wb, wb_nab, wb_fourb, wb_ssdb, w_outb = (t.astype(jnp.bfloat16) for t in (four_w, wb_na, wb_four, wb_ssd, w_out))
    for l in range(DEPTH):
        xc, xl = _layer(xc, xl, c, c_ctx, w_ada, l, b_ada[l], norm_w[l], w_main_all, w_dt_all, na_rpb[l], four_wb,
                        ssd_conv_w[l], ssd_conv_b[l], ssd_dt_bias[l], ssd_a_log[l], ssd_d[l], ssd_norm_w[l],
                        wb_nab, wb_fourb, wb_ssdb, w_outb, update_ctx=(l < DEPTH - 1),
                        final_norm_w=final_norm_w if l == DEPTH - 1 else None)
    return xl
```
